```python
import math
import jax, jax.numpy as jnp
from jax import lax
import numpy as np

D_MODEL = 1024
BATCH = 8
SEQ = 4096
DEPTH = 2

GRID_W = 64
CTX_LEN = 256
N_EVEN = (DEPTH + 1) // 2
N_ODD = DEPTH // 2
DN_ALPHA = (2 * DEPTH) ** 0.25
DN_BETA = (8 * DEPTH) ** -0.25
LN_EPS = 1e-5
F32 = jnp.float32

GLA_HEADS = 4
GLA_DK = 64
GLA_DV = 128
GLA_QK = GLA_HEADS * GLA_DK
GLA_V = GLA_HEADS * GLA_DV
GLA_GATE_RANK = 16
GLA_GATE_NORM = 16.0
CHUNK = 64
HG_HEADS = 4
HG_EXPAND = 128
HG_W = HG_HEADS * HG_EXPAND
EVEN_COLS = (GLA_QK, GLA_QK, GLA_V, GLA_V, GLA_GATE_RANK, GLA_GATE_RANK, HG_W, HG_W, HG_W, HG_W, HG_W)
EVEN_WIDTH = sum(EVEN_COLS)
HY_CH = 512
HY_ORDER = 2
HY_WIDTH = (HY_ORDER + 1) * HY_CH
HY_SHORT = 3
HY_EMB = 33
HY_FILTER_HIDDEN = 64
HY_FAST_DECAY = 0.3
HY_SLOW_DECAY = 1.5
HY_TARGET = 1e-2
RW_HEADS = 8
RW_HEAD_DIM = 64
RW_W = RW_HEADS * RW_HEAD_DIM
RW_DECAY_LORA = 64
RW_AAA_LORA = 64
RW_GATE_LORA = 128
RW_GN_EPS = 64e-5
RW_COLS = (RW_W, RW_W, RW_W, RW_DECAY_LORA, RW_DECAY_LORA, RW_AAA_LORA, RW_GATE_LORA)
RW_WIDTH = sum(RW_COLS)
ODD_WIDTH = HY_WIDTH + RW_WIDTH
N_EXPERTS = 256
TOP_K = 8
N_GROUPS = 8
TOPK_GROUPS = 4
EXPERT_FF = 256
SHARED_FF = 256
ROUTED_SCALE = 2.5
MOE_BLOCK = 128

kernel_name = 'hybrid_gla_hgrn2_hyena_rwkv7_moe_dit'


def split_cols(t, sizes):
    return jnp.split(t, np.cumsum(sizes)[:-1].tolist(), axis=-1)


def layer_norm(x, g, b, eps=LN_EPS):
    xf = x.astype(F32)
    mu = xf.mean(-1, keepdims=True)
    var = jnp.square(xf - mu).mean(-1, keepdims=True)
    return ((xf - mu) * lax.rsqrt(var + eps)).astype(x.dtype) * g + b


def rms_norm(x, g, eps=1e-6):
    xf = x.astype(F32)
    return (xf * lax.rsqrt(jnp.mean(xf * xf, -1, keepdims=True) + eps)).astype(x.dtype) * g


def to_heads(t, n_heads):
    B, L, _ = t.shape
    return t.reshape(B, L, n_heads, -1).transpose(0, 2, 1, 3)


def from_heads(t):
    B, H, L, d = t.shape
    return t.transpose(0, 2, 1, 3).reshape(B, L, H * d)


def stack_dirs(t_fwd, t_bwd, axis):
    return jnp.stack([t_fwd, jnp.flip(t_bwd, axis)], 0)


def merge_dirs(o, axis):
    return o[0] + jnp.flip(o[1], axis)


def chunked_gated_recurrence(q, k, v, log_a, s0):
    L = q.shape[-2]
    nc = L // CHUNK

    def blocks(t):
        t = t.astype(F32)
        return jnp.moveaxis(t.reshape(t.shape[:-2] + (nc, CHUNK, t.shape[-1])), -3, 0)

    qc, kc, vc, ac = blocks(q), blocks(k), blocks(v), blocks(log_a)
    b = jnp.cumsum(ac, axis=-2)
    b_ref = b[..., CHUNK // 2:CHUNK // 2 + 1, :]
    b_last = b[..., -1:, :]
    scores = jnp.einsum('n...id,n...jd->n...ij', qc * jnp.exp(b - b_ref), kc * jnp.exp(b_ref - b))
    lower = jnp.tril(jnp.ones((CHUNK, CHUNK), bool))
    o_intra = jnp.einsum('n...ij,n...jv->n...iv', jnp.where(lower, scores, 0.0), vc)
    q_in = qc * jnp.exp(b)
    k_out = kc * jnp.exp(b_last - b)
    dec = jnp.exp(b_last[..., 0, :])

    def step(S, xs):
        q_t, k_t, v_t, d_t = xs
        o = jnp.einsum('...id,...dv->...iv', q_t, S)
        S = S * d_t[..., :, None] + jnp.einsum('...jd,...jv->...dv', k_t, v_t)
        return S, o

    S_fin, o_inter = lax.scan(step, s0.astype(F32), (q_in, k_out, vc, dec))
    o = jnp.moveaxis(o_intra + o_inter, 0, -3)
    return o.reshape(o.shape[:-3] + (L, o.shape[-1])), S_fin


def bidir_chunked(ctx_in, lat_in):
    def stacked(q, k_f, k_b, v, la_f, la_b):
        return (stack_dirs(q, q, -2), stack_dirs(k_f, k_b, -2), stack_dirs(v, v, -2), stack_dirs(la_f, la_b, -2))

    qc, kc, vc, lac = stacked(*ctx_in)
    s0 = jnp.zeros(kc.shape[:-2] + (kc.shape[-1], vc.shape[-1]), F32)
    o_c, s_c = chunked_gated_recurrence(qc, kc, vc, lac, s0)
    o_l, _ = chunked_gated_recurrence(*stacked(*lat_in), s_c)
    return merge_dirs(o_c, -2), merge_dirs(o_l, -2)


def gla_hgrn_mixer(h, hc, w_in, w_out, gate_w2, gate_b, gla_g, lb, hg_g):
    def prep(hh):
        q, k, v, r, a_f, a_b, hq, hf_f, hf_b, hi, hgate = split_cols(hh @ w_in, EVEN_COLS)
        la = [jax.nn.log_sigmoid((a @ gate_w2[d] + gate_b[d]).astype(F32)) / GLA_GATE_NORM
              for d, a in enumerate((a_f, a_b))]
        gla = (to_heads(q * GLA_DK ** -0.5, GLA_HEADS), to_heads(k, GLA_HEADS), to_heads(k, GLA_HEADS),
               to_heads(v, GLA_HEADS), to_heads(la[0], GLA_HEADS), to_heads(la[1], GLA_HEADS))
        f = [lb[d] + (1.0 - lb[d]) * jax.nn.sigmoid(z.astype(F32)) for d, z in enumerate((hf_f, hf_b))]
        hg = (to_heads(jax.nn.silu(hq), HG_HEADS), to_heads(1.0 - f[0], HG_HEADS), to_heads(1.0 - f[1], HG_HEADS),
              to_heads(hi, HG_HEADS), to_heads(jnp.log(f[0]), HG_HEADS), to_heads(jnp.log(f[1]), HG_HEADS))
        return gla, hg, r, hgate

    gla_c, hg_c, r_c, gt_c = prep(hc)
    gla_l, hg_l, r_l, gt_l = prep(h)
    o_gla_c, o_gla_l = bidir_chunked(gla_c, gla_l)
    o_hg_c, o_hg_l = bidir_chunked(hg_c, hg_l)

    def readout(o_gla, o_hg, r, gt, dtype):
        y = jnp.concatenate([from_heads(rms_norm(o_gla, gla_g)) * jax.nn.silu(r),
                             from_heads(rms_norm(o_hg, hg_g)) * jax.nn.silu(gt)], -1)
        return (y @ w_out).astype(dtype)

    return readout(o_gla_l, o_hg_l, r_l, gt_l, h.dtype), readout(o_gla_c, o_hg_c, r_c, gt_c, hc.dtype)


def hyena_filters(L, w1, b1, w2, b2, w3, sin_freq):
    t = jnp.linspace(0.0, 1.0, L, dtype=F32)[:, None]
    bands = (HY_EMB - 1) // 2
    wpos = 2.0 * math.pi * jnp.arange(L, dtype=F32)[:, None] / L
    fr = jnp.linspace(1e-4, bands - 1, bands, dtype=F32)[None, :]
    z = jnp.concatenate([t, jnp.cos(fr * wpos), -jnp.sin(fr * wpos)], -1)
    hdn = jnp.sin(sin_freq[0] * (z @ w1 + b1))
    hdn = jnp.sin(sin_freq[1] * (hdn @ w2 + b2))
    filt = (hdn @ w3).reshape(L, HY_ORDER, 2, HY_CH)
    max_decay = math.log(HY_TARGET) / HY_FAST_DECAY
    min_decay = math.log(HY_TARGET) / HY_SLOW_DECAY
    deltas = jnp.linspace(min_decay, max_decay, HY_CH, dtype=F32)
    window = jnp.exp(-t * jnp.abs(deltas))
    return filt * window[:, None, None, :]


def centred_long_conv(u, h_fwd, h_bwd):
    L = u.shape[1]
    filt_full = jnp.concatenate([h_fwd, jnp.zeros_like(h_fwd[:1]), jnp.flip(h_bwd[1:], 0)], 0)
    uf = jnp.fft.rfft(u.astype(F32), n=2 * L, axis=1)
    ff = jnp.fft.rfft(filt_full.astype(F32), n=2 * L, axis=0)
    return jnp.fft.irfft(uf * ff[None], n=2 * L, axis=1)[:, :L].astype(u.dtype)


def hyena(P, conv_w, conv_b, f_w1, f_b1, f_w2, f_b2, f_w3, sin_freq, hy_bias):
    L = P.shape[1]
    u = lax.conv_general_dilated(P, conv_w[:, None, :], window_strides=(1,),
                                 padding=[(HY_SHORT // 2, HY_SHORT // 2)],
                                 dimension_numbers=('NWC', 'WIO', 'NWC'),
                                 feature_group_count=HY_WIDTH) + conv_b
    v, x1, x2 = jnp.split(u, HY_ORDER + 1, axis=-1)
    filt = hyena_filters(L, f_w1, f_b1, f_w2, f_b2, f_w3, sin_freq)
    z = v
    for n, gate in enumerate((x1, x2)):
        z = gate * (centred_long_conv(z, filt[:, n, 0], filt[:, n, 1]) + z * hy_bias[n])
    return z


def grid_shift(t):
    B, L, C = t.shape
    rows = L // GRID_W
    g = jnp.pad(t.reshape(B, rows, GRID_W, C), ((0, 0), (1, 1), (1, 1), (0, 0)))
    g = g.reshape(B, rows + 2, GRID_W + 2, C // 4, 4)
    out = jnp.stack([g[:, 1:-1, :-2, :, 0], g[:, 1:-1, 2:, :, 1],
                     g[:, :-2, 1:-1, :, 2], g[:, 2:, 1:-1, :, 3]], -1)
    return out.reshape(B, L, C)


def seq_shift(t):
    B, L, C = t.shape
    p = jnp.pad(t, ((0, 0), (1, 1), (0, 0))).reshape(B, L + 2, C // 2, 2)
    return jnp.stack([p[:, :-2, :, 0], p[:, 2:, :, 1]], -1).reshape(B, L, C)


def rwkv_streams(P, shift_fn, mu, w0, w2, a0, a2, g2, k_k, k_a):
    P = P + mu * (shift_fn(P) - P)
    r, k, v, wl_f, wl_b, al, gl = split_cols(P, RW_COLS)

    def heads(t):
        return t.reshape(t.shape[:-1] + (RW_HEADS, RW_HEAD_DIM))

    def log_decay(wl, d):
        w = -jax.nn.softplus(-(w0[d] + jnp.tanh(wl) @ w2[d]).astype(F32)) - 0.5
        return heads(-jnp.exp(w))

    a = jax.nn.sigmoid(a0 + al @ a2)
    g = jax.nn.sigmoid(gl) @ g2
    kk = heads(k * k_k).astype(F32)
    kk = kk / jnp.maximum(jnp.linalg.norm(kk, axis=-1, keepdims=True), 1e-12)
    k = k * (1.0 + (a - 1.0) * k_a)
    return (heads(r), heads(k), heads(v), kk, heads(a), g, log_decay(wl_f, 0), log_decay(wl_b, 1))


def rwkv_bidir_scan(st, s0, emit):
    r, k, v, kk, a, g, ld_f, ld_b = st

    def tm(t_f, t_b=None):
        t_b = t_f if t_b is None else t_b
        return jnp.moveaxis(stack_dirs(t_f, t_b, 1), 2, 0).astype(F32)

    xs = (tm(jnp.exp(ld_f), jnp.exp(ld_b)), tm(k), tm(v), tm(kk), tm(kk * a)) + ((tm(r),) if emit else ())

    def step(S, xt):
        d_t, k_t, v_t, kk_t, b_t = xt[:5]
        S = (S * d_t[..., None, :]
             - jnp.einsum('...vk,...k->...v', S, kk_t)[..., None] * b_t[..., None, :]
             + v_t[..., :, None] * k_t[..., None, :])
        return S, (jnp.einsum('...vk,...k->...v', S, xt[5]) if emit else None)

    S, o = lax.scan(step, s0, xs)
    if emit:
        o = merge_dirs(jnp.moveaxis(o, 0, 2), 1)
    return S, o


def rwkv_readout(o, st, r_k, ln_g, ln_b):
    r, k, v, kk, a, g, _, _ = st
    B, L = o.shape[:2]
    on = layer_norm(o, ln_g, ln_b, RW_GN_EPS)
    bonus = jnp.sum(r * k * r_k, -1, keepdims=True) * v
    return (on + bonus).reshape(B, L, RW_W) * g


def hyena_rwkv_mixer(h, hc, need_ctx_out, w_in, w_out, conv_w, conv_b, f_w1, f_b1, f_w2, f_b2, f_w3,
                     sin_freq, hy_bias, mu, w0, w2, a0, a2, g2, k_k, k_a, r_k, ln_g, ln_b):
    w_hy, w_rw = w_in[:, :HY_WIDTH], w_in[:, HY_WIDTH:]
    rw_args = (mu, w0, w2, a0, a2, g2, k_k, k_a)
    st_c = rwkv_streams(hc @ w_rw, seq_shift, *rw_args)
    st_l = rwkv_streams(h @ w_rw, grid_shift, *rw_args)
    s0 = jnp.zeros((2, h.shape[0], RW_HEADS, RW_HEAD_DIM, RW_HEAD_DIM), F32)
    s_c, o_c = rwkv_bidir_scan(st_c, s0, need_ctx_out)
    _, o_l = rwkv_bidir_scan(st_l, s_c, True)
    hy_args = (conv_w, conv_b, f_w1, f_b1, f_w2, f_b2, f_w3, sin_freq, hy_bias)

    def readout(hh, st, o):
        y = jnp.concatenate([hyena(hh @ w_hy, *hy_args), rwkv_readout(o, st, r_k, ln_g, ln_b)], -1)
        return (y @ w_out).astype(hh.dtype)

    y_c = readout(hc, st_c, o_c) if need_ctx_out else None
    return readout(h, st_l, o_l), y_c


def routed_experts(u, eidx, gate, w13, w2):
    T, D = u.shape
    N = T * TOP_K
    flat_e = eidx.reshape(N)
    order = jnp.argsort(flat_e)
    sorted_e = flat_e[order]
    counts = jnp.bincount(flat_e, length=N_EXPERTS)
    padded = (counts + MOE_BLOCK - 1) // MOE_BLOCK * MOE_BLOCK
    pad_end = jnp.cumsum(padded)
    pad_start = pad_end - padded
    cnt_start = jnp.cumsum(counts) - counts
    dest = pad_start[sorted_e] + jnp.arange(N) - cnt_start[sorted_e]
    n_blocks = -(-N // MOE_BLOCK) + N_EXPERTS
    slot_tok = jnp.full((n_blocks * MOE_BLOCK,), T, jnp.int32).at[dest].set((order // TOP_K).astype(jnp.int32))
    slot_gate = jnp.zeros((n_blocks * MOE_BLOCK,), u.dtype).at[dest].set(gate.reshape(N)[order])
    blk_exp = jnp.minimum(jnp.searchsorted(pad_end, jnp.arange(n_blocks) * MOE_BLOCK, side='right'), N_EXPERTS - 1)
    u_pad = jnp.concatenate([u, jnp.zeros((1, D), u.dtype)], 0)

    def body(acc, xs):
        tok, gt, e = xs
        h13 = u_pad[tok] @ w13[e]
        yb = (jax.nn.silu(h13[:, :EXPERT_FF]) * h13[:, EXPERT_FF:]) @ w2[e]
        return acc.at[tok].add(yb * gt[:, None]), None

    acc, _ = lax.scan(body, jnp.zeros((T + 1, D), u.dtype),
                      (slot_tok.reshape(n_blocks, MOE_BLOCK), slot_gate.reshape(n_blocks, MOE_BLOCK), blk_exp))
    return acc[:T]


def moe(u, router_w, router_bias, exp_w13, exp_w2, sh_w13, sh_w2):
    T = u.shape[0]
    per_group = N_EXPERTS // N_GROUPS
    scores = jax.nn.sigmoid((u @ router_w).astype(F32))
    sel = scores + router_bias.astype(F32)
    grp_score = lax.top_k(sel.reshape(T, N_GROUPS, per_group), 2)[0].sum(-1)
    _, grp_idx = lax.top_k(grp_score, TOPK_GROUPS)
    grp_mask = jax.nn.one_hot(grp_idx, N_GROUPS, dtype=F32).sum(1) > 0
    sel = jnp.where(jnp.repeat(grp_mask, per_group, axis=1), sel, -jnp.inf)
    _, eidx = lax.top_k(sel, TOP_K)
    gate = jnp.take_along_axis(scores, eidx, 1)
    gate = gate / gate.sum(-1, keepdims=True) * ROUTED_SCALE
    routed = routed_experts(u, eidx, gate.astype(u.dtype), exp_w13, exp_w2)
    hs = u @ sh_w13
    shared = (jax.nn.silu(hs[:, :SHARED_FF]) * hs[:, SHARED_FF:]) @ sh_w2
    return routed + shared


def setup_inputs(seed: int = 0) -> dict:
    key = jax.random.key(seed)
    ks = iter(jax.random.split(key, 64))
    D = D_MODEL

    def nrm(shape, scale):
        return jax.random.normal(next(ks), shape, jnp.float32) * scale

    def gain(shape):
        return 1.0 + nrm(shape, 0.02)

    return {
        'x': nrm((BATCH, SEQ, D), 1.0),
        'c': nrm((BATCH, D), 1.0),
        'ctx': nrm((BATCH, CTX_LEN, D), 1.0),
        'c_ctx': nrm((D,), 1.0),
        'mod_w': nrm((DEPTH, D, 6 * D), 0.5 * D ** -0.5),
        'mod_b': nrm((DEPTH, 6 * D), 0.02),
        'ln1_g': gain((DEPTH, D)),
        'ln1_b': nrm((DEPTH, D), 0.02),
        'ln2_g': gain((DEPTH, D)),
        'ln2_b': nrm((DEPTH, D), 0.02),
        'ev_w_in': nrm((N_EVEN, D, EVEN_WIDTH), D ** -0.5),
        'ev_w_out': nrm((N_EVEN, GLA_V + HG_W, D), (GLA_V + HG_W) ** -0.5 * DN_BETA),
        'gla_gate_w2': nrm((N_EVEN, 2, GLA_GATE_RANK, GLA_QK), GLA_GATE_RANK ** -0.5),
        'gla_gate_b': nrm((N_EVEN, 2, GLA_QK), 0.1),
        'gla_norm_g': gain((N_EVEN, GLA_DV)),
        'hg_lb_logits': nrm((N_EVEN + 1, 2, HG_W), 0.1),
        'hg_norm_g': gain((N_EVEN, HG_EXPAND)),
        'od_w_in': nrm((N_ODD, D, ODD_WIDTH), D ** -0.5),
        'od_w_out': nrm((N_ODD, HY_CH + RW_W, D), (HY_CH + RW_W) ** -0.5 * DN_BETA),
        'hy_conv_w': nrm((N_ODD, HY_SHORT, HY_WIDTH), HY_SHORT ** -0.5),
        'hy_conv_b': nrm((N_ODD, HY_WIDTH), 0.02),
        'hy_ffn_w1': nrm((N_ODD, HY_EMB, HY_FILTER_HIDDEN), HY_EMB ** -0.5),
        'hy_ffn_b1': nrm((N_ODD, HY_FILTER_HIDDEN), 0.02),
        'hy_ffn_w2': nrm((N_ODD, HY_FILTER_HIDDEN, HY_FILTER_HIDDEN), HY_FILTER_HIDDEN ** -0.5),
        'hy_ffn_b2': nrm((N_ODD, HY_FILTER_HIDDEN), 0.02),
        'hy_ffn_w3': nrm((N_ODD, HY_FILTER_HIDDEN, HY_ORDER * 2 * HY_CH), 0.1 * HY_FILTER_HIDDEN ** -0.5),
        'hy_sin_freq': 1.0 + nrm((N_ODD, 2, HY_FILTER_HIDDEN), 0.1),
        'hy_bias': nrm((N_ODD, HY_ORDER, HY_CH), 0.5),
        'rw_mu': jax.random.uniform(next(ks), (N_ODD, RW_WIDTH), jnp.float32),
        'rw_w0': jax.random.uniform(next(ks), (N_ODD, 2, RW_W), jnp.float32, -6.5, -1.5),
        'rw_w2': nrm((N_ODD, 2, RW_DECAY_LORA, RW_W), 0.1 * RW_DECAY_LORA ** -0.5),
        'rw_a0': nrm((N_ODD, RW_W), 0.1),
        'rw_a2': nrm((N_ODD, RW_AAA_LORA, RW_W), 0.1 * RW_AAA_LORA ** -0.5),
        'rw_g2': nrm((N_ODD, RW_GATE_LORA, RW_W), RW_GATE_LORA ** -0.5),
        'rw_k_k': 0.85 + nrm((N_ODD, RW_W), 0.05),
        'rw_k_a': 1.0 + nrm((N_ODD, RW_W), 0.05),
        'rw_r_k': nrm((N_ODD, RW_HEADS, RW_HEAD_DIM), 0.1),
        'rw_ln_g': gain((N_ODD, RW_HEADS, RW_HEAD_DIM)),
        'rw_ln_b': nrm((N_ODD, RW_HEADS, RW_HEAD_DIM), 0.02),
        'router_w': nrm((DEPTH, D, N_EXPERTS), D ** -0.5),
        'router_bias': nrm((DEPTH, N_EXPERTS), 0.01),
        'exp_w13': nrm((DEPTH, N_EXPERTS, D, 2 * EXPERT_FF), D ** -0.5),
        'exp_w2': nrm((DEPTH, N_EXPERTS, EXPERT_FF, D), EXPERT_FF ** -0.5 * DN_BETA),
        'sh_w13': nrm((DEPTH, D, 2 * SHARED_FF), D ** -0.5),
        'sh_w2': nrm((DEPTH, SHARED_FF, D), SHARED_FF ** -0.5 * DN_BETA),
    }


def reference(x, c, ctx, c_ctx, mod_w, mod_b, ln1_g, ln1_b, ln2_g, ln2_b,
              ev_w_in, ev_w_out, gla_gate_w2, gla_gate_b, gla_norm_g, hg_lb_logits, hg_norm_g,
              od_w_in, od_w_out, hy_conv_w, hy_conv_b, hy_ffn_w1, hy_ffn_b1, hy_ffn_w2, hy_ffn_b2,
              hy_ffn_w3, hy_sin_freq, hy_bias, rw_mu, rw_w0, rw_w2, rw_a0, rw_a2, rw_g2, rw_k_k,
              rw_k_a, rw_r_k, rw_ln_g, rw_ln_b, router_w, router_bias, exp_w13, exp_w2, sh_w13, sh_w2):
    B, L, D = x.shape
    Lc = ctx.shape[1]
    hg_lb = jnp.cumsum(jax.nn.softmax(hg_lb_logits.astype(F32), axis=0), axis=0)
    xl, xc = x, ctx
    for l in range(DEPTH):
        last = l == DEPTH - 1
        j = l // 2
        m = jax.nn.silu(c) @ mod_w[l] + mod_b[l]
        mc = jax.nn.silu(c_ctx) @ mod_w[l] + mod_b[l]
        sh1, sc1, g1, sh2, sc2, g2 = jnp.split(m[:, None, :], 6, -1)
        sh1c, sc1c, g1c, sh2c, sc2c, g2c = jnp.split(mc, 6, -1)
        h = xl * (1.0 + sc1) + sh1
        hc = xc * (1.0 + sc1c) + sh1c
        if l % 2 == 0:
            y, yc = gla_hgrn_mixer(h, hc, ev_w_in[j], ev_w_out[j], gla_gate_w2[j], gla_gate_b[j],
                                   gla_norm_g[j], hg_lb[j], hg_norm_g[j])
        else:
            y, yc = hyena_rwkv_mixer(h, hc, not last, od_w_in[j], od_w_out[j], hy_conv_w[j], hy_conv_b[j],
                                     hy_ffn_w1[j], hy_ffn_b1[j], hy_ffn_w2[j], hy_ffn_b2[j], hy_ffn_w3[j],
                                     hy_sin_freq[j], hy_bias[j], rw_mu[j], rw_w0[j], rw_w2[j], rw_a0[j],
                                     rw_a2[j], rw_g2[j], rw_k_k[j], rw_k_a[j], rw_r_k[j], rw_ln_g[j], rw_ln_b[j])
        xl = layer_norm(DN_ALPHA * xl + g1 * y, ln1_g[l], ln1_b[l])
        moe_p = (router_w[l], router_bias[l], exp_w13[l], exp_w2[l], sh_w13[l], sh_w2[l])
        u_l = xl * (1.0 + sc2) + sh2
        if last:
            mo_l = moe(u_l.reshape(B * L, D), *moe_p).reshape(B, L, D)
        else:
            xc = layer_norm(DN_ALPHA * xc + g1c * yc, ln1_g[l], ln1_b[l])
            u_c = xc * (1.0 + sc2c) + sh2c
            mo = moe(jnp.concatenate([u_c.reshape(B * Lc, D), u_l.reshape(B * L, D)], 0), *moe_p)
            mo_l = mo[B * Lc:].reshape(B, L, D)
            xc = layer_norm(DN_ALPHA * xc + g2c * mo[:B * Lc].reshape(B, Lc, D), ln2_g[l], ln2_b[l])
        xl = layer_norm(DN_ALPHA * xl + g2 * mo_l, ln2_g[l], ln2_b[l])
    return xl
```

```python
import functools
import math

import numpy as np
import jax
import jax.numpy as jnp
from jax import lax
from jax.experimental import pallas as pl
from jax.experimental.pallas import tpu as pltpu

F32 = jnp.float32
BF16 = jnp.bfloat16
I32 = jnp.int32
HI = lax.Precision.HIGHEST

LN_EPS = 1e-5
GLA_HEADS, GLA_DK, GLA_DV = 4, 64, 128
GLA_QK, GLA_V = GLA_HEADS * GLA_DK, GLA_HEADS * GLA_DV
GLA_GATE_RANK = 16
GLA_GATE_NORM = 16.0
CHUNK = 64
HG_HEADS, HG_EXPAND = 4, 128
HG_W = HG_HEADS * HG_EXPAND
HY_CH, HY_ORDER, HY_SHORT, HY_EMB = 512, 2, 3, 33
HY_WIDTH = (HY_ORDER + 1) * HY_CH
HY_FAST_DECAY, HY_SLOW_DECAY, HY_TARGET = 0.3, 1.5, 1e-2
RW_HEADS, RW_HEAD_DIM = 8, 64
RW_W = RW_HEADS * RW_HEAD_DIM
RW_DECAY_LORA, RW_AAA_LORA, RW_GATE_LORA = 64, 64, 128
RW_GN_EPS = 64e-5
GRID_W = 64
N_EXPERTS, TOP_K, N_GROUPS, TOPK_GROUPS = 256, 8, 8, 4
PER_GROUP = N_EXPERTS // N_GROUPS
EXPERT_FF, SHARED_FF = 256, 256
ROUTED_SCALE = 2.5

LANES = 128
SUBLANES = 8
VMEM_LIMIT = 56 * 1024 * 1024
ROW_TILE = 256
MOE_BLK = 256
CMB_TILE = 128
RW_CHUNK = 64
FFT_N2 = 128


def _cparams(sem):
    return pltpu.CompilerParams(dimension_semantics=sem, vmem_limit_bytes=VMEM_LIMIT)


def _bdot(a, b):
    return jnp.dot(a.astype(BF16), b.astype(BF16), preferred_element_type=F32)


def _bdot_nt(a, b):
    return lax.dot_general(a.astype(BF16), b.astype(BF16), (((1,), (1,)), ((), ())),
                           preferred_element_type=F32)


def _bdot_tn(a, b):
    return lax.dot_general(a.astype(BF16), b.astype(BF16), (((0,), (0,)), ((), ())),
                           preferred_element_type=F32)


def _hdot(a, b):
    return jnp.dot(a, b, precision=HI, preferred_element_type=F32)


def _silu(x):
    return x * jax.nn.sigmoid(x)


def _layer_norm_rows(x, g, b, eps):
    mu = jnp.mean(x, axis=-1, keepdims=True)
    xc = x - mu
    var = jnp.mean(xc * xc, axis=-1, keepdims=True)
    return xc * lax.rsqrt(var + eps) * g + b


def _mod_kernel(c_ref, w_ref, b_ref, o_ref):
    o_ref[...] = _hdot(_silu(c_ref[...]), w_ref[...]) + b_ref[...]


def _modulation(cc, w, b):
    R, D = cc.shape
    out = pl.pallas_call(
        _mod_kernel,
        grid=(6,),
        in_specs=[pl.BlockSpec((R, D), lambda j: (0, 0)),
                  pl.BlockSpec((D, D), lambda j: (0, j)),
                  pl.BlockSpec((1, D), lambda j: (0, j))],
        out_specs=pl.BlockSpec((R, D), lambda j: (0, j)),
        out_shape=jax.ShapeDtypeStruct((R, 6 * D), F32),
        compiler_params=_cparams(("parallel",)),
        name="modulation",
    )(cc, w, b.reshape(1, 6 * D))
    return out.reshape(R, 6, D)


def _in0_kernel(x_ref, mod_ref, w_ref, wa_ref, w2_ref, gb_ref, lb_ref,
                gq_ref, gk_ref, gv_ref, gla_ref, r_ref, hq_ref, hk_ref, hla_ref, hv_ref, hg_ref):
    sh1 = mod_ref[0:1, :]
    sc1 = mod_ref[1:2, :]
    h = (x_ref[...] * (1.0 + sc1) + sh1).astype(BF16)

    def proj(off, width):
        return jnp.dot(h, w_ref[:, off:off + width], preferred_element_type=F32)

    gq_ref[...] = proj(0, GLA_QK) * (GLA_DK ** -0.5)
    gk_ref[...] = proj(GLA_QK, GLA_QK)
    gv_ref[...] = proj(2 * GLA_QK, GLA_V)
    r_ref[...] = proj(2 * GLA_QK + GLA_V, GLA_V)
    base = 2 * GLA_QK + 2 * GLA_V
    a = jnp.dot(h, wa_ref[...], preferred_element_type=F32)
    z = _bdot(a, w2_ref[...]) + gb_ref[...]
    ls = (jnp.minimum(z, 0.0) - jnp.log(1.0 + jnp.exp(-jnp.abs(z)))) * (1.0 / GLA_GATE_NORM)
    gla_ref[0] = ls[:, :GLA_QK]
    gla_ref[1] = ls[:, GLA_QK:]
    hq_ref[...] = _silu(proj(base, HG_W))
    for d in range(2):
        zf = proj(base + (1 + d) * HG_W, HG_W)
        lb = lb_ref[d:d + 1, :]
        f = lb + (1.0 - lb) * jax.nn.sigmoid(zf)
        hk_ref[d] = 1.0 - f
        hla_ref[d] = jnp.log(f)
    hv_ref[...] = proj(base + 3 * HG_W, HG_W)
    hg_ref[...] = proj(base + 4 * HG_W, HG_W)


def _in_proj0(xcat, mods, nctx_tiles, w_in, gate_w2, gate_b, lb):
    B, S, D = xcat.shape
    J = S // ROW_TILE
    T = B * S
    a_off = 2 * GLA_QK + 2 * GLA_V
    wmain = jnp.concatenate([w_in[:, :a_off], w_in[:, a_off + 2 * GLA_GATE_RANK:]], axis=1).astype(BF16)
    wa = jnp.pad(w_in[:, a_off:a_off + 2 * GLA_GATE_RANK], ((0, 0), (0, LANES - 2 * GLA_GATE_RANK))).astype(BF16)
    w2 = jnp.zeros((LANES, 2 * GLA_QK), F32)
    w2 = w2.at[:GLA_GATE_RANK, :GLA_QK].set(gate_w2[0]).at[GLA_GATE_RANK:2 * GLA_GATE_RANK, GLA_QK:].set(gate_w2[1])
    gb = gate_b.reshape(1, 2 * GLA_QK)
    WM = wmain.shape[1]

    def rows(w):
        return pl.BlockSpec((ROW_TILE, w), lambda b, j: (b * J + j, 0))

    def rows2(w):
        return pl.BlockSpec((2, ROW_TILE, w), lambda b, j: (0, b * J + j, 0))

    def const(shape):
        return pl.BlockSpec(shape, lambda b, j: (0,) * len(shape))

    sd = jax.ShapeDtypeStruct
    outs = pl.pallas_call(
        _in0_kernel,
        grid=(B, J),
        in_specs=[rows(D),
                  pl.BlockSpec((None, 6, D), lambda b, j: (jnp.where(j < nctx_tiles, B, b), 0, 0)),
                  const((D, WM)), const((D, LANES)), const((LANES, 2 * GLA_QK)), const((1, 2 * GLA_QK)),
                  const((2, HG_W))],
        out_specs=[rows(GLA_QK), rows(GLA_QK), rows(GLA_V), rows2(GLA_QK), rows(GLA_V),
                   rows(HG_W), rows2(HG_W), rows2(HG_W), rows(HG_W), rows(HG_W)],
        out_shape=[sd((T, GLA_QK), F32), sd((T, GLA_QK), F32), sd((T, GLA_V), F32), sd((2, T, GLA_QK), F32),
                   sd((T, GLA_V), F32), sd((T, HG_W), F32), sd((2, T, HG_W), F32), sd((2, T, HG_W), F32),
                   sd((T, HG_W), F32), sd((T, HG_W), F32)],
        compiler_params=_cparams(("parallel", "parallel")),
        name="in_proj0",
    )(xcat.reshape(T, D), mods, wmain, wa, w2, gb, lb)
    return outs


def _rec_kernel(q_ref, k_ref, v_ref, la_ref, o_ref, st_ref, *, rev, nh, dk, dv, nchunk):
    t = pl.program_id(2)

    @pl.when(t == 0)
    def _():
        st_ref[...] = jnp.zeros_like(st_ref)

    C = CHUNK
    row = lax.broadcasted_iota(I32, (C, C), 0)
    col = lax.broadcasted_iota(I32, (C, C), 1)
    incl = (col >= row) if rev else (col <= row)
    tri = jnp.where(incl, 1.0, 0.0).astype(F32)
    ref_i = C // 2 - 1 if rev else C // 2
    last_i = 0 if rev else C - 1
    order = range(nchunk - 1, -1, -1) if rev else range(nchunk)
    for hh in range(nh):
        ks = slice(hh * dk, (hh + 1) * dk)
        vs = slice(hh * dv, (hh + 1) * dv)
        pre = []
        for c in range(nchunk):
            rs = slice(c * C, (c + 1) * C)
            la = la_ref[rs, ks]
            q = q_ref[rs, ks]
            k = k_ref[rs, ks]
            v = v_ref[rs, vs].astype(BF16)
            b = _hdot(tri, la)
            b_mid = b[ref_i:ref_i + 1, :]
            b_last = b[last_i:last_i + 1, :]
            sc = _bdot_nt(q * jnp.exp(b - b_mid), k * jnp.exp(b_mid - b))
            sc = jnp.where(incl, sc, 0.0)
            o_intra = _bdot(sc, v)
            q_in = q * jnp.exp(b)
            k_out = k * jnp.exp(b_last - b)
            dec = jnp.exp(b_last)
            kv_t = _bdot_tn(v, k_out)
            pre.append((o_intra, q_in, dec, kv_t))
        s_t = st_ref[hh]
        for c in order:
            o_intra, q_in, dec, kv_t = pre[c]
            o_ref[c * C:(c + 1) * C, vs] = o_intra + _bdot_nt(q_in, s_t)
            s_t = s_t * dec + kv_t
        st_ref[hh] = s_t


def _gated_recurrence(q, k, v, la, kdir, B, S, nctx_blocks, rev, nh, dk, dv):
    T = B * S
    NB = S // ROW_TILE
    wk, wv = nh * dk, nh * dv
    HB = q.shape[1] // wk
    d = 1 if rev else 0
    nc = nctx_blocks

    def blk(t):
        if not rev:
            return t
        return jnp.where(t < nc, nc - 1 - t, NB - 1 - (t - nc))

    kern = functools.partial(_rec_kernel, rev=rev, nh=nh, dk=dk, dv=dv, nchunk=ROW_TILE // CHUNK)
    return pl.pallas_call(
        kern,
        grid=(B, HB, NB),
        in_specs=[pl.BlockSpec((ROW_TILE, wk), lambda b, h, t: (b * NB + blk(t), h)),
                  pl.BlockSpec((None, ROW_TILE, wk), lambda b, h, t: (kdir, b * NB + blk(t), h)),
                  pl.BlockSpec((ROW_TILE, wv), lambda b, h, t: (b * NB + blk(t), h)),
                  pl.BlockSpec((None, ROW_TILE, wk), lambda b, h, t: (d, b * NB + blk(t), h))],
        out_specs=pl.BlockSpec((ROW_TILE, wv), lambda b, h, t: (b * NB + blk(t), h)),
        out_shape=jax.ShapeDtypeStruct((T, HB * wv), F32),
        scratch_shapes=[pltpu.VMEM((nh, dv, dk), F32)],
        compiler_params=_cparams(("parallel", "parallel", "arbitrary")),
        name="gated_rec_bwd" if rev else "gated_rec_fwd",
    )(q, k, v, la)


def _post_mix(y, x, mod_ref, wo_ref, lng_ref, lnb_ref, alpha, xl_ref, u_ref):
    g1 = mod_ref[2:3, :]
    sh2 = mod_ref[3:4, :]
    sc2 = mod_ref[4:5, :]
    yo = jnp.dot(y.astype(BF16), wo_ref[...], preferred_element_type=F32)
    xl = _layer_norm_rows(alpha * x + g1 * yo, lng_ref[...], lnb_ref[...], LN_EPS)
    xl_ref[...] = xl
    u_ref[...] = xl * (1.0 + sc2) + sh2


def _out0_kernel(gf_ref, gb_ref, hf_ref, hb_ref, r_ref, hg_ref, x_ref, mod_ref, wo_ref,
                 gg_ref, hgg_ref, lng_ref, lnb_ref, xl_ref, u_ref, *, alpha):
    def heads(o, g, gate):
        parts = []
        for hh in range(o.shape[1] // LANES):
            seg = o[:, hh * LANES:(hh + 1) * LANES]
            ms = jnp.mean(seg * seg, axis=-1, keepdims=True)
            parts.append(seg * lax.rsqrt(ms + 1e-6) * g)
        return jnp.concatenate(parts, axis=1) * _silu(gate)

    y = jnp.concatenate([heads(gf_ref[...] + gb_ref[...], gg_ref[...], r_ref[...]),
                         heads(hf_ref[...] + hb_ref[...], hgg_ref[...], hg_ref[...])], axis=1)
    _post_mix(y, x_ref[...], mod_ref, wo_ref, lng_ref, lnb_ref, alpha, xl_ref, u_ref)


def _readout0(o_gf, o_gb, o_hf, o_hb, r, hgate, xcat, mods, nctx_tiles, w_out, gla_g, hg_g, ln_g, ln_b, alpha):
    B, S, D = xcat.shape
    J = S // ROW_TILE
    T = B * S

    def rows(w):
        return pl.BlockSpec((ROW_TILE, w), lambda b, j: (b * J + j, 0))

    def const(shape):
        return pl.BlockSpec(shape, lambda b, j: (0,) * len(shape))

    sd = jax.ShapeDtypeStruct
    return pl.pallas_call(
        functools.partial(_out0_kernel, alpha=alpha),
        grid=(B, J),
        in_specs=[rows(GLA_V), rows(GLA_V), rows(HG_W), rows(HG_W), rows(GLA_V), rows(HG_W), rows(D),
                  pl.BlockSpec((None, 6, D), lambda b, j: (jnp.where(j < nctx_tiles, B, b), 0, 0)),
                  const((GLA_V + HG_W, D)), const((1, GLA_DV)), const((1, HG_EXPAND)), const((1, D)), const((1, D))],
        out_specs=[rows(D), rows(D)],
        out_shape=[sd((T, D), F32), sd((T, D), F32)],
        compiler_params=_cparams(("parallel", "parallel")),
        name="readout0",
    )(o_gf, o_gb, o_hf, o_hb, r, hgate, xcat.reshape(T, D), mods, w_out.astype(BF16),
      gla_g.reshape(1, -1), hg_g.reshape(1, -1), ln_g.reshape(1, D), ln_b.reshape(1, D))


def _route_kernel(u_ref, rwt_ref, rb_ref, tri_ref, eidx_ref, gate_ref, rank_ref, cnt_ref, carry_ref):
    i = pl.program_id(0)

    @pl.when(i == 0)
    def _():
        carry_ref[...] = jnp.zeros_like(carry_ref)

    u = u_ref[...]
    TM = u.shape[0]
    E = N_EXPERTS
    logits = lax.dot_general(rwt_ref[...], u, (((1,), (1,)), ((), ())), precision=HI,
                             preferred_element_type=F32)
    scores = jax.nn.sigmoid(logits)
    sel = scores + rb_ref[...]
    neg = -jnp.inf
    sel3 = sel.reshape(N_GROUPS, PER_GROUP, TM)
    io_g = lax.broadcasted_iota(I32, sel3.shape, 1)
    m1 = jnp.max(sel3, axis=1, keepdims=True)
    i1 = jnp.min(jnp.where(sel3 == m1, io_g, PER_GROUP), axis=1, keepdims=True)
    m2 = jnp.max(jnp.where(io_g == i1, neg, sel3), axis=1, keepdims=True)
    grp = m1 + m2
    io_n = lax.broadcasted_iota(I32, grp.shape, 0)
    keep = jnp.zeros(grp.shape, jnp.bool_)
    for _ in range(TOPK_GROUPS):
        m = jnp.max(grp, axis=0, keepdims=True)
        idx = jnp.min(jnp.where(grp == m, io_n, N_GROUPS), axis=0, keepdims=True)
        hit = io_n == idx
        keep = jnp.logical_or(keep, hit)
        grp = jnp.where(hit, neg, grp)
    sel = jnp.where(keep, sel3, neg).reshape(E, TM)
    io_e = lax.broadcasted_iota(I32, (E, TM), 0)
    base = carry_ref[...]
    tri = tri_ref[...]
    e_rows, g_rows, r_rows = [], [], []
    for _ in range(TOP_K):
        m = jnp.max(sel, axis=0, keepdims=True)
        idx = jnp.min(jnp.where(sel == m, io_e, E), axis=0, keepdims=True)
        hit = io_e == idx
        hit_f = jnp.where(hit, 1.0, 0.0)
        g_rows.append(jnp.sum(jnp.where(hit, scores, 0.0), axis=0, keepdims=True))
        prefix = jnp.dot(hit_f.astype(BF16), tri, preferred_element_type=F32)
        r_rows.append(jnp.sum(jnp.where(hit, prefix + base, 0.0), axis=0, keepdims=True))
        base = base + jnp.sum(hit_f, axis=1, keepdims=True)
        e_rows.append(idx)
        sel = jnp.where(hit, neg, sel)
    carry_ref[...] = base
    cnt_ref[...] = base
    g = jnp.concatenate(g_rows, axis=0)
    gate_ref[...] = g / jnp.sum(g, axis=0, keepdims=True) * ROUTED_SCALE
    eidx_ref[...] = jnp.concatenate(e_rows, axis=0)
    rank_ref[...] = jnp.concatenate(r_rows, axis=0).astype(I32)


def _route(u, router_w, router_bias):
    T, D = u.shape
    E = N_EXPERTS
    n = T // ROW_TILE
    tri = jnp.asarray(np.triu(np.ones((ROW_TILE, ROW_TILE), np.float32), 1), BF16)
    sd = jax.ShapeDtypeStruct
    cols = pl.BlockSpec((TOP_K, ROW_TILE), lambda i: (0, i))
    return pl.pallas_call(
        _route_kernel,
        grid=(n,),
        in_specs=[pl.BlockSpec((ROW_TILE, D), lambda i: (i, 0)),
                  pl.BlockSpec((E, D), lambda i: (0, 0)),
                  pl.BlockSpec((E, 1), lambda i: (0, 0)),
                  pl.BlockSpec((ROW_TILE, ROW_TILE), lambda i: (0, 0))],
        out_specs=[cols, cols, cols, pl.BlockSpec((E, 1), lambda i: (0, 0))],
        out_shape=[sd((TOP_K, T), I32), sd((TOP_K, T), F32), sd((TOP_K, T), I32), sd((E, 1), F32)],
        scratch_shapes=[pltpu.VMEM((E, 1), F32)],
        compiler_params=_cparams(("arbitrary",)),
        name="moe_route",
    )(u, router_w.T, router_bias.reshape(E, 1), tri)


def _dest_kernel(e_ref, r_ref, ps_ref, d_ref):
    e = e_ref[...]
    TM = e.shape[1]
    io_e = lax.broadcasted_iota(I32, (N_EXPERTS, TM), 0)
    ps = ps_ref[...]
    rows = []
    for k in range(TOP_K):
        rows.append(jnp.sum(jnp.where(io_e == e[k:k + 1, :], ps, 0.0), axis=0, keepdims=True))
    d_ref[...] = jnp.concatenate(rows, axis=0).astype(I32) + r_ref[...]


def _slot_of(eidx, rank, pad_start):
    T = eidx.shape[1]
    cols = pl.BlockSpec((TOP_K, ROW_TILE), lambda i: (0, i))
    return pl.pallas_call(
        _dest_kernel,
        grid=(T // ROW_TILE,),
        in_specs=[cols, cols, pl.BlockSpec((N_EXPERTS, 1), lambda i: (0, 0))],
        out_specs=cols,
        out_shape=jax.ShapeDtypeStruct((TOP_K, T), I32),
        compiler_params=_cparams(("parallel",)),
        name="moe_slot",
    )(eidx, rank, pad_start.astype(F32).reshape(N_EXPERTS, 1))


def _dispatch_kernel(dest_ref, u_ref, xs_in_ref, xs_ref, sem):
    del xs_in_ref
    TM = u_ref.shape[0]

    def copy(t, k):
        return pltpu.make_async_copy(u_ref.at[pl.ds(t, 1)], xs_ref.at[pl.ds(dest_ref[k, t], 1)], sem)

    def issue(t, c):
        for k in range(TOP_K):
            copy(t, k).start()
        return c

    lax.fori_loop(0, TM, issue, 0)

    def drain(t, c):
        for k in range(TOP_K):
            copy(t, k).wait()
        return c

    lax.fori_loop(0, TM, drain, 0)


def _dispatch(u, dest, n_slots):
    T, D = u.shape
    xs0 = jnp.zeros((n_slots, D), F32)
    return pl.pallas_call(
        _dispatch_kernel,
        grid=(T // ROW_TILE,),
        in_specs=[pl.BlockSpec((TOP_K, ROW_TILE), lambda i: (0, i), memory_space=pltpu.SMEM),
                  pl.BlockSpec((ROW_TILE, D), lambda i: (i, 0)),
                  pl.BlockSpec(memory_space=pl.ANY)],
        out_specs=pl.BlockSpec(memory_space=pl.ANY),
        out_shape=jax.ShapeDtypeStruct((n_slots, D), F32),
        scratch_shapes=[pltpu.SemaphoreType.DMA],
        input_output_aliases={2: 0},
        compiler_params=_cparams(("arbitrary",)),
        name="moe_dispatch",
    )(dest, u, xs0)


def _expert_kernel(be_ref, nu_ref, xs_ref, w13_ref, w2_ref, ys_ref):
    del be_ref

    @pl.when(pl.program_id(0) < nu_ref[0])
    def _():
        h = _bdot(xs_ref[...], w13_ref[...])
        a = _silu(h[:, :EXPERT_FF]) * h[:, EXPERT_FF:]
        ys_ref[...] = _bdot(a, w2_ref[...])


def _expert_gemm(xs, blk_exp, n_used, w13, w2):
    NP, D = xs.shape
    nblk = NP // MOE_BLK
    F2 = w13.shape[2]

    def xmap(i, be, nu):
        return (jnp.minimum(i, nu[0] - 1), 0)

    return pl.pallas_call(
        _expert_kernel,
        grid_spec=pltpu.PrefetchScalarGridSpec(
            num_scalar_prefetch=2,
            grid=(nblk,),
            in_specs=[pl.BlockSpec((MOE_BLK, D), xmap),
                      pl.BlockSpec((None, D, F2), lambda i, be, nu: (be[i], 0, 0)),
                      pl.BlockSpec((None, F2 // 2, D), lambda i, be, nu: (be[i], 0, 0))],
            out_specs=pl.BlockSpec((MOE_BLK, D), xmap)),
        out_shape=jax.ShapeDtypeStruct((NP, D), F32),
        compiler_params=_cparams(("arbitrary",)),
        name="moe_experts",
    )(blk_exp, n_used, xs, w13, w2)


def _combine_kernel(mt_ref, dest_ref, gate_ref, u_ref, xl_ref, mod_ref, s13_ref, s2_ref, lng_ref, lnb_ref, ys_ref,
                    o_ref, rows_ref, sem, *, alpha):
    del mt_ref
    TM = u_ref.shape[0]

    def copy(t, k):
        return pltpu.make_async_copy(ys_ref.at[pl.ds(dest_ref[k, t], 1)], rows_ref.at[k, pl.ds(t, 1)], sem)

    def issue(t, c):
        for k in range(TOP_K):
            copy(t, k).start()
        return c

    lax.fori_loop(0, TM, issue, 0)
    hs = _bdot(u_ref[...], s13_ref[...])
    acc = _bdot(_silu(hs[:, :SHARED_FF]) * hs[:, SHARED_FF:], s2_ref[...])

    def drain(t, c):
        for k in range(TOP_K):
            copy(t, k).wait()
        return c

    lax.fori_loop(0, TM, drain, 0)
    gate = gate_ref[...]
    for k in range(TOP_K):
        acc = acc + rows_ref[k] * gate[:, k:k + 1]
    g2 = mod_ref[5:6, :]
    o_ref[...] = _layer_norm_rows(alpha * xl_ref[...] + g2 * acc, lng_ref[...], lnb_ref[...], LN_EPS)


def _combine(dest, gate_t, u, xl, ys, mods, mod_of_tile, sh_w13, sh_w2, ln_g, ln_b, alpha):
    T, D = u.shape
    n = T // CMB_TILE

    def rows(w):
        return pl.BlockSpec((CMB_TILE, w), lambda i, mt: (i, 0))

    def const(shape):
        return pl.BlockSpec(shape, lambda i, mt: (0,) * len(shape))

    return pl.pallas_call(
        functools.partial(_combine_kernel, alpha=alpha),
        grid_spec=pltpu.PrefetchScalarGridSpec(
            num_scalar_prefetch=1,
            grid=(n,),
            in_specs=[pl.BlockSpec((TOP_K, CMB_TILE), lambda i, mt: (0, i), memory_space=pltpu.SMEM),
                      rows(TOP_K), rows(D), rows(D),
                      pl.BlockSpec((None, 6, D), lambda i, mt: (mt[i], 0, 0)),
                      const((D, 2 * SHARED_FF)), const((SHARED_FF, D)), const((1, D)), const((1, D)),
                      pl.BlockSpec(memory_space=pl.ANY)],
            out_specs=rows(D),
            scratch_shapes=[pltpu.VMEM((TOP_K, CMB_TILE, D), F32), pltpu.SemaphoreType.DMA]),
        out_shape=jax.ShapeDtypeStruct((T, D), F32),
        compiler_params=_cparams(("arbitrary",)),
        name="moe_combine",
    )(mod_of_tile, dest, gate_t, u, xl, mods, sh_w13.astype(BF16), sh_w2.astype(BF16),
      ln_g.reshape(1, D), ln_b.reshape(1, D), ys)


def _moe_block(u, xl, mods, mod_of_tile, router_w, router_bias, w13, w2, sh_w13, sh_w2, ln_g, ln_b, alpha):
    T, D = u.shape
    eidx, gate, rank, counts = _route(u, router_w, router_bias)
    cnt = counts.reshape(N_EXPERTS).astype(I32)
    padded = (cnt + MOE_BLK - 1) // MOE_BLK * MOE_BLK
    pad_end = jnp.cumsum(padded)
    pad_start = pad_end - padded
    nblk = T * TOP_K // MOE_BLK + N_EXPERTS
    n_used = (pad_end[-1] // MOE_BLK).astype(I32)
    blk_exp = jnp.searchsorted(pad_end, jnp.arange(nblk, dtype=I32) * MOE_BLK, side='right').astype(I32)
    last_e = jnp.max(jnp.where(cnt > 0, jnp.arange(N_EXPERTS, dtype=I32), 0))
    blk_exp = jnp.minimum(blk_exp, last_e)
    dest = _slot_of(eidx, rank, pad_start)
    xs = _dispatch(u, dest, nblk * MOE_BLK)
    ys = _expert_gemm(xs, blk_exp, n_used.reshape(1), w13, w2)
    return _combine(dest, gate.T, u, xl, ys, mods, mod_of_tile, sh_w13, sh_w2, ln_g, ln_b, alpha)


def _in1_kernel(x_ref, mod_ref, wh_ref, wr_ref, ph_ref, pr_ref):
    sh1 = mod_ref[0:1, :]
    sc1 = mod_ref[1:2, :]
    h = (x_ref[...] * (1.0 + sc1) + sh1).astype(BF16)
    ph_ref[...] = jnp.dot(h, wh_ref[...], preferred_element_type=F32)
    pr_ref[...] = jnp.dot(h, wr_ref[...], preferred_element_type=F32)


RW_PW = 1920


def _rw_reorder(t):
    a = 3 * RW_W
    lo = 2 * RW_DECAY_LORA + RW_AAA_LORA
    out = jnp.concatenate([t[..., :a], t[..., a + lo:], t[..., a:a + lo]], axis=-1)
    pad = [(0, 0)] * (t.ndim - 1) + [(0, RW_PW - out.shape[-1])]
    return jnp.pad(out, pad)


def _in_proj1(xcat, mods, nctx_tiles, w_in):
    B, S, D = xcat.shape
    J = S // ROW_TILE
    T = B * S
    wh = w_in[:, :HY_WIDTH].astype(BF16)
    wr = _rw_reorder(w_in[:, HY_WIDTH:]).astype(BF16)

    def rows(w):
        return pl.BlockSpec((ROW_TILE, w), lambda b, j: (b * J + j, 0))

    def const(shape):
        return pl.BlockSpec(shape, lambda b, j: (0,) * len(shape))

    sd = jax.ShapeDtypeStruct
    return pl.pallas_call(
        _in1_kernel,
        grid=(B, J),
        in_specs=[rows(D), pl.BlockSpec((None, 6, D), lambda b, j: (jnp.where(j < nctx_tiles, B, b), 0, 0)),
                  const((D, HY_WIDTH)), const((D, RW_PW))],
        out_specs=[rows(HY_WIDTH), rows(RW_PW)],
        out_shape=[sd((T, HY_WIDTH), F32), sd((T, RW_PW), F32)],
        compiler_params=_cparams(("parallel", "parallel")),
        name="in_proj1",
    )(xcat.reshape(T, D), mods, wh, wr)


def _rw_streams_kernel(prev_ref, main_ref, next_ref, mu_ref, w0_ref, w2_ref, a0_ref, a2_ref, g2_ref,
                       kk_ref, ka_ref, bd_ref,
                       r_ref, k_ref, v_ref, kkn_ref, bb_ref, ld_ref, g_ref, sh_scr, *, nctx_tiles, n_lat_tiles):
    j = pl.program_id(1)
    TM = main_ref.shape[0]
    W = main_ref.shape[1]
    H = GRID_W
    p = main_ref[...]
    ext = jnp.concatenate([prev_ref[...], p, next_ref[...]], axis=0)
    left = ext[H - 1:H - 1 + TM]
    right = ext[H + 1:H + 1 + TM]
    up = ext[0:TM]
    down = ext[2 * H:2 * H + TM]
    i = lax.broadcasted_iota(I32, (TM, W), 0)
    lane = lax.broadcasted_iota(I32, (TM, W), 1)
    even = (lane & 1) == 0
    c4 = lane & 3
    jl = j - nctx_tiles

    @pl.when(j < nctx_tiles)
    def _():
        lo = jnp.where(j == 0, 1, 0)
        hi = jnp.where(j == nctx_tiles - 1, TM - 1, TM)
        sh_scr[...] = jnp.where(even, jnp.where(i >= lo, left, 0.0), jnp.where(i < hi, right, 0.0))

    @pl.when(j >= nctx_tiles)
    def _():
        col = i & (H - 1)
        up_lo = jnp.where(jl == 0, H, 0)
        down_hi = jnp.where(jl == n_lat_tiles - 1, TM - H, TM)
        l_v = jnp.where(col != 0, left, 0.0)
        r_v = jnp.where(col != H - 1, right, 0.0)
        u_v = jnp.where(i >= up_lo, up, 0.0)
        d_v = jnp.where(i < down_hi, down, 0.0)
        sh_scr[...] = jnp.where(c4 == 0, l_v, jnp.where(c4 == 1, r_v, jnp.where(c4 == 2, u_v, d_v)))

    pm = p + mu_ref[...] * (sh_scr[...] - p)
    r = pm[:, 0:RW_W]
    k = pm[:, RW_W:2 * RW_W]
    v = pm[:, 2 * RW_W:3 * RW_W]
    o = 3 * RW_W
    gl = pm[:, o:o + RW_GATE_LORA]
    o += RW_GATE_LORA
    wl_f = pm[:, o:o + RW_DECAY_LORA]
    wl_b = pm[:, o + RW_DECAY_LORA:o + 2 * RW_DECAY_LORA]
    al = pm[:, o + 2 * RW_DECAY_LORA:o + 2 * RW_DECAY_LORA + RW_AAA_LORA]
    for d, wl in enumerate((wl_f, wl_b)):
        z = w0_ref[d:d + 1, :] + _bdot(jnp.tanh(wl), w2_ref[d])
        w = -(jnp.maximum(-z, 0.0) + jnp.log(1.0 + jnp.exp(-jnp.abs(z)))) - 0.5
        ld_ref[d] = -jnp.exp(w)
    a = jax.nn.sigmoid(a0_ref[...] + _bdot(al, a2_ref[...]))
    g_ref[...] = _bdot(jax.nn.sigmoid(gl), g2_ref[...])
    kk = k * kk_ref[...]
    n2 = _hdot(kk * kk, bd_ref[...])
    kkn = kk / jnp.maximum(jnp.sqrt(n2), 1e-12)
    r_ref[...] = r
    k_ref[...] = k * (1.0 + (a - 1.0) * ka_ref[...])
    v_ref[...] = v
    kkn_ref[...] = kkn
    bb_ref[...] = kkn * a


def _head_sum_matrix(width, hd):
    i = np.arange(width)
    return jnp.asarray((i[:, None] // hd == i[None, :] // hd).astype(np.float32))


def _rw_streams(p_rw, B, S, nctx_tiles, mu, w0, w2, a0, a2, g2, k_k, k_a):
    T = B * S
    J = S // ROW_TILE
    HB = ROW_TILE // GRID_W
    NH = S // GRID_W

    def rows(w):
        return pl.BlockSpec((ROW_TILE, w), lambda b, j: (b * J + j, 0))

    def rows2(w):
        return pl.BlockSpec((2, ROW_TILE, w), lambda b, j: (0, b * J + j, 0))

    def const(shape):
        return pl.BlockSpec(shape, lambda b, j: (0,) * len(shape))

    sd = jax.ShapeDtypeStruct
    kern = functools.partial(_rw_streams_kernel, nctx_tiles=nctx_tiles, n_lat_tiles=J - nctx_tiles)
    return pl.pallas_call(
        kern,
        grid=(B, J),
        in_specs=[pl.BlockSpec((GRID_W, RW_PW), lambda b, j: (b * NH + jnp.maximum(j * HB - 1, 0), 0)),
                  rows(RW_PW),
                  pl.BlockSpec((GRID_W, RW_PW), lambda b, j: (b * NH + jnp.minimum((j + 1) * HB, NH - 1), 0)),
                  const((1, RW_PW)), const((2, RW_W)), const((2, RW_DECAY_LORA, RW_W)), const((1, RW_W)),
                  const((RW_AAA_LORA, RW_W)), const((RW_GATE_LORA, RW_W)), const((1, RW_W)), const((1, RW_W)),
                  const((RW_W, RW_W))],
        out_specs=[rows(RW_W), rows(RW_W), rows(RW_W), rows(RW_W), rows(RW_W), rows2(RW_W), rows(RW_W)],
        out_shape=[sd((T, RW_W), F32)] * 5 + [sd((2, T, RW_W), F32), sd((T, RW_W), F32)],
        scratch_shapes=[pltpu.VMEM((ROW_TILE, RW_PW), F32)],
        compiler_params=_cparams(("parallel", "parallel")),
        name="rwkv_streams",
    )(p_rw, p_rw, p_rw, _rw_reorder(mu).reshape(1, RW_PW), w0, w2, a0.reshape(1, RW_W), a2, g2,
      k_k.reshape(1, RW_W), k_a.reshape(1, RW_W), _head_sum_matrix(RW_W, RW_HEAD_DIM))


def _rwkv_kernel(r_ref, k_ref, v_ref, kk_ref, bb_ref, ld_ref, o_ref, st_ref, *, rev, nh, nchunk):
    t = pl.program_id(2)

    @pl.when(t == 0)
    def _():
        st_ref[...] = jnp.zeros_like(st_ref)

    C = RW_CHUNK
    hd = RW_HEAD_DIM
    row = lax.broadcasted_iota(I32, (C, C), 0)
    col = lax.broadcasted_iota(I32, (C, C), 1)
    incl = (col >= row) if rev else (col <= row)
    strict = (col > row) if rev else (col < row)
    tri = jnp.where(incl, 1.0, 0.0).astype(F32)
    eye_c = jnp.where(row == col, 1.0, 0.0).astype(F32)
    rk = lax.broadcasted_iota(I32, (hd, hd), 0)
    ck = lax.broadcasted_iota(I32, (hd, hd), 1)
    eye_k = rk == ck
    last_i = 0 if rev else C - 1
    order = range(nchunk - 1, -1, -1) if rev else range(nchunk)
    n_double = int(math.log2(C)) - 1
    for hh in range(nh):
        hs = slice(hh * hd, (hh + 1) * hd)
        pre = []
        for c in range(nchunk):
            rs = slice(c * C, (c + 1) * C)
            ld = ld_ref[rs, hs]
            r = r_ref[rs, hs]
            k = k_ref[rs, hs]
            v = v_ref[rs, hs]
            kk = kk_ref[rs, hs]
            bb = bb_ref[rs, hs]
            g = _hdot(tri, ld)
            g_last = g[last_i:last_i + 1, :]
            eng = jnp.exp(-g)
            kk_t = kk * jnp.exp(g - ld)
            b_t = bb * eng
            k_t = k * eng
            r_t = r * jnp.exp(g)
            e_end = jnp.exp(g_last - g)
            b_bar = bb * e_end
            k_bar = k * e_end
            m_b = jnp.where(strict, _bdot_nt(kk_t, b_t), 0.0)
            m_k = jnp.where(strict, _bdot_nt(kk_t, k_t), 0.0)
            a_rb = jnp.where(incl, _bdot_nt(r_t, b_t), 0.0)
            a_rk = jnp.where(incl, _bdot_nt(r_t, k_t), 0.0)
            pw = -m_b
            tinv = eye_c + pw
            for _ in range(n_double):
                pw = _bdot(pw, pw)
                tinv = tinv + _bdot(tinv, pw)
            a_t = _bdot(tinv, kk_t)
            u_b = _bdot(tinv, _bdot(m_k, v))
            q_h = r_t - _bdot(a_rb, a_t)
            o_h = _bdot(a_rk, v) - _bdot(a_rb, u_b)
            g_m = jnp.where(eye_k, jnp.exp(g_last), 0.0) - _bdot_tn(a_t, b_bar)
            h_m = _bdot_tn(v, k_bar) - _bdot_tn(u_b, b_bar)
            pre.append((q_h, o_h, g_m, h_m))
        s = st_ref[hh]
        for c in order:
            q_h, o_h, g_m, h_m = pre[c]
            o_ref[c * C:(c + 1) * C, hs] = _bdot_nt(q_h, s) + o_h
            s = _bdot(s, g_m) + h_m
        st_ref[hh] = s


def _rwkv_scan(r, k, v, kk, bb, ld, B, S, nctx_blocks, rev):
    T = B * S
    NB = S // ROW_TILE
    nh = LANES // RW_HEAD_DIM
    HB = RW_W // LANES
    d = 1 if rev else 0
    nc = nctx_blocks

    def blk(t):
        if not rev:
            return t
        return jnp.where(t < nc, nc - 1 - t, NB - 1 - (t - nc))

    spec = pl.BlockSpec((ROW_TILE, LANES), lambda b, h, t: (b * NB + blk(t), h))
    kern = functools.partial(_rwkv_kernel, rev=rev, nh=nh, nchunk=ROW_TILE // RW_CHUNK)
    return pl.pallas_call(
        kern,
        grid=(B, HB, NB),
        in_specs=[spec, spec, spec, spec, spec,
                  pl.BlockSpec((None, ROW_TILE, LANES), lambda b, h, t: (d, b * NB + blk(t), h))],
        out_specs=spec,
        out_shape=jax.ShapeDtypeStruct((T, RW_W), F32),
        scratch_shapes=[pltpu.VMEM((nh, RW_HEAD_DIM, RW_HEAD_DIM), F32)],
        compiler_params=_cparams(("parallel", "parallel", "arbitrary")),
        name="rwkv_scan_bwd" if rev else "rwkv_scan_fwd",
    )(r, k, v, kk, bb, ld)


def _filter_mlp_kernel(z_ref, w1_ref, b1_ref, w2_ref, b2_ref, w3_ref, sf_ref, win_ref, f_ref):
    h = jnp.sin(sf_ref[0:1, :] * (_hdot(z_ref[...], w1_ref[...]) + b1_ref[...]))
    h = jnp.sin(sf_ref[1:2, :] * (_hdot(h, w2_ref[...]) + b2_ref[...]))
    f_ref[...] = _hdot(h, w3_ref[...]) * win_ref[...]


def _hyena_filters(L, w1, b1, w2, b2, w3, sin_freq):
    t = np.linspace(0.0, 1.0, L, dtype=np.float32)[:, None]
    bands = (HY_EMB - 1) // 2
    wpos = (2.0 * math.pi * np.arange(L, dtype=np.float32)[:, None] / L).astype(np.float32)
    fr = np.linspace(1e-4, bands - 1, bands, dtype=np.float32)[None, :]
    z = np.concatenate([t, np.cos(fr * wpos), -np.sin(fr * wpos)], -1).astype(np.float32)
    zp = jnp.asarray(np.pad(z, ((0, 0), (0, LANES - HY_EMB))))
    w1p = jnp.pad(w1, ((0, LANES - HY_EMB), (0, 0)))
    max_decay = math.log(HY_TARGET) / HY_FAST_DECAY
    min_decay = math.log(HY_TARGET) / HY_SLOW_DECAY
    deltas = np.linspace(min_decay, max_decay, HY_CH, dtype=np.float32)
    window = np.exp(-t * np.abs(deltas)).astype(np.float32)
    FW = HY_ORDER * 2 * HY_CH
    win = jnp.asarray(np.tile(window, (1, HY_ORDER * 2)))
    TR = min(L, 512)
    Hd = w1.shape[1]
    return pl.pallas_call(
        _filter_mlp_kernel,
        grid=(L // TR,),
        in_specs=[pl.BlockSpec((TR, LANES), lambda i: (i, 0)),
                  pl.BlockSpec((LANES, Hd), lambda i: (0, 0)), pl.BlockSpec((1, Hd), lambda i: (0, 0)),
                  pl.BlockSpec((Hd, Hd), lambda i: (0, 0)), pl.BlockSpec((1, Hd), lambda i: (0, 0)),
                  pl.BlockSpec((Hd, FW), lambda i: (0, 0)), pl.BlockSpec((2, Hd), lambda i: (0, 0)),
                  pl.BlockSpec((TR, FW), lambda i: (i, 0))],
        out_specs=pl.BlockSpec((TR, FW), lambda i: (i, 0)),
        out_shape=jax.ShapeDtypeStruct((L, FW), F32),
        compiler_params=_cparams(("parallel",)),
        name="hyena_filter_mlp",
    )(zp, w1p, b1.reshape(1, Hd), w2, b2.reshape(1, Hd), w3, sin_freq, win)


def _sconv_kernel(p_ref, w_ref, b_ref, o_ref):
    p = p_ref[...]
    L = p.shape[0]
    i = lax.broadcasted_iota(I32, p.shape, 0)
    prev = jnp.where(i == 0, 0.0, pltpu.roll(p, 1, 0))
    nxt = jnp.where(i == L - 1, 0.0, pltpu.roll(p, L - 1, 0))
    o_ref[...] = prev * w_ref[0:1, :] + p * w_ref[1:2, :] + nxt * w_ref[2:3, :] + b_ref[...]


def _short_conv(p_hy, B, S, Lc, conv_w, conv_b):
    L = S - Lc
    p3 = p_hy.reshape(B, S, HY_WIDTH)[:, Lc:, :]
    return pl.pallas_call(
        _sconv_kernel,
        grid=(B, HY_WIDTH // LANES),
        in_specs=[pl.BlockSpec((None, L, LANES), lambda b, c: (b, 0, c)),
                  pl.BlockSpec((HY_SHORT, LANES), lambda b, c: (0, c)),
                  pl.BlockSpec((1, LANES), lambda b, c: (0, c))],
        out_specs=pl.BlockSpec((None, L, LANES), lambda b, c: (b, 0, c)),
        out_shape=jax.ShapeDtypeStruct((B, L, HY_WIDTH), F32),
        compiler_params=_cparams(("parallel", "parallel")),
        name="hyena_short_conv",
    )(p3, conv_w, conv_b.reshape(1, HY_WIDTH))


def _dft_constants(L):
    n = 2 * L
    n2 = FFT_N2
    n1 = n // n2
    na = n1 // 2
    g8 = SUBLANES
    w1 = np.exp(-2j * np.pi * np.outer(np.arange(n1), np.arange(n1)) / n1)
    eye8 = np.eye(g8)

    def kron_fwd(w, real_in):
        k1n, an = w.shape
        blocks = np.stack([np.stack([w.real, -w.imag], 1), np.stack([w.imag, w.real], 1)], 1)
        if real_in:
            blocks = blocks[:, :, 0:1, :]
        m = np.einsum('kria,bc->krbiac', blocks, eye8)
        return m.reshape(k1n * 2 * g8, blocks.shape[2] * an * g8)

    m1 = kron_fwd(w1[:, :na], False)
    m1f = kron_fwd(w1, True)
    cw = np.conj(w1[:, :na]).T / n
    blocks = np.stack([np.stack([cw.real, -cw.imag], 1), np.stack([cw.imag, cw.real], 1)], 0)
    m1inv = np.einsum('iark,bc->iabkrc', blocks, eye8).reshape(2 * na * g8, n1 * 2 * g8)
    w2 = np.exp(-2j * np.pi * np.outer(np.arange(n2), np.arange(n2)) / n2)
    w2big = np.block([[w2.real, -w2.imag], [w2.imag, w2.real]])
    w2c = np.conj(w2)
    iw2big = np.block([[w2c.real, -w2c.imag], [w2c.imag, w2c.real]])
    tw = np.exp(-2j * np.pi * np.arange(n2) / n)
    tw1 = np.stack([np.broadcast_to(tw.real[:, None], (n2, LANES)),
                    np.broadcast_to(tw.imag[:, None], (n2, LANES))], 0)
    c = lambda x, dt: jnp.asarray(np.ascontiguousarray(x), dt)
    return dict(m1=c(m1, BF16), m1f=c(m1f, BF16), m1inv=c(m1inv, BF16), w2=c(w2big, BF16), iw2=c(iw2big, BF16),
                tw1=c(tw1, F32), n1=n1, na=na)


def _cplx_rows(z, twr, twi, conj):
    n2 = z.shape[0] // 2
    zr, zi = z[:n2], z[n2:]
    if conj:
        return jnp.concatenate([zr * twr + zi * twi, zi * twr - zr * twi], axis=0)
    return jnp.concatenate([zr * twr - zi * twi, zi * twr + zr * twi], axis=0)


def _split_dot(m, x):
    hi = x.astype(BF16)
    lo = (x - hi.astype(F32)).astype(BF16)
    return jnp.dot(m, hi, preferred_element_type=F32) + jnp.dot(m, lo, preferred_element_type=F32)


def _next_twiddle(tw, tw1_ref):
    twr, twi = tw
    b_r, b_i = tw1_ref[0], tw1_ref[1]
    return twr * b_r - twi * b_i, twr * b_i + twi * b_r


def _filt_fft_kernel(f_ref, m1_ref, w2_ref, tw1_ref, o_ref, y_scr, *, n1):
    nbg = FFT_N2 // SUBLANES
    C = f_ref.shape[-1]
    for bg in range(nbg):
        yg = _split_dot(m1_ref[...], f_ref[:, bg].reshape(n1 * SUBLANES, C))
        y_scr[:, :, bg] = yg.reshape(n1, 2, SUBLANES, C)

    def body(k1, tw):
        y = _cplx_rows(y_scr[k1].reshape(2 * FFT_N2, C), tw[0], tw[1], False)
        o_ref[k1] = _split_dot(w2_ref[...], y)
        return _next_twiddle(tw, tw1_ref)

    lax.fori_loop(0, n1, body, (jnp.ones((FFT_N2, C), F32), jnp.zeros((FFT_N2, C), F32)))


def _filter_spectrum(filt, L, dc):
    n1 = dc['n1']
    f4 = filt.reshape(L, HY_ORDER, 2, HY_CH)
    h_f, h_b = f4[:, :, 0, :], f4[:, :, 1, :]
    full = jnp.concatenate([h_f, jnp.zeros_like(h_f[:1]), jnp.flip(h_b[1:], 0)], 0)
    full = jnp.transpose(full, (1, 0, 2)).reshape(HY_ORDER, n1, FFT_N2 // SUBLANES, SUBLANES, HY_CH)
    nbg = FFT_N2 // SUBLANES
    return pl.pallas_call(
        functools.partial(_filt_fft_kernel, n1=n1),
        grid=(HY_ORDER, HY_CH // LANES),
        in_specs=[pl.BlockSpec((None, n1, nbg, SUBLANES, LANES), lambda o, c: (o, 0, 0, 0, c)),
                  pl.BlockSpec(dc['m1f'].shape, lambda o, c: (0, 0)),
                  pl.BlockSpec(dc['w2'].shape, lambda o, c: (0, 0)),
                  pl.BlockSpec(dc['tw1'].shape, lambda o, c: (0, 0, 0))],
        out_specs=pl.BlockSpec((None, n1, 2 * FFT_N2, LANES), lambda o, c: (o, 0, 0, c)),
        out_shape=jax.ShapeDtypeStruct((HY_ORDER, n1, 2 * FFT_N2, HY_CH), F32),
        scratch_shapes=[pltpu.VMEM((n1, 2, nbg, SUBLANES, LANES), F32)],
        compiler_params=_cparams(("parallel", "parallel")),
        name="hyena_filter_fft",
    )(full, dc['m1f'], dc['w2'], dc['tw1'])


def _hyconv_kernel(za_ref, zb_ref, ga_ref, gb_ref, ff_ref, bias_ref, m1_ref, m1i_ref, w2_ref, iw2_ref,
                   tw1_ref, oa_ref, ob_ref, y_scr, *, n1, na):
    nbg = FFT_N2 // SUBLANES
    C = za_ref.shape[-1]
    half = na * SUBLANES
    for bg in range(nbg):
        xg = jnp.concatenate([za_ref[:, bg].reshape(half, C), zb_ref[:, bg].reshape(half, C)], axis=0)
        yg = jnp.dot(m1_ref[...], xg.astype(BF16), preferred_element_type=F32)
        y_scr[:, :, bg] = yg.reshape(n1, 2, SUBLANES, C)

    def body(k1, tw):
        y = _cplx_rows(y_scr[k1].reshape(2 * FFT_N2, C), tw[0], tw[1], False)
        z = jnp.dot(w2_ref[...], y.astype(BF16), preferred_element_type=F32)
        f = ff_ref[k1]
        zr, zi = z[:FFT_N2], z[FFT_N2:]
        fr, fi = f[:FFT_N2], f[FFT_N2:]
        p = jnp.concatenate([zr * fr - zi * fi, zr * fi + zi * fr], axis=0)
        v = jnp.dot(iw2_ref[...], p.astype(BF16), preferred_element_type=F32)
        v = _cplx_rows(v, tw[0], tw[1], True)
        y_scr[k1] = v.reshape(2, nbg, SUBLANES, C)
        return _next_twiddle(tw, tw1_ref)

    lax.fori_loop(0, n1, body, (jnp.ones((FFT_N2, C), F32), jnp.zeros((FFT_N2, C), F32)))
    bias = bias_ref[...]
    for bg in range(nbg):
        vg = y_scr[:, :, bg].reshape(n1 * 2 * SUBLANES, C)
        out = jnp.dot(m1i_ref[...], vg.astype(BF16), preferred_element_type=F32)
        ya = out[:half].reshape(na, SUBLANES, C)
        yb = out[half:].reshape(na, SUBLANES, C)
        za = za_ref[:, bg]
        zb = zb_ref[:, bg]
        oa_ref[:, bg] = ga_ref[:, bg] * (ya + za * bias)
        ob_ref[:, bg] = gb_ref[:, bg] * (yb + zb * bias)


def _hyena_conv(z, z_col0, gates, g_col0, spec, bias, dc, B, L):
    n1, na = dc['n1'], dc['na']
    nbg = FFT_N2 // SUBLANES
    NCB = HY_CH // LANES

    def view(a):
        return a.reshape(B, na, nbg, SUBLANES, a.shape[-1])

    def seq(col0, which):
        return pl.BlockSpec((None, na, nbg, SUBLANES, LANES),
                            lambda c, p: (2 * p + which, 0, 0, 0, col0 + c))

    def const(a):
        nd = a.ndim
        return pl.BlockSpec(a.shape, lambda c, p: (0,) * nd)

    out_a, out_b = pl.pallas_call(
        functools.partial(_hyconv_kernel, n1=n1, na=na),
        grid=(NCB, B // 2),
        in_specs=[seq(z_col0, 0), seq(z_col0, 1), seq(g_col0, 0), seq(g_col0, 1),
                  pl.BlockSpec((n1, 2 * FFT_N2, LANES), lambda c, p: (0, 0, c), pipeline_mode=pl.Buffered(1)),
                  pl.BlockSpec((1, LANES), lambda c, p: (0, c)),
                  const(dc['m1']), const(dc['m1inv']), const(dc['w2']), const(dc['iw2']), const(dc['tw1'])],
        out_specs=[pl.BlockSpec((None, na, nbg, SUBLANES, LANES), lambda c, p: (p, 0, 0, 0, c)),
                   pl.BlockSpec((None, na, nbg, SUBLANES, LANES), lambda c, p: (p, 0, 0, 0, c))],
        out_shape=[jax.ShapeDtypeStruct((B // 2, na, nbg, SUBLANES, HY_CH), F32)] * 2,
        scratch_shapes=[pltpu.VMEM((n1, 2, nbg, SUBLANES, LANES), F32)],
        compiler_params=_cparams(("parallel", "arbitrary")),
        name="hyena_long_conv",
    )(view(z), view(z), view(gates), view(gates), spec, bias.reshape(1, HY_CH),
      dc['m1'], dc['m1inv'], dc['w2'], dc['iw2'], dc['tw1'])
    out = jnp.stack([out_a, out_b], axis=1)
    return out.reshape(B, L, HY_CH)


def _out1_kernel(hy_ref, of_ref, ob_ref, r_ref, k_ref, v_ref, g_ref, x_ref, mod_ref, wo_ref,
                 rk_ref, lg_ref, lb_ref, bd_ref, lng_ref, lnb_ref, xl_ref, u_ref, *, alpha):
    o = of_ref[...] + ob_ref[...]
    bd = bd_ref[...]
    inv = 1.0 / RW_HEAD_DIM
    mu = _hdot(o, bd) * inv
    oc = o - mu
    var = _hdot(oc * oc, bd) * inv
    on = oc * lax.rsqrt(var + RW_GN_EPS) * lg_ref[...] + lb_ref[...]
    bonus = _hdot(r_ref[...] * k_ref[...] * rk_ref[...], bd) * v_ref[...]
    y = jnp.concatenate([hy_ref[...], (on + bonus) * g_ref[...]], axis=1)
    _post_mix(y, x_ref[...], mod_ref, wo_ref, lng_ref, lnb_ref, alpha, xl_ref, u_ref)


def _readout1(hy, o_f, o_b, r, k, v, g, xcat, mods, B, S, Lc, w_out, r_k, ln_g, ln_b, dn_g, dn_b, alpha):
    D = xcat.shape[-1]
    L = S - Lc
    J = S // ROW_TILE
    JL = L // ROW_TILE
    JC = Lc // ROW_TILE

    def cat_rows(w):
        return pl.BlockSpec((ROW_TILE, w), lambda b, j: (b * J + JC + j, 0))

    def lat_rows(w):
        return pl.BlockSpec((ROW_TILE, w), lambda b, j: (b * JL + j, 0))

    def const(shape):
        return pl.BlockSpec(shape, lambda b, j: (0,) * len(shape))

    sd = jax.ShapeDtypeStruct
    return pl.pallas_call(
        functools.partial(_out1_kernel, alpha=alpha),
        grid=(B, JL),
        in_specs=[lat_rows(HY_CH), cat_rows(RW_W), cat_rows(RW_W), cat_rows(RW_W), cat_rows(RW_W), cat_rows(RW_W),
                  cat_rows(RW_W), cat_rows(D),
                  pl.BlockSpec((None, 6, D), lambda b, j: (b, 0, 0)),
                  const((HY_CH + RW_W, D)), const((1, RW_W)), const((1, RW_W)), const((1, RW_W)),
                  const((RW_W, RW_W)), const((1, D)), const((1, D))],
        out_specs=[lat_rows(D), lat_rows(D)],
        out_shape=[sd((B * L, D), F32), sd((B * L, D), F32)],
        compiler_params=_cparams(("parallel", "parallel")),
        name="readout1",
    )(hy, o_f, o_b, r, k, v, g, xcat.reshape(B * S, D), mods, w_out.astype(BF16),
      r_k.reshape(1, RW_W), ln_g.reshape(1, RW_W), ln_b.reshape(1, RW_W),
      _head_sum_matrix(RW_W, RW_HEAD_DIM), dn_g.reshape(1, D), dn_b.reshape(1, D))


def kernel(x, c, ctx, c_ctx, mod_w, mod_b, ln1_g, ln1_b, ln2_g, ln2_b, ev_w_in, ev_w_out, gla_gate_w2, gla_gate_b, gla_norm_g, hg_lb_logits, hg_norm_g, od_w_in, od_w_out, hy_conv_w, hy_conv_b, hy_ffn_w1, hy_ffn_b1, hy_ffn_w2, hy_ffn_b2, hy_ffn_w3, hy_sin_freq, hy_bias, rw_mu, rw_w0, rw_w2, rw_a0, rw_a2, rw_g2, rw_k_k, rw_k_a, rw_r_k, rw_ln_g, rw_ln_b, router_w, router_bias, exp_w13, exp_w2, sh_w13, sh_w2):
    B, L, D = x.shape
    Lc = ctx.shape[1]
    S = Lc + L
    depth = mod_w.shape[0]
    assert depth == 2 and L % ROW_TILE == 0 and Lc % ROW_TILE == 0 and B % 2 == 0
    assert L % (FFT_N2 * 2) == 0 and L % GRID_W == 0
    alpha = (2 * depth) ** 0.25
    nctx = Lc // ROW_TILE
    J = S // ROW_TILE
    T = B * S

    cc = jnp.concatenate([c, c_ctx[None, :]], axis=0)
    cc = jnp.pad(cc, ((0, (-cc.shape[0]) % SUBLANES), (0, 0)))
    hg_lb = jnp.cumsum(jax.nn.softmax(hg_lb_logits.astype(F32), axis=0), axis=0)
    xcat = jnp.concatenate([ctx, x], axis=1)

    mods = _modulation(cc, mod_w[0], mod_b[0])
    gq, gk, gv, gla, r, hq, hk, hla, hv, hgate = _in_proj0(xcat, mods, nctx, ev_w_in[0], gla_gate_w2[0],
                                                            gla_gate_b[0], hg_lb[0])
    gk3 = gk.reshape(1, T, GLA_QK)
    nh_g = LANES // GLA_DK
    o_gf = _gated_recurrence(gq, gk3, gv, gla, 0, B, S, nctx, False, nh_g, GLA_DK, GLA_DV)
    o_gb = _gated_recurrence(gq, gk3, gv, gla, 0, B, S, nctx, True, nh_g, GLA_DK, GLA_DV)
    o_hf = _gated_recurrence(hq, hk, hv, hla, 0, B, S, nctx, False, 1, HG_EXPAND, HG_EXPAND)
    o_hb = _gated_recurrence(hq, hk, hv, hla, 1, B, S, nctx, True, 1, HG_EXPAND, HG_EXPAND)
    xl, u = _readout0(o_gf, o_gb, o_hf, o_hb, r, hgate, xcat, mods, nctx, ev_w_out[0], gla_norm_g[0],
                      hg_norm_g[0], ln1_g[0], ln1_b[0], alpha)
    tiles_per_b = S // CMB_TILE
    tile_in_b = jnp.arange(B * tiles_per_b, dtype=I32) % tiles_per_b
    mod_of_tile = jnp.where(tile_in_b < Lc // CMB_TILE, B, jnp.arange(B * tiles_per_b, dtype=I32) // tiles_per_b)
    x1 = _moe_block(u, xl, mods, mod_of_tile.astype(I32), router_w[0], router_bias[0], exp_w13[0], exp_w2[0],
                    sh_w13[0], sh_w2[0], ln2_g[0], ln2_b[0], alpha)
    xcat = x1.reshape(B, S, D)

    mods = _modulation(cc, mod_w[1], mod_b[1])
    p_hy, p_rw = _in_proj1(xcat, mods, nctx, od_w_in[0])
    rr, rk, rv, rkk, rbb, rld, rg = _rw_streams(p_rw, B, S, nctx, rw_mu[0], rw_w0[0], rw_w2[0], rw_a0[0],
                                                 rw_a2[0], rw_g2[0], rw_k_k[0], rw_k_a[0])
    o_f = _rwkv_scan(rr, rk, rv, rkk, rbb, rld, B, S, nctx, False)
    o_b = _rwkv_scan(rr, rk, rv, rkk, rbb, rld, B, S, nctx, True)
    dc = _dft_constants(L)
    filt = _hyena_filters(L, hy_ffn_w1[0], hy_ffn_b1[0], hy_ffn_w2[0], hy_ffn_b2[0], hy_ffn_w3[0], hy_sin_freq[0])
    spec = _filter_spectrum(filt, L, dc)
    uu = _short_conv(p_hy, B, S, Lc, hy_conv_w[0], hy_conv_b[0])
    NCB = HY_CH // LANES
    z1 = _hyena_conv(uu, 0, uu, NCB, spec[0], hy_bias[0, 0], dc, B, L)
    z2 = _hyena_conv(z1, 0, uu, 2 * NCB, spec[1], hy_bias[0, 1], dc, B, L)
    xl, u = _readout1(z2.reshape(B * L, HY_CH), o_f, o_b, rr, rk, rv, rg, xcat, mods, B, S, Lc, od_w_out[0],
                      rw_r_k[0], rw_ln_g[0], rw_ln_b[0], ln1_g[1], ln1_b[1], alpha)
    mod_of_tile = (jnp.arange(B * (L // CMB_TILE), dtype=I32) // (L // CMB_TILE)).astype(I32)
    out = _moe_block(u, xl, mods, mod_of_tile, router_w[1], router_bias[1], exp_w13[1], exp_w2[1],
                     sh_w13[1], sh_w2[1], ln2_g[1], ln2_b[1], alpha)
    return out.reshape(B, L, D)
```

```python
import functools
import math

import numpy as np
import jax
import jax.numpy as jnp
from jax import lax
from jax.experimental import pallas as pl
from jax.experimental.pallas import tpu as pltpu

F32 = jnp.float32
BF16 = jnp.bfloat16
I32 = jnp.int32
HI = lax.Precision.HIGHEST

LN_EPS = 1e-5
GLA_HEADS, GLA_DK, GLA_DV = 4, 64, 128
GLA_QK, GLA_V = GLA_HEADS * GLA_DK, GLA_HEADS * GLA_DV
GLA_GATE_RANK = 16
GLA_GATE_NORM = 16.0
CHUNK = 64
HG_HEADS, HG_EXPAND = 4, 128
HG_W = HG_HEADS * HG_EXPAND
HY_CH, HY_ORDER, HY_SHORT, HY_EMB = 512, 2, 3, 33
HY_WIDTH = (HY_ORDER + 1) * HY_CH
HY_FAST_DECAY, HY_SLOW_DECAY, HY_TARGET = 0.3, 1.5, 1e-2
RW_HEADS, RW_HEAD_DIM = 8, 64
RW_W = RW_HEADS * RW_HEAD_DIM
RW_DECAY_LORA, RW_AAA_LORA, RW_GATE_LORA = 64, 64, 128
RW_GN_EPS = 64e-5
GRID_W = 64
N_EXPERTS, TOP_K, N_GROUPS, TOPK_GROUPS = 256, 8, 8, 4
PER_GROUP = N_EXPERTS // N_GROUPS
EXPERT_FF, SHARED_FF = 256, 256
ROUTED_SCALE = 2.5

LANES = 128
SUBLANES = 8
VMEM_LIMIT = 56 * 1024 * 1024
ROW_TILE = 256
MOE_BLK = 256
CMB_TILE = 128
RW_CHUNK = 64
FFT_N2 = 128


def _cparams(sem):
    return pltpu.CompilerParams(dimension_semantics=sem, vmem_limit_bytes=VMEM_LIMIT)


def _bdot(a, b):
    return jnp.dot(a.astype(BF16), b.astype(BF16), preferred_element_type=F32)


def _bdot_nt(a, b):
    return lax.dot_general(a.astype(BF16), b.astype(BF16), (((1,), (1,)), ((), ())),
                           preferred_element_type=F32)


def _bdot_tn(a, b):
    return lax.dot_general(a.astype(BF16), b.astype(BF16), (((0,), (0,)), ((), ())),
                           preferred_element_type=F32)


def _bmm(a, b):
    return jnp.einsum('nij,njk->nik', a.astype(BF16), b.astype(BF16), preferred_element_type=F32)


def _bmm_nt(a, b):
    return jnp.einsum('nik,njk->nij', a.astype(BF16), b.astype(BF16), preferred_element_type=F32)


def _bmm_tn(a, b):
    return _bmm(jnp.swapaxes(a, 1, 2), b)


def _hdot(a, b):
    return jnp.dot(a, b, precision=HI, preferred_element_type=F32)


def _silu(x):
    return x * jax.nn.sigmoid(x)


def _layer_norm_rows(x, g, b, eps):
    mu = jnp.mean(x, axis=-1, keepdims=True)
    xc = x - mu
    var = jnp.mean(xc * xc, axis=-1, keepdims=True)
    return xc * lax.rsqrt(var + eps) * g + b


def _mod_kernel(c_ref, w_ref, b_ref, o_ref):
    o_ref[...] = _hdot(_silu(c_ref[...]), w_ref[...]) + b_ref[...]


def _modulation(cc, w, b):
    R, D = cc.shape
    out = pl.pallas_call(
        _mod_kernel,
        grid=(6,),
        in_specs=[pl.BlockSpec((R, D), lambda j: (0, 0)),
                  pl.BlockSpec((D, D), lambda j: (0, j)),
                  pl.BlockSpec((1, D), lambda j: (0, j))],
        out_specs=pl.BlockSpec((R, D), lambda j: (0, j)),
        out_shape=jax.ShapeDtypeStruct((R, 6 * D), F32),
        compiler_params=_cparams(("parallel",)),
        name="modulation",
    )(cc, w, b.reshape(1, 6 * D))
    return out.reshape(R, 6, D)


def _in0_kernel(x_ref, mod_ref, w_ref, wa_ref, w2_ref, gb_ref, lb_ref,
                gq_ref, gk_ref, gv_ref, gla_ref, r_ref, hq_ref, hk_ref, hla_ref, hv_ref, hg_ref):
    sh1 = mod_ref[0:1, :]
    sc1 = mod_ref[1:2, :]
    h = (x_ref[...] * (1.0 + sc1) + sh1).astype(BF16)

    def proj(off, width):
        return jnp.dot(h, w_ref[:, off:off + width], preferred_element_type=F32)

    gq_ref[...] = proj(0, GLA_QK) * (GLA_DK ** -0.5)
    gk_ref[...] = proj(GLA_QK, GLA_QK)
    gv_ref[...] = proj(2 * GLA_QK, GLA_V)
    r_ref[...] = proj(2 * GLA_QK + GLA_V, GLA_V)
    base = 2 * GLA_QK + 2 * GLA_V
    a = jnp.dot(h, wa_ref[...], preferred_element_type=F32)
    z = _bdot(a, w2_ref[...]) + gb_ref[...]
    ls = (jnp.minimum(z, 0.0) - jnp.log(1.0 + jnp.exp(-jnp.abs(z)))) * (1.0 / GLA_GATE_NORM)
    gla_ref[0] = ls[:, :GLA_QK]
    gla_ref[1] = ls[:, GLA_QK:]
    hq_ref[...] = _silu(proj(base, HG_W))
    for d in range(2):
        zf = proj(base + (1 + d) * HG_W, HG_W)
        lb = lb_ref[d:d + 1, :]
        f = lb + (1.0 - lb) * jax.nn.sigmoid(zf)
        hk_ref[d] = 1.0 - f
        hla_ref[d] = jnp.log(f)
    hv_ref[...] = proj(base + 3 * HG_W, HG_W)
    hg_ref[...] = proj(base + 4 * HG_W, HG_W)


def _in_proj0(xcat, mods, nctx_tiles, w_in, gate_w2, gate_b, lb):
    B, S, D = xcat.shape
    J = S // ROW_TILE
    T = B * S
    a_off = 2 * GLA_QK + 2 * GLA_V
    wmain = jnp.concatenate([w_in[:, :a_off], w_in[:, a_off + 2 * GLA_GATE_RANK:]], axis=1).astype(BF16)
    wa = jnp.pad(w_in[:, a_off:a_off + 2 * GLA_GATE_RANK], ((0, 0), (0, LANES - 2 * GLA_GATE_RANK))).astype(BF16)
    w2 = jnp.zeros((LANES, 2 * GLA_QK), F32)
    w2 = w2.at[:GLA_GATE_RANK, :GLA_QK].set(gate_w2[0]).at[GLA_GATE_RANK:2 * GLA_GATE_RANK, GLA_QK:].set(gate_w2[1])
    gb = gate_b.reshape(1, 2 * GLA_QK)
    WM = wmain.shape[1]

    def rows(w):
        return pl.BlockSpec((ROW_TILE, w), lambda b, j: (b * J + j, 0))

    def rows2(w):
        return pl.BlockSpec((2, ROW_TILE, w), lambda b, j: (0, b * J + j, 0))

    def const(shape):
        return pl.BlockSpec(shape, lambda b, j: (0,) * len(shape))

    sd = jax.ShapeDtypeStruct
    outs = pl.pallas_call(
        _in0_kernel,
        grid=(B, J),
        in_specs=[rows(D),
                  pl.BlockSpec((None, 6, D), lambda b, j: (jnp.where(j < nctx_tiles, B, b), 0, 0)),
                  const((D, WM)), const((D, LANES)), const((LANES, 2 * GLA_QK)), const((1, 2 * GLA_QK)),
                  const((2, HG_W))],
        out_specs=[rows(GLA_QK), rows(GLA_QK), rows(GLA_V), rows2(GLA_QK), rows(GLA_V),
                   rows(HG_W), rows2(HG_W), rows2(HG_W), rows(HG_W), rows(HG_W)],
        out_shape=[sd((T, GLA_QK), F32), sd((T, GLA_QK), F32), sd((T, GLA_V), F32), sd((2, T, GLA_QK), F32),
                   sd((T, GLA_V), F32), sd((T, HG_W), F32), sd((2, T, HG_W), F32), sd((2, T, HG_W), F32),
                   sd((T, HG_W), F32), sd((T, HG_W), F32)],
        compiler_params=_cparams(("parallel", "parallel")),
        name="in_proj0",
    )(xcat.reshape(T, D), mods, wmain, wa, w2, gb, lb)
    return outs


def _rec_kernel(q_ref, k_ref, v_ref, la_ref, o_ref, st_ref, *, rev, nh, dk, dv, nchunk):
    t = pl.program_id(1)

    @pl.when(t == 0)
    def _():
        st_ref[...] = jnp.zeros_like(st_ref)

    C = CHUNK
    row = lax.broadcasted_iota(I32, (C, C), 0)
    col = lax.broadcasted_iota(I32, (C, C), 1)
    incl = (col >= row) if rev else (col <= row)
    tri = jnp.where(incl, 1.0, 0.0).astype(F32)
    ref_i = C // 2 - 1 if rev else C // 2
    last_i = 0 if rev else C - 1
    order = range(nchunk - 1, -1, -1) if rev else range(nchunk)

    def stack(x, hd):
        return jnp.stack([x[c * C:(c + 1) * C, h * hd:(h + 1) * hd] for c in range(nchunk) for h in range(nh)], 0)

    def rows(x, i):
        return jnp.concatenate([jnp.broadcast_to(x[c * C + i:c * C + i + 1], (C, x.shape[1]))
                                for c in range(nchunk)], axis=0)

    la = la_ref[...]
    q = q_ref[...]
    k = k_ref[...]
    b = jnp.concatenate([_hdot(tri, la[c * C:(c + 1) * C]) for c in range(nchunk)], axis=0)
    b_mid = rows(b, ref_i)
    b_last = rows(b, last_i)
    v = stack(v_ref[...], dv)
    sc = _bmm_nt(stack(q * jnp.exp(b - b_mid), dk), stack(k * jnp.exp(b_mid - b), dk))
    o_intra = _bmm(jnp.where(incl[None], sc, 0.0), v)
    q_in = stack(q * jnp.exp(b), dk)
    kv_t = _bmm_tn(v, stack(k * jnp.exp(b_last - b), dk))
    dec = stack(jnp.exp(b_last), dk)[:, 0:1, :]
    s_t = st_ref[...]
    for c in order:
        sl = slice(c * nh, (c + 1) * nh)
        o_c = o_intra[sl] + _bmm_nt(q_in[sl], s_t)
        o_ref[c * C:(c + 1) * C, :] = jnp.concatenate([o_c[h] for h in range(nh)], axis=1)
        s_t = s_t * dec[sl] + kv_t[sl]
    st_ref[...] = s_t


def _gated_recurrence(q, k, v, la, kdir, B, S, nctx_blocks, rev, nh, dk, dv):
    T = B * S
    NB = S // ROW_TILE
    wk, wv = nh * dk, nh * dv
    assert q.shape[1] == wk
    d = 1 if rev else 0
    nc = nctx_blocks

    def blk(t):
        if not rev:
            return t
        return jnp.where(t < nc, nc - 1 - t, NB - 1 - (t - nc))

    kern = functools.partial(_rec_kernel, rev=rev, nh=nh, dk=dk, dv=dv, nchunk=ROW_TILE // CHUNK)
    return pl.pallas_call(
        kern,
        grid=(B, NB),
        in_specs=[pl.BlockSpec((ROW_TILE, wk), lambda b, t: (b * NB + blk(t), 0)),
                  pl.BlockSpec((None, ROW_TILE, wk), lambda b, t: (kdir, b * NB + blk(t), 0)),
                  pl.BlockSpec((ROW_TILE, wv), lambda b, t: (b * NB + blk(t), 0)),
                  pl.BlockSpec((None, ROW_TILE, wk), lambda b, t: (d, b * NB + blk(t), 0))],
        out_specs=pl.BlockSpec((ROW_TILE, wv), lambda b, t: (b * NB + blk(t), 0)),
        out_shape=jax.ShapeDtypeStruct((T, wv), F32),
        scratch_shapes=[pltpu.VMEM((nh, dv, dk), F32)],
        compiler_params=_cparams(("parallel", "arbitrary")),
        name="gated_rec_bwd" if rev else "gated_rec_fwd",
    )(q, k, v, la)


def _post_mix(y, x, mod_ref, wo_ref, lng_ref, lnb_ref, alpha, xl_ref, u_ref):
    g1 = mod_ref[2:3, :]
    sh2 = mod_ref[3:4, :]
    sc2 = mod_ref[4:5, :]
    yo = jnp.dot(y.astype(BF16), wo_ref[...], preferred_element_type=F32)
    xl = _layer_norm_rows(alpha * x + g1 * yo, lng_ref[...], lnb_ref[...], LN_EPS)
    xl_ref[...] = xl
    u_ref[...] = xl * (1.0 + sc2) + sh2


def _out0_kernel(gf_ref, gb_ref, hf_ref, hb_ref, r_ref, hg_ref, x_ref, mod_ref, wo_ref,
                 gg_ref, hgg_ref, lng_ref, lnb_ref, xl_ref, u_ref, *, alpha):
    def heads(o, g, gate):
        parts = []
        for hh in range(o.shape[1] // LANES):
            seg = o[:, hh * LANES:(hh + 1) * LANES]
            ms = jnp.mean(seg * seg, axis=-1, keepdims=True)
            parts.append(seg * lax.rsqrt(ms + 1e-6) * g)
        return jnp.concatenate(parts, axis=1) * _silu(gate)

    y = jnp.concatenate([heads(gf_ref[...] + gb_ref[...], gg_ref[...], r_ref[...]),
                         heads(hf_ref[...] + hb_ref[...], hgg_ref[...], hg_ref[...])], axis=1)
    _post_mix(y, x_ref[...], mod_ref, wo_ref, lng_ref, lnb_ref, alpha, xl_ref, u_ref)


def _readout0(o_gf, o_gb, o_hf, o_hb, r, hgate, xcat, mods, nctx_tiles, w_out, gla_g, hg_g, ln_g, ln_b, alpha):
    B, S, D = xcat.shape
    J = S // ROW_TILE
    T = B * S

    def rows(w):
        return pl.BlockSpec((ROW_TILE, w), lambda b, j: (b * J + j, 0))

    def const(shape):
        return pl.BlockSpec(shape, lambda b, j: (0,) * len(shape))

    sd = jax.ShapeDtypeStruct
    return pl.pallas_call(
        functools.partial(_out0_kernel, alpha=alpha),
        grid=(B, J),
        in_specs=[rows(GLA_V), rows(GLA_V), rows(HG_W), rows(HG_W), rows(GLA_V), rows(HG_W), rows(D),
                  pl.BlockSpec((None, 6, D), lambda b, j: (jnp.where(j < nctx_tiles, B, b), 0, 0)),
                  const((GLA_V + HG_W, D)), const((1, GLA_DV)), const((1, HG_EXPAND)), const((1, D)), const((1, D))],
        out_specs=[rows(D), rows(D)],
        out_shape=[sd((T, D), F32), sd((T, D), F32)],
        compiler_params=_cparams(("parallel", "parallel")),
        name="readout0",
    )(o_gf, o_gb, o_hf, o_hb, r, hgate, xcat.reshape(T, D), mods, w_out.astype(BF16),
      gla_g.reshape(1, -1), hg_g.reshape(1, -1), ln_g.reshape(1, D), ln_b.reshape(1, D))


def _route_kernel(u_ref, rwt_ref, rb_ref, tri_ref, eidx_ref, gate_ref, rank_ref, cnt_ref, carry_ref):
    i = pl.program_id(0)

    @pl.when(i == 0)
    def _():
        carry_ref[...] = jnp.zeros_like(carry_ref)

    u = u_ref[...]
    TM = u.shape[0]
    E = N_EXPERTS
    logits = lax.dot_general(rwt_ref[...], u, (((1,), (1,)), ((), ())), precision=HI,
                             preferred_element_type=F32)
    scores = jax.nn.sigmoid(logits)
    sel = scores + rb_ref[...]
    neg = -jnp.inf
    sel3 = sel.reshape(N_GROUPS, PER_GROUP, TM)
    io_g = lax.broadcasted_iota(I32, sel3.shape, 1)
    m1 = jnp.max(sel3, axis=1, keepdims=True)
    i1 = jnp.min(jnp.where(sel3 == m1, io_g, PER_GROUP), axis=1, keepdims=True)
    m2 = jnp.max(jnp.where(io_g == i1, neg, sel3), axis=1, keepdims=True)
    grp = m1 + m2
    io_n = lax.broadcasted_iota(I32, grp.shape, 0)
    keep = jnp.zeros(grp.shape, jnp.bool_)
    for _ in range(TOPK_GROUPS):
        m = jnp.max(grp, axis=0, keepdims=True)
        idx = jnp.min(jnp.where(grp == m, io_n, N_GROUPS), axis=0, keepdims=True)
        hit = io_n == idx
        keep = jnp.logical_or(keep, hit)
        grp = jnp.where(hit, neg, grp)
    sel = jnp.where(keep, sel3, neg).reshape(E, TM)
    io_e = lax.broadcasted_iota(I32, (E, TM), 0)
    base = carry_ref[...]
    tri = tri_ref[...]
    e_rows, g_rows, r_rows = [], [], []
    for _ in range(TOP_K):
        m = jnp.max(sel, axis=0, keepdims=True)
        idx = jnp.min(jnp.where(sel == m, io_e, E), axis=0, keepdims=True)
        hit = io_e == idx
        hit_f = jnp.where(hit, 1.0, 0.0)
        g_rows.append(jnp.sum(jnp.where(hit, scores, 0.0), axis=0, keepdims=True))
        prefix = jnp.dot(hit_f.astype(BF16), tri, preferred_element_type=F32)
        r_rows.append(jnp.sum(jnp.where(hit, prefix + base, 0.0), axis=0, keepdims=True))
        base = base + jnp.sum(hit_f, axis=1, keepdims=True)
        e_rows.append(idx)
        sel = jnp.where(hit, neg, sel)
    carry_ref[...] = base
    cnt_ref[...] = base
    g = jnp.concatenate(g_rows, axis=0)
    gate_ref[...] = g / jnp.sum(g, axis=0, keepdims=True) * ROUTED_SCALE
    eidx_ref[...] = jnp.concatenate(e_rows, axis=0)
    rank_ref[...] = jnp.concatenate(r_rows, axis=0).astype(I32)


def _route(u, router_w, router_bias):
    T, D = u.shape
    E = N_EXPERTS
    n = T // ROW_TILE
    tri = jnp.asarray(np.triu(np.ones((ROW_TILE, ROW_TILE), np.float32), 1), BF16)
    sd = jax.ShapeDtypeStruct
    cols = pl.BlockSpec((TOP_K, ROW_TILE), lambda i: (0, i))
    return pl.pallas_call(
        _route_kernel,
        grid=(n,),
        in_specs=[pl.BlockSpec((ROW_TILE, D), lambda i: (i, 0)),
                  pl.BlockSpec((E, D), lambda i: (0, 0)),
                  pl.BlockSpec((E, 1), lambda i: (0, 0)),
                  pl.BlockSpec((ROW_TILE, ROW_TILE), lambda i: (0, 0))],
        out_specs=[cols, cols, cols, pl.BlockSpec((E, 1), lambda i: (0, 0))],
        out_shape=[sd((TOP_K, T), I32), sd((TOP_K, T), F32), sd((TOP_K, T), I32), sd((E, 1), F32)],
        scratch_shapes=[pltpu.VMEM((E, 1), F32)],
        compiler_params=_cparams(("arbitrary",)),
        name="moe_route",
    )(u, router_w.T, router_bias.reshape(E, 1), tri)


def _dest_kernel(e_ref, r_ref, ps_ref, d_ref):
    e = e_ref[...]
    TM = e.shape[1]
    io_e = lax.broadcasted_iota(I32, (N_EXPERTS, TM), 0)
    ps = ps_ref[...]
    rows = []
    for k in range(TOP_K):
        rows.append(jnp.sum(jnp.where(io_e == e[k:k + 1, :], ps, 0.0), axis=0, keepdims=True))
    d_ref[...] = jnp.concatenate(rows, axis=0).astype(I32) + r_ref[...]


def _slot_of(eidx, rank, pad_start):
    T = eidx.shape[1]
    cols = pl.BlockSpec((TOP_K, ROW_TILE), lambda i: (0, i))
    return pl.pallas_call(
        _dest_kernel,
        grid=(T // ROW_TILE,),
        in_specs=[cols, cols, pl.BlockSpec((N_EXPERTS, 1), lambda i: (0, 0))],
        out_specs=cols,
        out_shape=jax.ShapeDtypeStruct((TOP_K, T), I32),
        compiler_params=_cparams(("parallel",)),
        name="moe_slot",
    )(eidx, rank, pad_start.astype(F32).reshape(N_EXPERTS, 1))


def _dispatch_kernel(dest_ref, u_ref, xs_in_ref, xs_ref, sem):
    del xs_in_ref
    TM = u_ref.shape[0]

    def copy(t, k):
        return pltpu.make_async_copy(u_ref.at[pl.ds(t, 1)], xs_ref.at[pl.ds(dest_ref[k, t], 1)], sem)

    def issue(t, c):
        for k in range(TOP_K):
            copy(t, k).start()
        return c

    lax.fori_loop(0, TM, issue, 0)

    def drain(t, c):
        for k in range(TOP_K):
            copy(t, k).wait()
        return c

    lax.fori_loop(0, TM, drain, 0)


def _dispatch(u, dest, n_slots):
    T, D = u.shape
    xs0 = jnp.zeros((n_slots, D), F32)
    return pl.pallas_call(
        _dispatch_kernel,
        grid=(T // ROW_TILE,),
        in_specs=[pl.BlockSpec((TOP_K, ROW_TILE), lambda i: (0, i), memory_space=pltpu.SMEM),
                  pl.BlockSpec((ROW_TILE, D), lambda i: (i, 0)),
                  pl.BlockSpec(memory_space=pl.ANY)],
        out_specs=pl.BlockSpec(memory_space=pl.ANY),
        out_shape=jax.ShapeDtypeStruct((n_slots, D), F32),
        scratch_shapes=[pltpu.SemaphoreType.DMA],
        input_output_aliases={2: 0},
        compiler_params=_cparams(("arbitrary",)),
        name="moe_dispatch",
    )(dest, u, xs0)


def _expert_kernel(be_ref, nu_ref, xs_ref, w13_ref, w2_ref, ys_ref):
    del be_ref

    @pl.when(pl.program_id(0) < nu_ref[0])
    def _():
        h = _bdot(xs_ref[...], w13_ref[...])
        a = _silu(h[:, :EXPERT_FF]) * h[:, EXPERT_FF:]
        ys_ref[...] = _bdot(a, w2_ref[...])


def _expert_gemm(xs, blk_exp, n_used, w13, w2, layer):
    NP, D = xs.shape
    nblk = NP // MOE_BLK
    F2 = w13.shape[3]

    def xmap(i, be, nu):
        return (jnp.minimum(i, nu[0] - 1), 0)

    return pl.pallas_call(
        _expert_kernel,
        grid_spec=pltpu.PrefetchScalarGridSpec(
            num_scalar_prefetch=2,
            grid=(nblk,),
            in_specs=[pl.BlockSpec((MOE_BLK, D), xmap),
                      pl.BlockSpec((None, None, D, F2), lambda i, be, nu: (layer, be[i], 0, 0)),
                      pl.BlockSpec((None, None, F2 // 2, D), lambda i, be, nu: (layer, be[i], 0, 0))],
            out_specs=pl.BlockSpec((MOE_BLK, D), xmap)),
        out_shape=jax.ShapeDtypeStruct((NP, D), F32),
        compiler_params=_cparams(("arbitrary",)),
        name="moe_experts",
    )(blk_exp, n_used, xs, w13, w2)


def _combine_kernel(mt_ref, dest_ref, gate_ref, u_ref, xl_ref, mod_ref, s13_ref, s2_ref, lng_ref, lnb_ref, ys_ref,
                    o_ref, rows_ref, sem, *, alpha):
    del mt_ref
    TM = u_ref.shape[0]

    def copy(t, k):
        return pltpu.make_async_copy(ys_ref.at[pl.ds(dest_ref[k, t], 1)], rows_ref.at[k, pl.ds(t, 1)], sem)

    def issue(t, c):
        for k in range(TOP_K):
            copy(t, k).start()
        return c

    lax.fori_loop(0, TM, issue, 0)
    hs = _bdot(u_ref[...], s13_ref[...])
    acc = _bdot(_silu(hs[:, :SHARED_FF]) * hs[:, SHARED_FF:], s2_ref[...])

    def drain(t, c):
        for k in range(TOP_K):
            copy(t, k).wait()
        return c

    lax.fori_loop(0, TM, drain, 0)
    gate = gate_ref[...]
    for k in range(TOP_K):
        acc = acc + rows_ref[k] * gate[:, k:k + 1]
    g2 = mod_ref[5:6, :]
    o_ref[...] = _layer_norm_rows(alpha * xl_ref[...] + g2 * acc, lng_ref[...], lnb_ref[...], LN_EPS)


def _combine(dest, gate_t, u, xl, ys, mods, mod_of_tile, sh_w13, sh_w2, ln_g, ln_b, alpha):
    T, D = u.shape
    n = T // CMB_TILE

    def rows(w):
        return pl.BlockSpec((CMB_TILE, w), lambda i, mt: (i, 0))

    def const(shape):
        return pl.BlockSpec(shape, lambda i, mt: (0,) * len(shape))

    return pl.pallas_call(
        functools.partial(_combine_kernel, alpha=alpha),
        grid_spec=pltpu.PrefetchScalarGridSpec(
            num_scalar_prefetch=1,
            grid=(n,),
            in_specs=[pl.BlockSpec((TOP_K, CMB_TILE), lambda i, mt: (0, i), memory_space=pltpu.SMEM),
                      rows(TOP_K), rows(D), rows(D),
                      pl.BlockSpec((None, 6, D), lambda i, mt: (mt[i], 0, 0)),
                      const((D, 2 * SHARED_FF)), const((SHARED_FF, D)), const((1, D)), const((1, D)),
                      pl.BlockSpec(memory_space=pl.ANY)],
            out_specs=rows(D),
            scratch_shapes=[pltpu.VMEM((TOP_K, CMB_TILE, D), F32), pltpu.SemaphoreType.DMA]),
        out_shape=jax.ShapeDtypeStruct((T, D), F32),
        compiler_params=_cparams(("arbitrary",)),
        name="moe_combine",
    )(mod_of_tile, dest, gate_t, u, xl, mods, sh_w13.astype(BF16), sh_w2.astype(BF16),
      ln_g.reshape(1, D), ln_b.reshape(1, D), ys)


def _moe_block(u, xl, mods, mod_of_tile, router_w, router_bias, w13, w2, layer, sh_w13, sh_w2, ln_g, ln_b, alpha):
    T, D = u.shape
    eidx, gate, rank, counts = _route(u, router_w, router_bias)
    cnt = counts.reshape(N_EXPERTS).astype(I32)
    padded = (cnt + MOE_BLK - 1) // MOE_BLK * MOE_BLK
    pad_end = jnp.cumsum(padded)
    pad_start = pad_end - padded
    nblk = T * TOP_K // MOE_BLK + N_EXPERTS
    n_used = (pad_end[-1] // MOE_BLK).astype(I32)
    blk_exp = jnp.searchsorted(pad_end, jnp.arange(nblk, dtype=I32) * MOE_BLK, side='right').astype(I32)
    last_e = jnp.max(jnp.where(cnt > 0, jnp.arange(N_EXPERTS, dtype=I32), 0))
    blk_exp = jnp.minimum(blk_exp, last_e)
    dest = _slot_of(eidx, rank, pad_start)
    xs = _dispatch(u, dest, nblk * MOE_BLK)
    ys = _expert_gemm(xs, blk_exp, n_used.reshape(1), w13, w2, layer)
    return _combine(dest, gate.T, u, xl, ys, mods, mod_of_tile, sh_w13, sh_w2, ln_g, ln_b, alpha)


def _in1_kernel(x_ref, mod_ref, wh_ref, wr_ref, ph_ref, pr_ref):
    sh1 = mod_ref[0:1, :]
    sc1 = mod_ref[1:2, :]
    h = (x_ref[...] * (1.0 + sc1) + sh1).astype(BF16)
    ph_ref[...] = jnp.dot(h, wh_ref[...], preferred_element_type=F32)
    pr_ref[...] = jnp.dot(h, wr_ref[...], preferred_element_type=F32)


RW_PW = 1920


def _rw_reorder(t):
    a = 3 * RW_W
    lo = 2 * RW_DECAY_LORA + RW_AAA_LORA
    out = jnp.concatenate([t[..., :a], t[..., a + lo:], t[..., a:a + lo]], axis=-1)
    pad = [(0, 0)] * (t.ndim - 1) + [(0, RW_PW - out.shape[-1])]
    return jnp.pad(out, pad)


def _in_proj1(xcat, mods, nctx_tiles, w_in):
    B, S, D = xcat.shape
    J = S // ROW_TILE
    T = B * S
    wh = w_in[:, :HY_WIDTH].astype(BF16)
    wr = _rw_reorder(w_in[:, HY_WIDTH:]).astype(BF16)

    def rows(w):
        return pl.BlockSpec((ROW_TILE, w), lambda b, j: (b * J + j, 0))

    def const(shape):
        return pl.BlockSpec(shape, lambda b, j: (0,) * len(shape))

    sd = jax.ShapeDtypeStruct
    return pl.pallas_call(
        _in1_kernel,
        grid=(B, J),
        in_specs=[rows(D), pl.BlockSpec((None, 6, D), lambda b, j: (jnp.where(j < nctx_tiles, B, b), 0, 0)),
                  const((D, HY_WIDTH)), const((D, RW_PW))],
        out_specs=[rows(HY_WIDTH), rows(RW_PW)],
        out_shape=[sd((T, HY_WIDTH), F32), sd((T, RW_PW), F32)],
        compiler_params=_cparams(("parallel", "parallel")),
        name="in_proj1",
    )(xcat.reshape(T, D), mods, wh, wr)


def _rw_streams_kernel(prev_ref, main_ref, next_ref, mu_ref, w0_ref, w2_ref, a0_ref, a2_ref, g2_ref,
                       kk_ref, ka_ref, bd_ref,
                       r_ref, k_ref, v_ref, kkn_ref, bb_ref, ld_ref, g_ref, sh_scr, *, nctx_tiles, n_lat_tiles):
    j = pl.program_id(1)
    TM = main_ref.shape[0]
    W = main_ref.shape[1]
    H = GRID_W
    p = main_ref[...]
    ext = jnp.concatenate([prev_ref[...], p, next_ref[...]], axis=0)
    left = ext[H - 1:H - 1 + TM]
    right = ext[H + 1:H + 1 + TM]
    up = ext[0:TM]
    down = ext[2 * H:2 * H + TM]
    i = lax.broadcasted_iota(I32, (TM, W), 0)
    lane = lax.broadcasted_iota(I32, (TM, W), 1)
    even = (lane & 1) == 0
    c4 = lane & 3
    jl = j - nctx_tiles

    @pl.when(j < nctx_tiles)
    def _():
        lo = jnp.where(j == 0, 1, 0)
        hi = jnp.where(j == nctx_tiles - 1, TM - 1, TM)
        sh_scr[...] = jnp.where(even, jnp.where(i >= lo, left, 0.0), jnp.where(i < hi, right, 0.0))

    @pl.when(j >= nctx_tiles)
    def _():
        col = i & (H - 1)
        up_lo = jnp.where(jl == 0, H, 0)
        down_hi = jnp.where(jl == n_lat_tiles - 1, TM - H, TM)
        l_v = jnp.where(col != 0, left, 0.0)
        r_v = jnp.where(col != H - 1, right, 0.0)
        u_v = jnp.where(i >= up_lo, up, 0.0)
        d_v = jnp.where(i < down_hi, down, 0.0)
        sh_scr[...] = jnp.where(c4 == 0, l_v, jnp.where(c4 == 1, r_v, jnp.where(c4 == 2, u_v, d_v)))

    pm = p + mu_ref[...] * (sh_scr[...] - p)
    r = pm[:, 0:RW_W]
    k = pm[:, RW_W:2 * RW_W]
    v = pm[:, 2 * RW_W:3 * RW_W]
    o = 3 * RW_W
    gl = pm[:, o:o + RW_GATE_LORA]
    o += RW_GATE_LORA
    wl_f = pm[:, o:o + RW_DECAY_LORA]
    wl_b = pm[:, o + RW_DECAY_LORA:o + 2 * RW_DECAY_LORA]
    al = pm[:, o + 2 * RW_DECAY_LORA:o + 2 * RW_DECAY_LORA + RW_AAA_LORA]
    for d, wl in enumerate((wl_f, wl_b)):
        z = w0_ref[d:d + 1, :] + _bdot(jnp.tanh(wl), w2_ref[d])
        w = -(jnp.maximum(-z, 0.0) + jnp.log(1.0 + jnp.exp(-jnp.abs(z)))) - 0.5
        ld_ref[d] = -jnp.exp(w)
    a = jax.nn.sigmoid(a0_ref[...] + _bdot(al, a2_ref[...]))
    g_ref[...] = _bdot(jax.nn.sigmoid(gl), g2_ref[...])
    kk = k * kk_ref[...]
    n2 = _hdot(kk * kk, bd_ref[...])
    kkn = kk / jnp.maximum(jnp.sqrt(n2), 1e-12)
    r_ref[...] = r
    k_ref[...] = k * (1.0 + (a - 1.0) * ka_ref[...])
    v_ref[...] = v
    kkn_ref[...] = kkn
    bb_ref[...] = kkn * a


def _head_sum_matrix(width, hd):
    i = np.arange(width)
    return jnp.asarray((i[:, None] // hd == i[None, :] // hd).astype(np.float32))


def _rw_streams(p_rw, B, S, nctx_tiles, mu, w0, w2, a0, a2, g2, k_k, k_a):
    T = B * S
    J = S // ROW_TILE
    HB = ROW_TILE // GRID_W
    NH = S // GRID_W

    def rows(w):
        return pl.BlockSpec((ROW_TILE, w), lambda b, j: (b * J + j, 0))

    def rows2(w):
        return pl.BlockSpec((2, ROW_TILE, w), lambda b, j: (0, b * J + j, 0))

    def const(shape):
        return pl.BlockSpec(shape, lambda b, j: (0,) * len(shape))

    sd = jax.ShapeDtypeStruct
    kern = functools.partial(_rw_streams_kernel, nctx_tiles=nctx_tiles, n_lat_tiles=J - nctx_tiles)
    return pl.pallas_call(
        kern,
        grid=(B, J),
        in_specs=[pl.BlockSpec((GRID_W, RW_PW), lambda b, j: (b * NH + jnp.maximum(j * HB - 1, 0), 0)),
                  rows(RW_PW),
                  pl.BlockSpec((GRID_W, RW_PW), lambda b, j: (b * NH + jnp.minimum((j + 1) * HB, NH - 1), 0)),
                  const((1, RW_PW)), const((2, RW_W)), const((2, RW_DECAY_LORA, RW_W)), const((1, RW_W)),
                  const((RW_AAA_LORA, RW_W)), const((RW_GATE_LORA, RW_W)), const((1, RW_W)), const((1, RW_W)),
                  const((RW_W, RW_W))],
        out_specs=[rows(RW_W), rows(RW_W), rows(RW_W), rows(RW_W), rows(RW_W), rows2(RW_W), rows(RW_W)],
        out_shape=[sd((T, RW_W), F32)] * 5 + [sd((2, T, RW_W), F32), sd((T, RW_W), F32)],
        scratch_shapes=[pltpu.VMEM((ROW_TILE, RW_PW), F32)],
        compiler_params=_cparams(("parallel", "parallel")),
        name="rwkv_streams",
    )(p_rw, p_rw, p_rw, _rw_reorder(mu).reshape(1, RW_PW), w0, w2, a0.reshape(1, RW_W), a2, g2,
      k_k.reshape(1, RW_W), k_a.reshape(1, RW_W), _head_sum_matrix(RW_W, RW_HEAD_DIM))


def _rwkv_kernel(r_ref, k_ref, v_ref, kk_ref, bb_ref, ld_ref, o_ref, st_ref, *, rev, nchunk):
    t = pl.program_id(1)

    @pl.when(t == 0)
    def _():
        st_ref[...] = jnp.zeros_like(st_ref)

    C = RW_CHUNK
    hd = RW_HEAD_DIM
    NH = RW_HEADS
    row = lax.broadcasted_iota(I32, (C, C), 0)
    col = lax.broadcasted_iota(I32, (C, C), 1)
    incl = ((col >= row) if rev else (col <= row))[None]
    strict = ((col > row) if rev else (col < row))[None]
    tri = jnp.where(incl[0], 1.0, 0.0).astype(F32)
    eye = (row == col)[None]
    last_i = 0 if rev else C - 1
    order = range(nchunk - 1, -1, -1) if rev else range(nchunk)
    n_double = int(math.log2(C)) - 1

    def stack(x):
        return jnp.stack([x[c * C:(c + 1) * C, h * hd:(h + 1) * hd] for c in range(nchunk) for h in range(NH)], 0)

    ld = ld_ref[...]
    g = jnp.concatenate([_hdot(tri, ld[c * C:(c + 1) * C]) for c in range(nchunk)], axis=0)
    g_last = jnp.concatenate([jnp.broadcast_to(g[c * C + last_i:c * C + last_i + 1], (C, NH * hd))
                              for c in range(nchunk)], axis=0)
    k = k_ref[...]
    bb = bb_ref[...]
    eng = jnp.exp(-g)
    e_end = jnp.exp(g_last - g)
    kk_t = stack(kk_ref[...] * jnp.exp(g - ld))
    b_t = stack(bb * eng)
    k_t = stack(k * eng)
    r_t = stack(r_ref[...] * jnp.exp(g))
    b_bar = stack(bb * e_end)
    k_bar = stack(k * e_end)
    dec = stack(jnp.exp(g_last))
    v = stack(v_ref[...])
    m_b = jnp.where(strict, _bmm_nt(kk_t, b_t), 0.0)
    m_k = jnp.where(strict, _bmm_nt(kk_t, k_t), 0.0)
    a_rb = jnp.where(incl, _bmm_nt(r_t, b_t), 0.0)
    a_rk = jnp.where(incl, _bmm_nt(r_t, k_t), 0.0)
    pw = -m_b
    tinv = jnp.where(eye, 1.0, 0.0) + pw
    for _ in range(n_double):
        pw = _bmm(pw, pw)
        tinv = tinv + _bmm(tinv, pw)
    a_t = _bmm(tinv, kk_t)
    u_b = _bmm(tinv, _bmm(m_k, v))
    q_h = r_t - _bmm(a_rb, a_t)
    o_h = _bmm(a_rk, v) - _bmm(a_rb, u_b)
    g_m = jnp.where(eye, dec, 0.0) - _bmm_tn(a_t, b_bar)
    h_m = _bmm_tn(v, k_bar) - _bmm_tn(u_b, b_bar)
    s = st_ref[...]
    for c in order:
        sl = slice(c * NH, (c + 1) * NH)
        o_c = _bmm_nt(q_h[sl], s) + o_h[sl]
        o_ref[c * C:(c + 1) * C, :] = jnp.concatenate([o_c[h] for h in range(NH)], axis=1)
        s = _bmm(s, g_m[sl]) + h_m[sl]
    st_ref[...] = s


def _rwkv_scan(r, k, v, kk, bb, ld, B, S, nctx_blocks, rev):
    T = B * S
    NB = S // ROW_TILE
    d = 1 if rev else 0
    nc = nctx_blocks

    def blk(t):
        if not rev:
            return t
        return jnp.where(t < nc, nc - 1 - t, NB - 1 - (t - nc))

    spec = pl.BlockSpec((ROW_TILE, RW_W), lambda b, t: (b * NB + blk(t), 0))
    kern = functools.partial(_rwkv_kernel, rev=rev, nchunk=ROW_TILE // RW_CHUNK)
    return pl.pallas_call(
        kern,
        grid=(B, NB),
        in_specs=[spec, spec, spec, spec, spec,
                  pl.BlockSpec((None, ROW_TILE, RW_W), lambda b, t: (d, b * NB + blk(t), 0))],
        out_specs=spec,
        out_shape=jax.ShapeDtypeStruct((T, RW_W), F32),
        scratch_shapes=[pltpu.VMEM((RW_HEADS, RW_HEAD_DIM, RW_HEAD_DIM), F32)],
        compiler_params=_cparams(("parallel", "arbitrary")),
        name="rwkv_scan_bwd" if rev else "rwkv_scan_fwd",
    )(r, k, v, kk, bb, ld)


def _filter_mlp_kernel(z_ref, w1_ref, b1_ref, w2_ref, b2_ref, w3_ref, sf_ref, win_ref, f_ref):
    h = jnp.sin(sf_ref[0:1, :] * (_hdot(z_ref[...], w1_ref[...]) + b1_ref[...]))
    h = jnp.sin(sf_ref[1:2, :] * (_hdot(h, w2_ref[...]) + b2_ref[...]))
    f_ref[...] = _hdot(h, w3_ref[...]) * win_ref[...]


def _hyena_filters(L, w1, b1, w2, b2, w3, sin_freq):
    t = np.linspace(0.0, 1.0, L, dtype=np.float32)[:, None]
    bands = (HY_EMB - 1) // 2
    wpos = (2.0 * math.pi * np.arange(L, dtype=np.float32)[:, None] / L).astype(np.float32)
    fr = np.linspace(1e-4, bands - 1, bands, dtype=np.float32)[None, :]
    z = np.concatenate([t, np.cos(fr * wpos), -np.sin(fr * wpos)], -1).astype(np.float32)
    zp = jnp.asarray(np.pad(z, ((0, 0), (0, LANES - HY_EMB))))
    w1p = jnp.pad(w1, ((0, LANES - HY_EMB), (0, 0)))
    max_decay = math.log(HY_TARGET) / HY_FAST_DECAY
    min_decay = math.log(HY_TARGET) / HY_SLOW_DECAY
    deltas = np.linspace(min_decay, max_decay, HY_CH, dtype=np.float32)
    window = np.exp(-t * np.abs(deltas)).astype(np.float32)
    FW = HY_ORDER * 2 * HY_CH
    win = jnp.asarray(np.tile(window, (1, HY_ORDER * 2)))
    TR = min(L, 512)
    Hd = w1.shape[1]
    return pl.pallas_call(
        _filter_mlp_kernel,
        grid=(L // TR,),
        in_specs=[pl.BlockSpec((TR, LANES), lambda i: (i, 0)),
                  pl.BlockSpec((LANES, Hd), lambda i: (0, 0)), pl.BlockSpec((1, Hd), lambda i: (0, 0)),
                  pl.BlockSpec((Hd, Hd), lambda i: (0, 0)), pl.BlockSpec((1, Hd), lambda i: (0, 0)),
                  pl.BlockSpec((Hd, FW), lambda i: (0, 0)), pl.BlockSpec((2, Hd), lambda i: (0, 0)),
                  pl.BlockSpec((TR, FW), lambda i: (i, 0))],
        out_specs=pl.BlockSpec((TR, FW), lambda i: (i, 0)),
        out_shape=jax.ShapeDtypeStruct((L, FW), F32),
        compiler_params=_cparams(("parallel",)),
        name="hyena_filter_mlp",
    )(zp, w1p, b1.reshape(1, Hd), w2, b2.reshape(1, Hd), w3, sin_freq, win)


def _sconv_kernel(p_ref, w_ref, b_ref, o_ref):
    p = p_ref[...]
    L = p.shape[0]
    i = lax.broadcasted_iota(I32, p.shape, 0)
    prev = jnp.where(i == 0, 0.0, pltpu.roll(p, 1, 0))
    nxt = jnp.where(i == L - 1, 0.0, pltpu.roll(p, L - 1, 0))
    o_ref[...] = prev * w_ref[0:1, :] + p * w_ref[1:2, :] + nxt * w_ref[2:3, :] + b_ref[...]


def _short_conv(p_hy, B, S, Lc, conv_w, conv_b):
    L = S - Lc
    p3 = p_hy.reshape(B, S, HY_WIDTH)[:, Lc:, :]
    return pl.pallas_call(
        _sconv_kernel,
        grid=(B, HY_WIDTH // LANES),
        in_specs=[pl.BlockSpec((None, L, LANES), lambda b, c: (b, 0, c)),
                  pl.BlockSpec((HY_SHORT, LANES), lambda b, c: (0, c)),
                  pl.BlockSpec((1, LANES), lambda b, c: (0, c))],
        out_specs=pl.BlockSpec((None, L, LANES), lambda b, c: (b, 0, c)),
        out_shape=jax.ShapeDtypeStruct((B, L, HY_WIDTH), F32),
        compiler_params=_cparams(("parallel", "parallel")),
        name="hyena_short_conv",
    )(p3, conv_w, conv_b.reshape(1, HY_WIDTH))


def _dft_constants(L):
    n = 2 * L
    n2 = FFT_N2
    n1 = n // n2
    na = n1 // 2
    g8 = SUBLANES
    w1 = np.exp(-2j * np.pi * np.outer(np.arange(n1), np.arange(n1)) / n1)
    eye8 = np.eye(g8)

    def kron_fwd(w, real_in):
        k1n, an = w.shape
        blocks = np.stack([np.stack([w.real, -w.imag], 1), np.stack([w.imag, w.real], 1)], 1)
        if real_in:
            blocks = blocks[:, :, 0:1, :]
        m = np.einsum('kria,bc->krbiac', blocks, eye8)
        return m.reshape(k1n * 2 * g8, blocks.shape[2] * an * g8)

    m1 = kron_fwd(w1[:, :na], False)
    m1f = kron_fwd(w1, True)
    cw = np.conj(w1[:, :na]).T / n
    blocks = np.stack([np.stack([cw.real, -cw.imag], 1), np.stack([cw.imag, cw.real], 1)], 0)
    m1inv = np.einsum('iark,bc->iabkrc', blocks, eye8).reshape(2 * na * g8, n1 * 2 * g8)
    w2 = np.exp(-2j * np.pi * np.outer(np.arange(n2), np.arange(n2)) / n2)
    w2big = np.block([[w2.real, -w2.imag], [w2.imag, w2.real]])
    w2c = np.conj(w2)
    iw2big = np.block([[w2c.real, -w2c.imag], [w2c.imag, w2c.real]])
    tw = np.exp(-2j * np.pi * np.arange(n2) / n)
    tw1 = np.stack([np.broadcast_to(tw.real[:, None], (n2, LANES)),
                    np.broadcast_to(tw.imag[:, None], (n2, LANES))], 0)
    c = lambda x, dt: jnp.asarray(np.ascontiguousarray(x), dt)
    return dict(m1=c(m1, BF16), m1f=c(m1f, BF16), m1inv=c(m1inv, BF16), w2=c(w2big, BF16), iw2=c(iw2big, BF16),
                tw1=c(tw1, F32), n1=n1, na=na)


def _cplx_rows(z, twr, twi, conj):
    n2 = z.shape[0] // 2
    zr, zi = z[:n2], z[n2:]
    if conj:
        return jnp.concatenate([zr * twr + zi * twi, zi * twr - zr * twi], axis=0)
    return jnp.concatenate([zr * twr - zi * twi, zi * twr + zr * twi], axis=0)


def _split_dot(m, x):
    hi = x.astype(BF16)
    lo = (x - hi.astype(F32)).astype(BF16)
    return jnp.dot(m, hi, preferred_element_type=F32) + jnp.dot(m, lo, preferred_element_type=F32)


def _next_twiddle(tw, tw1_ref):
    twr, twi = tw
    b_r, b_i = tw1_ref[0], tw1_ref[1]
    return twr * b_r - twi * b_i, twr * b_i + twi * b_r


def _filt_fft_kernel(f_ref, m1_ref, w2_ref, tw1_ref, o_ref, y_scr, *, n1):
    nbg = FFT_N2 // SUBLANES
    C = f_ref.shape[-1]
    for bg in range(nbg):
        yg = _split_dot(m1_ref[...], f_ref[:, bg].reshape(n1 * SUBLANES, C))
        y_scr[:, :, bg] = yg.reshape(n1, 2, SUBLANES, C)

    def body(k1, tw):
        y = _cplx_rows(y_scr[k1].reshape(2 * FFT_N2, C), tw[0], tw[1], False)
        o_ref[k1] = _split_dot(w2_ref[...], y)
        return _next_twiddle(tw, tw1_ref)

    lax.fori_loop(0, n1, body, (jnp.ones((FFT_N2, C), F32), jnp.zeros((FFT_N2, C), F32)))


def _filter_spectrum(filt, L, dc):
    n1 = dc['n1']
    f4 = filt.reshape(L, HY_ORDER, 2, HY_CH)
    h_f, h_b = f4[:, :, 0, :], f4[:, :, 1, :]
    full = jnp.concatenate([h_f, jnp.zeros_like(h_f[:1]), jnp.flip(h_b[1:], 0)], 0)
    full = jnp.transpose(full, (1, 0, 2)).reshape(HY_ORDER, n1, FFT_N2 // SUBLANES, SUBLANES, HY_CH)
    nbg = FFT_N2 // SUBLANES
    return pl.pallas_call(
        functools.partial(_filt_fft_kernel, n1=n1),
        grid=(HY_ORDER, HY_CH // LANES),
        in_specs=[pl.BlockSpec((None, n1, nbg, SUBLANES, LANES), lambda o, c: (o, 0, 0, 0, c)),
                  pl.BlockSpec(dc['m1f'].shape, lambda o, c: (0, 0)),
                  pl.BlockSpec(dc['w2'].shape, lambda o, c: (0, 0)),
                  pl.BlockSpec(dc['tw1'].shape, lambda o, c: (0, 0, 0))],
        out_specs=pl.BlockSpec((None, n1, 2 * FFT_N2, LANES), lambda o, c: (o, 0, 0, c)),
        out_shape=jax.ShapeDtypeStruct((HY_ORDER, n1, 2 * FFT_N2, HY_CH), F32),
        scratch_shapes=[pltpu.VMEM((n1, 2, nbg, SUBLANES, LANES), F32)],
        compiler_params=_cparams(("parallel", "parallel")),
        name="hyena_filter_fft",
    )(full, dc['m1f'], dc['w2'], dc['tw1'])


def _hyconv_kernel(za_ref, zb_ref, ga_ref, gb_ref, ff_ref, bias_ref, m1_ref, m1i_ref, w2_ref, iw2_ref,
                   tw1_ref, oa_ref, ob_ref, y_scr, *, n1, na):
    nbg = FFT_N2 // SUBLANES
    C = za_ref.shape[-1]
    half = na * SUBLANES
    for bg in range(nbg):
        xg = jnp.concatenate([za_ref[:, bg].reshape(half, C), zb_ref[:, bg].reshape(half, C)], axis=0)
        yg = jnp.dot(m1_ref[...], xg.astype(BF16), preferred_element_type=F32)
        y_scr[:, :, bg] = yg.reshape(n1, 2, SUBLANES, C)

    def body(k1, tw):
        y = _cplx_rows(y_scr[k1].reshape(2 * FFT_N2, C), tw[0], tw[1], False)
        z = jnp.dot(w2_ref[...], y.astype(BF16), preferred_element_type=F32)
        f = ff_ref[k1]
        zr, zi = z[:FFT_N2], z[FFT_N2:]
        fr, fi = f[:FFT_N2], f[FFT_N2:]
        p = jnp.concatenate([zr * fr - zi * fi, zr * fi + zi * fr], axis=0)
        v = jnp.dot(iw2_ref[...], p.astype(BF16), preferred_element_type=F32)
        v = _cplx_rows(v, tw[0], tw[1], True)
        y_scr[k1] = v.reshape(2, nbg, SUBLANES, C)
        return _next_twiddle(tw, tw1_ref)

    lax.fori_loop(0, n1, body, (jnp.ones((FFT_N2, C), F32), jnp.zeros((FFT_N2, C), F32)))
    bias = bias_ref[...]
    for bg in range(nbg):
        vg = y_scr[:, :, bg].reshape(n1 * 2 * SUBLANES, C)
        out = jnp.dot(m1i_ref[...], vg.astype(BF16), preferred_element_type=F32)
        ya = out[:half].reshape(na, SUBLANES, C)
        yb = out[half:].reshape(na, SUBLANES, C)
        za = za_ref[:, bg]
        zb = zb_ref[:, bg]
        oa_ref[:, bg] = ga_ref[:, bg] * (ya + za * bias)
        ob_ref[:, bg] = gb_ref[:, bg] * (yb + zb * bias)


def _hyena_conv(z, z_col0, gates, g_col0, spec, bias, dc, B, L):
    n1, na = dc['n1'], dc['na']
    nbg = FFT_N2 // SUBLANES
    NCB = HY_CH // LANES

    def view(a):
        return a.reshape(B, na, nbg, SUBLANES, a.shape[-1])

    def seq(col0, which):
        return pl.BlockSpec((None, na, nbg, SUBLANES, LANES),
                            lambda c, p: (2 * p + which, 0, 0, 0, col0 + c))

    def const(a):
        nd = a.ndim
        return pl.BlockSpec(a.shape, lambda c, p: (0,) * nd)

    out_a, out_b = pl.pallas_call(
        functools.partial(_hyconv_kernel, n1=n1, na=na),
        grid=(NCB, B // 2),
        in_specs=[seq(z_col0, 0), seq(z_col0, 1), seq(g_col0, 0), seq(g_col0, 1),
                  pl.BlockSpec((n1, 2 * FFT_N2, LANES), lambda c, p: (0, 0, c), pipeline_mode=pl.Buffered(1)),
                  pl.BlockSpec((1, LANES), lambda c, p: (0, c)),
                  const(dc['m1']), const(dc['m1inv']), const(dc['w2']), const(dc['iw2']), const(dc['tw1'])],
        out_specs=[pl.BlockSpec((None, na, nbg, SUBLANES, LANES), lambda c, p: (p, 0, 0, 0, c)),
                   pl.BlockSpec((None, na, nbg, SUBLANES, LANES), lambda c, p: (p, 0, 0, 0, c))],
        out_shape=[jax.ShapeDtypeStruct((B // 2, na, nbg, SUBLANES, HY_CH), F32)] * 2,
        scratch_shapes=[pltpu.VMEM((n1, 2, nbg, SUBLANES, LANES), F32)],
        compiler_params=_cparams(("parallel", "arbitrary")),
        name="hyena_long_conv",
    )(view(z), view(z), view(gates), view(gates), spec, bias.reshape(1, HY_CH),
      dc['m1'], dc['m1inv'], dc['w2'], dc['iw2'], dc['tw1'])
    out = jnp.stack([out_a, out_b], axis=1)
    return out.reshape(B, L, HY_CH)


def _out1_kernel(hy_ref, of_ref, ob_ref, r_ref, k_ref, v_ref, g_ref, x_ref, mod_ref, wo_ref,
                 rk_ref, lg_ref, lb_ref, bd_ref, lng_ref, lnb_ref, xl_ref, u_ref, *, alpha):
    o = of_ref[...] + ob_ref[...]
    bd = bd_ref[...]
    inv = 1.0 / RW_HEAD_DIM
    mu = _hdot(o, bd) * inv
    oc = o - mu
    var = _hdot(oc * oc, bd) * inv
    on = oc * lax.rsqrt(var + RW_GN_EPS) * lg_ref[...] + lb_ref[...]
    bonus = _hdot(r_ref[...] * k_ref[...] * rk_ref[...], bd) * v_ref[...]
    y = jnp.concatenate([hy_ref[...], (on + bonus) * g_ref[...]], axis=1)
    _post_mix(y, x_ref[...], mod_ref, wo_ref, lng_ref, lnb_ref, alpha, xl_ref, u_ref)


def _readout1(hy, o_f, o_b, r, k, v, g, xcat, mods, B, S, Lc, w_out, r_k, ln_g, ln_b, dn_g, dn_b, alpha):
    D = xcat.shape[-1]
    L = S - Lc
    J = S // ROW_TILE
    JL = L // ROW_TILE
    JC = Lc // ROW_TILE

    def cat_rows(w):
        return pl.BlockSpec((ROW_TILE, w), lambda b, j: (b * J + JC + j, 0))

    def lat_rows(w):
        return pl.BlockSpec((ROW_TILE, w), lambda b, j: (b * JL + j, 0))

    def const(shape):
        return pl.BlockSpec(shape, lambda b, j: (0,) * len(shape))

    sd = jax.ShapeDtypeStruct
    return pl.pallas_call(
        functools.partial(_out1_kernel, alpha=alpha),
        grid=(B, JL),
        in_specs=[lat_rows(HY_CH), cat_rows(RW_W), cat_rows(RW_W), cat_rows(RW_W), cat_rows(RW_W), cat_rows(RW_W),
                  cat_rows(RW_W), cat_rows(D),
                  pl.BlockSpec((None, 6, D), lambda b, j: (b, 0, 0)),
                  const((HY_CH + RW_W, D)), const((1, RW_W)), const((1, RW_W)), const((1, RW_W)),
                  const((RW_W, RW_W)), const((1, D)), const((1, D))],
        out_specs=[lat_rows(D), lat_rows(D)],
        out_shape=[sd((B * L, D), F32), sd((B * L, D), F32)],
        compiler_params=_cparams(("parallel", "parallel")),
        name="readout1",
    )(hy, o_f, o_b, r, k, v, g, xcat.reshape(B * S, D), mods, w_out.astype(BF16),
      r_k.reshape(1, RW_W), ln_g.reshape(1, RW_W), ln_b.reshape(1, RW_W),
      _head_sum_matrix(RW_W, RW_HEAD_DIM), dn_g.reshape(1, D), dn_b.reshape(1, D))


def kernel(x, c, ctx, c_ctx, mod_w, mod_b, ln1_g, ln1_b, ln2_g, ln2_b, ev_w_in, ev_w_out, gla_gate_w2, gla_gate_b, gla_norm_g, hg_lb_logits, hg_norm_g, od_w_in, od_w_out, hy_conv_w, hy_conv_b, hy_ffn_w1, hy_ffn_b1, hy_ffn_w2, hy_ffn_b2, hy_ffn_w3, hy_sin_freq, hy_bias, rw_mu, rw_w0, rw_w2, rw_a0, rw_a2, rw_g2, rw_k_k, rw_k_a, rw_r_k, rw_ln_g, rw_ln_b, router_w, router_bias, exp_w13, exp_w2, sh_w13, sh_w2):
    B, L, D = x.shape
    Lc = ctx.shape[1]
    S = Lc + L
    depth = mod_w.shape[0]
    assert depth == 2 and L % ROW_TILE == 0 and Lc % ROW_TILE == 0 and B % 2 == 0
    assert L % (FFT_N2 * 2) == 0 and L % GRID_W == 0
    alpha = (2 * depth) ** 0.25
    nctx = Lc // ROW_TILE
    J = S // ROW_TILE
    T = B * S

    cc = jnp.concatenate([c, c_ctx[None, :]], axis=0)
    cc = jnp.pad(cc, ((0, (-cc.shape[0]) % SUBLANES), (0, 0)))
    hg_lb = jnp.cumsum(jax.nn.softmax(hg_lb_logits.astype(F32), axis=0), axis=0)
    xcat = jnp.concatenate([ctx, x], axis=1)

    mods = _modulation(cc, mod_w[0], mod_b[0])
    gq, gk, gv, gla, r, hq, hk, hla, hv, hgate = _in_proj0(xcat, mods, nctx, ev_w_in[0], gla_gate_w2[0],
                                                            gla_gate_b[0], hg_lb[0])
    gk3 = gk.reshape(1, T, GLA_QK)
    o_gf = _gated_recurrence(gq, gk3, gv, gla, 0, B, S, nctx, False, GLA_HEADS, GLA_DK, GLA_DV)
    o_gb = _gated_recurrence(gq, gk3, gv, gla, 0, B, S, nctx, True, GLA_HEADS, GLA_DK, GLA_DV)
    o_hf = _gated_recurrence(hq, hk, hv, hla, 0, B, S, nctx, False, HG_HEADS, HG_EXPAND, HG_EXPAND)
    o_hb = _gated_recurrence(hq, hk, hv, hla, 1, B, S, nctx, True, HG_HEADS, HG_EXPAND, HG_EXPAND)
    xl, u = _readout0(o_gf, o_gb, o_hf, o_hb, r, hgate, xcat, mods, nctx, ev_w_out[0], gla_norm_g[0],
                      hg_norm_g[0], ln1_g[0], ln1_b[0], alpha)
    tiles_per_b = S // CMB_TILE
    tile_in_b = jnp.arange(B * tiles_per_b, dtype=I32) % tiles_per_b
    mod_of_tile = jnp.where(tile_in_b < Lc // CMB_TILE, B, jnp.arange(B * tiles_per_b, dtype=I32) // tiles_per_b)
    x1 = _moe_block(u, xl, mods, mod_of_tile.astype(I32), router_w[0], router_bias[0], exp_w13, exp_w2, 0,
                    sh_w13[0], sh_w2[0], ln2_g[0], ln2_b[0], alpha)
    xcat = x1.reshape(B, S, D)

    mods = _modulation(cc, mod_w[1], mod_b[1])
    p_hy, p_rw = _in_proj1(xcat, mods, nctx, od_w_in[0])
    rr, rk, rv, rkk, rbb, rld, rg = _rw_streams(p_rw, B, S, nctx, rw_mu[0], rw_w0[0], rw_w2[0], rw_a0[0],
                                                 rw_a2[0], rw_g2[0], rw_k_k[0], rw_k_a[0])
    o_f = _rwkv_scan(rr, rk, rv, rkk, rbb, rld, B, S, nctx, False)
    o_b = _rwkv_scan(rr, rk, rv, rkk, rbb, rld, B, S, nctx, True)
    dc = _dft_constants(L)
    filt = _hyena_filters(L, hy_ffn_w1[0], hy_ffn_b1[0], hy_ffn_w2[0], hy_ffn_b2[0], hy_ffn_w3[0], hy_sin_freq[0])
    spec = _filter_spectrum(filt, L, dc)
    uu = _short_conv(p_hy, B, S, Lc, hy_conv_w[0], hy_conv_b[0])
    NCB = HY_CH // LANES
    z1 = _hyena_conv(uu, 0, uu, NCB, spec[0], hy_bias[0, 0], dc, B, L)
    z2 = _hyena_conv(z1, 0, uu, 2 * NCB, spec[1], hy_bias[0, 1], dc, B, L)
    xl, u = _readout1(z2.reshape(B * L, HY_CH), o_f, o_b, rr, rk, rv, rg, xcat, mods, B, S, Lc, od_w_out[0],
                      rw_r_k[0], rw_ln_g[0], rw_ln_b[0], ln1_g[1], ln1_b[1], alpha)
    mod_of_tile = (jnp.arange(B * (L // CMB_TILE), dtype=I32) // (L // CMB_TILE)).astype(I32)
    out = _moe_block(u, xl, mods, mod_of_tile, router_w[1], router_bias[1], exp_w13, exp_w2, 1,
                     sh_w13[1], sh_w2[1], ln2_g[1], ln2_b[1], alpha)
    return out.reshape(B, L, D)
```

```python
import functools
import math

import numpy as np
import jax
import jax.numpy as jnp
from jax import lax
from jax.experimental import pallas as pl
from jax.experimental.pallas import tpu as pltpu

F32 = jnp.float32
BF16 = jnp.bfloat16
I32 = jnp.int32
U32 = jnp.uint32
HI = lax.Precision.HIGHEST

LN_EPS = 1e-5
GLA_HEADS, GLA_DK, GLA_DV = 4, 64, 128
GLA_QK, GLA_V = GLA_HEADS * GLA_DK, GLA_HEADS * GLA_DV
GLA_GATE_RANK = 16
GLA_GATE_NORM = 16.0
CHUNK = 64
HG_HEADS, HG_EXPAND = 4, 128
HG_W = HG_HEADS * HG_EXPAND
HY_CH, HY_ORDER, HY_SHORT, HY_EMB = 512, 2, 3, 33
HY_WIDTH = (HY_ORDER + 1) * HY_CH
HY_FAST_DECAY, HY_SLOW_DECAY, HY_TARGET = 0.3, 1.5, 1e-2
RW_HEADS, RW_HEAD_DIM = 8, 64
RW_W = RW_HEADS * RW_HEAD_DIM
RW_DECAY_LORA, RW_AAA_LORA, RW_GATE_LORA = 64, 64, 128
RW_GN_EPS = 64e-5
GRID_W = 64
N_EXPERTS, TOP_K, N_GROUPS, TOPK_GROUPS = 256, 8, 8, 4
PER_GROUP = N_EXPERTS // N_GROUPS
EXPERT_FF, SHARED_FF = 256, 256
ROUTED_SCALE = 2.5

LANES = 128
SUBLANES = 8
VMEM_LIMIT = 56 * 1024 * 1024
ROW_TILE = 256
MOE_BLK = 256
CMB_TILE = 128
RW_CHUNK = 64
FFT_N2 = 128


def _cparams(sem):
    return pltpu.CompilerParams(dimension_semantics=sem, vmem_limit_bytes=VMEM_LIMIT)


def _bdot(a, b):
    return jnp.dot(a.astype(BF16), b.astype(BF16), preferred_element_type=F32)


def _bdot_nt(a, b):
    return lax.dot_general(a.astype(BF16), b.astype(BF16), (((1,), (1,)), ((), ())),
                           preferred_element_type=F32)


def _bdot_tn(a, b):
    return lax.dot_general(a.astype(BF16), b.astype(BF16), (((0,), (0,)), ((), ())),
                           preferred_element_type=F32)


def _bmm(a, b):
    return jnp.einsum('nij,njk->nik', a.astype(BF16), b.astype(BF16), preferred_element_type=F32)


def _bmm_nt(a, b):
    return jnp.einsum('nik,njk->nij', a.astype(BF16), b.astype(BF16), preferred_element_type=F32)


def _bmm_tn(a, b):
    return _bmm(jnp.swapaxes(a, 1, 2), b)


def _hdot(a, b):
    return jnp.dot(a, b, precision=HI, preferred_element_type=F32)


def _silu(x):
    return x * jax.nn.sigmoid(x)


def _layer_norm_rows(x, g, b, eps):
    mu = jnp.mean(x, axis=-1, keepdims=True)
    xc = x - mu
    var = jnp.mean(xc * xc, axis=-1, keepdims=True)
    return xc * lax.rsqrt(var + eps) * g + b


def _mod_kernel(c_ref, w_ref, b_ref, o_ref):
    o_ref[...] = _hdot(_silu(c_ref[...]), w_ref[...]) + b_ref[...]


def _modulation(cc, w, b):
    R, D = cc.shape
    out = pl.pallas_call(
        _mod_kernel,
        grid=(6,),
        in_specs=[pl.BlockSpec((R, D), lambda j: (0, 0)),
                  pl.BlockSpec((D, D), lambda j: (0, j)),
                  pl.BlockSpec((1, D), lambda j: (0, j))],
        out_specs=pl.BlockSpec((R, D), lambda j: (0, j)),
        out_shape=jax.ShapeDtypeStruct((R, 6 * D), F32),
        compiler_params=_cparams(("parallel",)),
        name="modulation",
    )(cc, w, b.reshape(1, 6 * D))
    return out.reshape(R, 6, D)


def _in0_kernel(x_ref, mod_ref, w_ref, wa_ref, w2_ref, gb_ref, lb_ref,
                gq_ref, gk_ref, gv_ref, gla_ref, r_ref, hq_ref, hk_ref, hla_ref, hv_ref, hg_ref):
    sh1 = mod_ref[0:1, :]
    sc1 = mod_ref[1:2, :]
    h = (x_ref[...] * (1.0 + sc1) + sh1).astype(BF16)

    def proj(off, width):
        return jnp.dot(h, w_ref[:, off:off + width], preferred_element_type=F32)

    gq_ref[...] = proj(0, GLA_QK) * (GLA_DK ** -0.5)
    gk_ref[...] = proj(GLA_QK, GLA_QK)
    gv_ref[...] = proj(2 * GLA_QK, GLA_V)
    r_ref[...] = proj(2 * GLA_QK + GLA_V, GLA_V)
    base = 2 * GLA_QK + 2 * GLA_V
    a = jnp.dot(h, wa_ref[...], preferred_element_type=F32)
    z = _bdot(a, w2_ref[...]) + gb_ref[...]
    ls = (jnp.minimum(z, 0.0) - jnp.log(1.0 + jnp.exp(-jnp.abs(z)))) * (1.0 / GLA_GATE_NORM)
    gla_ref[0] = ls[:, :GLA_QK]
    gla_ref[1] = ls[:, GLA_QK:]
    hq_ref[...] = _silu(proj(base, HG_W))
    for d in range(2):
        zf = proj(base + (1 + d) * HG_W, HG_W)
        lb = lb_ref[d:d + 1, :]
        f = lb + (1.0 - lb) * jax.nn.sigmoid(zf)
        hk_ref[d] = 1.0 - f
        hla_ref[d] = jnp.log(f)
    hv_ref[...] = proj(base + 3 * HG_W, HG_W)
    hg_ref[...] = proj(base + 4 * HG_W, HG_W)


def _in_proj0(xcat, mods, nctx_tiles, w_in, gate_w2, gate_b, lb):
    B, S, D = xcat.shape
    J = S // ROW_TILE
    T = B * S
    a_off = 2 * GLA_QK + 2 * GLA_V
    wmain = jnp.concatenate([w_in[:, :a_off], w_in[:, a_off + 2 * GLA_GATE_RANK:]], axis=1).astype(BF16)
    wa = jnp.pad(w_in[:, a_off:a_off + 2 * GLA_GATE_RANK], ((0, 0), (0, LANES - 2 * GLA_GATE_RANK))).astype(BF16)
    w2 = jnp.zeros((LANES, 2 * GLA_QK), F32)
    w2 = w2.at[:GLA_GATE_RANK, :GLA_QK].set(gate_w2[0]).at[GLA_GATE_RANK:2 * GLA_GATE_RANK, GLA_QK:].set(gate_w2[1])
    gb = gate_b.reshape(1, 2 * GLA_QK)
    WM = wmain.shape[1]

    def rows(w):
        return pl.BlockSpec((ROW_TILE, w), lambda b, j: (b * J + j, 0))

    def rows2(w):
        return pl.BlockSpec((2, ROW_TILE, w), lambda b, j: (0, b * J + j, 0))

    def const(shape):
        return pl.BlockSpec(shape, lambda b, j: (0,) * len(shape))

    sd = jax.ShapeDtypeStruct
    outs = pl.pallas_call(
        _in0_kernel,
        grid=(B, J),
        in_specs=[rows(D),
                  pl.BlockSpec((None, 6, D), lambda b, j: (jnp.where(j < nctx_tiles, B, b), 0, 0)),
                  const((D, WM)), const((D, LANES)), const((LANES, 2 * GLA_QK)), const((1, 2 * GLA_QK)),
                  const((2, HG_W))],
        out_specs=[rows(GLA_QK), rows(GLA_QK), rows(GLA_V), rows2(GLA_QK), rows(GLA_V),
                   rows(HG_W), rows2(HG_W), rows2(HG_W), rows(HG_W), rows(HG_W)],
        out_shape=[sd((T, GLA_QK), F32), sd((T, GLA_QK), F32), sd((T, GLA_V), F32), sd((2, T, GLA_QK), F32),
                   sd((T, GLA_V), F32), sd((T, HG_W), F32), sd((2, T, HG_W), F32), sd((2, T, HG_W), F32),
                   sd((T, HG_W), F32), sd((T, HG_W), F32)],
        compiler_params=_cparams(("parallel", "parallel")),
        name="in_proj0",
    )(xcat.reshape(T, D), mods, wmain, wa, w2, gb, lb)
    return outs


def _rec_kernel(q_ref, k_ref, v_ref, la_ref, o_ref, st_ref, *, rev, nh, dk, dv, nchunk):
    t = pl.program_id(1)

    @pl.when(t == 0)
    def _():
        st_ref[...] = jnp.zeros_like(st_ref)

    C = CHUNK
    row = lax.broadcasted_iota(I32, (C, C), 0)
    col = lax.broadcasted_iota(I32, (C, C), 1)
    incl = (col >= row) if rev else (col <= row)
    tri = jnp.where(incl, 1.0, 0.0).astype(F32)
    ref_i = C // 2 - 1 if rev else C // 2
    last_i = 0 if rev else C - 1
    order = range(nchunk - 1, -1, -1) if rev else range(nchunk)

    def stack(x, hd):
        return jnp.stack([x[c * C:(c + 1) * C, h * hd:(h + 1) * hd] for c in range(nchunk) for h in range(nh)], 0)

    def rows(x, i):
        return jnp.concatenate([jnp.broadcast_to(x[c * C + i:c * C + i + 1], (C, x.shape[1]))
                                for c in range(nchunk)], axis=0)

    la = la_ref[...]
    q = q_ref[...]
    k = k_ref[...]
    b = jnp.concatenate([_hdot(tri, la[c * C:(c + 1) * C]) for c in range(nchunk)], axis=0)
    b_mid = rows(b, ref_i)
    b_last = rows(b, last_i)
    v = stack(v_ref[...], dv)
    sc = _bmm_nt(stack(q * jnp.exp(b - b_mid), dk), stack(k * jnp.exp(b_mid - b), dk))
    o_intra = _bmm(jnp.where(incl[None], sc, 0.0), v)
    q_in = stack(q * jnp.exp(b), dk)
    kv_t = _bmm_tn(v, stack(k * jnp.exp(b_last - b), dk))
    dec = stack(jnp.exp(b_last), dk)[:, 0:1, :]
    s_t = st_ref[...]
    for c in order:
        sl = slice(c * nh, (c + 1) * nh)
        o_c = o_intra[sl] + _bmm_nt(q_in[sl], s_t)
        o_ref[c * C:(c + 1) * C, :] = jnp.concatenate([o_c[h] for h in range(nh)], axis=1)
        s_t = s_t * dec[sl] + kv_t[sl]
    st_ref[...] = s_t


def _gated_recurrence(q, k, v, la, kdir, B, S, nctx_blocks, rev, nh, dk, dv):
    T = B * S
    NB = S // ROW_TILE
    wk, wv = nh * dk, nh * dv
    assert q.shape[1] == wk
    d = 1 if rev else 0
    nc = nctx_blocks

    def blk(t):
        if not rev:
            return t
        return jnp.where(t < nc, nc - 1 - t, NB - 1 - (t - nc))

    kern = functools.partial(_rec_kernel, rev=rev, nh=nh, dk=dk, dv=dv, nchunk=ROW_TILE // CHUNK)
    return pl.pallas_call(
        kern,
        grid=(B, NB),
        in_specs=[pl.BlockSpec((ROW_TILE, wk), lambda b, t: (b * NB + blk(t), 0)),
                  pl.BlockSpec((None, ROW_TILE, wk), lambda b, t: (kdir, b * NB + blk(t), 0)),
                  pl.BlockSpec((ROW_TILE, wv), lambda b, t: (b * NB + blk(t), 0)),
                  pl.BlockSpec((None, ROW_TILE, wk), lambda b, t: (d, b * NB + blk(t), 0))],
        out_specs=pl.BlockSpec((ROW_TILE, wv), lambda b, t: (b * NB + blk(t), 0)),
        out_shape=jax.ShapeDtypeStruct((T, wv), F32),
        scratch_shapes=[pltpu.VMEM((nh, dv, dk), F32)],
        compiler_params=_cparams(("parallel", "arbitrary")),
        name="gated_rec_bwd" if rev else "gated_rec_fwd",
    )(q, k, v, la)


def _pack_rows(x):
    w = x.shape[1] // 2
    lo = lax.bitcast_convert_type(x[:, :w].astype(BF16).astype(F32), U32)
    hi = lax.bitcast_convert_type(x[:, w:].astype(BF16).astype(F32), U32)
    return (lo >> 16) | (hi & jnp.uint32(0xFFFF0000))


def _unpack_rows(p):
    lo = lax.bitcast_convert_type(p << 16, F32)
    hi = lax.bitcast_convert_type(p & jnp.uint32(0xFFFF0000), F32)
    return lo, hi


def _post_mix(y, x, mod_ref, wo_ref, lng_ref, lnb_ref, alpha, xl_ref, u_ref, up_ref):
    g1 = mod_ref[2:3, :]
    sh2 = mod_ref[3:4, :]
    sc2 = mod_ref[4:5, :]
    yo = jnp.dot(y.astype(BF16), wo_ref[...], preferred_element_type=F32)
    xl = _layer_norm_rows(alpha * x + g1 * yo, lng_ref[...], lnb_ref[...], LN_EPS)
    xl_ref[...] = xl
    u = xl * (1.0 + sc2) + sh2
    u_ref[...] = u
    up_ref[...] = _pack_rows(u)


def _out0_kernel(gf_ref, gb_ref, hf_ref, hb_ref, r_ref, hg_ref, x_ref, mod_ref, wo_ref,
                 gg_ref, hgg_ref, lng_ref, lnb_ref, xl_ref, u_ref, up_ref, *, alpha):
    def heads(o, g, gate):
        parts = []
        for hh in range(o.shape[1] // LANES):
            seg = o[:, hh * LANES:(hh + 1) * LANES]
            ms = jnp.mean(seg * seg, axis=-1, keepdims=True)
            parts.append(seg * lax.rsqrt(ms + 1e-6) * g)
        return jnp.concatenate(parts, axis=1) * _silu(gate)

    y = jnp.concatenate([heads(gf_ref[...] + gb_ref[...], gg_ref[...], r_ref[...]),
                         heads(hf_ref[...] + hb_ref[...], hgg_ref[...], hg_ref[...])], axis=1)
    _post_mix(y, x_ref[...], mod_ref, wo_ref, lng_ref, lnb_ref, alpha, xl_ref, u_ref, up_ref)


def _readout0(o_gf, o_gb, o_hf, o_hb, r, hgate, xcat, mods, nctx_tiles, w_out, gla_g, hg_g, ln_g, ln_b, alpha):
    B, S, D = xcat.shape
    J = S // ROW_TILE
    T = B * S

    def rows(w):
        return pl.BlockSpec((ROW_TILE, w), lambda b, j: (b * J + j, 0))

    def const(shape):
        return pl.BlockSpec(shape, lambda b, j: (0,) * len(shape))

    sd = jax.ShapeDtypeStruct
    return pl.pallas_call(
        functools.partial(_out0_kernel, alpha=alpha),
        grid=(B, J),
        in_specs=[rows(GLA_V), rows(GLA_V), rows(HG_W), rows(HG_W), rows(GLA_V), rows(HG_W), rows(D),
                  pl.BlockSpec((None, 6, D), lambda b, j: (jnp.where(j < nctx_tiles, B, b), 0, 0)),
                  const((GLA_V + HG_W, D)), const((1, GLA_DV)), const((1, HG_EXPAND)), const((1, D)), const((1, D))],
        out_specs=[rows(D), rows(D), rows(D // 2)],
        out_shape=[sd((T, D), F32), sd((T, D), F32), sd((T, D // 2), U32)],
        compiler_params=_cparams(("parallel", "parallel")),
        name="readout0",
    )(o_gf, o_gb, o_hf, o_hb, r, hgate, xcat.reshape(T, D), mods, w_out.astype(BF16),
      gla_g.reshape(1, -1), hg_g.reshape(1, -1), ln_g.reshape(1, D), ln_b.reshape(1, D))


def _route_kernel(u_ref, rwt_ref, rb_ref, tri_ref, eidx_ref, gate_ref, rank_ref, cnt_ref, carry_ref):
    i = pl.program_id(0)

    @pl.when(i == 0)
    def _():
        carry_ref[...] = jnp.zeros_like(carry_ref)

    u = u_ref[...]
    TM = u.shape[0]
    E = N_EXPERTS
    logits = lax.dot_general(rwt_ref[...], u, (((1,), (1,)), ((), ())), precision=HI,
                             preferred_element_type=F32)
    scores = jax.nn.sigmoid(logits)
    sel = scores + rb_ref[...]
    neg = -jnp.inf
    sel3 = sel.reshape(N_GROUPS, PER_GROUP, TM)
    io_g = lax.broadcasted_iota(I32, sel3.shape, 1)
    m1 = jnp.max(sel3, axis=1, keepdims=True)
    i1 = jnp.min(jnp.where(sel3 == m1, io_g, PER_GROUP), axis=1, keepdims=True)
    m2 = jnp.max(jnp.where(io_g == i1, neg, sel3), axis=1, keepdims=True)
    grp = m1 + m2
    io_n = lax.broadcasted_iota(I32, grp.shape, 0)
    keep = jnp.zeros(grp.shape, jnp.bool_)
    for _ in range(TOPK_GROUPS):
        m = jnp.max(grp, axis=0, keepdims=True)
        idx = jnp.min(jnp.where(grp == m, io_n, N_GROUPS), axis=0, keepdims=True)
        hit = io_n == idx
        keep = jnp.logical_or(keep, hit)
        grp = jnp.where(hit, neg, grp)
    sel = jnp.where(keep, sel3, neg).reshape(E, TM)
    io_e = lax.broadcasted_iota(I32, (E, TM), 0)
    base = carry_ref[...]
    tri = tri_ref[...]
    e_rows, g_rows, r_rows = [], [], []
    for _ in range(TOP_K):
        m = jnp.max(sel, axis=0, keepdims=True)
        idx = jnp.min(jnp.where(sel == m, io_e, E), axis=0, keepdims=True)
        hit = io_e == idx
        hit_f = jnp.where(hit, 1.0, 0.0)
        g_rows.append(jnp.sum(jnp.where(hit, scores, 0.0), axis=0, keepdims=True))
        prefix = jnp.dot(hit_f.astype(BF16), tri, preferred_element_type=F32)
        r_rows.append(jnp.sum(jnp.where(hit, prefix + base, 0.0), axis=0, keepdims=True))
        base = base + jnp.sum(hit_f, axis=1, keepdims=True)
        e_rows.append(idx)
        sel = jnp.where(hit, neg, sel)
    carry_ref[...] = base
    cnt_ref[...] = base
    g = jnp.concatenate(g_rows, axis=0)
    gate_ref[...] = g / jnp.sum(g, axis=0, keepdims=True) * ROUTED_SCALE
    eidx_ref[...] = jnp.concatenate(e_rows, axis=0)
    rank_ref[...] = jnp.concatenate(r_rows, axis=0).astype(I32)


def _route(u, router_w, router_bias):
    T, D = u.shape
    E = N_EXPERTS
    n = T // ROW_TILE
    tri = jnp.asarray(np.triu(np.ones((ROW_TILE, ROW_TILE), np.float32), 1), BF16)
    sd = jax.ShapeDtypeStruct
    cols = pl.BlockSpec((TOP_K, ROW_TILE), lambda i: (0, i))
    return pl.pallas_call(
        _route_kernel,
        grid=(n,),
        in_specs=[pl.BlockSpec((ROW_TILE, D), lambda i: (i, 0)),
                  pl.BlockSpec((E, D), lambda i: (0, 0)),
                  pl.BlockSpec((E, 1), lambda i: (0, 0)),
                  pl.BlockSpec((ROW_TILE, ROW_TILE), lambda i: (0, 0))],
        out_specs=[cols, cols, cols, pl.BlockSpec((E, 1), lambda i: (0, 0))],
        out_shape=[sd((TOP_K, T), I32), sd((TOP_K, T), F32), sd((TOP_K, T), I32), sd((E, 1), F32)],
        scratch_shapes=[pltpu.VMEM((E, 1), F32)],
        compiler_params=_cparams(("arbitrary",)),
        name="moe_route",
    )(u, router_w.T, router_bias.reshape(E, 1), tri)


def _dest_kernel(e_ref, r_ref, ps_ref, d_ref):
    e = e_ref[...]
    TM = e.shape[1]
    io_e = lax.broadcasted_iota(I32, (N_EXPERTS, TM), 0)
    ps = ps_ref[...]
    rows = []
    for k in range(TOP_K):
        rows.append(jnp.sum(jnp.where(io_e == e[k:k + 1, :], ps, 0.0), axis=0, keepdims=True))
    d_ref[...] = jnp.concatenate(rows, axis=0).astype(I32) + r_ref[...]


def _slot_of(eidx, rank, pad_start):
    T = eidx.shape[1]
    cols = pl.BlockSpec((TOP_K, ROW_TILE), lambda i: (0, i))
    return pl.pallas_call(
        _dest_kernel,
        grid=(T // ROW_TILE,),
        in_specs=[cols, cols, pl.BlockSpec((N_EXPERTS, 1), lambda i: (0, 0))],
        out_specs=cols,
        out_shape=jax.ShapeDtypeStruct((TOP_K, T), I32),
        compiler_params=_cparams(("parallel",)),
        name="moe_slot",
    )(eidx, rank, pad_start.astype(F32).reshape(N_EXPERTS, 1))


def _dispatch_kernel(dest_ref, u_ref, xs_in_ref, xs_ref, sem):
    del xs_in_ref
    TM = u_ref.shape[0]

    def copy(t, k):
        return pltpu.make_async_copy(u_ref.at[pl.ds(t, 1)], xs_ref.at[pl.ds(dest_ref[k, t], 1)], sem)

    def issue(t, c):
        for k in range(TOP_K):
            copy(t, k).start()
        return c

    lax.fori_loop(0, TM, issue, 0)
    for k in range(TOP_K):
        pltpu.make_async_copy(u_ref, xs_ref.at[pl.ds(0, TM)], sem).wait()


def _dispatch(up, dest, n_slots):
    T, W = up.shape
    xs0 = jnp.zeros((n_slots, W), U32)
    return pl.pallas_call(
        _dispatch_kernel,
        grid=(T // ROW_TILE,),
        in_specs=[pl.BlockSpec((TOP_K, ROW_TILE), lambda i: (0, i), memory_space=pltpu.SMEM),
                  pl.BlockSpec((ROW_TILE, W), lambda i: (i, 0)),
                  pl.BlockSpec(memory_space=pl.ANY)],
        out_specs=pl.BlockSpec(memory_space=pl.ANY),
        out_shape=jax.ShapeDtypeStruct((n_slots, W), U32),
        scratch_shapes=[pltpu.SemaphoreType.DMA],
        input_output_aliases={2: 0},
        compiler_params=_cparams(("arbitrary",)),
        name="moe_dispatch",
    )(dest, up, xs0)


def _expert_kernel(be_ref, nu_ref, xs_ref, w13_ref, w2_ref, ys_ref):
    del be_ref

    @pl.when(pl.program_id(0) < nu_ref[0])
    def _():
        lo, hi = _unpack_rows(xs_ref[...])
        h = _bdot(jnp.concatenate([lo, hi], axis=1), w13_ref[...])
        a = _silu(h[:, :EXPERT_FF]) * h[:, EXPERT_FF:]
        ys_ref[...] = _pack_rows(_bdot(a, w2_ref[...]))


def _expert_gemm(xs, blk_exp, n_used, w13, w2, layer):
    NP, W = xs.shape
    D = 2 * W
    nblk = NP // MOE_BLK
    F2 = w13.shape[3]

    def xmap(i, be, nu):
        return (jnp.minimum(i, nu[0] - 1), 0)

    return pl.pallas_call(
        _expert_kernel,
        grid_spec=pltpu.PrefetchScalarGridSpec(
            num_scalar_prefetch=2,
            grid=(nblk,),
            in_specs=[pl.BlockSpec((MOE_BLK, W), xmap),
                      pl.BlockSpec((None, None, D, F2), lambda i, be, nu: (layer, be[i], 0, 0)),
                      pl.BlockSpec((None, None, F2 // 2, D), lambda i, be, nu: (layer, be[i], 0, 0))],
            out_specs=pl.BlockSpec((MOE_BLK, W), xmap)),
        out_shape=jax.ShapeDtypeStruct((NP, W), U32),
        compiler_params=_cparams(("arbitrary",)),
        name="moe_experts",
    )(blk_exp, n_used, xs, w13, w2)


def _combine_kernel(mt_ref, dcur_ref, dnxt_ref, gate_ref, up_ref, xl_ref, mod_ref, s13_ref, s2_ref, lng_ref, lnb_ref,
                    ys_ref, o_ref, rows_a, rows_b, sem_a, sem_b, *, alpha):
    del mt_ref
    i = pl.program_id(0)
    H = CMB_TILE
    W = up_ref.shape[1]

    def issue(dref, col0, rows, sem):
        for t in range(H):
            for k in range(TOP_K):
                pltpu.make_async_copy(ys_ref.at[pl.ds(dref[k, col0 + t], 1)], rows.at[k, pl.ds(t, 1)], sem).start()

    def wait(rows, sem):
        for k in range(TOP_K):
            pltpu.make_async_copy(ys_ref.at[pl.ds(0, H)], rows.at[k], sem).wait()

    def compute(rows, r0):
        lo, hi = _unpack_rows(up_ref[r0:r0 + H])
        hs = _bdot(jnp.concatenate([lo, hi], axis=1), s13_ref[...])
        sh = _bdot(_silu(hs[:, :SHARED_FF]) * hs[:, SHARED_FF:], s2_ref[...])
        acc_lo, acc_hi = sh[:, :W], sh[:, W:]
        gate = gate_ref[r0:r0 + H]
        for k in range(TOP_K):
            lo, hi = _unpack_rows(rows[k])
            g = gate[:, k:k + 1]
            acc_lo = acc_lo + lo * g
            acc_hi = acc_hi + hi * g
        acc = jnp.concatenate([acc_lo, acc_hi], axis=1)
        g2 = mod_ref[5:6, :]
        o_ref[r0:r0 + H] = _layer_norm_rows(alpha * xl_ref[r0:r0 + H] + g2 * acc, lng_ref[...], lnb_ref[...], LN_EPS)

    @pl.when(i == 0)
    def _():
        issue(dcur_ref, 0, rows_a, sem_a)

    wait(rows_a, sem_a)
    issue(dcur_ref, H, rows_b, sem_b)
    compute(rows_a, 0)
    wait(rows_b, sem_b)
    issue(dnxt_ref, 0, rows_a, sem_a)
    compute(rows_b, H)

    @pl.when(i == pl.num_programs(0) - 1)
    def _():
        wait(rows_a, sem_a)


def _combine(dest, gate_t, up, xl, ys, mods, mod_of_tile, sh_w13, sh_w2, ln_g, ln_b, alpha):
    T, D = xl.shape
    W = D // 2
    TM = 2 * CMB_TILE
    n = T // TM

    def rows(w):
        return pl.BlockSpec((TM, w), lambda i, mt: (i, 0))

    def const(shape):
        return pl.BlockSpec(shape, lambda i, mt: (0,) * len(shape))

    return pl.pallas_call(
        functools.partial(_combine_kernel, alpha=alpha),
        grid_spec=pltpu.PrefetchScalarGridSpec(
            num_scalar_prefetch=1,
            grid=(n,),
            in_specs=[pl.BlockSpec((TOP_K, TM), lambda i, mt: (0, i), memory_space=pltpu.SMEM),
                      pl.BlockSpec((TOP_K, TM), lambda i, mt: (0, jnp.minimum(i + 1, n - 1)), memory_space=pltpu.SMEM),
                      rows(TOP_K), rows(W), rows(D),
                      pl.BlockSpec((None, 6, D), lambda i, mt: (mt[i], 0, 0)),
                      const((D, 2 * SHARED_FF)), const((SHARED_FF, D)), const((1, D)), const((1, D)),
                      pl.BlockSpec(memory_space=pl.ANY)],
            out_specs=rows(D),
            scratch_shapes=[pltpu.VMEM((TOP_K, CMB_TILE, W), U32), pltpu.VMEM((TOP_K, CMB_TILE, W), U32),
                            pltpu.SemaphoreType.DMA, pltpu.SemaphoreType.DMA]),
        out_shape=jax.ShapeDtypeStruct((T, D), F32),
        compiler_params=_cparams(("arbitrary",)),
        name="moe_combine",
    )(mod_of_tile, dest, dest, gate_t, up, xl, mods, sh_w13.astype(BF16), sh_w2.astype(BF16),
      ln_g.reshape(1, D), ln_b.reshape(1, D), ys)


def _moe_block(u, up, xl, mods, mod_of_tile, router_w, router_bias, w13, w2, layer, sh_w13, sh_w2, ln_g, ln_b, alpha):
    T, D = u.shape
    eidx, gate, rank, counts = _route(u, router_w, router_bias)
    cnt = counts.reshape(N_EXPERTS).astype(I32)
    padded = (cnt + MOE_BLK - 1) // MOE_BLK * MOE_BLK
    pad_end = jnp.cumsum(padded)
    pad_start = pad_end - padded
    nblk = T * TOP_K // MOE_BLK + N_EXPERTS
    n_used = (pad_end[-1] // MOE_BLK).astype(I32)
    blk_exp = jnp.searchsorted(pad_end, jnp.arange(nblk, dtype=I32) * MOE_BLK, side='right').astype(I32)
    last_e = jnp.max(jnp.where(cnt > 0, jnp.arange(N_EXPERTS, dtype=I32), 0))
    blk_exp = jnp.minimum(blk_exp, last_e)
    dest = _slot_of(eidx, rank, pad_start)
    xs = _dispatch(up, dest, nblk * MOE_BLK)
    ys = _expert_gemm(xs, blk_exp, n_used.reshape(1), w13, w2, layer)
    return _combine(dest, gate.T, up, xl, ys, mods, mod_of_tile, sh_w13, sh_w2, ln_g, ln_b, alpha)


def _in1_kernel(x_ref, mod_ref, wh_ref, wr_ref, ph_ref, pr_ref):
    sh1 = mod_ref[0:1, :]
    sc1 = mod_ref[1:2, :]
    h = (x_ref[...] * (1.0 + sc1) + sh1).astype(BF16)
    ph_ref[...] = jnp.dot(h, wh_ref[...], preferred_element_type=F32)
    pr_ref[...] = jnp.dot(h, wr_ref[...], preferred_element_type=F32)


RW_PW = 1920


def _rw_reorder(t):
    a = 3 * RW_W
    lo = 2 * RW_DECAY_LORA + RW_AAA_LORA
    out = jnp.concatenate([t[..., :a], t[..., a + lo:], t[..., a:a + lo]], axis=-1)
    pad = [(0, 0)] * (t.ndim - 1) + [(0, RW_PW - out.shape[-1])]
    return jnp.pad(out, pad)


def _in_proj1(xcat, mods, nctx_tiles, w_in):
    B, S, D = xcat.shape
    J = S // ROW_TILE
    T = B * S
    wh = w_in[:, :HY_WIDTH].astype(BF16)
    wr = _rw_reorder(w_in[:, HY_WIDTH:]).astype(BF16)

    def rows(w):
        return pl.BlockSpec((ROW_TILE, w), lambda b, j: (b * J + j, 0))

    def const(shape):
        return pl.BlockSpec(shape, lambda b, j: (0,) * len(shape))

    sd = jax.ShapeDtypeStruct
    return pl.pallas_call(
        _in1_kernel,
        grid=(B, J),
        in_specs=[rows(D), pl.BlockSpec((None, 6, D), lambda b, j: (jnp.where(j < nctx_tiles, B, b), 0, 0)),
                  const((D, HY_WIDTH)), const((D, RW_PW))],
        out_specs=[rows(HY_WIDTH), rows(RW_PW)],
        out_shape=[sd((T, HY_WIDTH), F32), sd((T, RW_PW), F32)],
        compiler_params=_cparams(("parallel", "parallel")),
        name="in_proj1",
    )(xcat.reshape(T, D), mods, wh, wr)


def _rw_streams_kernel(prev_ref, main_ref, next_ref, mu_ref, w0_ref, w2_ref, a0_ref, a2_ref, g2_ref,
                       kk_ref, ka_ref, bd_ref,
                       r_ref, k_ref, v_ref, kkn_ref, bb_ref, ld_ref, g_ref, sh_scr, *, nctx_tiles, n_lat_tiles):
    j = pl.program_id(1)
    TM = main_ref.shape[0]
    W = main_ref.shape[1]
    H = GRID_W
    p = main_ref[...]
    ext = jnp.concatenate([prev_ref[...], p, next_ref[...]], axis=0)
    left = ext[H - 1:H - 1 + TM]
    right = ext[H + 1:H + 1 + TM]
    up = ext[0:TM]
    down = ext[2 * H:2 * H + TM]
    i = lax.broadcasted_iota(I32, (TM, W), 0)
    lane = lax.broadcasted_iota(I32, (TM, W), 1)
    even = (lane & 1) == 0
    c4 = lane & 3
    jl = j - nctx_tiles

    @pl.when(j < nctx_tiles)
    def _():
        lo = jnp.where(j == 0, 1, 0)
        hi = jnp.where(j == nctx_tiles - 1, TM - 1, TM)
        sh_scr[...] = jnp.where(even, jnp.where(i >= lo, left, 0.0), jnp.where(i < hi, right, 0.0))

    @pl.when(j >= nctx_tiles)
    def _():
        col = i & (H - 1)
        up_lo = jnp.where(jl == 0, H, 0)
        down_hi = jnp.where(jl == n_lat_tiles - 1, TM - H, TM)
        l_v = jnp.where(col != 0, left, 0.0)
        r_v = jnp.where(col != H - 1, right, 0.0)
        u_v = jnp.where(i >= up_lo, up, 0.0)
        d_v = jnp.where(i < down_hi, down, 0.0)
        sh_scr[...] = jnp.where(c4 == 0, l_v, jnp.where(c4 == 1, r_v, jnp.where(c4 == 2, u_v, d_v)))

    pm = p + mu_ref[...] * (sh_scr[...] - p)
    r = pm[:, 0:RW_W]
    k = pm[:, RW_W:2 * RW_W]
    v = pm[:, 2 * RW_W:3 * RW_W]
    o = 3 * RW_W
    gl = pm[:, o:o + RW_GATE_LORA]
    o += RW_GATE_LORA
    wl_f = pm[:, o:o + RW_DECAY_LORA]
    wl_b = pm[:, o + RW_DECAY_LORA:o + 2 * RW_DECAY_LORA]
    al = pm[:, o + 2 * RW_DECAY_LORA:o + 2 * RW_DECAY_LORA + RW_AAA_LORA]
    for d, wl in enumerate((wl_f, wl_b)):
        z = w0_ref[d:d + 1, :] + _bdot(jnp.tanh(wl), w2_ref[d])
        w = -(jnp.maximum(-z, 0.0) + jnp.log(1.0 + jnp.exp(-jnp.abs(z)))) - 0.5
        ld_ref[d] = -jnp.exp(w)
    a = jax.nn.sigmoid(a0_ref[...] + _bdot(al, a2_ref[...]))
    g_ref[...] = _bdot(jax.nn.sigmoid(gl), g2_ref[...])
    kk = k * kk_ref[...]
    n2 = _hdot(kk * kk, bd_ref[...])
    kkn = kk / jnp.maximum(jnp.sqrt(n2), 1e-12)
    r_ref[...] = r
    k_ref[...] = k * (1.0 + (a - 1.0) * ka_ref[...])
    v_ref[...] = v
    kkn_ref[...] = kkn
    bb_ref[...] = kkn * a


def _head_sum_matrix(width, hd):
    i = np.arange(width)
    return jnp.asarray((i[:, None] // hd == i[None, :] // hd).astype(np.float32))


def _rw_streams(p_rw, B, S, nctx_tiles, mu, w0, w2, a0, a2, g2, k_k, k_a):
    T = B * S
    J = S // ROW_TILE
    HB = ROW_TILE // GRID_W
    NH = S // GRID_W

    def rows(w):
        return pl.BlockSpec((ROW_TILE, w), lambda b, j: (b * J + j, 0))

    def rows2(w):
        return pl.BlockSpec((2, ROW_TILE, w), lambda b, j: (0, b * J + j, 0))

    def const(shape):
        return pl.BlockSpec(shape, lambda b, j: (0,) * len(shape))

    sd = jax.ShapeDtypeStruct
    kern = functools.partial(_rw_streams_kernel, nctx_tiles=nctx_tiles, n_lat_tiles=J - nctx_tiles)
    return pl.pallas_call(
        kern,
        grid=(B, J),
        in_specs=[pl.BlockSpec((GRID_W, RW_PW), lambda b, j: (b * NH + jnp.maximum(j * HB - 1, 0), 0)),
                  rows(RW_PW),
                  pl.BlockSpec((GRID_W, RW_PW), lambda b, j: (b * NH + jnp.minimum((j + 1) * HB, NH - 1), 0)),
                  const((1, RW_PW)), const((2, RW_W)), const((2, RW_DECAY_LORA, RW_W)), const((1, RW_W)),
                  const((RW_AAA_LORA, RW_W)), const((RW_GATE_LORA, RW_W)), const((1, RW_W)), const((1, RW_W)),
                  const((RW_W, RW_W))],
        out_specs=[rows(RW_W), rows(RW_W), rows(RW_W), rows(RW_W), rows(RW_W), rows2(RW_W), rows(RW_W)],
        out_shape=[sd((T, RW_W), F32)] * 5 + [sd((2, T, RW_W), F32), sd((T, RW_W), F32)],
        scratch_shapes=[pltpu.VMEM((ROW_TILE, RW_PW), F32)],
        compiler_params=_cparams(("parallel", "parallel")),
        name="rwkv_streams",
    )(p_rw, p_rw, p_rw, _rw_reorder(mu).reshape(1, RW_PW), w0, w2, a0.reshape(1, RW_W), a2, g2,
      k_k.reshape(1, RW_W), k_a.reshape(1, RW_W), _head_sum_matrix(RW_W, RW_HEAD_DIM))


def _rwkv_kernel(r_ref, k_ref, v_ref, kk_ref, bb_ref, ld_ref, o_ref, st_ref, *, rev, nchunk):
    t = pl.program_id(1)

    @pl.when(t == 0)
    def _():
        st_ref[...] = jnp.zeros_like(st_ref)

    C = RW_CHUNK
    hd = RW_HEAD_DIM
    NH = RW_HEADS
    row = lax.broadcasted_iota(I32, (C, C), 0)
    col = lax.broadcasted_iota(I32, (C, C), 1)
    incl = ((col >= row) if rev else (col <= row))[None]
    strict = ((col > row) if rev else (col < row))[None]
    tri = jnp.where(incl[0], 1.0, 0.0).astype(F32)
    eye = (row == col)[None]
    last_i = 0 if rev else C - 1
    order = range(nchunk - 1, -1, -1) if rev else range(nchunk)
    n_double = int(math.log2(C)) - 1

    def stack(x):
        return jnp.stack([x[c * C:(c + 1) * C, h * hd:(h + 1) * hd] for c in range(nchunk) for h in range(NH)], 0)

    ld = ld_ref[...]
    g = jnp.concatenate([_hdot(tri, ld[c * C:(c + 1) * C]) for c in range(nchunk)], axis=0)
    g_last = jnp.concatenate([jnp.broadcast_to(g[c * C + last_i:c * C + last_i + 1], (C, NH * hd))
                              for c in range(nchunk)], axis=0)
    k = k_ref[...]
    bb = bb_ref[...]
    eng = jnp.exp(-g)
    e_end = jnp.exp(g_last - g)
    kk_t = stack(kk_ref[...] * jnp.exp(g - ld))
    b_t = stack(bb * eng)
    k_t = stack(k * eng)
    r_t = stack(r_ref[...] * jnp.exp(g))
    b_bar = stack(bb * e_end)
    k_bar = stack(k * e_end)
    dec = stack(jnp.exp(g_last))
    v = stack(v_ref[...])
    m_b = jnp.where(strict, _bmm_nt(kk_t, b_t), 0.0)
    m_k = jnp.where(strict, _bmm_nt(kk_t, k_t), 0.0)
    a_rb = jnp.where(incl, _bmm_nt(r_t, b_t), 0.0)
    a_rk = jnp.where(incl, _bmm_nt(r_t, k_t), 0.0)
    pw = -m_b
    tinv = jnp.where(eye, 1.0, 0.0) + pw
    for _ in range(n_double):
        pw = _bmm(pw, pw)
        tinv = tinv + _bmm(tinv, pw)
    a_t = _bmm(tinv, kk_t)
    u_b = _bmm(tinv, _bmm(m_k, v))
    q_h = r_t - _bmm(a_rb, a_t)
    o_h = _bmm(a_rk, v) - _bmm(a_rb, u_b)
    g_m = jnp.where(eye, dec, 0.0) - _bmm_tn(a_t, b_bar)
    h_m = _bmm_tn(v, k_bar) - _bmm_tn(u_b, b_bar)
    s = st_ref[...]
    for c in order:
        sl = slice(c * NH, (c + 1) * NH)
        o_c = _bmm_nt(q_h[sl], s) + o_h[sl]
        o_ref[c * C:(c + 1) * C, :] = jnp.concatenate([o_c[h] for h in range(NH)], axis=1)
        s = _bmm(s, g_m[sl]) + h_m[sl]
    st_ref[...] = s


def _rwkv_scan(r, k, v, kk, bb, ld, B, S, nctx_blocks, rev):
    T = B * S
    NB = S // ROW_TILE
    d = 1 if rev else 0
    nc = nctx_blocks

    def blk(t):
        if not rev:
            return t
        return jnp.where(t < nc, nc - 1 - t, NB - 1 - (t - nc))

    spec = pl.BlockSpec((ROW_TILE, RW_W), lambda b, t: (b * NB + blk(t), 0))
    kern = functools.partial(_rwkv_kernel, rev=rev, nchunk=ROW_TILE // RW_CHUNK)
    return pl.pallas_call(
        kern,
        grid=(B, NB),
        in_specs=[spec, spec, spec, spec, spec,
                  pl.BlockSpec((None, ROW_TILE, RW_W), lambda b, t: (d, b * NB + blk(t), 0))],
        out_specs=spec,
        out_shape=jax.ShapeDtypeStruct((T, RW_W), F32),
        scratch_shapes=[pltpu.VMEM((RW_HEADS, RW_HEAD_DIM, RW_HEAD_DIM), F32)],
        compiler_params=_cparams(("parallel", "arbitrary")),
        name="rwkv_scan_bwd" if rev else "rwkv_scan_fwd",
    )(r, k, v, kk, bb, ld)


def _filter_mlp_kernel(z_ref, w1_ref, b1_ref, w2_ref, b2_ref, w3_ref, sf_ref, win_ref, f_ref):
    h = jnp.sin(sf_ref[0:1, :] * (_hdot(z_ref[...], w1_ref[...]) + b1_ref[...]))
    h = jnp.sin(sf_ref[1:2, :] * (_hdot(h, w2_ref[...]) + b2_ref[...]))
    f_ref[...] = _hdot(h, w3_ref[...]) * win_ref[...]


def _hyena_filters(L, w1, b1, w2, b2, w3, sin_freq):
    t = np.linspace(0.0, 1.0, L, dtype=np.float32)[:, None]
    bands = (HY_EMB - 1) // 2
    wpos = (2.0 * math.pi * np.arange(L, dtype=np.float32)[:, None] / L).astype(np.float32)
    fr = np.linspace(1e-4, bands - 1, bands, dtype=np.float32)[None, :]
    z = np.concatenate([t, np.cos(fr * wpos), -np.sin(fr * wpos)], -1).astype(np.float32)
    zp = jnp.asarray(np.pad(z, ((0, 0), (0, LANES - HY_EMB))))
    w1p = jnp.pad(w1, ((0, LANES - HY_EMB), (0, 0)))
    max_decay = math.log(HY_TARGET) / HY_FAST_DECAY
    min_decay = math.log(HY_TARGET) / HY_SLOW_DECAY
    deltas = np.linspace(min_decay, max_decay, HY_CH, dtype=np.float32)
    window = np.exp(-t * np.abs(deltas)).astype(np.float32)
    FW = HY_ORDER * 2 * HY_CH
    win = jnp.asarray(np.tile(window, (1, HY_ORDER * 2)))
    TR = min(L, 512)
    Hd = w1.shape[1]
    return pl.pallas_call(
        _filter_mlp_kernel,
        grid=(L // TR,),
        in_specs=[pl.BlockSpec((TR, LANES), lambda i: (i, 0)),
                  pl.BlockSpec((LANES, Hd), lambda i: (0, 0)), pl.BlockSpec((1, Hd), lambda i: (0, 0)),
                  pl.BlockSpec((Hd, Hd), lambda i: (0, 0)), pl.BlockSpec((1, Hd), lambda i: (0, 0)),
                  pl.BlockSpec((Hd, FW), lambda i: (0, 0)), pl.BlockSpec((2, Hd), lambda i: (0, 0)),
                  pl.BlockSpec((TR, FW), lambda i: (i, 0))],
        out_specs=pl.BlockSpec((TR, FW), lambda i: (i, 0)),
        out_shape=jax.ShapeDtypeStruct((L, FW), F32),
        compiler_params=_cparams(("parallel",)),
        name="hyena_filter_mlp",
    )(zp, w1p, b1.reshape(1, Hd), w2, b2.reshape(1, Hd), w3, sin_freq, win)


def _sconv_kernel(p_ref, w_ref, b_ref, o_ref):
    p = p_ref[...]
    L = p.shape[0]
    i = lax.broadcasted_iota(I32, p.shape, 0)
    prev = jnp.where(i == 0, 0.0, pltpu.roll(p, 1, 0))
    nxt = jnp.where(i == L - 1, 0.0, pltpu.roll(p, L - 1, 0))
    o_ref[...] = prev * w_ref[0:1, :] + p * w_ref[1:2, :] + nxt * w_ref[2:3, :] + b_ref[...]


def _short_conv(p_hy, B, S, Lc, conv_w, conv_b):
    L = S - Lc
    p3 = p_hy.reshape(B, S, HY_WIDTH)[:, Lc:, :]
    return pl.pallas_call(
        _sconv_kernel,
        grid=(B, HY_WIDTH // LANES),
        in_specs=[pl.BlockSpec((None, L, LANES), lambda b, c: (b, 0, c)),
                  pl.BlockSpec((HY_SHORT, LANES), lambda b, c: (0, c)),
                  pl.BlockSpec((1, LANES), lambda b, c: (0, c))],
        out_specs=pl.BlockSpec((None, L, LANES), lambda b, c: (b, 0, c)),
        out_shape=jax.ShapeDtypeStruct((B, L, HY_WIDTH), F32),
        compiler_params=_cparams(("parallel", "parallel")),
        name="hyena_short_conv",
    )(p3, conv_w, conv_b.reshape(1, HY_WIDTH))


def _dft_constants(L):
    n = 2 * L
    n2 = FFT_N2
    n1 = n // n2
    na = n1 // 2
    g8 = SUBLANES
    w1 = np.exp(-2j * np.pi * np.outer(np.arange(n1), np.arange(n1)) / n1)
    eye8 = np.eye(g8)

    def kron_fwd(w, real_in):
        k1n, an = w.shape
        blocks = np.stack([np.stack([w.real, -w.imag], 1), np.stack([w.imag, w.real], 1)], 1)
        if real_in:
            blocks = blocks[:, :, 0:1, :]
        m = np.einsum('kria,bc->krbiac', blocks, eye8)
        return m.reshape(k1n * 2 * g8, blocks.shape[2] * an * g8)

    m1 = kron_fwd(w1[:, :na], False)
    m1f = kron_fwd(w1, True)
    cw = np.conj(w1[:, :na]).T / n
    blocks = np.stack([np.stack([cw.real, -cw.imag], 1), np.stack([cw.imag, cw.real], 1)], 0)
    m1inv = np.einsum('iark,bc->iabkrc', blocks, eye8).reshape(2 * na * g8, n1 * 2 * g8)
    w2 = np.exp(-2j * np.pi * np.outer(np.arange(n2), np.arange(n2)) / n2)
    w2big = np.block([[w2.real, -w2.imag], [w2.imag, w2.real]])
    w2c = np.conj(w2)
    iw2big = np.block([[w2c.real, -w2c.imag], [w2c.imag, w2c.real]])
    tw = np.exp(-2j * np.pi * np.arange(n2) / n)
    tw1 = np.stack([np.broadcast_to(tw.real[:, None], (n2, LANES)),
                    np.broadcast_to(tw.imag[:, None], (n2, LANES))], 0)
    c = lambda x, dt: jnp.asarray(np.ascontiguousarray(x), dt)
    return dict(m1=c(m1, BF16), m1f=c(m1f, BF16), m1inv=c(m1inv, BF16), w2=c(w2big, BF16), iw2=c(iw2big, BF16),
                tw1=c(tw1, F32), n1=n1, na=na)


def _cplx_rows(z, twr, twi, conj):
    n2 = z.shape[0] // 2
    zr, zi = z[:n2], z[n2:]
    if conj:
        return jnp.concatenate([zr * twr + zi * twi, zi * twr - zr * twi], axis=0)
    return jnp.concatenate([zr * twr - zi * twi, zi * twr + zr * twi], axis=0)


def _split_dot(m, x):
    hi = x.astype(BF16)
    lo = (x - hi.astype(F32)).astype(BF16)
    return jnp.dot(m, hi, preferred_element_type=F32) + jnp.dot(m, lo, preferred_element_type=F32)


def _next_twiddle(tw, tw1_ref):
    twr, twi = tw
    b_r, b_i = tw1_ref[0], tw1_ref[1]
    return twr * b_r - twi * b_i, twr * b_i + twi * b_r


def _filt_fft_kernel(f_ref, m1_ref, w2_ref, tw1_ref, o_ref, y_scr, *, n1):
    nbg = FFT_N2 // SUBLANES
    C = f_ref.shape[-1]
    for bg in range(nbg):
        yg = _split_dot(m1_ref[...], f_ref[:, bg].reshape(n1 * SUBLANES, C))
        y_scr[:, :, bg] = yg.reshape(n1, 2, SUBLANES, C)

    def body(k1, tw):
        y = _cplx_rows(y_scr[k1].reshape(2 * FFT_N2, C), tw[0], tw[1], False)
        o_ref[k1] = _split_dot(w2_ref[...], y)
        return _next_twiddle(tw, tw1_ref)

    lax.fori_loop(0, n1, body, (jnp.ones((FFT_N2, C), F32), jnp.zeros((FFT_N2, C), F32)))


def _filter_spectrum(filt, L, dc):
    n1 = dc['n1']
    f4 = filt.reshape(L, HY_ORDER, 2, HY_CH)
    h_f, h_b = f4[:, :, 0, :], f4[:, :, 1, :]
    full = jnp.concatenate([h_f, jnp.zeros_like(h_f[:1]), jnp.flip(h_b[1:], 0)], 0)
    full = jnp.transpose(full, (1, 0, 2)).reshape(HY_ORDER, n1, FFT_N2 // SUBLANES, SUBLANES, HY_CH)
    nbg = FFT_N2 // SUBLANES
    return pl.pallas_call(
        functools.partial(_filt_fft_kernel, n1=n1),
        grid=(HY_ORDER, HY_CH // LANES),
        in_specs=[pl.BlockSpec((None, n1, nbg, SUBLANES, LANES), lambda o, c: (o, 0, 0, 0, c)),
                  pl.BlockSpec(dc['m1f'].shape, lambda o, c: (0, 0)),
                  pl.BlockSpec(dc['w2'].shape, lambda o, c: (0, 0)),
                  pl.BlockSpec(dc['tw1'].shape, lambda o, c: (0, 0, 0))],
        out_specs=pl.BlockSpec((None, n1, 2 * FFT_N2, LANES), lambda o, c: (o, 0, 0, c)),
        out_shape=jax.ShapeDtypeStruct((HY_ORDER, n1, 2 * FFT_N2, HY_CH), F32),
        scratch_shapes=[pltpu.VMEM((n1, 2, nbg, SUBLANES, LANES), F32)],
        compiler_params=_cparams(("parallel", "parallel")),
        name="hyena_filter_fft",
    )(full, dc['m1f'], dc['w2'], dc['tw1'])


def _hyconv_kernel(za_ref, zb_ref, ga_ref, gb_ref, ff_ref, bias_ref, m1_ref, m1i_ref, w2_ref, iw2_ref,
                   tw1_ref, oa_ref, ob_ref, y_scr, *, n1, na):
    nbg = FFT_N2 // SUBLANES
    C = za_ref.shape[-1]
    half = na * SUBLANES
    for bg in range(nbg):
        xg = jnp.concatenate([za_ref[:, bg].reshape(half, C), zb_ref[:, bg].reshape(half, C)], axis=0)
        yg = jnp.dot(m1_ref[...], xg.astype(BF16), preferred_element_type=F32)
        y_scr[:, :, bg] = yg.reshape(n1, 2, SUBLANES, C)

    def body(k1, tw):
        y = _cplx_rows(y_scr[k1].reshape(2 * FFT_N2, C), tw[0], tw[1], False)
        z = jnp.dot(w2_ref[...], y.astype(BF16), preferred_element_type=F32)
        f = ff_ref[k1]
        zr, zi = z[:FFT_N2], z[FFT_N2:]
        fr, fi = f[:FFT_N2], f[FFT_N2:]
        p = jnp.concatenate([zr * fr - zi * fi, zr * fi + zi * fr], axis=0)
        v = jnp.dot(iw2_ref[...], p.astype(BF16), preferred_element_type=F32)
        v = _cplx_rows(v, tw[0], tw[1], True)
        y_scr[k1] = v.reshape(2, nbg, SUBLANES, C)
        return _next_twiddle(tw, tw1_ref)

    lax.fori_loop(0, n1, body, (jnp.ones((FFT_N2, C), F32), jnp.zeros((FFT_N2, C), F32)))
    bias = bias_ref[...]
    for bg in range(nbg):
        vg = y_scr[:, :, bg].reshape(n1 * 2 * SUBLANES, C)
        out = jnp.dot(m1i_ref[...], vg.astype(BF16), preferred_element_type=F32)
        ya = out[:half].reshape(na, SUBLANES, C)
        yb = out[half:].reshape(na, SUBLANES, C)
        za = za_ref[:, bg]
        zb = zb_ref[:, bg]
        oa_ref[:, bg] = ga_ref[:, bg] * (ya + za * bias)
        ob_ref[:, bg] = gb_ref[:, bg] * (yb + zb * bias)


def _hyena_conv(z, z_col0, gates, g_col0, spec, bias, dc, B, L):
    n1, na = dc['n1'], dc['na']
    nbg = FFT_N2 // SUBLANES
    NCB = HY_CH // LANES

    def view(a):
        return a.reshape(B, na, nbg, SUBLANES, a.shape[-1])

    def seq(col0, which):
        return pl.BlockSpec((None, na, nbg, SUBLANES, LANES),
                            lambda c, p: (2 * p + which, 0, 0, 0, col0 + c))

    def const(a):
        nd = a.ndim
        return pl.BlockSpec(a.shape, lambda c, p: (0,) * nd)

    out_a, out_b = pl.pallas_call(
        functools.partial(_hyconv_kernel, n1=n1, na=na),
        grid=(NCB, B // 2),
        in_specs=[seq(z_col0, 0), seq(z_col0, 1), seq(g_col0, 0), seq(g_col0, 1),
                  pl.BlockSpec((n1, 2 * FFT_N2, LANES), lambda c, p: (0, 0, c), pipeline_mode=pl.Buffered(1)),
                  pl.BlockSpec((1, LANES), lambda c, p: (0, c)),
                  const(dc['m1']), const(dc['m1inv']), const(dc['w2']), const(dc['iw2']), const(dc['tw1'])],
        out_specs=[pl.BlockSpec((None, na, nbg, SUBLANES, LANES), lambda c, p: (p, 0, 0, 0, c)),
                   pl.BlockSpec((None, na, nbg, SUBLANES, LANES), lambda c, p: (p, 0, 0, 0, c))],
        out_shape=[jax.ShapeDtypeStruct((B // 2, na, nbg, SUBLANES, HY_CH), F32)] * 2,
        scratch_shapes=[pltpu.VMEM((n1, 2, nbg, SUBLANES, LANES), F32)],
        compiler_params=_cparams(("parallel", "arbitrary")),
        name="hyena_long_conv",
    )(view(z), view(z), view(gates), view(gates), spec, bias.reshape(1, HY_CH),
      dc['m1'], dc['m1inv'], dc['w2'], dc['iw2'], dc['tw1'])
    out = jnp.stack([out_a, out_b], axis=1)
    return out.reshape(B, L, HY_CH)


def _out1_kernel(hy_ref, of_ref, ob_ref, r_ref, k_ref, v_ref, g_ref, x_ref, mod_ref, wo_ref,
                 rk_ref, lg_ref, lb_ref, bd_ref, lng_ref, lnb_ref, xl_ref, u_ref, up_ref, *, alpha):
    o = of_ref[...] + ob_ref[...]
    bd = bd_ref[...]
    inv = 1.0 / RW_HEAD_DIM
    mu = _hdot(o, bd) * inv
    oc = o - mu
    var = _hdot(oc * oc, bd) * inv
    on = oc * lax.rsqrt(var + RW_GN_EPS) * lg_ref[...] + lb_ref[...]
    bonus = _hdot(r_ref[...] * k_ref[...] * rk_ref[...], bd) * v_ref[...]
    y = jnp.concatenate([hy_ref[...], (on + bonus) * g_ref[...]], axis=1)
    _post_mix(y, x_ref[...], mod_ref, wo_ref, lng_ref, lnb_ref, alpha, xl_ref, u_ref, up_ref)


def _readout1(hy, o_f, o_b, r, k, v, g, xcat, mods, B, S, Lc, w_out, r_k, ln_g, ln_b, dn_g, dn_b, alpha):
    D = xcat.shape[-1]
    L = S - Lc
    J = S // ROW_TILE
    JL = L // ROW_TILE
    JC = Lc // ROW_TILE

    def cat_rows(w):
        return pl.BlockSpec((ROW_TILE, w), lambda b, j: (b * J + JC + j, 0))

    def lat_rows(w):
        return pl.BlockSpec((ROW_TILE, w), lambda b, j: (b * JL + j, 0))

    def const(shape):
        return pl.BlockSpec(shape, lambda b, j: (0,) * len(shape))

    sd = jax.ShapeDtypeStruct
    return pl.pallas_call(
        functools.partial(_out1_kernel, alpha=alpha),
        grid=(B, JL),
        in_specs=[lat_rows(HY_CH), cat_rows(RW_W), cat_rows(RW_W), cat_rows(RW_W), cat_rows(RW_W), cat_rows(RW_W),
                  cat_rows(RW_W), cat_rows(D),
                  pl.BlockSpec((None, 6, D), lambda b, j: (b, 0, 0)),
                  const((HY_CH + RW_W, D)), const((1, RW_W)), const((1, RW_W)), const((1, RW_W)),
                  const((RW_W, RW_W)), const((1, D)), const((1, D))],
        out_specs=[lat_rows(D), lat_rows(D), lat_rows(D // 2)],
        out_shape=[sd((B * L, D), F32), sd((B * L, D), F32), sd((B * L, D // 2), U32)],
        compiler_params=_cparams(("parallel", "parallel")),
        name="readout1",
    )(hy, o_f, o_b, r, k, v, g, xcat.reshape(B * S, D), mods, w_out.astype(BF16),
      r_k.reshape(1, RW_W), ln_g.reshape(1, RW_W), ln_b.reshape(1, RW_W),
      _head_sum_matrix(RW_W, RW_HEAD_DIM), dn_g.reshape(1, D), dn_b.reshape(1, D))


def kernel(x, c, ctx, c_ctx, mod_w, mod_b, ln1_g, ln1_b, ln2_g, ln2_b, ev_w_in, ev_w_out, gla_gate_w2, gla_gate_b, gla_norm_g, hg_lb_logits, hg_norm_g, od_w_in, od_w_out, hy_conv_w, hy_conv_b, hy_ffn_w1, hy_ffn_b1, hy_ffn_w2, hy_ffn_b2, hy_ffn_w3, hy_sin_freq, hy_bias, rw_mu, rw_w0, rw_w2, rw_a0, rw_a2, rw_g2, rw_k_k, rw_k_a, rw_r_k, rw_ln_g, rw_ln_b, router_w, router_bias, exp_w13, exp_w2, sh_w13, sh_w2):
    B, L, D = x.shape
    Lc = ctx.shape[1]
    S = Lc + L
    depth = mod_w.shape[0]
    assert depth == 2 and L % ROW_TILE == 0 and Lc % ROW_TILE == 0 and B % 2 == 0
    assert L % (FFT_N2 * 2) == 0 and L % GRID_W == 0
    alpha = (2 * depth) ** 0.25
    nctx = Lc // ROW_TILE
    J = S // ROW_TILE
    T = B * S

    cc = jnp.concatenate([c, c_ctx[None, :]], axis=0)
    cc = jnp.pad(cc, ((0, (-cc.shape[0]) % SUBLANES), (0, 0)))
    hg_lb = jnp.cumsum(jax.nn.softmax(hg_lb_logits.astype(F32), axis=0), axis=0)
    xcat = jnp.concatenate([ctx, x], axis=1)

    mods = _modulation(cc, mod_w[0], mod_b[0])
    gq, gk, gv, gla, r, hq, hk, hla, hv, hgate = _in_proj0(xcat, mods, nctx, ev_w_in[0], gla_gate_w2[0],
                                                            gla_gate_b[0], hg_lb[0])
    gk3 = gk.reshape(1, T, GLA_QK)
    o_gf = _gated_recurrence(gq, gk3, gv, gla, 0, B, S, nctx, False, GLA_HEADS, GLA_DK, GLA_DV)
    o_gb = _gated_recurrence(gq, gk3, gv, gla, 0, B, S, nctx, True, GLA_HEADS, GLA_DK, GLA_DV)
    o_hf = _gated_recurrence(hq, hk, hv, hla, 0, B, S, nctx, False, HG_HEADS, HG_EXPAND, HG_EXPAND)
    o_hb = _gated_recurrence(hq, hk, hv, hla, 1, B, S, nctx, True, HG_HEADS, HG_EXPAND, HG_EXPAND)
    xl, u, up = _readout0(o_gf, o_gb, o_hf, o_hb, r, hgate, xcat, mods, nctx, ev_w_out[0], gla_norm_g[0],
                          hg_norm_g[0], ln1_g[0], ln1_b[0], alpha)
    cmb_rows = 2 * CMB_TILE
    tiles_per_b = S // cmb_rows
    tile_in_b = jnp.arange(B * tiles_per_b, dtype=I32) % tiles_per_b
    mod_of_tile = jnp.where(tile_in_b < Lc // cmb_rows, B, jnp.arange(B * tiles_per_b, dtype=I32) // tiles_per_b)
    x1 = _moe_block(u, up, xl, mods, mod_of_tile.astype(I32), router_w[0], router_bias[0], exp_w13, exp_w2, 0,
                    sh_w13[0], sh_w2[0], ln2_g[0], ln2_b[0], alpha)
    xcat = x1.reshape(B, S, D)

    mods = _modulation(cc, mod_w[1], mod_b[1])
    p_hy, p_rw = _in_proj1(xcat, mods, nctx, od_w_in[0])
    rr, rk, rv, rkk, rbb, rld, rg = _rw_streams(p_rw, B, S, nctx, rw_mu[0], rw_w0[0], rw_w2[0], rw_a0[0],
                                                 rw_a2[0], rw_g2[0], rw_k_k[0], rw_k_a[0])
    o_f = _rwkv_scan(rr, rk, rv, rkk, rbb, rld, B, S, nctx, False)
    o_b = _rwkv_scan(rr, rk, rv, rkk, rbb, rld, B, S, nctx, True)
    dc = _dft_constants(L)
    filt = _hyena_filters(L, hy_ffn_w1[0], hy_ffn_b1[0], hy_ffn_w2[0], hy_ffn_b2[0], hy_ffn_w3[0], hy_sin_freq[0])
    spec = _filter_spectrum(filt, L, dc)
    uu = _short_conv(p_hy, B, S, Lc, hy_conv_w[0], hy_conv_b[0])
    NCB = HY_CH // LANES
    z1 = _hyena_conv(uu, 0, uu, NCB, spec[0], hy_bias[0, 0], dc, B, L)
    z2 = _hyena_conv(z1, 0, uu, 2 * NCB, spec[1], hy_bias[0, 1], dc, B, L)
    xl, u, up = _readout1(z2.reshape(B * L, HY_CH), o_f, o_b, rr, rk, rv, rg, xcat, mods, B, S, Lc, od_w_out[0],
                          rw_r_k[0], rw_ln_g[0], rw_ln_b[0], ln1_g[1], ln1_b[1], alpha)
    mod_of_tile = (jnp.arange(B * (L // cmb_rows), dtype=I32) // (L // cmb_rows)).astype(I32)
    out = _moe_block(u, up, xl, mods, mod_of_tile, router_w[1], router_bias[1], exp_w13, exp_w2, 1,
                     sh_w13[1], sh_w2[1], ln2_g[1], ln2_b[1], alpha)
    return out.reshape(B, L, D)
```

```python
import functools
import math

import numpy as np
import jax
import jax.numpy as jnp
from jax import lax
from jax.experimental import pallas as pl
from jax.experimental.pallas import tpu as pltpu

F32 = jnp.float32
BF16 = jnp.bfloat16
I32 = jnp.int32
U32 = jnp.uint32
HI = lax.Precision.HIGHEST

LN_EPS = 1e-5
GLA_HEADS, GLA_DK, GLA_DV = 4, 64, 128
GLA_QK, GLA_V = GLA_HEADS * GLA_DK, GLA_HEADS * GLA_DV
GLA_GATE_RANK = 16
GLA_GATE_NORM = 16.0
CHUNK = 64
HG_HEADS, HG_EXPAND = 4, 128
HG_W = HG_HEADS * HG_EXPAND
HY_CH, HY_ORDER, HY_SHORT, HY_EMB = 512, 2, 3, 33
HY_WIDTH = (HY_ORDER + 1) * HY_CH
HY_FAST_DECAY, HY_SLOW_DECAY, HY_TARGET = 0.3, 1.5, 1e-2
RW_HEADS, RW_HEAD_DIM = 8, 64
RW_W = RW_HEADS * RW_HEAD_DIM
RW_DECAY_LORA, RW_AAA_LORA, RW_GATE_LORA = 64, 64, 128
RW_GN_EPS = 64e-5
GRID_W = 64
N_EXPERTS, TOP_K, N_GROUPS, TOPK_GROUPS = 256, 8, 8, 4
PER_GROUP = N_EXPERTS // N_GROUPS
EXPERT_FF, SHARED_FF = 256, 256
ROUTED_SCALE = 2.5

LANES = 128
SUBLANES = 8
VMEM_LIMIT = 56 * 1024 * 1024
ROW_TILE = 256
MOE_BLK = 256
CMB_TILE = 128
RW_CHUNK = 64
FFT_N2 = 128


def _cparams(sem):
    return pltpu.CompilerParams(dimension_semantics=sem, vmem_limit_bytes=VMEM_LIMIT)


def _bdot(a, b):
    return jnp.dot(a.astype(BF16), b.astype(BF16), preferred_element_type=F32)


def _bdot_nt(a, b):
    return lax.dot_general(a.astype(BF16), b.astype(BF16), (((1,), (1,)), ((), ())),
                           preferred_element_type=F32)


def _bdot_tn(a, b):
    return lax.dot_general(a.astype(BF16), b.astype(BF16), (((0,), (0,)), ((), ())),
                           preferred_element_type=F32)


def _bmm(a, b):
    return jnp.einsum('nij,njk->nik', a.astype(BF16), b.astype(BF16), preferred_element_type=F32)


def _bmm_nt(a, b):
    return jnp.einsum('nik,njk->nij', a.astype(BF16), b.astype(BF16), preferred_element_type=F32)


def _bmm_tn(a, b):
    return _bmm(jnp.swapaxes(a, 1, 2), b)


def _hdot(a, b):
    return jnp.dot(a, b, precision=HI, preferred_element_type=F32)


def _silu(x):
    return x * jax.nn.sigmoid(x)


def _layer_norm_rows(x, g, b, eps):
    mu = jnp.mean(x, axis=-1, keepdims=True)
    xc = x - mu
    var = jnp.mean(xc * xc, axis=-1, keepdims=True)
    return xc * lax.rsqrt(var + eps) * g + b


def _mod_kernel(c_ref, w_ref, b_ref, o_ref):
    o_ref[...] = _hdot(_silu(c_ref[...]), w_ref[...]) + b_ref[...]


def _modulation(cc, w, b):
    R, D = cc.shape
    out = pl.pallas_call(
        _mod_kernel,
        grid=(6,),
        in_specs=[pl.BlockSpec((R, D), lambda j: (0, 0)),
                  pl.BlockSpec((D, D), lambda j: (0, j)),
                  pl.BlockSpec((1, D), lambda j: (0, j))],
        out_specs=pl.BlockSpec((R, D), lambda j: (0, j)),
        out_shape=jax.ShapeDtypeStruct((R, 6 * D), F32),
        compiler_params=_cparams(("parallel",)),
        name="modulation",
    )(cc, w, b.reshape(1, 6 * D))
    return out.reshape(R, 6, D)


def _in0_kernel(x_ref, mod_ref, w_ref, wa_ref, w2_ref, gb_ref, lb_ref,
                gq_ref, gk_ref, gv_ref, gla_ref, r_ref, hq_ref, hk_ref, hla_ref, hv_ref, hg_ref):
    sh1 = mod_ref[0:1, :]
    sc1 = mod_ref[1:2, :]
    h = (x_ref[...] * (1.0 + sc1) + sh1).astype(BF16)

    def proj(off, width):
        return jnp.dot(h, w_ref[:, off:off + width], preferred_element_type=F32)

    gq_ref[...] = proj(0, GLA_QK) * (GLA_DK ** -0.5)
    gk_ref[...] = proj(GLA_QK, GLA_QK)
    gv_ref[...] = proj(2 * GLA_QK, GLA_V)
    r_ref[...] = proj(2 * GLA_QK + GLA_V, GLA_V)
    base = 2 * GLA_QK + 2 * GLA_V
    a = jnp.dot(h, wa_ref[...], preferred_element_type=F32)
    z = _bdot(a, w2_ref[...]) + gb_ref[...]
    ls = (jnp.minimum(z, 0.0) - jnp.log(1.0 + jnp.exp(-jnp.abs(z)))) * (1.0 / GLA_GATE_NORM)
    gla_ref[0] = ls[:, :GLA_QK]
    gla_ref[1] = ls[:, GLA_QK:]
    hq_ref[...] = _silu(proj(base, HG_W))
    for d in range(2):
        zf = proj(base + (1 + d) * HG_W, HG_W)
        lb = lb_ref[d:d + 1, :]
        f = lb + (1.0 - lb) * jax.nn.sigmoid(zf)
        hk_ref[d] = 1.0 - f
        hla_ref[d] = jnp.log(f)
    hv_ref[...] = proj(base + 3 * HG_W, HG_W)
    hg_ref[...] = proj(base + 4 * HG_W, HG_W)


def _in_proj0(xcat, mods, nctx_tiles, w_in, gate_w2, gate_b, lb):
    B, S, D = xcat.shape
    J = S // ROW_TILE
    T = B * S
    a_off = 2 * GLA_QK + 2 * GLA_V
    wmain = jnp.concatenate([w_in[:, :a_off], w_in[:, a_off + 2 * GLA_GATE_RANK:]], axis=1).astype(BF16)
    wa = jnp.pad(w_in[:, a_off:a_off + 2 * GLA_GATE_RANK], ((0, 0), (0, LANES - 2 * GLA_GATE_RANK))).astype(BF16)
    w2 = jnp.zeros((LANES, 2 * GLA_QK), F32)
    w2 = w2.at[:GLA_GATE_RANK, :GLA_QK].set(gate_w2[0]).at[GLA_GATE_RANK:2 * GLA_GATE_RANK, GLA_QK:].set(gate_w2[1])
    gb = gate_b.reshape(1, 2 * GLA_QK)
    WM = wmain.shape[1]

    def rows(w):
        return pl.BlockSpec((ROW_TILE, w), lambda b, j: (b * J + j, 0))

    def rows2(w):
        return pl.BlockSpec((2, ROW_TILE, w), lambda b, j: (0, b * J + j, 0))

    def const(shape):
        return pl.BlockSpec(shape, lambda b, j: (0,) * len(shape))

    sd = jax.ShapeDtypeStruct
    outs = pl.pallas_call(
        _in0_kernel,
        grid=(B, J),
        in_specs=[rows(D),
                  pl.BlockSpec((None, 6, D), lambda b, j: (jnp.where(j < nctx_tiles, B, b), 0, 0)),
                  const((D, WM)), const((D, LANES)), const((LANES, 2 * GLA_QK)), const((1, 2 * GLA_QK)),
                  const((2, HG_W))],
        out_specs=[rows(GLA_QK), rows(GLA_QK), rows(GLA_V), rows2(GLA_QK), rows(GLA_V),
                   rows(HG_W), rows2(HG_W), rows2(HG_W), rows(HG_W), rows(HG_W)],
        out_shape=[sd((T, GLA_QK), F32), sd((T, GLA_QK), F32), sd((T, GLA_V), F32), sd((2, T, GLA_QK), F32),
                   sd((T, GLA_V), F32), sd((T, HG_W), F32), sd((2, T, HG_W), F32), sd((2, T, HG_W), F32),
                   sd((T, HG_W), F32), sd((T, HG_W), F32)],
        compiler_params=_cparams(("parallel", "parallel")),
        name="in_proj0",
    )(xcat.reshape(T, D), mods, wmain, wa, w2, gb, lb)
    return outs


def _rec_kernel(q_ref, k_ref, v_ref, la_ref, o_ref, st_ref, *, rev, nh, dk, dv, nchunk):
    t = pl.program_id(1)

    @pl.when(t == 0)
    def _():
        st_ref[...] = jnp.zeros_like(st_ref)

    C = CHUNK
    row = lax.broadcasted_iota(I32, (C, C), 0)
    col = lax.broadcasted_iota(I32, (C, C), 1)
    incl = (col >= row) if rev else (col <= row)
    tri = jnp.where(incl, 1.0, 0.0).astype(F32)
    ref_i = C // 2 - 1 if rev else C // 2
    last_i = 0 if rev else C - 1
    order = range(nchunk - 1, -1, -1) if rev else range(nchunk)

    def stack(x, hd):
        return jnp.stack([x[c * C:(c + 1) * C, h * hd:(h + 1) * hd] for c in range(nchunk) for h in range(nh)], 0)

    def rows(x, i):
        return jnp.concatenate([jnp.broadcast_to(x[c * C + i:c * C + i + 1], (C, x.shape[1]))
                                for c in range(nchunk)], axis=0)

    la = la_ref[...]
    q = q_ref[...]
    k = k_ref[...]
    b = jnp.concatenate([_hdot(tri, la[c * C:(c + 1) * C]) for c in range(nchunk)], axis=0)
    b_mid = rows(b, ref_i)
    b_last = rows(b, last_i)
    v = stack(v_ref[...], dv)
    sc = _bmm_nt(stack(q * jnp.exp(b - b_mid), dk), stack(k * jnp.exp(b_mid - b), dk))
    o_intra = _bmm(jnp.where(incl[None], sc, 0.0), v)
    q_in = stack(q * jnp.exp(b), dk)
    kv_t = _bmm_tn(v, stack(k * jnp.exp(b_last - b), dk))
    dec = stack(jnp.exp(b_last), dk)[:, 0:1, :]
    s_t = st_ref[...]
    for c in order:
        sl = slice(c * nh, (c + 1) * nh)
        o_c = o_intra[sl] + _bmm_nt(q_in[sl], s_t)
        o_ref[c * C:(c + 1) * C, :] = jnp.concatenate([o_c[h] for h in range(nh)], axis=1)
        s_t = s_t * dec[sl] + kv_t[sl]
    st_ref[...] = s_t


def _gated_recurrence(q, k, v, la, kdir, B, S, nctx_blocks, rev, nh, dk, dv):
    T = B * S
    NB = S // ROW_TILE
    wk, wv = nh * dk, nh * dv
    assert q.shape[1] == wk
    d = 1 if rev else 0
    nc = nctx_blocks

    def blk(t):
        if not rev:
            return t
        return jnp.where(t < nc, nc - 1 - t, NB - 1 - (t - nc))

    kern = functools.partial(_rec_kernel, rev=rev, nh=nh, dk=dk, dv=dv, nchunk=ROW_TILE // CHUNK)
    return pl.pallas_call(
        kern,
        grid=(B, NB),
        in_specs=[pl.BlockSpec((ROW_TILE, wk), lambda b, t: (b * NB + blk(t), 0)),
                  pl.BlockSpec((None, ROW_TILE, wk), lambda b, t: (kdir, b * NB + blk(t), 0)),
                  pl.BlockSpec((ROW_TILE, wv), lambda b, t: (b * NB + blk(t), 0)),
                  pl.BlockSpec((None, ROW_TILE, wk), lambda b, t: (d, b * NB + blk(t), 0))],
        out_specs=pl.BlockSpec((ROW_TILE, wv), lambda b, t: (b * NB + blk(t), 0)),
        out_shape=jax.ShapeDtypeStruct((T, wv), F32),
        scratch_shapes=[pltpu.VMEM((nh, dv, dk), F32)],
        compiler_params=_cparams(("parallel", "arbitrary")),
        name="gated_rec_bwd" if rev else "gated_rec_fwd",
    )(q, k, v, la)


def _pack_rows(x):
    w = x.shape[1] // 2
    lo = lax.bitcast_convert_type(x[:, :w].astype(BF16).astype(F32), U32)
    hi = lax.bitcast_convert_type(x[:, w:].astype(BF16).astype(F32), U32)
    return (lo >> 16) | (hi & jnp.uint32(0xFFFF0000))


def _unpack_rows(p):
    lo = lax.bitcast_convert_type(p << 16, F32)
    hi = lax.bitcast_convert_type(p & jnp.uint32(0xFFFF0000), F32)
    return lo, hi


def _post_mix(y, x, mod_ref, wo_ref, lng_ref, lnb_ref, alpha, xl_ref, u_ref, up_ref):
    g1 = mod_ref[2:3, :]
    sh2 = mod_ref[3:4, :]
    sc2 = mod_ref[4:5, :]
    yo = jnp.dot(y.astype(BF16), wo_ref[...], preferred_element_type=F32)
    xl = _layer_norm_rows(alpha * x + g1 * yo, lng_ref[...], lnb_ref[...], LN_EPS)
    xl_ref[...] = xl
    u = xl * (1.0 + sc2) + sh2
    u_ref[...] = u
    up_ref[...] = _pack_rows(u)


def _out0_kernel(gf_ref, gb_ref, hf_ref, hb_ref, r_ref, hg_ref, x_ref, mod_ref, wo_ref,
                 gg_ref, hgg_ref, lng_ref, lnb_ref, xl_ref, u_ref, up_ref, *, alpha):
    def heads(o, g, gate):
        parts = []
        for hh in range(o.shape[1] // LANES):
            seg = o[:, hh * LANES:(hh + 1) * LANES]
            ms = jnp.mean(seg * seg, axis=-1, keepdims=True)
            parts.append(seg * lax.rsqrt(ms + 1e-6) * g)
        return jnp.concatenate(parts, axis=1) * _silu(gate)

    y = jnp.concatenate([heads(gf_ref[...] + gb_ref[...], gg_ref[...], r_ref[...]),
                         heads(hf_ref[...] + hb_ref[...], hgg_ref[...], hg_ref[...])], axis=1)
    _post_mix(y, x_ref[...], mod_ref, wo_ref, lng_ref, lnb_ref, alpha, xl_ref, u_ref, up_ref)


def _readout0(o_gf, o_gb, o_hf, o_hb, r, hgate, xcat, mods, nctx_tiles, w_out, gla_g, hg_g, ln_g, ln_b, alpha):
    B, S, D = xcat.shape
    J = S // ROW_TILE
    T = B * S

    def rows(w):
        return pl.BlockSpec((ROW_TILE, w), lambda b, j: (b * J + j, 0))

    def const(shape):
        return pl.BlockSpec(shape, lambda b, j: (0,) * len(shape))

    sd = jax.ShapeDtypeStruct
    return pl.pallas_call(
        functools.partial(_out0_kernel, alpha=alpha),
        grid=(B, J),
        in_specs=[rows(GLA_V), rows(GLA_V), rows(HG_W), rows(HG_W), rows(GLA_V), rows(HG_W), rows(D),
                  pl.BlockSpec((None, 6, D), lambda b, j: (jnp.where(j < nctx_tiles, B, b), 0, 0)),
                  const((GLA_V + HG_W, D)), const((1, GLA_DV)), const((1, HG_EXPAND)), const((1, D)), const((1, D))],
        out_specs=[rows(D), rows(D), rows(D // 2)],
        out_shape=[sd((T, D), F32), sd((T, D), F32), sd((T, D // 2), U32)],
        compiler_params=_cparams(("parallel", "parallel")),
        name="readout0",
    )(o_gf, o_gb, o_hf, o_hb, r, hgate, xcat.reshape(T, D), mods, w_out.astype(BF16),
      gla_g.reshape(1, -1), hg_g.reshape(1, -1), ln_g.reshape(1, D), ln_b.reshape(1, D))


def _route_kernel(u_ref, rwt_ref, rb_ref, tri_ref, eidx_ref, gate_ref, rank_ref, cnt_ref, carry_ref):
    i = pl.program_id(0)

    @pl.when(i == 0)
    def _():
        carry_ref[...] = jnp.zeros_like(carry_ref)

    u = u_ref[...]
    TM = u.shape[0]
    E = N_EXPERTS
    logits = lax.dot_general(rwt_ref[...], u, (((1,), (1,)), ((), ())), precision=HI,
                             preferred_element_type=F32)
    scores = jax.nn.sigmoid(logits)
    sel = scores + rb_ref[...]
    neg = -jnp.inf
    sel3 = sel.reshape(N_GROUPS, PER_GROUP, TM)
    io_g = lax.broadcasted_iota(I32, sel3.shape, 1)
    m1 = jnp.max(sel3, axis=1, keepdims=True)
    i1 = jnp.min(jnp.where(sel3 == m1, io_g, PER_GROUP), axis=1, keepdims=True)
    m2 = jnp.max(jnp.where(io_g == i1, neg, sel3), axis=1, keepdims=True)
    grp = m1 + m2
    io_n = lax.broadcasted_iota(I32, grp.shape, 0)
    keep = jnp.zeros(grp.shape, jnp.bool_)
    for _ in range(TOPK_GROUPS):
        m = jnp.max(grp, axis=0, keepdims=True)
        idx = jnp.min(jnp.where(grp == m, io_n, N_GROUPS), axis=0, keepdims=True)
        hit = io_n == idx
        keep = jnp.logical_or(keep, hit)
        grp = jnp.where(hit, neg, grp)
    sel = jnp.where(keep, sel3, neg).reshape(E, TM)
    io_e = lax.broadcasted_iota(I32, (E, TM), 0)
    base = carry_ref[...]
    tri = tri_ref[...]
    e_rows, g_rows, r_rows = [], [], []
    for _ in range(TOP_K):
        m = jnp.max(sel, axis=0, keepdims=True)
        idx = jnp.min(jnp.where(sel == m, io_e, E), axis=0, keepdims=True)
        hit = io_e == idx
        hit_f = jnp.where(hit, 1.0, 0.0)
        g_rows.append(jnp.sum(jnp.where(hit, scores, 0.0), axis=0, keepdims=True))
        prefix = jnp.dot(hit_f.astype(BF16), tri, preferred_element_type=F32)
        r_rows.append(jnp.sum(jnp.where(hit, prefix + base, 0.0), axis=0, keepdims=True))
        base = base + jnp.sum(hit_f, axis=1, keepdims=True)
        e_rows.append(idx)
        sel = jnp.where(hit, neg, sel)
    carry_ref[...] = base
    cnt_ref[...] = base
    g = jnp.concatenate(g_rows, axis=0)
    gate_ref[...] = g / jnp.sum(g, axis=0, keepdims=True) * ROUTED_SCALE
    eidx_ref[...] = jnp.concatenate(e_rows, axis=0)
    rank_ref[...] = jnp.concatenate(r_rows, axis=0).astype(I32)


def _route(u, router_w, router_bias):
    T, D = u.shape
    E = N_EXPERTS
    n = T // ROW_TILE
    tri = jnp.asarray(np.triu(np.ones((ROW_TILE, ROW_TILE), np.float32), 1), BF16)
    sd = jax.ShapeDtypeStruct
    cols = pl.BlockSpec((TOP_K, ROW_TILE), lambda i: (0, i))
    return pl.pallas_call(
        _route_kernel,
        grid=(n,),
        in_specs=[pl.BlockSpec((ROW_TILE, D), lambda i: (i, 0)),
                  pl.BlockSpec((E, D), lambda i: (0, 0)),
                  pl.BlockSpec((E, 1), lambda i: (0, 0)),
                  pl.BlockSpec((ROW_TILE, ROW_TILE), lambda i: (0, 0))],
        out_specs=[cols, cols, cols, pl.BlockSpec((E, 1), lambda i: (0, 0))],
        out_shape=[sd((TOP_K, T), I32), sd((TOP_K, T), F32), sd((TOP_K, T), I32), sd((E, 1), F32)],
        scratch_shapes=[pltpu.VMEM((E, 1), F32)],
        compiler_params=_cparams(("arbitrary",)),
        name="moe_route",
    )(u, router_w.T, router_bias.reshape(E, 1), tri)


def _dest_kernel(e_ref, r_ref, ps_ref, d_ref):
    e = e_ref[...]
    TM = e.shape[1]
    io_e = lax.broadcasted_iota(I32, (N_EXPERTS, TM), 0)
    ps = ps_ref[...]
    rows = []
    for k in range(TOP_K):
        rows.append(jnp.sum(jnp.where(io_e == e[k:k + 1, :], ps, 0.0), axis=0, keepdims=True))
    d_ref[...] = jnp.concatenate(rows, axis=0).astype(I32) + r_ref[...]


def _slot_of(eidx, rank, pad_start):
    T = eidx.shape[1]
    cols = pl.BlockSpec((TOP_K, ROW_TILE), lambda i: (0, i))
    return pl.pallas_call(
        _dest_kernel,
        grid=(T // ROW_TILE,),
        in_specs=[cols, cols, pl.BlockSpec((N_EXPERTS, 1), lambda i: (0, 0))],
        out_specs=cols,
        out_shape=jax.ShapeDtypeStruct((TOP_K, T), I32),
        compiler_params=_cparams(("parallel",)),
        name="moe_slot",
    )(eidx, rank, pad_start.astype(F32).reshape(N_EXPERTS, 1))


def _dispatch_kernel(z0_ref, zn_ref, dest_ref, u_ref, xs_ref, zero_scr, sem, zsem):
    TM = u_ref.shape[0]

    @pl.when(pl.program_id(0) == 0)
    def _():
        zero_scr[...] = jnp.zeros_like(zero_scr)

        def zcopy(e):
            row0 = pl.multiple_of(z0_ref[e], MOE_BLK)
            return pltpu.make_async_copy(zero_scr, xs_ref.at[pl.ds(row0, MOE_BLK)], zsem)

        def zstart(e, c):
            @pl.when(zn_ref[e] > 0)
            def _():
                zcopy(e).start()
            return c

        def zwait(e, c):
            @pl.when(zn_ref[e] > 0)
            def _():
                zcopy(e).wait()
            return c

        lax.fori_loop(0, N_EXPERTS, zstart, 0)
        lax.fori_loop(0, N_EXPERTS, zwait, 0)

    def copy(t, k):
        return pltpu.make_async_copy(u_ref.at[pl.ds(t, 1)], xs_ref.at[pl.ds(dest_ref[k, t], 1)], sem)

    def issue(t, c):
        for k in range(TOP_K):
            copy(t, k).start()
        return c

    lax.fori_loop(0, TM, issue, 0)
    for k in range(TOP_K):
        pltpu.make_async_copy(u_ref, xs_ref.at[pl.ds(0, TM)], sem).wait()


def _dispatch(up, dest, n_slots, pad_row0, pad_rows):
    T, W = up.shape
    return pl.pallas_call(
        _dispatch_kernel,
        grid_spec=pltpu.PrefetchScalarGridSpec(
            num_scalar_prefetch=2,
            grid=(T // ROW_TILE,),
            in_specs=[pl.BlockSpec((TOP_K, ROW_TILE), lambda i, z0, zn: (0, i), memory_space=pltpu.SMEM),
                      pl.BlockSpec((ROW_TILE, W), lambda i, z0, zn: (i, 0))],
            out_specs=pl.BlockSpec(memory_space=pl.ANY),
            scratch_shapes=[pltpu.VMEM((MOE_BLK, W), U32), pltpu.SemaphoreType.DMA, pltpu.SemaphoreType.DMA]),
        out_shape=jax.ShapeDtypeStruct((n_slots, W), U32),
        compiler_params=_cparams(("arbitrary",)),
        name="moe_dispatch",
    )(pad_row0, pad_rows, dest, up)


def _expert_kernel(be_ref, nu_ref, first_ref, nxt_ref, par_ref, xs_ref, w13_hbm, w2_hbm, ys_ref,
                   w13_buf, w2_buf, w13_bf, w2_bf, sem, *, layer):
    i = pl.program_id(0)

    def fetch(e, slot):
        return (pltpu.make_async_copy(w13_hbm.at[layer, e], w13_buf.at[slot], sem.at[0, slot]),
                pltpu.make_async_copy(w2_hbm.at[layer, e], w2_buf.at[slot], sem.at[1, slot]))

    @pl.when(i == 0)
    def _():
        for c in fetch(be_ref[0], 0):
            c.start()

    @pl.when(jnp.logical_and(i < nu_ref[0], first_ref[i] == 1))
    def _():
        slot = par_ref[i]
        for c in fetch(be_ref[i], slot):
            c.wait()

        @pl.when(nxt_ref[i] >= 0)
        def _():
            for c in fetch(nxt_ref[i], 1 - slot):
                c.start()

        w13_bf[...] = w13_buf[slot].astype(BF16)
        w2_bf[...] = w2_buf[slot].astype(BF16)

    @pl.when(i < nu_ref[0])
    def _():
        lo, hi = _unpack_rows(xs_ref[...])
        x = jnp.concatenate([lo, hi], axis=1).astype(BF16)
        h = jnp.dot(x, w13_bf[...], preferred_element_type=F32)
        a = _silu(h[:, :EXPERT_FF]) * h[:, EXPERT_FF:]
        ys_ref[...] = _pack_rows(jnp.dot(a.astype(BF16), w2_bf[...], preferred_element_type=F32))


def _expert_gemm(xs, blk_exp, n_used, cnt, w13, w2, layer):
    NP, W = xs.shape
    D = 2 * W
    nblk = NP // MOE_BLK
    F2 = w13.shape[3]
    E = cnt.shape[0]
    blk = jnp.arange(nblk, dtype=I32)
    first = jnp.concatenate([jnp.ones((1,), I32), (blk_exp[1:] != blk_exp[:-1]).astype(I32)])
    first = jnp.where(blk < n_used[0], first, 0)
    par = (jnp.cumsum(first) - 1) % 2
    ids = jnp.where(cnt > 0, jnp.arange(E, dtype=I32), E)
    suffix_min = lax.cummin(ids, axis=0, reverse=True)
    next_active = jnp.concatenate([suffix_min[1:], jnp.full((1,), E, I32)])
    next_active = jnp.where(next_active >= E, -1, next_active)
    nxt = jnp.take(next_active, blk_exp)

    def xmap(i, be, nu, fi, nx, pa):
        return (jnp.minimum(i, nu[0] - 1), 0)

    return pl.pallas_call(
        functools.partial(_expert_kernel, layer=layer),
        grid_spec=pltpu.PrefetchScalarGridSpec(
            num_scalar_prefetch=5,
            grid=(nblk,),
            in_specs=[pl.BlockSpec((MOE_BLK, W), xmap),
                      pl.BlockSpec(memory_space=pl.ANY), pl.BlockSpec(memory_space=pl.ANY)],
            out_specs=pl.BlockSpec((MOE_BLK, W), xmap),
            scratch_shapes=[pltpu.VMEM((2, D, F2), F32), pltpu.VMEM((2, F2 // 2, D), F32),
                            pltpu.VMEM((D, F2), BF16), pltpu.VMEM((F2 // 2, D), BF16),
                            pltpu.SemaphoreType.DMA((2, 2))]),
        out_shape=jax.ShapeDtypeStruct((NP, W), U32),
        compiler_params=_cparams(("arbitrary",)),
        name="moe_experts",
    )(blk_exp, n_used, first, nxt.astype(I32), par.astype(I32), xs, w13, w2)


def _combine_kernel(mt_ref, dcur_ref, dnxt_ref, gate_ref, up_ref, xl_ref, mod_ref, s13_ref, s2_ref, lng_ref, lnb_ref,
                    ys_ref, o_ref, rows_a, rows_b, sem_a, sem_b, *, alpha):
    del mt_ref
    i = pl.program_id(0)
    H = CMB_TILE
    W = up_ref.shape[1]

    def issue(dref, col0, rows, sem):
        for t in range(H):
            for k in range(TOP_K):
                pltpu.make_async_copy(ys_ref.at[pl.ds(dref[k, col0 + t], 1)], rows.at[k, pl.ds(t, 1)], sem).start()

    def wait(rows, sem):
        for k in range(TOP_K):
            pltpu.make_async_copy(ys_ref.at[pl.ds(0, H)], rows.at[k], sem).wait()

    def compute(rows, r0):
        lo, hi = _unpack_rows(up_ref[r0:r0 + H])
        hs = _bdot(jnp.concatenate([lo, hi], axis=1), s13_ref[...])
        sh = _bdot(_silu(hs[:, :SHARED_FF]) * hs[:, SHARED_FF:], s2_ref[...])
        acc_lo, acc_hi = sh[:, :W], sh[:, W:]
        gate = gate_ref[r0:r0 + H]
        for k in range(TOP_K):
            lo, hi = _unpack_rows(rows[k])
            g = gate[:, k:k + 1]
            acc_lo = acc_lo + lo * g
            acc_hi = acc_hi + hi * g
        acc = jnp.concatenate([acc_lo, acc_hi], axis=1)
        g2 = mod_ref[5:6, :]
        o_ref[r0:r0 + H] = _layer_norm_rows(alpha * xl_ref[r0:r0 + H] + g2 * acc, lng_ref[...], lnb_ref[...], LN_EPS)

    @pl.when(i == 0)
    def _():
        issue(dcur_ref, 0, rows_a, sem_a)

    wait(rows_a, sem_a)
    issue(dcur_ref, H, rows_b, sem_b)
    compute(rows_a, 0)
    wait(rows_b, sem_b)
    issue(dnxt_ref, 0, rows_a, sem_a)
    compute(rows_b, H)

    @pl.when(i == pl.num_programs(0) - 1)
    def _():
        wait(rows_a, sem_a)


def _combine(dest, gate_t, up, xl, ys, mods, mod_of_tile, sh_w13, sh_w2, ln_g, ln_b, alpha):
    T, D = xl.shape
    W = D // 2
    TM = 2 * CMB_TILE
    n = T // TM

    def rows(w):
        return pl.BlockSpec((TM, w), lambda i, mt: (i, 0))

    def const(shape):
        return pl.BlockSpec(shape, lambda i, mt: (0,) * len(shape))

    return pl.pallas_call(
        functools.partial(_combine_kernel, alpha=alpha),
        grid_spec=pltpu.PrefetchScalarGridSpec(
            num_scalar_prefetch=1,
            grid=(n,),
            in_specs=[pl.BlockSpec((TOP_K, TM), lambda i, mt: (0, i), memory_space=pltpu.SMEM),
                      pl.BlockSpec((TOP_K, TM), lambda i, mt: (0, jnp.minimum(i + 1, n - 1)), memory_space=pltpu.SMEM),
                      rows(TOP_K), rows(W), rows(D),
                      pl.BlockSpec((None, 6, D), lambda i, mt: (mt[i], 0, 0)),
                      const((D, 2 * SHARED_FF)), const((SHARED_FF, D)), const((1, D)), const((1, D)),
                      pl.BlockSpec(memory_space=pl.ANY)],
            out_specs=rows(D),
            scratch_shapes=[pltpu.VMEM((TOP_K, CMB_TILE, W), U32), pltpu.VMEM((TOP_K, CMB_TILE, W), U32),
                            pltpu.SemaphoreType.DMA, pltpu.SemaphoreType.DMA]),
        out_shape=jax.ShapeDtypeStruct((T, D), F32),
        compiler_params=_cparams(("arbitrary",)),
        name="moe_combine",
    )(mod_of_tile, dest, dest, gate_t, up, xl, mods, sh_w13.astype(BF16), sh_w2.astype(BF16),
      ln_g.reshape(1, D), ln_b.reshape(1, D), ys)


def _moe_block(u, up, xl, mods, mod_of_tile, router_w, router_bias, w13, w2, layer, sh_w13, sh_w2, ln_g, ln_b, alpha):
    T, D = u.shape
    eidx, gate, rank, counts = _route(u, router_w, router_bias)
    cnt = counts.reshape(N_EXPERTS).astype(I32)
    padded = (cnt + MOE_BLK - 1) // MOE_BLK * MOE_BLK
    pad_end = jnp.cumsum(padded)
    pad_start = pad_end - padded
    nblk = T * TOP_K // MOE_BLK + N_EXPERTS
    n_used = (pad_end[-1] // MOE_BLK).astype(I32)
    blk_row0 = jnp.arange(nblk, dtype=I32) * MOE_BLK
    blk_exp = jnp.sum((pad_end[None, :] <= blk_row0[:, None]).astype(I32), axis=1)
    last_e = jnp.max(jnp.where(cnt > 0, jnp.arange(N_EXPERTS, dtype=I32), 0))
    blk_exp = jnp.minimum(blk_exp, last_e)
    dest = _slot_of(eidx, rank, pad_start)
    xs = _dispatch(up, dest, nblk * MOE_BLK, jnp.maximum(pad_end - MOE_BLK, 0), padded)
    ys = _expert_gemm(xs, blk_exp, n_used.reshape(1), cnt, w13, w2, layer)
    return _combine(dest, gate.T, up, xl, ys, mods, mod_of_tile, sh_w13, sh_w2, ln_g, ln_b, alpha)


def _in1_kernel(x_ref, mod_ref, wh_ref, wr_ref, ph_ref, pr_ref):
    sh1 = mod_ref[0:1, :]
    sc1 = mod_ref[1:2, :]
    h = (x_ref[...] * (1.0 + sc1) + sh1).astype(BF16)
    ph_ref[...] = jnp.dot(h, wh_ref[...], preferred_element_type=F32)
    pr_ref[...] = jnp.dot(h, wr_ref[...], preferred_element_type=F32)


RW_PW = 1920


def _rw_reorder(t):
    a = 3 * RW_W
    lo = 2 * RW_DECAY_LORA + RW_AAA_LORA
    out = jnp.concatenate([t[..., :a], t[..., a + lo:], t[..., a:a + lo]], axis=-1)
    pad = [(0, 0)] * (t.ndim - 1) + [(0, RW_PW - out.shape[-1])]
    return jnp.pad(out, pad)


def _in_proj1(xcat, mods, nctx_tiles, w_in):
    B, S, D = xcat.shape
    J = S // ROW_TILE
    T = B * S
    wh = w_in[:, :HY_WIDTH].astype(BF16)
    wr = _rw_reorder(w_in[:, HY_WIDTH:]).astype(BF16)

    def rows(w):
        return pl.BlockSpec((ROW_TILE, w), lambda b, j: (b * J + j, 0))

    def const(shape):
        return pl.BlockSpec(shape, lambda b, j: (0,) * len(shape))

    sd = jax.ShapeDtypeStruct
    return pl.pallas_call(
        _in1_kernel,
        grid=(B, J),
        in_specs=[rows(D), pl.BlockSpec((None, 6, D), lambda b, j: (jnp.where(j < nctx_tiles, B, b), 0, 0)),
                  const((D, HY_WIDTH)), const((D, RW_PW))],
        out_specs=[rows(HY_WIDTH), rows(RW_PW)],
        out_shape=[sd((T, HY_WIDTH), F32), sd((T, RW_PW), F32)],
        compiler_params=_cparams(("parallel", "parallel")),
        name="in_proj1",
    )(xcat.reshape(T, D), mods, wh, wr)


def _rw_streams_kernel(prev_ref, main_ref, next_ref, mu_ref, w0_ref, w2_ref, a0_ref, a2_ref, g2_ref,
                       kk_ref, ka_ref, bd_ref,
                       r_ref, k_ref, v_ref, kkn_ref, bb_ref, ld_ref, g_ref, sh_scr, *, nctx_tiles, n_lat_tiles):
    j = pl.program_id(1)
    TM = main_ref.shape[0]
    W = main_ref.shape[1]
    H = GRID_W
    p = main_ref[...]
    ext = jnp.concatenate([prev_ref[...], p, next_ref[...]], axis=0)
    left = ext[H - 1:H - 1 + TM]
    right = ext[H + 1:H + 1 + TM]
    up = ext[0:TM]
    down = ext[2 * H:2 * H + TM]
    i = lax.broadcasted_iota(I32, (TM, W), 0)
    lane = lax.broadcasted_iota(I32, (TM, W), 1)
    even = (lane & 1) == 0
    c4 = lane & 3
    jl = j - nctx_tiles

    @pl.when(j < nctx_tiles)
    def _():
        lo = jnp.where(j == 0, 1, 0)
        hi = jnp.where(j == nctx_tiles - 1, TM - 1, TM)
        sh_scr[...] = jnp.where(even, jnp.where(i >= lo, left, 0.0), jnp.where(i < hi, right, 0.0))

    @pl.when(j >= nctx_tiles)
    def _():
        col = i & (H - 1)
        up_lo = jnp.where(jl == 0, H, 0)
        down_hi = jnp.where(jl == n_lat_tiles - 1, TM - H, TM)
        l_v = jnp.where(col != 0, left, 0.0)
        r_v = jnp.where(col != H - 1, right, 0.0)
        u_v = jnp.where(i >= up_lo, up, 0.0)
        d_v = jnp.where(i < down_hi, down, 0.0)
        sh_scr[...] = jnp.where(c4 == 0, l_v, jnp.where(c4 == 1, r_v, jnp.where(c4 == 2, u_v, d_v)))

    pm = p + mu_ref[...] * (sh_scr[...] - p)
    r = pm[:, 0:RW_W]
    k = pm[:, RW_W:2 * RW_W]
    v = pm[:, 2 * RW_W:3 * RW_W]
    o = 3 * RW_W
    gl = pm[:, o:o + RW_GATE_LORA]
    o += RW_GATE_LORA
    wl_f = pm[:, o:o + RW_DECAY_LORA]
    wl_b = pm[:, o + RW_DECAY_LORA:o + 2 * RW_DECAY_LORA]
    al = pm[:, o + 2 * RW_DECAY_LORA:o + 2 * RW_DECAY_LORA + RW_AAA_LORA]
    for d, wl in enumerate((wl_f, wl_b)):
        z = w0_ref[d:d + 1, :] + _bdot(jnp.tanh(wl), w2_ref[d])
        w = -(jnp.maximum(-z, 0.0) + jnp.log(1.0 + jnp.exp(-jnp.abs(z)))) - 0.5
        ld_ref[d] = -jnp.exp(w)
    a = jax.nn.sigmoid(a0_ref[...] + _bdot(al, a2_ref[...]))
    g_ref[...] = _bdot(jax.nn.sigmoid(gl), g2_ref[...])
    kk = k * kk_ref[...]
    n2 = _hdot(kk * kk, bd_ref[...])
    kkn = kk / jnp.maximum(jnp.sqrt(n2), 1e-12)
    r_ref[...] = r
    k_ref[...] = k * (1.0 + (a - 1.0) * ka_ref[...])
    v_ref[...] = v
    kkn_ref[...] = kkn
    bb_ref[...] = kkn * a


def _head_sum_matrix(width, hd):
    i = np.arange(width)
    return jnp.asarray((i[:, None] // hd == i[None, :] // hd).astype(np.float32))


def _rw_streams(p_rw, B, S, nctx_tiles, mu, w0, w2, a0, a2, g2, k_k, k_a):
    T = B * S
    J = S // ROW_TILE
    HB = ROW_TILE // GRID_W
    NH = S // GRID_W

    def rows(w):
        return pl.BlockSpec((ROW_TILE, w), lambda b, j: (b * J + j, 0))

    def rows2(w):
        return pl.BlockSpec((2, ROW_TILE, w), lambda b, j: (0, b * J + j, 0))

    def const(shape):
        return pl.BlockSpec(shape, lambda b, j: (0,) * len(shape))

    sd = jax.ShapeDtypeStruct
    kern = functools.partial(_rw_streams_kernel, nctx_tiles=nctx_tiles, n_lat_tiles=J - nctx_tiles)
    return pl.pallas_call(
        kern,
        grid=(B, J),
        in_specs=[pl.BlockSpec((GRID_W, RW_PW), lambda b, j: (b * NH + jnp.maximum(j * HB - 1, 0), 0)),
                  rows(RW_PW),
                  pl.BlockSpec((GRID_W, RW_PW), lambda b, j: (b * NH + jnp.minimum((j + 1) * HB, NH - 1), 0)),
                  const((1, RW_PW)), const((2, RW_W)), const((2, RW_DECAY_LORA, RW_W)), const((1, RW_W)),
                  const((RW_AAA_LORA, RW_W)), const((RW_GATE_LORA, RW_W)), const((1, RW_W)), const((1, RW_W)),
                  const((RW_W, RW_W))],
        out_specs=[rows(RW_W), rows(RW_W), rows(RW_W), rows(RW_W), rows(RW_W), rows2(RW_W), rows(RW_W)],
        out_shape=[sd((T, RW_W), F32)] * 5 + [sd((2, T, RW_W), F32), sd((T, RW_W), F32)],
        scratch_shapes=[pltpu.VMEM((ROW_TILE, RW_PW), F32)],
        compiler_params=_cparams(("parallel", "parallel")),
        name="rwkv_streams",
    )(p_rw, p_rw, p_rw, _rw_reorder(mu).reshape(1, RW_PW), w0, w2, a0.reshape(1, RW_W), a2, g2,
      k_k.reshape(1, RW_W), k_a.reshape(1, RW_W), _head_sum_matrix(RW_W, RW_HEAD_DIM))


def _rwkv_kernel(r_ref, k_ref, v_ref, kk_ref, bb_ref, ld_ref, o_ref, st_ref, *, rev, nchunk):
    t = pl.program_id(1)

    @pl.when(t == 0)
    def _():
        st_ref[...] = jnp.zeros_like(st_ref)

    C = RW_CHUNK
    hd = RW_HEAD_DIM
    NH = RW_HEADS
    row = lax.broadcasted_iota(I32, (C, C), 0)
    col = lax.broadcasted_iota(I32, (C, C), 1)
    incl = ((col >= row) if rev else (col <= row))[None]
    strict = ((col > row) if rev else (col < row))[None]
    tri = jnp.where(incl[0], 1.0, 0.0).astype(F32)
    eye = (row == col)[None]
    last_i = 0 if rev else C - 1
    order = range(nchunk - 1, -1, -1) if rev else range(nchunk)
    n_double = int(math.log2(C)) - 1

    def stack(x):
        return jnp.stack([x[c * C:(c + 1) * C, h * hd:(h + 1) * hd] for c in range(nchunk) for h in range(NH)], 0)

    ld = ld_ref[...]
    g = jnp.concatenate([_hdot(tri, ld[c * C:(c + 1) * C]) for c in range(nchunk)], axis=0)
    g_last = jnp.concatenate([jnp.broadcast_to(g[c * C + last_i:c * C + last_i + 1], (C, NH * hd))
                              for c in range(nchunk)], axis=0)
    k = k_ref[...]
    bb = bb_ref[...]
    eng = jnp.exp(-g)
    e_end = jnp.exp(g_last - g)
    kk_t = stack(kk_ref[...] * jnp.exp(g - ld))
    b_t = stack(bb * eng)
    k_t = stack(k * eng)
    r_t = stack(r_ref[...] * jnp.exp(g))
    b_bar = stack(bb * e_end)
    k_bar = stack(k * e_end)
    dec = stack(jnp.exp(g_last))
    v = stack(v_ref[...])
    m_b = jnp.where(strict, _bmm_nt(kk_t, b_t), 0.0)
    m_k = jnp.where(strict, _bmm_nt(kk_t, k_t), 0.0)
    a_rb = jnp.where(incl, _bmm_nt(r_t, b_t), 0.0)
    a_rk = jnp.where(incl, _bmm_nt(r_t, k_t), 0.0)
    pw = -m_b
    tinv = jnp.where(eye, 1.0, 0.0) + pw
    for _ in range(n_double):
        pw = _bmm(pw, pw)
        tinv = tinv + _bmm(tinv, pw)
    a_t = _bmm(tinv, kk_t)
    u_b = _bmm(tinv, _bmm(m_k, v))
    q_h = r_t - _bmm(a_rb, a_t)
    o_h = _bmm(a_rk, v) - _bmm(a_rb, u_b)
    g_m = jnp.where(eye, dec, 0.0) - _bmm_tn(a_t, b_bar)
    h_m = _bmm_tn(v, k_bar) - _bmm_tn(u_b, b_bar)
    s = st_ref[...]
    for c in order:
        sl = slice(c * NH, (c + 1) * NH)
        o_c = _bmm_nt(q_h[sl], s) + o_h[sl]
        o_ref[c * C:(c + 1) * C, :] = jnp.concatenate([o_c[h] for h in range(NH)], axis=1)
        s = _bmm(s, g_m[sl]) + h_m[sl]
    st_ref[...] = s


def _rwkv_scan(r, k, v, kk, bb, ld, B, S, nctx_blocks, rev):
    T = B * S
    NB = S // ROW_TILE
    d = 1 if rev else 0
    nc = nctx_blocks

    def blk(t):
        if not rev:
            return t
        return jnp.where(t < nc, nc - 1 - t, NB - 1 - (t - nc))

    spec = pl.BlockSpec((ROW_TILE, RW_W), lambda b, t: (b * NB + blk(t), 0))
    kern = functools.partial(_rwkv_kernel, rev=rev, nchunk=ROW_TILE // RW_CHUNK)
    return pl.pallas_call(
        kern,
        grid=(B, NB),
        in_specs=[spec, spec, spec, spec, spec,
                  pl.BlockSpec((None, ROW_TILE, RW_W), lambda b, t: (d, b * NB + blk(t), 0))],
        out_specs=spec,
        out_shape=jax.ShapeDtypeStruct((T, RW_W), F32),
        scratch_shapes=[pltpu.VMEM((RW_HEADS, RW_HEAD_DIM, RW_HEAD_DIM), F32)],
        compiler_params=_cparams(("parallel", "arbitrary")),
        name="rwkv_scan_bwd" if rev else "rwkv_scan_fwd",
    )(r, k, v, kk, bb, ld)


def _filter_mlp_kernel(z_ref, w1_ref, b1_ref, w2_ref, b2_ref, w3_ref, sf_ref, win_ref, f_ref):
    h = jnp.sin(sf_ref[0:1, :] * (_hdot(z_ref[...], w1_ref[...]) + b1_ref[...]))
    h = jnp.sin(sf_ref[1:2, :] * (_hdot(h, w2_ref[...]) + b2_ref[...]))
    f_ref[...] = _hdot(h, w3_ref[...]) * win_ref[...]


def _hyena_filters(L, w1, b1, w2, b2, w3, sin_freq):
    t = np.linspace(0.0, 1.0, L, dtype=np.float32)[:, None]
    bands = (HY_EMB - 1) // 2
    wpos = (2.0 * math.pi * np.arange(L, dtype=np.float32)[:, None] / L).astype(np.float32)
    fr = np.linspace(1e-4, bands - 1, bands, dtype=np.float32)[None, :]
    z = np.concatenate([t, np.cos(fr * wpos), -np.sin(fr * wpos)], -1).astype(np.float32)
    zp = jnp.asarray(np.pad(z, ((0, 0), (0, LANES - HY_EMB))))
    w1p = jnp.pad(w1, ((0, LANES - HY_EMB), (0, 0)))
    max_decay = math.log(HY_TARGET) / HY_FAST_DECAY
    min_decay = math.log(HY_TARGET) / HY_SLOW_DECAY
    deltas = np.linspace(min_decay, max_decay, HY_CH, dtype=np.float32)
    window = np.exp(-t * np.abs(deltas)).astype(np.float32)
    FW = HY_ORDER * 2 * HY_CH
    win = jnp.asarray(np.tile(window, (1, HY_ORDER * 2)))
    TR = min(L, 512)
    Hd = w1.shape[1]
    return pl.pallas_call(
        _filter_mlp_kernel,
        grid=(L // TR,),
        in_specs=[pl.BlockSpec((TR, LANES), lambda i: (i, 0)),
                  pl.BlockSpec((LANES, Hd), lambda i: (0, 0)), pl.BlockSpec((1, Hd), lambda i: (0, 0)),
                  pl.BlockSpec((Hd, Hd), lambda i: (0, 0)), pl.BlockSpec((1, Hd), lambda i: (0, 0)),
                  pl.BlockSpec((Hd, FW), lambda i: (0, 0)), pl.BlockSpec((2, Hd), lambda i: (0, 0)),
                  pl.BlockSpec((TR, FW), lambda i: (i, 0))],
        out_specs=pl.BlockSpec((TR, FW), lambda i: (i, 0)),
        out_shape=jax.ShapeDtypeStruct((L, FW), F32),
        compiler_params=_cparams(("parallel",)),
        name="hyena_filter_mlp",
    )(zp, w1p, b1.reshape(1, Hd), w2, b2.reshape(1, Hd), w3, sin_freq, win)


def _sconv_kernel(p_ref, w_ref, b_ref, o_ref):
    p = p_ref[...]
    L = p.shape[0]
    i = lax.broadcasted_iota(I32, p.shape, 0)
    prev = jnp.where(i == 0, 0.0, pltpu.roll(p, 1, 0))
    nxt = jnp.where(i == L - 1, 0.0, pltpu.roll(p, L - 1, 0))
    o_ref[...] = prev * w_ref[0:1, :] + p * w_ref[1:2, :] + nxt * w_ref[2:3, :] + b_ref[...]


def _short_conv(p_hy, B, S, Lc, conv_w, conv_b):
    L = S - Lc
    p3 = p_hy.reshape(B, S, HY_WIDTH)[:, Lc:, :]
    return pl.pallas_call(
        _sconv_kernel,
        grid=(B, HY_WIDTH // LANES),
        in_specs=[pl.BlockSpec((None, L, LANES), lambda b, c: (b, 0, c)),
                  pl.BlockSpec((HY_SHORT, LANES), lambda b, c: (0, c)),
                  pl.BlockSpec((1, LANES), lambda b, c: (0, c))],
        out_specs=pl.BlockSpec((None, L, LANES), lambda b, c: (b, 0, c)),
        out_shape=jax.ShapeDtypeStruct((B, L, HY_WIDTH), F32),
        compiler_params=_cparams(("parallel", "parallel")),
        name="hyena_short_conv",
    )(p3, conv_w, conv_b.reshape(1, HY_WIDTH))


def _dft_constants(L):
    n = 2 * L
    n2 = FFT_N2
    n1 = n // n2
    na = n1 // 2
    g8 = SUBLANES
    w1 = np.exp(-2j * np.pi * np.outer(np.arange(n1), np.arange(n1)) / n1)
    eye8 = np.eye(g8)

    def kron_fwd(w, real_in):
        k1n, an = w.shape
        blocks = np.stack([np.stack([w.real, -w.imag], 1), np.stack([w.imag, w.real], 1)], 1)
        if real_in:
            blocks = blocks[:, :, 0:1, :]
        m = np.einsum('kria,bc->krbiac', blocks, eye8)
        return m.reshape(k1n * 2 * g8, blocks.shape[2] * an * g8)

    m1 = kron_fwd(w1[:, :na], False)
    m1f = kron_fwd(w1, True)
    cw = np.conj(w1[:, :na]).T / n
    blocks = np.stack([np.stack([cw.real, -cw.imag], 1), np.stack([cw.imag, cw.real], 1)], 0)
    m1inv = np.einsum('iark,bc->iabkrc', blocks, eye8).reshape(2 * na * g8, n1 * 2 * g8)
    w2 = np.exp(-2j * np.pi * np.outer(np.arange(n2), np.arange(n2)) / n2)
    w2big = np.block([[w2.real, -w2.imag], [w2.imag, w2.real]])
    w2c = np.conj(w2)
    iw2big = np.block([[w2c.real, -w2c.imag], [w2c.imag, w2c.real]])
    tw = np.exp(-2j * np.pi * np.arange(n2) / n)
    tw1 = np.stack([np.broadcast_to(tw.real[:, None], (n2, LANES)),
                    np.broadcast_to(tw.imag[:, None], (n2, LANES))], 0)
    c = lambda x, dt: jnp.asarray(np.ascontiguousarray(x), dt)
    return dict(m1=c(m1, BF16), m1f=c(m1f, BF16), m1inv=c(m1inv, BF16), w2=c(w2big, BF16), iw2=c(iw2big, BF16),
                tw1=c(tw1, F32), n1=n1, na=na)


def _cplx_rows(z, twr, twi, conj):
    n2 = z.shape[0] // 2
    zr, zi = z[:n2], z[n2:]
    if conj:
        return jnp.concatenate([zr * twr + zi * twi, zi * twr - zr * twi], axis=0)
    return jnp.concatenate([zr * twr - zi * twi, zi * twr + zr * twi], axis=0)


def _split_dot(m, x):
    hi = x.astype(BF16)
    lo = (x - hi.astype(F32)).astype(BF16)
    return jnp.dot(m, hi, preferred_element_type=F32) + jnp.dot(m, lo, preferred_element_type=F32)


def _next_twiddle(tw, tw1_ref):
    twr, twi = tw
    b_r, b_i = tw1_ref[0], tw1_ref[1]
    return twr * b_r - twi * b_i, twr * b_i + twi * b_r


def _filt_fft_kernel(f_ref, m1_ref, w2_ref, tw1_ref, o_ref, y_scr, *, n1):
    nbg = FFT_N2 // SUBLANES
    C = f_ref.shape[-1]
    for bg in range(nbg):
        yg = _split_dot(m1_ref[...], f_ref[:, bg].reshape(n1 * SUBLANES, C))
        y_scr[:, :, bg] = yg.reshape(n1, 2, SUBLANES, C)

    def body(k1, tw):
        y = _cplx_rows(y_scr[k1].reshape(2 * FFT_N2, C), tw[0], tw[1], False)
        o_ref[k1] = _split_dot(w2_ref[...], y)
        return _next_twiddle(tw, tw1_ref)

    lax.fori_loop(0, n1, body, (jnp.ones((FFT_N2, C), F32), jnp.zeros((FFT_N2, C), F32)))


def _filter_spectrum(filt, L, dc):
    n1 = dc['n1']
    f4 = filt.reshape(L, HY_ORDER, 2, HY_CH)
    h_f, h_b = f4[:, :, 0, :], f4[:, :, 1, :]
    full = jnp.concatenate([h_f, jnp.zeros_like(h_f[:1]), jnp.flip(h_b[1:], 0)], 0)
    full = jnp.transpose(full, (1, 0, 2)).reshape(HY_ORDER, n1, FFT_N2 // SUBLANES, SUBLANES, HY_CH)
    nbg = FFT_N2 // SUBLANES
    return pl.pallas_call(
        functools.partial(_filt_fft_kernel, n1=n1),
        grid=(HY_ORDER, HY_CH // LANES),
        in_specs=[pl.BlockSpec((None, n1, nbg, SUBLANES, LANES), lambda o, c: (o, 0, 0, 0, c)),
                  pl.BlockSpec(dc['m1f'].shape, lambda o, c: (0, 0)),
                  pl.BlockSpec(dc['w2'].shape, lambda o, c: (0, 0)),
                  pl.BlockSpec(dc['tw1'].shape, lambda o, c: (0, 0, 0))],
        out_specs=pl.BlockSpec((None, n1, 2 * FFT_N2, LANES), lambda o, c: (o, 0, 0, c)),
        out_shape=jax.ShapeDtypeStruct((HY_ORDER, n1, 2 * FFT_N2, HY_CH), F32),
        scratch_shapes=[pltpu.VMEM((n1, 2, nbg, SUBLANES, LANES), F32)],
        compiler_params=_cparams(("parallel", "parallel")),
        name="hyena_filter_fft",
    )(full, dc['m1f'], dc['w2'], dc['tw1'])


def _hyconv_kernel(za_ref, zb_ref, ga_ref, gb_ref, ff_ref, bias_ref, m1_ref, m1i_ref, w2_ref, iw2_ref,
                   tw1_ref, oa_ref, ob_ref, y_scr, *, n1, na):
    nbg = FFT_N2 // SUBLANES
    C = za_ref.shape[-1]
    half = na * SUBLANES
    for bg in range(nbg):
        xg = jnp.concatenate([za_ref[:, bg].reshape(half, C), zb_ref[:, bg].reshape(half, C)], axis=0)
        yg = jnp.dot(m1_ref[...], xg.astype(BF16), preferred_element_type=F32)
        y_scr[:, :, bg] = yg.reshape(n1, 2, SUBLANES, C)

    def body(k1, tw):
        y = _cplx_rows(y_scr[k1].reshape(2 * FFT_N2, C), tw[0], tw[1], False)
        z = jnp.dot(w2_ref[...], y.astype(BF16), preferred_element_type=F32)
        f = ff_ref[k1]
        zr, zi = z[:FFT_N2], z[FFT_N2:]
        fr, fi = f[:FFT_N2], f[FFT_N2:]
        p = jnp.concatenate([zr * fr - zi * fi, zr * fi + zi * fr], axis=0)
        v = jnp.dot(iw2_ref[...], p.astype(BF16), preferred_element_type=F32)
        v = _cplx_rows(v, tw[0], tw[1], True)
        y_scr[k1] = v.reshape(2, nbg, SUBLANES, C)
        return _next_twiddle(tw, tw1_ref)

    lax.fori_loop(0, n1, body, (jnp.ones((FFT_N2, C), F32), jnp.zeros((FFT_N2, C), F32)))
    bias = bias_ref[...]
    for bg in range(nbg):
        vg = y_scr[:, :, bg].reshape(n1 * 2 * SUBLANES, C)
        out = jnp.dot(m1i_ref[...], vg.astype(BF16), preferred_element_type=F32)
        ya = out[:half].reshape(na, SUBLANES, C)
        yb = out[half:].reshape(na, SUBLANES, C)
        za = za_ref[:, bg]
        zb = zb_ref[:, bg]
        oa_ref[:, bg] = ga_ref[:, bg] * (ya + za * bias)
        ob_ref[:, bg] = gb_ref[:, bg] * (yb + zb * bias)


def _hyena_conv(z, z_col0, gates, g_col0, spec, bias, dc, B, L):
    n1, na = dc['n1'], dc['na']
    nbg = FFT_N2 // SUBLANES
    NCB = HY_CH // LANES

    def view(a):
        return a.reshape(B, na, nbg, SUBLANES, a.shape[-1])

    def seq(col0, which):
        return pl.BlockSpec((None, na, nbg, SUBLANES, LANES),
                            lambda c, p: (2 * p + which, 0, 0, 0, col0 + c))

    def const(a):
        nd = a.ndim
        return pl.BlockSpec(a.shape, lambda c, p: (0,) * nd)

    out_a, out_b = pl.pallas_call(
        functools.partial(_hyconv_kernel, n1=n1, na=na),
        grid=(NCB, B // 2),
        in_specs=[seq(z_col0, 0), seq(z_col0, 1), seq(g_col0, 0), seq(g_col0, 1),
                  pl.BlockSpec((n1, 2 * FFT_N2, LANES), lambda c, p: (0, 0, c), pipeline_mode=pl.Buffered(1)),
                  pl.BlockSpec((1, LANES), lambda c, p: (0, c)),
                  const(dc['m1']), const(dc['m1inv']), const(dc['w2']), const(dc['iw2']), const(dc['tw1'])],
        out_specs=[pl.BlockSpec((None, na, nbg, SUBLANES, LANES), lambda c, p: (p, 0, 0, 0, c)),
                   pl.BlockSpec((None, na, nbg, SUBLANES, LANES), lambda c, p: (p, 0, 0, 0, c))],
        out_shape=[jax.ShapeDtypeStruct((B // 2, na, nbg, SUBLANES, HY_CH), F32)] * 2,
        scratch_shapes=[pltpu.VMEM((n1, 2, nbg, SUBLANES, LANES), F32)],
        compiler_params=_cparams(("parallel", "arbitrary")),
        name="hyena_long_conv",
    )(view(z), view(z), view(gates), view(gates), spec, bias.reshape(1, HY_CH),
      dc['m1'], dc['m1inv'], dc['w2'], dc['iw2'], dc['tw1'])
    out = jnp.stack([out_a, out_b], axis=1)
    return out.reshape(B, L, HY_CH)


def _out1_kernel(hy_ref, of_ref, ob_ref, r_ref, k_ref, v_ref, g_ref, x_ref, mod_ref, wo_ref,
                 rk_ref, lg_ref, lb_ref, bd_ref, lng_ref, lnb_ref, xl_ref, u_ref, up_ref, *, alpha):
    o = of_ref[...] + ob_ref[...]
    bd = bd_ref[...]
    inv = 1.0 / RW_HEAD_DIM
    mu = _hdot(o, bd) * inv
    oc = o - mu
    var = _hdot(oc * oc, bd) * inv
    on = oc * lax.rsqrt(var + RW_GN_EPS) * lg_ref[...] + lb_ref[...]
    bonus = _hdot(r_ref[...] * k_ref[...] * rk_ref[...], bd) * v_ref[...]
    y = jnp.concatenate([hy_ref[...], (on + bonus) * g_ref[...]], axis=1)
    _post_mix(y, x_ref[...], mod_ref, wo_ref, lng_ref, lnb_ref, alpha, xl_ref, u_ref, up_ref)


def _readout1(hy, o_f, o_b, r, k, v, g, xcat, mods, B, S, Lc, w_out, r_k, ln_g, ln_b, dn_g, dn_b, alpha):
    D = xcat.shape[-1]
    L = S - Lc
    J = S // ROW_TILE
    JL = L // ROW_TILE
    JC = Lc // ROW_TILE

    def cat_rows(w):
        return pl.BlockSpec((ROW_TILE, w), lambda b, j: (b * J + JC + j, 0))

    def lat_rows(w):
        return pl.BlockSpec((ROW_TILE, w), lambda b, j: (b * JL + j, 0))

    def const(shape):
        return pl.BlockSpec(shape, lambda b, j: (0,) * len(shape))

    sd = jax.ShapeDtypeStruct
    return pl.pallas_call(
        functools.partial(_out1_kernel, alpha=alpha),
        grid=(B, JL),
        in_specs=[lat_rows(HY_CH), cat_rows(RW_W), cat_rows(RW_W), cat_rows(RW_W), cat_rows(RW_W), cat_rows(RW_W),
                  cat_rows(RW_W), cat_rows(D),
                  pl.BlockSpec((None, 6, D), lambda b, j: (b, 0, 0)),
                  const((HY_CH + RW_W, D)), const((1, RW_W)), const((1, RW_W)), const((1, RW_W)),
                  const((RW_W, RW_W)), const((1, D)), const((1, D))],
        out_specs=[lat_rows(D), lat_rows(D), lat_rows(D // 2)],
        out_shape=[sd((B * L, D), F32), sd((B * L, D), F32), sd((B * L, D // 2), U32)],
        compiler_params=_cparams(("parallel", "parallel")),
        name="readout1",
    )(hy, o_f, o_b, r, k, v, g, xcat.reshape(B * S, D), mods, w_out.astype(BF16),
      r_k.reshape(1, RW_W), ln_g.reshape(1, RW_W), ln_b.reshape(1, RW_W),
      _head_sum_matrix(RW_W, RW_HEAD_DIM), dn_g.reshape(1, D), dn_b.reshape(1, D))


def kernel(x, c, ctx, c_ctx, mod_w, mod_b, ln1_g, ln1_b, ln2_g, ln2_b, ev_w_in, ev_w_out, gla_gate_w2, gla_gate_b, gla_norm_g, hg_lb_logits, hg_norm_g, od_w_in, od_w_out, hy_conv_w, hy_conv_b, hy_ffn_w1, hy_ffn_b1, hy_ffn_w2, hy_ffn_b2, hy_ffn_w3, hy_sin_freq, hy_bias, rw_mu, rw_w0, rw_w2, rw_a0, rw_a2, rw_g2, rw_k_k, rw_k_a, rw_r_k, rw_ln_g, rw_ln_b, router_w, router_bias, exp_w13, exp_w2, sh_w13, sh_w2):
    B, L, D = x.shape
    Lc = ctx.shape[1]
    S = Lc + L
    depth = mod_w.shape[0]
    assert depth == 2 and L % ROW_TILE == 0 and Lc % ROW_TILE == 0 and B % 2 == 0
    assert L % (FFT_N2 * 2) == 0 and L % GRID_W == 0
    alpha = (2 * depth) ** 0.25
    nctx = Lc // ROW_TILE
    J = S // ROW_TILE
    T = B * S

    cc = jnp.concatenate([c, c_ctx[None, :]], axis=0)
    cc = jnp.pad(cc, ((0, (-cc.shape[0]) % SUBLANES), (0, 0)))
    hg_lb = jnp.cumsum(jax.nn.softmax(hg_lb_logits.astype(F32), axis=0), axis=0)
    xcat = jnp.concatenate([ctx, x], axis=1)

    mods = _modulation(cc, mod_w[0], mod_b[0])
    gq, gk, gv, gla, r, hq, hk, hla, hv, hgate = _in_proj0(xcat, mods, nctx, ev_w_in[0], gla_gate_w2[0],
                                                            gla_gate_b[0], hg_lb[0])
    gk3 = gk.reshape(1, T, GLA_QK)
    o_gf = _gated_recurrence(gq, gk3, gv, gla, 0, B, S, nctx, False, GLA_HEADS, GLA_DK, GLA_DV)
    o_gb = _gated_recurrence(gq, gk3, gv, gla, 0, B, S, nctx, True, GLA_HEADS, GLA_DK, GLA_DV)
    o_hf = _gated_recurrence(hq, hk, hv, hla, 0, B, S, nctx, False, HG_HEADS, HG_EXPAND, HG_EXPAND)
    o_hb = _gated_recurrence(hq, hk, hv, hla, 1, B, S, nctx, True, HG_HEADS, HG_EXPAND, HG_EXPAND)
    xl, u, up = _readout0(o_gf, o_gb, o_hf, o_hb, r, hgate, xcat, mods, nctx, ev_w_out[0], gla_norm_g[0],
                          hg_norm_g[0], ln1_g[0], ln1_b[0], alpha)
    cmb_rows = 2 * CMB_TILE
    tiles_per_b = S // cmb_rows
    tile_in_b = jnp.arange(B * tiles_per_b, dtype=I32) % tiles_per_b
    mod_of_tile = jnp.where(tile_in_b < Lc // cmb_rows, B, jnp.arange(B * tiles_per_b, dtype=I32) // tiles_per_b)
    x1 = _moe_block(u, up, xl, mods, mod_of_tile.astype(I32), router_w[0], router_bias[0], exp_w13, exp_w2, 0,
                    sh_w13[0], sh_w2[0], ln2_g[0], ln2_b[0], alpha)
    xcat = x1.reshape(B, S, D)

    mods = _modulation(cc, mod_w[1], mod_b[1])
    p_hy, p_rw = _in_proj1(xcat, mods, nctx, od_w_in[0])
    rr, rk, rv, rkk, rbb, rld, rg = _rw_streams(p_rw, B, S, nctx, rw_mu[0], rw_w0[0], rw_w2[0], rw_a0[0],
                                                 rw_a2[0], rw_g2[0], rw_k_k[0], rw_k_a[0])
    o_f = _rwkv_scan(rr, rk, rv, rkk, rbb, rld, B, S, nctx, False)
    o_b = _rwkv_scan(rr, rk, rv, rkk, rbb, rld, B, S, nctx, True)
    dc = _dft_constants(L)
    filt = _hyena_filters(L, hy_ffn_w1[0], hy_ffn_b1[0], hy_ffn_w2[0], hy_ffn_b2[0], hy_ffn_w3[0], hy_sin_freq[0])
    spec = _filter_spectrum(filt, L, dc)
    uu = _short_conv(p_hy, B, S, Lc, hy_conv_w[0], hy_conv_b[0])
    NCB = HY_CH // LANES
    z1 = _hyena_conv(uu, 0, uu, NCB, spec[0], hy_bias[0, 0], dc, B, L)
    z2 = _hyena_conv(z1, 0, uu, 2 * NCB, spec[1], hy_bias[0, 1], dc, B, L)
    xl, u, up = _readout1(z2.reshape(B * L, HY_CH), o_f, o_b, rr, rk, rv, rg, xcat, mods, B, S, Lc, od_w_out[0],
                          rw_r_k[0], rw_ln_g[0], rw_ln_b[0], ln1_g[1], ln1_b[1], alpha)
    mod_of_tile = (jnp.arange(B * (L // cmb_rows), dtype=I32) // (L // cmb_rows)).astype(I32)
    out = _moe_block(u, up, xl, mods, mod_of_tile, router_w[1], router_bias[1], exp_w13, exp_w2, 1,
                     sh_w13[1], sh_w2[1], ln2_g[1], ln2_b[1], alpha)
    return out.reshape(B, L, D)
```

```python
import functools
import math

import numpy as np
import jax
import jax.numpy as jnp
from jax import lax
from jax.experimental import pallas as pl
from jax.experimental.pallas import tpu as pltpu

F32 = jnp.float32
BF16 = jnp.bfloat16
I32 = jnp.int32
U32 = jnp.uint32
HI = lax.Precision.HIGHEST

LN_EPS = 1e-5
GLA_HEADS, GLA_DK, GLA_DV = 4, 64, 128
GLA_QK, GLA_V = GLA_HEADS * GLA_DK, GLA_HEADS * GLA_DV
GLA_GATE_RANK = 16
GLA_GATE_NORM = 16.0
CHUNK = 64
HG_HEADS, HG_EXPAND = 4, 128
HG_W = HG_HEADS * HG_EXPAND
HY_CH, HY_ORDER, HY_SHORT, HY_EMB = 512, 2, 3, 33
HY_WIDTH = (HY_ORDER + 1) * HY_CH
HY_FAST_DECAY, HY_SLOW_DECAY, HY_TARGET = 0.3, 1.5, 1e-2
RW_HEADS, RW_HEAD_DIM = 8, 64
RW_W = RW_HEADS * RW_HEAD_DIM
RW_DECAY_LORA, RW_AAA_LORA, RW_GATE_LORA = 64, 64, 128
RW_GN_EPS = 64e-5
GRID_W = 64
N_EXPERTS, TOP_K, N_GROUPS, TOPK_GROUPS = 256, 8, 8, 4
PER_GROUP = N_EXPERTS // N_GROUPS
EXPERT_FF, SHARED_FF = 256, 256
ROUTED_SCALE = 2.5

LANES = 128
SUBLANES = 8
VMEM_LIMIT = 56 * 1024 * 1024
ROW_TILE = 256
MOE_BLK = 256
CMB_TILE = 128
RW_CHUNK = 64
FFT_N2 = 128
FFT_G = 4


def _cparams(sem):
    return pltpu.CompilerParams(dimension_semantics=sem, vmem_limit_bytes=VMEM_LIMIT)


def _bdot(a, b):
    return jnp.dot(a.astype(BF16), b.astype(BF16), preferred_element_type=F32)


def _bdot_nt(a, b):
    return lax.dot_general(a.astype(BF16), b.astype(BF16), (((1,), (1,)), ((), ())),
                           preferred_element_type=F32)


def _bdot_tn(a, b):
    return lax.dot_general(a.astype(BF16), b.astype(BF16), (((0,), (0,)), ((), ())),
                           preferred_element_type=F32)


def _bmm(a, b):
    return jnp.einsum('nij,njk->nik', a.astype(BF16), b.astype(BF16), preferred_element_type=F32)


def _bmm_nt(a, b):
    return jnp.einsum('nik,njk->nij', a.astype(BF16), b.astype(BF16), preferred_element_type=F32)


def _bmm_tn(a, b):
    return _bmm(jnp.swapaxes(a, 1, 2), b)


def _hdot(a, b):
    return jnp.dot(a, b, precision=HI, preferred_element_type=F32)


def _head_sums(x, ones_bd):
    hi = x.astype(BF16)
    lo = (x - hi.astype(F32)).astype(BF16)
    return (jnp.dot(hi, ones_bd, preferred_element_type=F32) + jnp.dot(lo, ones_bd, preferred_element_type=F32))


def _silu(x):
    return x * jax.nn.sigmoid(x)


def _layer_norm_rows(x, g, b, eps):
    mu = jnp.mean(x, axis=-1, keepdims=True)
    xc = x - mu
    var = jnp.mean(xc * xc, axis=-1, keepdims=True)
    return xc * lax.rsqrt(var + eps) * g + b


def _mod_kernel(c_ref, w_ref, b_ref, o_ref):
    o_ref[...] = _hdot(_silu(c_ref[...]), w_ref[...]) + b_ref[...]


def _modulation(cc, w, b):
    R, D = cc.shape
    out = pl.pallas_call(
        _mod_kernel,
        grid=(6,),
        in_specs=[pl.BlockSpec((R, D), lambda j: (0, 0)),
                  pl.BlockSpec((D, D), lambda j: (0, j)),
                  pl.BlockSpec((1, D), lambda j: (0, j))],
        out_specs=pl.BlockSpec((R, D), lambda j: (0, j)),
        out_shape=jax.ShapeDtypeStruct((R, 6 * D), F32),
        compiler_params=_cparams(("parallel",)),
        name="modulation",
    )(cc, w, b.reshape(1, 6 * D))
    return out.reshape(R, 6, D)


def _in0_kernel(x_ref, mod_ref, w_ref, wa_ref, w2_ref, gb_ref, lb_ref,
                gq_ref, gk_ref, gv_ref, gla_ref, r_ref, hq_ref, hk_ref, hla_ref, hv_ref, hg_ref):
    sh1 = mod_ref[0:1, :]
    sc1 = mod_ref[1:2, :]
    h = (x_ref[...] * (1.0 + sc1) + sh1).astype(BF16)

    def proj(off, width):
        return jnp.dot(h, w_ref[:, off:off + width], preferred_element_type=F32)

    gq_ref[...] = proj(0, GLA_QK) * (GLA_DK ** -0.5)
    gk_ref[...] = proj(GLA_QK, GLA_QK)
    gv_ref[...] = proj(2 * GLA_QK, GLA_V)
    r_ref[...] = proj(2 * GLA_QK + GLA_V, GLA_V)
    base = 2 * GLA_QK + 2 * GLA_V
    a = jnp.dot(h, wa_ref[...], preferred_element_type=F32)
    z = _bdot(a, w2_ref[...]) + gb_ref[...]
    ls = (jnp.minimum(z, 0.0) - jnp.log(1.0 + jnp.exp(-jnp.abs(z)))) * (1.0 / GLA_GATE_NORM)
    gla_ref[0] = ls[:, :GLA_QK]
    gla_ref[1] = ls[:, GLA_QK:]
    hq_ref[...] = _silu(proj(base, HG_W))
    for d in range(2):
        zf = proj(base + (1 + d) * HG_W, HG_W)
        lb = lb_ref[d:d + 1, :]
        f = lb + (1.0 - lb) * jax.nn.sigmoid(zf)
        hk_ref[d] = 1.0 - f
        hla_ref[d] = jnp.log(f)
    hv_ref[...] = proj(base + 3 * HG_W, HG_W)
    hg_ref[...] = proj(base + 4 * HG_W, HG_W)


def _in_proj0(xcat, mods, nctx_tiles, w_in, gate_w2, gate_b, lb):
    B, S, D = xcat.shape
    J = S // ROW_TILE
    T = B * S
    a_off = 2 * GLA_QK + 2 * GLA_V
    wmain = jnp.concatenate([w_in[:, :a_off], w_in[:, a_off + 2 * GLA_GATE_RANK:]], axis=1).astype(BF16)
    wa = jnp.pad(w_in[:, a_off:a_off + 2 * GLA_GATE_RANK], ((0, 0), (0, LANES - 2 * GLA_GATE_RANK))).astype(BF16)
    w2 = jnp.zeros((LANES, 2 * GLA_QK), F32)
    w2 = w2.at[:GLA_GATE_RANK, :GLA_QK].set(gate_w2[0]).at[GLA_GATE_RANK:2 * GLA_GATE_RANK, GLA_QK:].set(gate_w2[1])
    gb = gate_b.reshape(1, 2 * GLA_QK)
    WM = wmain.shape[1]

    def rows(w):
        return pl.BlockSpec((ROW_TILE, w), lambda b, j: (b * J + j, 0))

    def rows2(w):
        return pl.BlockSpec((2, ROW_TILE, w), lambda b, j: (0, b * J + j, 0))

    def const(shape):
        return pl.BlockSpec(shape, lambda b, j: (0,) * len(shape))

    sd = jax.ShapeDtypeStruct
    outs = pl.pallas_call(
        _in0_kernel,
        grid=(B, J),
        in_specs=[rows(D),
                  pl.BlockSpec((None, 6, D), lambda b, j: (jnp.where(j < nctx_tiles, B, b), 0, 0)),
                  const((D, WM)), const((D, LANES)), const((LANES, 2 * GLA_QK)), const((1, 2 * GLA_QK)),
                  const((2, HG_W))],
        out_specs=[rows(GLA_QK), rows(GLA_QK), rows(GLA_V), rows2(GLA_QK), rows(GLA_V),
                   rows(HG_W), rows2(HG_W), rows2(HG_W), rows(HG_W), rows(HG_W)],
        out_shape=[sd((T, GLA_QK), F32), sd((T, GLA_QK), F32), sd((T, GLA_V), F32), sd((2, T, GLA_QK), F32),
                   sd((T, GLA_V), F32), sd((T, HG_W), F32), sd((2, T, HG_W), F32), sd((2, T, HG_W), F32),
                   sd((T, HG_W), F32), sd((T, HG_W), F32)],
        compiler_params=_cparams(("parallel", "parallel")),
        name="in_proj0",
    )(xcat.reshape(T, D), mods, wmain, wa, w2, gb, lb)
    return outs


def _rec_kernel(q_ref, k_ref, v_ref, la_ref, o_ref, st_ref, *, rev, nh, dk, dv, nchunk):
    t = pl.program_id(1)

    @pl.when(t == 0)
    def _():
        st_ref[...] = jnp.zeros_like(st_ref)

    C = CHUNK
    row = lax.broadcasted_iota(I32, (C, C), 0)
    col = lax.broadcasted_iota(I32, (C, C), 1)
    incl = (col >= row) if rev else (col <= row)
    tri = jnp.where(incl, 1.0, 0.0).astype(BF16)
    ref_i = C // 2 - 1 if rev else C // 2
    last_i = 0 if rev else C - 1
    order = range(nchunk - 1, -1, -1) if rev else range(nchunk)

    def stack(x, hd):
        return jnp.stack([x[c * C:(c + 1) * C, h * hd:(h + 1) * hd] for c in range(nchunk) for h in range(nh)], 0)

    def rows(x, i):
        return jnp.concatenate([jnp.broadcast_to(x[c * C + i:c * C + i + 1], (C, x.shape[1]))
                                for c in range(nchunk)], axis=0)

    la = la_ref[...]
    q = q_ref[...]
    k = k_ref[...]
    b = jnp.concatenate([_split_dot(tri, la[c * C:(c + 1) * C]) for c in range(nchunk)], axis=0)
    b_mid = rows(b, ref_i)
    b_last = rows(b, last_i)
    v = stack(v_ref[...], dv)
    sc = _bmm_nt(stack(q * jnp.exp(b - b_mid), dk), stack(k * jnp.exp(b_mid - b), dk))
    o_intra = _bmm(jnp.where(incl[None], sc, 0.0), v)
    q_in = stack(q * jnp.exp(b), dk)
    kv_t = _bmm_tn(v, stack(k * jnp.exp(b_last - b), dk))
    dec = stack(jnp.exp(b_last), dk)[:, 0:1, :]
    s_t = st_ref[...]
    for c in order:
        sl = slice(c * nh, (c + 1) * nh)
        o_c = o_intra[sl] + _bmm_nt(q_in[sl], s_t)
        o_ref[c * C:(c + 1) * C, :] = jnp.concatenate([o_c[h] for h in range(nh)], axis=1)
        s_t = s_t * dec[sl] + kv_t[sl]
    st_ref[...] = s_t


def _gated_recurrence(q, k, v, la, kdir, B, S, nctx_blocks, rev, nh, dk, dv):
    T = B * S
    NB = S // ROW_TILE
    wk, wv = nh * dk, nh * dv
    assert q.shape[1] == wk
    d = 1 if rev else 0
    nc = nctx_blocks

    def blk(t):
        if not rev:
            return t
        return jnp.where(t < nc, nc - 1 - t, NB - 1 - (t - nc))

    kern = functools.partial(_rec_kernel, rev=rev, nh=nh, dk=dk, dv=dv, nchunk=ROW_TILE // CHUNK)
    return pl.pallas_call(
        kern,
        grid=(B, NB),
        in_specs=[pl.BlockSpec((ROW_TILE, wk), lambda b, t: (b * NB + blk(t), 0)),
                  pl.BlockSpec((None, ROW_TILE, wk), lambda b, t: (kdir, b * NB + blk(t), 0)),
                  pl.BlockSpec((ROW_TILE, wv), lambda b, t: (b * NB + blk(t), 0)),
                  pl.BlockSpec((None, ROW_TILE, wk), lambda b, t: (d, b * NB + blk(t), 0))],
        out_specs=pl.BlockSpec((ROW_TILE, wv), lambda b, t: (b * NB + blk(t), 0)),
        out_shape=jax.ShapeDtypeStruct((T, wv), F32),
        scratch_shapes=[pltpu.VMEM((nh, dv, dk), F32)],
        compiler_params=_cparams(("parallel", "arbitrary")),
        name="gated_rec_bwd" if rev else "gated_rec_fwd",
    )(q, k, v, la)


def _pack_rows(x):
    w = x.shape[1] // 2
    lo = lax.bitcast_convert_type(x[:, :w].astype(BF16).astype(F32), U32)
    hi = lax.bitcast_convert_type(x[:, w:].astype(BF16).astype(F32), U32)
    return (lo >> 16) | (hi & jnp.uint32(0xFFFF0000))


def _unpack_rows(p):
    lo = lax.bitcast_convert_type(p << 16, F32)
    hi = lax.bitcast_convert_type(p & jnp.uint32(0xFFFF0000), F32)
    return lo, hi


def _post_mix(y, x, mod_ref, wo_ref, lng_ref, lnb_ref, alpha, xl_ref, u_ref, up_ref):
    g1 = mod_ref[2:3, :]
    sh2 = mod_ref[3:4, :]
    sc2 = mod_ref[4:5, :]
    yo = jnp.dot(y.astype(BF16), wo_ref[...], preferred_element_type=F32)
    xl = _layer_norm_rows(alpha * x + g1 * yo, lng_ref[...], lnb_ref[...], LN_EPS)
    xl_ref[...] = xl
    u = xl * (1.0 + sc2) + sh2
    u_ref[...] = u
    up_ref[...] = _pack_rows(u)


def _out0_kernel(gf_ref, gb_ref, hf_ref, hb_ref, r_ref, hg_ref, x_ref, mod_ref, wo_ref,
                 gg_ref, hgg_ref, lng_ref, lnb_ref, xl_ref, u_ref, up_ref, *, alpha):
    def heads(o, g, gate):
        parts = []
        for hh in range(o.shape[1] // LANES):
            seg = o[:, hh * LANES:(hh + 1) * LANES]
            ms = jnp.mean(seg * seg, axis=-1, keepdims=True)
            parts.append(seg * lax.rsqrt(ms + 1e-6) * g)
        return jnp.concatenate(parts, axis=1) * _silu(gate)

    y = jnp.concatenate([heads(gf_ref[...] + gb_ref[...], gg_ref[...], r_ref[...]),
                         heads(hf_ref[...] + hb_ref[...], hgg_ref[...], hg_ref[...])], axis=1)
    _post_mix(y, x_ref[...], mod_ref, wo_ref, lng_ref, lnb_ref, alpha, xl_ref, u_ref, up_ref)


def _readout0(o_gf, o_gb, o_hf, o_hb, r, hgate, xcat, mods, nctx_tiles, w_out, gla_g, hg_g, ln_g, ln_b, alpha):
    B, S, D = xcat.shape
    J = S // ROW_TILE
    T = B * S

    def rows(w):
        return pl.BlockSpec((ROW_TILE, w), lambda b, j: (b * J + j, 0))

    def const(shape):
        return pl.BlockSpec(shape, lambda b, j: (0,) * len(shape))

    sd = jax.ShapeDtypeStruct
    return pl.pallas_call(
        functools.partial(_out0_kernel, alpha=alpha),
        grid=(B, J),
        in_specs=[rows(GLA_V), rows(GLA_V), rows(HG_W), rows(HG_W), rows(GLA_V), rows(HG_W), rows(D),
                  pl.BlockSpec((None, 6, D), lambda b, j: (jnp.where(j < nctx_tiles, B, b), 0, 0)),
                  const((GLA_V + HG_W, D)), const((1, GLA_DV)), const((1, HG_EXPAND)), const((1, D)), const((1, D))],
        out_specs=[rows(D), rows(D), rows(D // 2)],
        out_shape=[sd((T, D), F32), sd((T, D), F32), sd((T, D // 2), U32)],
        compiler_params=_cparams(("parallel", "parallel")),
        name="readout0",
    )(o_gf, o_gb, o_hf, o_hb, r, hgate, xcat.reshape(T, D), mods, w_out.astype(BF16),
      gla_g.reshape(1, -1), hg_g.reshape(1, -1), ln_g.reshape(1, D), ln_b.reshape(1, D))


def _route_kernel(u_ref, rwt_ref, rb_ref, tri_ref, eidx_ref, gate_ref, rank_ref, cnt_ref, carry_ref):
    i = pl.program_id(0)

    @pl.when(i == 0)
    def _():
        carry_ref[...] = jnp.zeros_like(carry_ref)

    u = u_ref[...]
    TM = u.shape[0]
    E = N_EXPERTS
    logits = lax.dot_general(rwt_ref[...], u, (((1,), (1,)), ((), ())), precision=HI,
                             preferred_element_type=F32)
    scores = jax.nn.sigmoid(logits)
    sel = scores + rb_ref[...]
    neg = -jnp.inf
    sel3 = sel.reshape(N_GROUPS, PER_GROUP, TM)
    io_g = lax.broadcasted_iota(I32, sel3.shape, 1)
    m1 = jnp.max(sel3, axis=1, keepdims=True)
    i1 = jnp.min(jnp.where(sel3 == m1, io_g, PER_GROUP), axis=1, keepdims=True)
    m2 = jnp.max(jnp.where(io_g == i1, neg, sel3), axis=1, keepdims=True)
    grp = m1 + m2
    io_n = lax.broadcasted_iota(I32, grp.shape, 0)
    keep = jnp.zeros(grp.shape, jnp.bool_)
    for _ in range(TOPK_GROUPS):
        m = jnp.max(grp, axis=0, keepdims=True)
        idx = jnp.min(jnp.where(grp == m, io_n, N_GROUPS), axis=0, keepdims=True)
        hit = io_n == idx
        keep = jnp.logical_or(keep, hit)
        grp = jnp.where(hit, neg, grp)
    sel = jnp.where(keep, sel3, neg).reshape(E, TM)
    io_e = lax.broadcasted_iota(I32, (E, TM), 0)
    base = carry_ref[...]
    tri = tri_ref[...]
    e_rows, g_rows, r_rows = [], [], []
    for _ in range(TOP_K):
        m = jnp.max(sel, axis=0, keepdims=True)
        idx = jnp.min(jnp.where(sel == m, io_e, E), axis=0, keepdims=True)
        hit = io_e == idx
        hit_f = jnp.where(hit, 1.0, 0.0)
        g_rows.append(jnp.sum(jnp.where(hit, scores, 0.0), axis=0, keepdims=True))
        prefix = jnp.dot(hit_f.astype(BF16), tri, preferred_element_type=F32)
        r_rows.append(jnp.sum(jnp.where(hit, prefix + base, 0.0), axis=0, keepdims=True))
        base = base + jnp.sum(hit_f, axis=1, keepdims=True)
        e_rows.append(idx)
        sel = jnp.where(hit, neg, sel)
    carry_ref[...] = base
    cnt_ref[...] = base
    g = jnp.concatenate(g_rows, axis=0)
    gate_ref[...] = g / jnp.sum(g, axis=0, keepdims=True) * ROUTED_SCALE
    eidx_ref[...] = jnp.concatenate(e_rows, axis=0)
    rank_ref[...] = jnp.concatenate(r_rows, axis=0).astype(I32)


def _route(u, router_w, router_bias):
    T, D = u.shape
    E = N_EXPERTS
    n = T // ROW_TILE
    tri = jnp.asarray(np.triu(np.ones((ROW_TILE, ROW_TILE), np.float32), 1), BF16)
    sd = jax.ShapeDtypeStruct
    cols = pl.BlockSpec((TOP_K, ROW_TILE), lambda i: (0, i))
    return pl.pallas_call(
        _route_kernel,
        grid=(n,),
        in_specs=[pl.BlockSpec((ROW_TILE, D), lambda i: (i, 0)),
                  pl.BlockSpec((E, D), lambda i: (0, 0)),
                  pl.BlockSpec((E, 1), lambda i: (0, 0)),
                  pl.BlockSpec((ROW_TILE, ROW_TILE), lambda i: (0, 0))],
        out_specs=[cols, cols, cols, pl.BlockSpec((E, 1), lambda i: (0, 0))],
        out_shape=[sd((TOP_K, T), I32), sd((TOP_K, T), F32), sd((TOP_K, T), I32), sd((E, 1), F32)],
        scratch_shapes=[pltpu.VMEM((E, 1), F32)],
        compiler_params=_cparams(("arbitrary",)),
        name="moe_route",
    )(u, router_w.T, router_bias.reshape(E, 1), tri)


def _dest_kernel(e_ref, r_ref, ps_ref, d_ref):
    e = e_ref[...]
    TM = e.shape[1]
    io_e = lax.broadcasted_iota(I32, (N_EXPERTS, TM), 0)
    ps = ps_ref[...]
    rows = []
    for k in range(TOP_K):
        rows.append(jnp.sum(jnp.where(io_e == e[k:k + 1, :], ps, 0.0), axis=0, keepdims=True))
    d_ref[...] = jnp.concatenate(rows, axis=0).astype(I32) + r_ref[...]


def _slot_of(eidx, rank, pad_start):
    T = eidx.shape[1]
    cols = pl.BlockSpec((TOP_K, ROW_TILE), lambda i: (0, i))
    return pl.pallas_call(
        _dest_kernel,
        grid=(T // ROW_TILE,),
        in_specs=[cols, cols, pl.BlockSpec((N_EXPERTS, 1), lambda i: (0, 0))],
        out_specs=cols,
        out_shape=jax.ShapeDtypeStruct((TOP_K, T), I32),
        compiler_params=_cparams(("parallel",)),
        name="moe_slot",
    )(eidx, rank, pad_start.astype(F32).reshape(N_EXPERTS, 1))


def _dispatch_kernel(z0_ref, zn_ref, dest_ref, u_ref, xs_ref, zero_scr, sem, zsem):
    TM = u_ref.shape[0]

    @pl.when(pl.program_id(0) == 0)
    def _():
        zero_scr[...] = jnp.zeros_like(zero_scr)

        def zcopy(e):
            row0 = pl.multiple_of(z0_ref[e], MOE_BLK)
            return pltpu.make_async_copy(zero_scr, xs_ref.at[pl.ds(row0, MOE_BLK)], zsem)

        def zstart(e, c):
            @pl.when(zn_ref[e] > 0)
            def _():
                zcopy(e).start()
            return c

        def zwait(e, c):
            @pl.when(zn_ref[e] > 0)
            def _():
                zcopy(e).wait()
            return c

        lax.fori_loop(0, N_EXPERTS, zstart, 0)
        lax.fori_loop(0, N_EXPERTS, zwait, 0)

    def copy(t, k):
        return pltpu.make_async_copy(u_ref.at[pl.ds(t, 1)], xs_ref.at[pl.ds(dest_ref[k, t], 1)], sem)

    def issue(t, c):
        for k in range(TOP_K):
            copy(t, k).start(priority=k % 2)
        return c

    lax.fori_loop(0, TM, issue, 0)
    for k in range(TOP_K):
        pltpu.make_async_copy(u_ref, xs_ref.at[pl.ds(0, TM)], sem).wait()


def _dispatch(up, dest, n_slots, pad_row0, pad_rows):
    T, W = up.shape
    return pl.pallas_call(
        _dispatch_kernel,
        grid_spec=pltpu.PrefetchScalarGridSpec(
            num_scalar_prefetch=2,
            grid=(T // ROW_TILE,),
            in_specs=[pl.BlockSpec((TOP_K, ROW_TILE), lambda i, z0, zn: (0, i), memory_space=pltpu.SMEM),
                      pl.BlockSpec((ROW_TILE, W), lambda i, z0, zn: (i, 0))],
            out_specs=pl.BlockSpec(memory_space=pl.ANY),
            scratch_shapes=[pltpu.VMEM((MOE_BLK, W), U32), pltpu.SemaphoreType.DMA, pltpu.SemaphoreType.DMA]),
        out_shape=jax.ShapeDtypeStruct((n_slots, W), U32),
        compiler_params=_cparams(("arbitrary",)),
        name="moe_dispatch",
    )(pad_row0, pad_rows, dest, up)


def _expert_kernel(be_ref, nu_ref, first_ref, nxt_ref, par_ref, xs_ref, w13_hbm, w2_hbm, ys_ref,
                   w13_buf, w2_buf, w13_bf, w2_bf, sem, *, layer):
    i = pl.program_id(0)

    def fetch(e, slot):
        return (pltpu.make_async_copy(w13_hbm.at[layer, e], w13_buf.at[slot], sem.at[0, slot]),
                pltpu.make_async_copy(w2_hbm.at[layer, e], w2_buf.at[slot], sem.at[1, slot]))

    @pl.when(i == 0)
    def _():
        for c in fetch(be_ref[0], 0):
            c.start()

    @pl.when(jnp.logical_and(i < nu_ref[0], first_ref[i] == 1))
    def _():
        slot = par_ref[i]
        for c in fetch(be_ref[i], slot):
            c.wait()

        @pl.when(nxt_ref[i] >= 0)
        def _():
            for c in fetch(nxt_ref[i], 1 - slot):
                c.start()

        w13_bf[...] = w13_buf[slot].astype(BF16)
        w2_bf[...] = w2_buf[slot].astype(BF16)

    @pl.when(i < nu_ref[0])
    def _():
        lo, hi = _unpack_rows(xs_ref[...])
        x = jnp.concatenate([lo, hi], axis=1).astype(BF16)
        h = jnp.dot(x, w13_bf[...], preferred_element_type=F32)
        a = _silu(h[:, :EXPERT_FF]) * h[:, EXPERT_FF:]
        ys_ref[...] = _pack_rows(jnp.dot(a.astype(BF16), w2_bf[...], preferred_element_type=F32))


def _expert_gemm(xs, blk_exp, n_used, cnt, w13, w2, layer):
    NP, W = xs.shape
    D = 2 * W
    nblk = NP // MOE_BLK
    F2 = w13.shape[3]
    E = cnt.shape[0]
    blk = jnp.arange(nblk, dtype=I32)
    first = jnp.concatenate([jnp.ones((1,), I32), (blk_exp[1:] != blk_exp[:-1]).astype(I32)])
    first = jnp.where(blk < n_used[0], first, 0)
    par = (jnp.cumsum(first) - 1) % 2
    ids = jnp.where(cnt > 0, jnp.arange(E, dtype=I32), E)
    suffix_min = lax.cummin(ids, axis=0, reverse=True)
    next_active = jnp.concatenate([suffix_min[1:], jnp.full((1,), E, I32)])
    next_active = jnp.where(next_active >= E, -1, next_active)
    nxt = jnp.take(next_active, blk_exp)

    def xmap(i, be, nu, fi, nx, pa):
        return (jnp.minimum(i, nu[0] - 1), 0)

    return pl.pallas_call(
        functools.partial(_expert_kernel, layer=layer),
        grid_spec=pltpu.PrefetchScalarGridSpec(
            num_scalar_prefetch=5,
            grid=(nblk,),
            in_specs=[pl.BlockSpec((MOE_BLK, W), xmap),
                      pl.BlockSpec(memory_space=pl.ANY), pl.BlockSpec(memory_space=pl.ANY)],
            out_specs=pl.BlockSpec((MOE_BLK, W), xmap),
            scratch_shapes=[pltpu.VMEM((2, D, F2), F32), pltpu.VMEM((2, F2 // 2, D), F32),
                            pltpu.VMEM((D, F2), BF16), pltpu.VMEM((F2 // 2, D), BF16),
                            pltpu.SemaphoreType.DMA((2, 2))]),
        out_shape=jax.ShapeDtypeStruct((NP, W), U32),
        compiler_params=_cparams(("arbitrary",)),
        name="moe_experts",
    )(blk_exp, n_used, first, nxt.astype(I32), par.astype(I32), xs, w13, w2)


def _combine_kernel(mt_ref, dcur_ref, dnxt_ref, gate_ref, up_ref, xl_ref, mod_ref, s13_ref, s2_ref, lng_ref, lnb_ref,
                    ys_ref, o_ref, rows_a, rows_b, sem_a, sem_b, *, alpha):
    del mt_ref
    i = pl.program_id(0)
    H = CMB_TILE
    W = up_ref.shape[1]

    def issue(dref, col0, rows, sem):
        for t in range(H):
            for k in range(TOP_K):
                pltpu.make_async_copy(ys_ref.at[pl.ds(dref[k, col0 + t], 1)], rows.at[k, pl.ds(t, 1)],
                                      sem).start(priority=k % 2)

    def wait(rows, sem):
        for k in range(TOP_K):
            pltpu.make_async_copy(ys_ref.at[pl.ds(0, H)], rows.at[k], sem).wait()

    def compute(rows, r0):
        lo, hi = _unpack_rows(up_ref[r0:r0 + H])
        hs = _bdot(jnp.concatenate([lo, hi], axis=1), s13_ref[...])
        sh = _bdot(_silu(hs[:, :SHARED_FF]) * hs[:, SHARED_FF:], s2_ref[...])
        acc_lo, acc_hi = sh[:, :W], sh[:, W:]
        gate = gate_ref[r0:r0 + H]
        for k in range(TOP_K):
            lo, hi = _unpack_rows(rows[k])
            g = gate[:, k:k + 1]
            acc_lo = acc_lo + lo * g
            acc_hi = acc_hi + hi * g
        acc = jnp.concatenate([acc_lo, acc_hi], axis=1)
        g2 = mod_ref[5:6, :]
        o_ref[r0:r0 + H] = _layer_norm_rows(alpha * xl_ref[r0:r0 + H] + g2 * acc, lng_ref[...], lnb_ref[...], LN_EPS)

    @pl.when(i == 0)
    def _():
        issue(dcur_ref, 0, rows_a, sem_a)

    wait(rows_a, sem_a)
    issue(dcur_ref, H, rows_b, sem_b)
    compute(rows_a, 0)
    wait(rows_b, sem_b)
    issue(dnxt_ref, 0, rows_a, sem_a)
    compute(rows_b, H)

    @pl.when(i == pl.num_programs(0) - 1)
    def _():
        wait(rows_a, sem_a)


def _combine(dest, gate_t, up, xl, ys, mods, mod_of_tile, sh_w13, sh_w2, ln_g, ln_b, alpha):
    T, D = xl.shape
    W = D // 2
    TM = 2 * CMB_TILE
    n = T // TM

    def rows(w):
        return pl.BlockSpec((TM, w), lambda i, mt: (i, 0))

    def const(shape):
        return pl.BlockSpec(shape, lambda i, mt: (0,) * len(shape))

    return pl.pallas_call(
        functools.partial(_combine_kernel, alpha=alpha),
        grid_spec=pltpu.PrefetchScalarGridSpec(
            num_scalar_prefetch=1,
            grid=(n,),
            in_specs=[pl.BlockSpec((TOP_K, TM), lambda i, mt: (0, i), memory_space=pltpu.SMEM),
                      pl.BlockSpec((TOP_K, TM), lambda i, mt: (0, jnp.minimum(i + 1, n - 1)), memory_space=pltpu.SMEM),
                      rows(TOP_K), rows(W), rows(D),
                      pl.BlockSpec((None, 6, D), lambda i, mt: (mt[i], 0, 0)),
                      const((D, 2 * SHARED_FF)), const((SHARED_FF, D)), const((1, D)), const((1, D)),
                      pl.BlockSpec(memory_space=pl.ANY)],
            out_specs=rows(D),
            scratch_shapes=[pltpu.VMEM((TOP_K, CMB_TILE, W), U32), pltpu.VMEM((TOP_K, CMB_TILE, W), U32),
                            pltpu.SemaphoreType.DMA, pltpu.SemaphoreType.DMA]),
        out_shape=jax.ShapeDtypeStruct((T, D), F32),
        compiler_params=_cparams(("arbitrary",)),
        name="moe_combine",
    )(mod_of_tile, dest, dest, gate_t, up, xl, mods, sh_w13.astype(BF16), sh_w2.astype(BF16),
      ln_g.reshape(1, D), ln_b.reshape(1, D), ys)


def _moe_block(u, up, xl, mods, mod_of_tile, router_w, router_bias, w13, w2, layer, sh_w13, sh_w2, ln_g, ln_b, alpha):
    T, D = u.shape
    eidx, gate, rank, counts = _route(u, router_w, router_bias)
    cnt = counts.reshape(N_EXPERTS).astype(I32)
    padded = (cnt + MOE_BLK - 1) // MOE_BLK * MOE_BLK
    pad_end = jnp.cumsum(padded)
    pad_start = pad_end - padded
    nblk = T * TOP_K // MOE_BLK + N_EXPERTS
    n_used = (pad_end[-1] // MOE_BLK).astype(I32)
    blk_row0 = jnp.arange(nblk, dtype=I32) * MOE_BLK
    blk_exp = jnp.sum((pad_end[None, :] <= blk_row0[:, None]).astype(I32), axis=1)
    last_e = jnp.max(jnp.where(cnt > 0, jnp.arange(N_EXPERTS, dtype=I32), 0))
    blk_exp = jnp.minimum(blk_exp, last_e)
    dest = _slot_of(eidx, rank, pad_start)
    xs = _dispatch(up, dest, nblk * MOE_BLK, jnp.maximum(pad_end - MOE_BLK, 0), padded)
    ys = _expert_gemm(xs, blk_exp, n_used.reshape(1), cnt, w13, w2, layer)
    return _combine(dest, gate.T, up, xl, ys, mods, mod_of_tile, sh_w13, sh_w2, ln_g, ln_b, alpha)


def _in1_kernel(x_ref, mod_ref, wh_ref, wr_ref, ph_ref, pr_ref):
    sh1 = mod_ref[0:1, :]
    sc1 = mod_ref[1:2, :]
    h = (x_ref[...] * (1.0 + sc1) + sh1).astype(BF16)
    ph_ref[...] = jnp.dot(h, wh_ref[...], preferred_element_type=F32)
    pr_ref[...] = jnp.dot(h, wr_ref[...], preferred_element_type=F32)


RW_PW = 1920


def _rw_reorder(t):
    a = 3 * RW_W
    lo = 2 * RW_DECAY_LORA + RW_AAA_LORA
    out = jnp.concatenate([t[..., :a], t[..., a + lo:], t[..., a:a + lo]], axis=-1)
    pad = [(0, 0)] * (t.ndim - 1) + [(0, RW_PW - out.shape[-1])]
    return jnp.pad(out, pad)


def _in_proj1(xcat, mods, nctx_tiles, w_in):
    B, S, D = xcat.shape
    J = S // ROW_TILE
    T = B * S
    wh = w_in[:, :HY_WIDTH].astype(BF16)
    wr = _rw_reorder(w_in[:, HY_WIDTH:]).astype(BF16)

    def rows(w):
        return pl.BlockSpec((ROW_TILE, w), lambda b, j: (b * J + j, 0))

    def const(shape):
        return pl.BlockSpec(shape, lambda b, j: (0,) * len(shape))

    sd = jax.ShapeDtypeStruct
    return pl.pallas_call(
        _in1_kernel,
        grid=(B, J),
        in_specs=[rows(D), pl.BlockSpec((None, 6, D), lambda b, j: (jnp.where(j < nctx_tiles, B, b), 0, 0)),
                  const((D, HY_WIDTH)), const((D, RW_PW))],
        out_specs=[rows(HY_WIDTH), rows(RW_PW)],
        out_shape=[sd((T, HY_WIDTH), F32), sd((T, RW_PW), F32)],
        compiler_params=_cparams(("parallel", "parallel")),
        name="in_proj1",
    )(xcat.reshape(T, D), mods, wh, wr)


def _rw_streams_kernel(prev_ref, main_ref, next_ref, mu_ref, w0_ref, w2_ref, a0_ref, a2_ref, g2_ref,
                       kk_ref, ka_ref, bd_ref,
                       r_ref, k_ref, v_ref, kkn_ref, bb_ref, ld_ref, g_ref, sh_scr, *, nctx_tiles, n_lat_tiles):
    j = pl.program_id(1)
    TM = main_ref.shape[0]
    W = main_ref.shape[1]
    H = GRID_W
    p = main_ref[...]
    ext = jnp.concatenate([prev_ref[...], p, next_ref[...]], axis=0)
    left = ext[H - 1:H - 1 + TM]
    right = ext[H + 1:H + 1 + TM]
    up = ext[0:TM]
    down = ext[2 * H:2 * H + TM]
    i = lax.broadcasted_iota(I32, (TM, W), 0)
    lane = lax.broadcasted_iota(I32, (TM, W), 1)
    even = (lane & 1) == 0
    c4 = lane & 3
    jl = j - nctx_tiles

    @pl.when(j < nctx_tiles)
    def _():
        lo = jnp.where(j == 0, 1, 0)
        hi = jnp.where(j == nctx_tiles - 1, TM - 1, TM)
        sh_scr[...] = jnp.where(even, jnp.where(i >= lo, left, 0.0), jnp.where(i < hi, right, 0.0))

    @pl.when(j >= nctx_tiles)
    def _():
        col = i & (H - 1)
        up_lo = jnp.where(jl == 0, H, 0)
        down_hi = jnp.where(jl == n_lat_tiles - 1, TM - H, TM)
        l_v = jnp.where(col != 0, left, 0.0)
        r_v = jnp.where(col != H - 1, right, 0.0)
        u_v = jnp.where(i >= up_lo, up, 0.0)
        d_v = jnp.where(i < down_hi, down, 0.0)
        sh_scr[...] = jnp.where(c4 == 0, l_v, jnp.where(c4 == 1, r_v, jnp.where(c4 == 2, u_v, d_v)))

    pm = p + mu_ref[...] * (sh_scr[...] - p)
    r = pm[:, 0:RW_W]
    k = pm[:, RW_W:2 * RW_W]
    v = pm[:, 2 * RW_W:3 * RW_W]
    o = 3 * RW_W
    gl = pm[:, o:o + RW_GATE_LORA]
    o += RW_GATE_LORA
    wl_f = pm[:, o:o + RW_DECAY_LORA]
    wl_b = pm[:, o + RW_DECAY_LORA:o + 2 * RW_DECAY_LORA]
    al = pm[:, o + 2 * RW_DECAY_LORA:o + 2 * RW_DECAY_LORA + RW_AAA_LORA]
    for d, wl in enumerate((wl_f, wl_b)):
        z = w0_ref[d:d + 1, :] + _bdot(jnp.tanh(wl), w2_ref[d])
        w = -(jnp.maximum(-z, 0.0) + jnp.log(1.0 + jnp.exp(-jnp.abs(z)))) - 0.5
        ld_ref[d] = -jnp.exp(w)
    a = jax.nn.sigmoid(a0_ref[...] + _bdot(al, a2_ref[...]))
    g_ref[...] = _bdot(jax.nn.sigmoid(gl), g2_ref[...])
    kk = k * kk_ref[...]
    n2 = _head_sums(kk * kk, bd_ref[...])
    kkn = kk / jnp.maximum(jnp.sqrt(n2), 1e-12)
    r_ref[...] = r
    k_ref[...] = k * (1.0 + (a - 1.0) * ka_ref[...])
    v_ref[...] = v
    kkn_ref[...] = kkn
    bb_ref[...] = kkn * a


def _head_sum_matrix(width, hd):
    i = np.arange(width)
    return jnp.asarray((i[:, None] // hd == i[None, :] // hd).astype(np.float32), BF16)


def _rw_streams(p_rw, B, S, nctx_tiles, mu, w0, w2, a0, a2, g2, k_k, k_a):
    T = B * S
    J = S // ROW_TILE
    HB = ROW_TILE // GRID_W
    NH = S // GRID_W

    def rows(w):
        return pl.BlockSpec((ROW_TILE, w), lambda b, j: (b * J + j, 0))

    def rows2(w):
        return pl.BlockSpec((2, ROW_TILE, w), lambda b, j: (0, b * J + j, 0))

    def const(shape):
        return pl.BlockSpec(shape, lambda b, j: (0,) * len(shape))

    sd = jax.ShapeDtypeStruct
    kern = functools.partial(_rw_streams_kernel, nctx_tiles=nctx_tiles, n_lat_tiles=J - nctx_tiles)
    return pl.pallas_call(
        kern,
        grid=(B, J),
        in_specs=[pl.BlockSpec((GRID_W, RW_PW), lambda b, j: (b * NH + jnp.maximum(j * HB - 1, 0), 0)),
                  rows(RW_PW),
                  pl.BlockSpec((GRID_W, RW_PW), lambda b, j: (b * NH + jnp.minimum((j + 1) * HB, NH - 1), 0)),
                  const((1, RW_PW)), const((2, RW_W)), const((2, RW_DECAY_LORA, RW_W)), const((1, RW_W)),
                  const((RW_AAA_LORA, RW_W)), const((RW_GATE_LORA, RW_W)), const((1, RW_W)), const((1, RW_W)),
                  const((RW_W, RW_W))],
        out_specs=[rows(RW_W), rows(RW_W), rows(RW_W), rows(RW_W), rows(RW_W), rows2(RW_W), rows(RW_W)],
        out_shape=[sd((T, RW_W), F32)] * 5 + [sd((2, T, RW_W), F32), sd((T, RW_W), F32)],
        scratch_shapes=[pltpu.VMEM((ROW_TILE, RW_PW), F32)],
        compiler_params=_cparams(("parallel", "parallel")),
        name="rwkv_streams",
    )(p_rw, p_rw, p_rw, _rw_reorder(mu).reshape(1, RW_PW), w0, w2, a0.reshape(1, RW_W), a2, g2,
      k_k.reshape(1, RW_W), k_a.reshape(1, RW_W), _head_sum_matrix(RW_W, RW_HEAD_DIM))


def _rwkv_kernel(r_ref, k_ref, v_ref, kk_ref, bb_ref, ld_ref, o_ref, st_ref, *, rev, nchunk):
    t = pl.program_id(1)

    @pl.when(t == 0)
    def _():
        st_ref[...] = jnp.zeros_like(st_ref)

    C = RW_CHUNK
    hd = RW_HEAD_DIM
    NH = RW_HEADS
    row = lax.broadcasted_iota(I32, (C, C), 0)
    col = lax.broadcasted_iota(I32, (C, C), 1)
    incl = ((col >= row) if rev else (col <= row))[None]
    strict = ((col > row) if rev else (col < row))[None]
    tri = jnp.where(incl[0], 1.0, 0.0).astype(BF16)
    eye = (row == col)[None]
    last_i = 0 if rev else C - 1
    order = range(nchunk - 1, -1, -1) if rev else range(nchunk)
    n_double = int(math.log2(C)) - 1

    def stack(x):
        return jnp.stack([x[c * C:(c + 1) * C, h * hd:(h + 1) * hd] for c in range(nchunk) for h in range(NH)], 0)

    ld = ld_ref[...]
    g = jnp.concatenate([_split_dot(tri, ld[c * C:(c + 1) * C]) for c in range(nchunk)], axis=0)
    g_last = jnp.concatenate([jnp.broadcast_to(g[c * C + last_i:c * C + last_i + 1], (C, NH * hd))
                              for c in range(nchunk)], axis=0)
    k = k_ref[...]
    bb = bb_ref[...]
    eng = jnp.exp(-g)
    e_end = jnp.exp(g_last - g)
    kk_t = stack(kk_ref[...] * jnp.exp(g - ld))
    b_t = stack(bb * eng)
    k_t = stack(k * eng)
    r_t = stack(r_ref[...] * jnp.exp(g))
    b_bar = stack(bb * e_end)
    k_bar = stack(k * e_end)
    dec = stack(jnp.exp(g_last))
    v = stack(v_ref[...])
    m_b = jnp.where(strict, _bmm_nt(kk_t, b_t), 0.0)
    m_k = jnp.where(strict, _bmm_nt(kk_t, k_t), 0.0)
    a_rb = jnp.where(incl, _bmm_nt(r_t, b_t), 0.0)
    a_rk = jnp.where(incl, _bmm_nt(r_t, k_t), 0.0)
    pw = -m_b
    tinv = jnp.where(eye, 1.0, 0.0) + pw
    for _ in range(n_double):
        pw = _bmm(pw, pw)
        tinv = tinv + _bmm(tinv, pw)
    a_t = _bmm(tinv, kk_t)
    u_b = _bmm(tinv, _bmm(m_k, v))
    q_h = r_t - _bmm(a_rb, a_t)
    o_h = _bmm(a_rk, v) - _bmm(a_rb, u_b)
    g_m = jnp.where(eye, dec, 0.0) - _bmm_tn(a_t, b_bar)
    h_m = _bmm_tn(v, k_bar) - _bmm_tn(u_b, b_bar)
    s = st_ref[...]
    for c in order:
        sl = slice(c * NH, (c + 1) * NH)
        o_c = _bmm_nt(q_h[sl], s) + o_h[sl]
        o_ref[c * C:(c + 1) * C, :] = jnp.concatenate([o_c[h] for h in range(NH)], axis=1)
        s = _bmm(s, g_m[sl]) + h_m[sl]
    st_ref[...] = s


def _rwkv_scan(r, k, v, kk, bb, ld, B, S, nctx_blocks, rev):
    T = B * S
    NB = S // ROW_TILE
    d = 1 if rev else 0
    nc = nctx_blocks

    def blk(t):
        if not rev:
            return t
        return jnp.where(t < nc, nc - 1 - t, NB - 1 - (t - nc))

    spec = pl.BlockSpec((ROW_TILE, RW_W), lambda b, t: (b * NB + blk(t), 0))
    kern = functools.partial(_rwkv_kernel, rev=rev, nchunk=ROW_TILE // RW_CHUNK)
    return pl.pallas_call(
        kern,
        grid=(B, NB),
        in_specs=[spec, spec, spec, spec, spec,
                  pl.BlockSpec((None, ROW_TILE, RW_W), lambda b, t: (d, b * NB + blk(t), 0))],
        out_specs=spec,
        out_shape=jax.ShapeDtypeStruct((T, RW_W), F32),
        scratch_shapes=[pltpu.VMEM((RW_HEADS, RW_HEAD_DIM, RW_HEAD_DIM), F32)],
        compiler_params=_cparams(("parallel", "arbitrary")),
        name="rwkv_scan_bwd" if rev else "rwkv_scan_fwd",
    )(r, k, v, kk, bb, ld)


def _filter_mlp_kernel(z_ref, w1_ref, b1_ref, w2_ref, b2_ref, w3_ref, sf_ref, win_ref, f_ref):
    h = jnp.sin(sf_ref[0:1, :] * (_hdot(z_ref[...], w1_ref[...]) + b1_ref[...]))
    h = jnp.sin(sf_ref[1:2, :] * (_hdot(h, w2_ref[...]) + b2_ref[...]))
    f_ref[...] = _hdot(h, w3_ref[...]) * win_ref[...]


def _hyena_filters(L, w1, b1, w2, b2, w3, sin_freq):
    t = np.linspace(0.0, 1.0, L, dtype=np.float32)[:, None]
    bands = (HY_EMB - 1) // 2
    wpos = (2.0 * math.pi * np.arange(L, dtype=np.float32)[:, None] / L).astype(np.float32)
    fr = np.linspace(1e-4, bands - 1, bands, dtype=np.float32)[None, :]
    z = np.concatenate([t, np.cos(fr * wpos), -np.sin(fr * wpos)], -1).astype(np.float32)
    zp = jnp.asarray(np.pad(z, ((0, 0), (0, LANES - HY_EMB))))
    w1p = jnp.pad(w1, ((0, LANES - HY_EMB), (0, 0)))
    max_decay = math.log(HY_TARGET) / HY_FAST_DECAY
    min_decay = math.log(HY_TARGET) / HY_SLOW_DECAY
    deltas = np.linspace(min_decay, max_decay, HY_CH, dtype=np.float32)
    window = np.exp(-t * np.abs(deltas)).astype(np.float32)
    FW = HY_ORDER * 2 * HY_CH
    win = jnp.asarray(np.tile(window, (1, HY_ORDER * 2)))
    TR = min(L, 512)
    Hd = w1.shape[1]
    return pl.pallas_call(
        _filter_mlp_kernel,
        grid=(L // TR,),
        in_specs=[pl.BlockSpec((TR, LANES), lambda i: (i, 0)),
                  pl.BlockSpec((LANES, Hd), lambda i: (0, 0)), pl.BlockSpec((1, Hd), lambda i: (0, 0)),
                  pl.BlockSpec((Hd, Hd), lambda i: (0, 0)), pl.BlockSpec((1, Hd), lambda i: (0, 0)),
                  pl.BlockSpec((Hd, FW), lambda i: (0, 0)), pl.BlockSpec((2, Hd), lambda i: (0, 0)),
                  pl.BlockSpec((TR, FW), lambda i: (i, 0))],
        out_specs=pl.BlockSpec((TR, FW), lambda i: (i, 0)),
        out_shape=jax.ShapeDtypeStruct((L, FW), F32),
        compiler_params=_cparams(("parallel",)),
        name="hyena_filter_mlp",
    )(zp, w1p, b1.reshape(1, Hd), w2, b2.reshape(1, Hd), w3, sin_freq, win)


def _sconv_kernel(p_ref, w_ref, b_ref, o_ref):
    p = p_ref[...]
    L = p.shape[0]
    i = lax.broadcasted_iota(I32, p.shape, 0)
    prev = jnp.where(i == 0, 0.0, pltpu.roll(p, 1, 0))
    nxt = jnp.where(i == L - 1, 0.0, pltpu.roll(p, L - 1, 0))
    o_ref[...] = prev * w_ref[0:1, :] + p * w_ref[1:2, :] + nxt * w_ref[2:3, :] + b_ref[...]


def _short_conv(p_hy, B, S, Lc, conv_w, conv_b):
    L = S - Lc
    p3 = p_hy.reshape(B, S, HY_WIDTH)[:, Lc:, :]
    return pl.pallas_call(
        _sconv_kernel,
        grid=(B, HY_WIDTH // LANES),
        in_specs=[pl.BlockSpec((None, L, LANES), lambda b, c: (b, 0, c)),
                  pl.BlockSpec((HY_SHORT, LANES), lambda b, c: (0, c)),
                  pl.BlockSpec((1, LANES), lambda b, c: (0, c))],
        out_specs=pl.BlockSpec((None, L, LANES), lambda b, c: (b, 0, c)),
        out_shape=jax.ShapeDtypeStruct((B, L, HY_WIDTH), F32),
        compiler_params=_cparams(("parallel", "parallel")),
        name="hyena_short_conv",
    )(p3, conv_w, conv_b.reshape(1, HY_WIDTH))


def _dft_constants(L):
    n = 2 * L
    n2 = FFT_N2
    n1 = n // n2
    na = n1 // 2
    g8 = SUBLANES
    w1 = np.exp(-2j * np.pi * np.outer(np.arange(n1), np.arange(n1)) / n1)
    eye8 = np.eye(g8)

    def kron_fwd(w, real_in):
        k1n, an = w.shape
        blocks = np.stack([np.stack([w.real, -w.imag], 1), np.stack([w.imag, w.real], 1)], 1)
        if real_in:
            blocks = blocks[:, :, 0:1, :]
        m = np.einsum('kria,bc->krbiac', blocks, eye8)
        return m.reshape(k1n * 2 * g8, blocks.shape[2] * an * g8)

    m1 = kron_fwd(w1[:, :na], False)
    m1f = kron_fwd(w1, True)
    cw = np.conj(w1[:, :na]).T / n
    blocks = np.stack([np.stack([cw.real, -cw.imag], 1), np.stack([cw.imag, cw.real], 1)], 0)
    m1inv = np.einsum('iark,bc->iabkrc', blocks, eye8).reshape(2 * na * g8, n1 * 2 * g8)
    w2 = np.exp(-2j * np.pi * np.outer(np.arange(n2), np.arange(n2)) / n2)
    w2big = np.block([[w2.real, -w2.imag], [w2.imag, w2.real]])
    w2c = np.conj(w2)
    iw2big = np.block([[w2c.real, -w2c.imag], [w2c.imag, w2c.real]])
    tw = np.exp(-2j * np.pi * np.arange(n2) / n)

    def lanes(cols):
        z = np.concatenate([np.broadcast_to(col[:, None], (n2, LANES)) for col in cols], axis=1)
        return np.stack([z.real, z.imag], 0)

    tw0 = lanes([tw ** q for q in range(FFT_G)])
    twg = lanes([tw ** FFT_G] * FFT_G)
    c = lambda x, dt: jnp.asarray(np.ascontiguousarray(x), dt)
    return dict(m1=c(m1, BF16), m1f=c(m1f, BF16), m1inv=c(m1inv, BF16), w2=c(w2big, BF16), iw2=c(iw2big, BF16),
                tw0=c(tw0, F32), twg=c(twg, F32), n1=n1, na=na)


def _cplx_rows(z, twr, twi, conj):
    n2 = z.shape[0] // 2
    zr, zi = z[:n2], z[n2:]
    if conj:
        return jnp.concatenate([zr * twr + zi * twi, zi * twr - zr * twi], axis=0)
    return jnp.concatenate([zr * twr - zi * twi, zi * twr + zr * twi], axis=0)


def _split_dot(m, x):
    hi = x.astype(BF16)
    lo = (x - hi.astype(F32)).astype(BF16)
    return jnp.dot(m, hi, preferred_element_type=F32) + jnp.dot(m, lo, preferred_element_type=F32)


def _next_twiddle(tw, twg_ref):
    twr, twi = tw
    b_r, b_i = twg_ref[0], twg_ref[1]
    return twr * b_r - twi * b_i, twr * b_i + twi * b_r


def _load_group(scr, j, C):
    return jnp.concatenate([scr[j * FFT_G + q].reshape(2 * FFT_N2, C) for q in range(FFT_G)], axis=1)


def _filt_fft_kernel(f_ref, m1_ref, w2_ref, tw0_ref, twg_ref, o_ref, y_scr, *, n1):
    nbg = FFT_N2 // SUBLANES
    C = f_ref.shape[-1]
    for bg in range(nbg):
        yg = _split_dot(m1_ref[...], f_ref[:, bg].reshape(n1 * SUBLANES, C))
        y_scr[:, :, bg] = yg.reshape(n1, 2, SUBLANES, C)

    def body(j, tw):
        y = _cplx_rows(_load_group(y_scr, j, C), tw[0], tw[1], False)
        z = _split_dot(w2_ref[...], y)
        for q in range(FFT_G):
            o_ref[j * FFT_G + q] = z[:, q * C:(q + 1) * C]
        return _next_twiddle(tw, twg_ref)

    lax.fori_loop(0, n1 // FFT_G, body, (tw0_ref[0], tw0_ref[1]))


def _filter_spectrum(filt, L, dc):
    n1 = dc['n1']
    f4 = filt.reshape(L, HY_ORDER, 2, HY_CH)
    h_f, h_b = f4[:, :, 0, :], f4[:, :, 1, :]
    full = jnp.concatenate([h_f, jnp.zeros_like(h_f[:1]), jnp.flip(h_b[1:], 0)], 0)
    full = jnp.transpose(full, (1, 0, 2)).reshape(HY_ORDER, n1, FFT_N2 // SUBLANES, SUBLANES, HY_CH)
    nbg = FFT_N2 // SUBLANES
    return pl.pallas_call(
        functools.partial(_filt_fft_kernel, n1=n1),
        grid=(HY_ORDER, HY_CH // LANES),
        in_specs=[pl.BlockSpec((None, n1, nbg, SUBLANES, LANES), lambda o, c: (o, 0, 0, 0, c)),
                  pl.BlockSpec(dc['m1f'].shape, lambda o, c: (0, 0)),
                  pl.BlockSpec(dc['w2'].shape, lambda o, c: (0, 0)),
                  pl.BlockSpec(dc['tw0'].shape, lambda o, c: (0, 0, 0)),
                  pl.BlockSpec(dc['twg'].shape, lambda o, c: (0, 0, 0))],
        out_specs=pl.BlockSpec((None, n1, 2 * FFT_N2, LANES), lambda o, c: (o, 0, 0, c)),
        out_shape=jax.ShapeDtypeStruct((HY_ORDER, n1, 2 * FFT_N2, HY_CH), F32),
        scratch_shapes=[pltpu.VMEM((n1, 2, nbg, SUBLANES, LANES), F32)],
        compiler_params=_cparams(("parallel", "parallel")),
        name="hyena_filter_fft",
    )(full, dc['m1f'], dc['w2'], dc['tw0'], dc['twg'])


def _hyconv_kernel(za_ref, zb_ref, ga_ref, gb_ref, ff_ref, bias_ref, m1_ref, m1i_ref, w2_ref, iw2_ref,
                   tw0_ref, twg_ref, oa_ref, ob_ref, y_scr, *, n1, na):
    nbg = FFT_N2 // SUBLANES
    C = za_ref.shape[-1]
    half = na * SUBLANES
    for bg in range(nbg):
        xg = jnp.concatenate([za_ref[:, bg].reshape(half, C), zb_ref[:, bg].reshape(half, C)], axis=0)
        yg = jnp.dot(m1_ref[...], xg.astype(BF16), preferred_element_type=F32)
        y_scr[:, :, bg] = yg.reshape(n1, 2, SUBLANES, C)

    def body(j, tw):
        y = _cplx_rows(_load_group(y_scr, j, C), tw[0], tw[1], False)
        z = jnp.dot(w2_ref[...], y.astype(BF16), preferred_element_type=F32)
        f = _load_group(ff_ref, j, C)
        zr, zi = z[:FFT_N2], z[FFT_N2:]
        fr, fi = f[:FFT_N2], f[FFT_N2:]
        p = jnp.concatenate([zr * fr - zi * fi, zr * fi + zi * fr], axis=0)
        v = jnp.dot(iw2_ref[...], p.astype(BF16), preferred_element_type=F32)
        v = _cplx_rows(v, tw[0], tw[1], True)
        for q in range(FFT_G):
            y_scr[j * FFT_G + q] = v[:, q * C:(q + 1) * C].reshape(2, nbg, SUBLANES, C)
        return _next_twiddle(tw, twg_ref)

    lax.fori_loop(0, n1 // FFT_G, body, (tw0_ref[0], tw0_ref[1]))
    bias = bias_ref[...]
    for bg in range(nbg):
        vg = y_scr[:, :, bg].reshape(n1 * 2 * SUBLANES, C)
        out = jnp.dot(m1i_ref[...], vg.astype(BF16), preferred_element_type=F32)
        ya = out[:half].reshape(na, SUBLANES, C)
        yb = out[half:].reshape(na, SUBLANES, C)
        za = za_ref[:, bg]
        zb = zb_ref[:, bg]
        oa_ref[:, bg] = ga_ref[:, bg] * (ya + za * bias)
        ob_ref[:, bg] = gb_ref[:, bg] * (yb + zb * bias)


def _hyena_conv(z, z_col0, gates, g_col0, spec, bias, dc, B, L):
    n1, na = dc['n1'], dc['na']
    nbg = FFT_N2 // SUBLANES
    NCB = HY_CH // LANES

    def view(a):
        return a.reshape(B, na, nbg, SUBLANES, a.shape[-1])

    def seq(col0, which):
        return pl.BlockSpec((None, na, nbg, SUBLANES, LANES),
                            lambda c, p: (2 * p + which, 0, 0, 0, col0 + c))

    def const(a):
        nd = a.ndim
        return pl.BlockSpec(a.shape, lambda c, p: (0,) * nd)

    out_a, out_b = pl.pallas_call(
        functools.partial(_hyconv_kernel, n1=n1, na=na),
        grid=(NCB, B // 2),
        in_specs=[seq(z_col0, 0), seq(z_col0, 1), seq(g_col0, 0), seq(g_col0, 1),
                  pl.BlockSpec((n1, 2 * FFT_N2, LANES), lambda c, p: (0, 0, c), pipeline_mode=pl.Buffered(1)),
                  pl.BlockSpec((1, LANES), lambda c, p: (0, c)),
                  const(dc['m1']), const(dc['m1inv']), const(dc['w2']), const(dc['iw2']),
                  const(dc['tw0']), const(dc['twg'])],
        out_specs=[pl.BlockSpec((None, na, nbg, SUBLANES, LANES), lambda c, p: (p, 0, 0, 0, c)),
                   pl.BlockSpec((None, na, nbg, SUBLANES, LANES), lambda c, p: (p, 0, 0, 0, c))],
        out_shape=[jax.ShapeDtypeStruct((B // 2, na, nbg, SUBLANES, HY_CH), F32)] * 2,
        scratch_shapes=[pltpu.VMEM((n1, 2, nbg, SUBLANES, LANES), F32)],
        compiler_params=_cparams(("parallel", "arbitrary")),
        name="hyena_long_conv",
    )(view(z), view(z), view(gates), view(gates), spec, bias.reshape(1, HY_CH),
      dc['m1'], dc['m1inv'], dc['w2'], dc['iw2'], dc['tw0'], dc['twg'])
    out = jnp.stack([out_a, out_b], axis=1)
    return out.reshape(B, L, HY_CH)


def _out1_kernel(hy_ref, of_ref, ob_ref, r_ref, k_ref, v_ref, g_ref, x_ref, mod_ref, wo_ref,
                 rk_ref, lg_ref, lb_ref, bd_ref, lng_ref, lnb_ref, xl_ref, u_ref, up_ref, *, alpha):
    o = of_ref[...] + ob_ref[...]
    bd = bd_ref[...]
    inv = 1.0 / RW_HEAD_DIM
    mu = _head_sums(o, bd) * inv
    oc = o - mu
    var = _head_sums(oc * oc, bd) * inv
    on = oc * lax.rsqrt(var + RW_GN_EPS) * lg_ref[...] + lb_ref[...]
    bonus = _head_sums(r_ref[...] * k_ref[...] * rk_ref[...], bd) * v_ref[...]
    y = jnp.concatenate([hy_ref[...], (on + bonus) * g_ref[...]], axis=1)
    _post_mix(y, x_ref[...], mod_ref, wo_ref, lng_ref, lnb_ref, alpha, xl_ref, u_ref, up_ref)


def _readout1(hy, o_f, o_b, r, k, v, g, xcat, mods, B, S, Lc, w_out, r_k, ln_g, ln_b, dn_g, dn_b, alpha):
    D = xcat.shape[-1]
    L = S - Lc
    J = S // ROW_TILE
    JL = L // ROW_TILE
    JC = Lc // ROW_TILE

    def cat_rows(w):
        return pl.BlockSpec((ROW_TILE, w), lambda b, j: (b * J + JC + j, 0))

    def lat_rows(w):
        return pl.BlockSpec((ROW_TILE, w), lambda b, j: (b * JL + j, 0))

    def const(shape):
        return pl.BlockSpec(shape, lambda b, j: (0,) * len(shape))

    sd = jax.ShapeDtypeStruct
    return pl.pallas_call(
        functools.partial(_out1_kernel, alpha=alpha),
        grid=(B, JL),
        in_specs=[lat_rows(HY_CH), cat_rows(RW_W), cat_rows(RW_W), cat_rows(RW_W), cat_rows(RW_W), cat_rows(RW_W),
                  cat_rows(RW_W), cat_rows(D),
                  pl.BlockSpec((None, 6, D), lambda b, j: (b, 0, 0)),
                  const((HY_CH + RW_W, D)), const((1, RW_W)), const((1, RW_W)), const((1, RW_W)),
                  const((RW_W, RW_W)), const((1, D)), const((1, D))],
        out_specs=[lat_rows(D), lat_rows(D), lat_rows(D // 2)],
        out_shape=[sd((B * L, D), F32), sd((B * L, D), F32), sd((B * L, D // 2), U32)],
        compiler_params=_cparams(("parallel", "parallel")),
        name="readout1",
    )(hy, o_f, o_b, r, k, v, g, xcat.reshape(B * S, D), mods, w_out.astype(BF16),
      r_k.reshape(1, RW_W), ln_g.reshape(1, RW_W), ln_b.reshape(1, RW_W),
      _head_sum_matrix(RW_W, RW_HEAD_DIM), dn_g.reshape(1, D), dn_b.reshape(1, D))


def kernel(x, c, ctx, c_ctx, mod_w, mod_b, ln1_g, ln1_b, ln2_g, ln2_b, ev_w_in, ev_w_out, gla_gate_w2, gla_gate_b, gla_norm_g, hg_lb_logits, hg_norm_g, od_w_in, od_w_out, hy_conv_w, hy_conv_b, hy_ffn_w1, hy_ffn_b1, hy_ffn_w2, hy_ffn_b2, hy_ffn_w3, hy_sin_freq, hy_bias, rw_mu, rw_w0, rw_w2, rw_a0, rw_a2, rw_g2, rw_k_k, rw_k_a, rw_r_k, rw_ln_g, rw_ln_b, router_w, router_bias, exp_w13, exp_w2, sh_w13, sh_w2):
    B, L, D = x.shape
    Lc = ctx.shape[1]
    S = Lc + L
    depth = mod_w.shape[0]
    assert depth == 2 and L % ROW_TILE == 0 and Lc % ROW_TILE == 0 and B % 2 == 0
    assert L % (FFT_N2 * 2) == 0 and L % GRID_W == 0
    alpha = (2 * depth) ** 0.25
    nctx = Lc // ROW_TILE
    J = S // ROW_TILE
    T = B * S

    cc = jnp.concatenate([c, c_ctx[None, :]], axis=0)
    cc = jnp.pad(cc, ((0, (-cc.shape[0]) % SUBLANES), (0, 0)))
    hg_lb = jnp.cumsum(jax.nn.softmax(hg_lb_logits.astype(F32), axis=0), axis=0)
    xcat = jnp.concatenate([ctx, x], axis=1)

    mods = _modulation(cc, mod_w[0], mod_b[0])
    gq, gk, gv, gla, r, hq, hk, hla, hv, hgate = _in_proj0(xcat, mods, nctx, ev_w_in[0], gla_gate_w2[0],
                                                            gla_gate_b[0], hg_lb[0])
    gk3 = gk.reshape(1, T, GLA_QK)
    o_gf = _gated_recurrence(gq, gk3, gv, gla, 0, B, S, nctx, False, GLA_HEADS, GLA_DK, GLA_DV)
    o_gb = _gated_recurrence(gq, gk3, gv, gla, 0, B, S, nctx, True, GLA_HEADS, GLA_DK, GLA_DV)
    o_hf = _gated_recurrence(hq, hk, hv, hla, 0, B, S, nctx, False, HG_HEADS, HG_EXPAND, HG_EXPAND)
    o_hb = _gated_recurrence(hq, hk, hv, hla, 1, B, S, nctx, True, HG_HEADS, HG_EXPAND, HG_EXPAND)
    xl, u, up = _readout0(o_gf, o_gb, o_hf, o_hb, r, hgate, xcat, mods, nctx, ev_w_out[0], gla_norm_g[0],
                          hg_norm_g[0], ln1_g[0], ln1_b[0], alpha)
    cmb_rows = 2 * CMB_TILE
    tiles_per_b = S // cmb_rows
    tile_in_b = jnp.arange(B * tiles_per_b, dtype=I32) % tiles_per_b
    mod_of_tile = jnp.where(tile_in_b < Lc // cmb_rows, B, jnp.arange(B * tiles_per_b, dtype=I32) // tiles_per_b)
    x1 = _moe_block(u, up, xl, mods, mod_of_tile.astype(I32), router_w[0], router_bias[0], exp_w13, exp_w2, 0,
                    sh_w13[0], sh_w2[0], ln2_g[0], ln2_b[0], alpha)
    xcat = x1.reshape(B, S, D)

    mods = _modulation(cc, mod_w[1], mod_b[1])
    p_hy, p_rw = _in_proj1(xcat, mods, nctx, od_w_in[0])
    rr, rk, rv, rkk, rbb, rld, rg = _rw_streams(p_rw, B, S, nctx, rw_mu[0], rw_w0[0], rw_w2[0], rw_a0[0],
                                                 rw_a2[0], rw_g2[0], rw_k_k[0], rw_k_a[0])
    o_f = _rwkv_scan(rr, rk, rv, rkk, rbb, rld, B, S, nctx, False)
    o_b = _rwkv_scan(rr, rk, rv, rkk, rbb, rld, B, S, nctx, True)
    dc = _dft_constants(L)
    filt = _hyena_filters(L, hy_ffn_w1[0], hy_ffn_b1[0], hy_ffn_w2[0], hy_ffn_b2[0], hy_ffn_w3[0], hy_sin_freq[0])
    spec = _filter_spectrum(filt, L, dc)
    uu = _short_conv(p_hy, B, S, Lc, hy_conv_w[0], hy_conv_b[0])
    NCB = HY_CH // LANES
    z1 = _hyena_conv(uu, 0, uu, NCB, spec[0], hy_bias[0, 0], dc, B, L)
    z2 = _hyena_conv(z1, 0, uu, 2 * NCB, spec[1], hy_bias[0, 1], dc, B, L)
    xl, u, up = _readout1(z2.reshape(B * L, HY_CH), o_f, o_b, rr, rk, rv, rg, xcat, mods, B, S, Lc, od_w_out[0],
                          rw_r_k[0], rw_ln_g[0], rw_ln_b[0], ln1_g[1], ln1_b[1], alpha)
    mod_of_tile = (jnp.arange(B * (L // cmb_rows), dtype=I32) // (L // cmb_rows)).astype(I32)
    out = _moe_block(u, up, xl, mods, mod_of_tile, router_w[1], router_bias[1], exp_w13, exp_w2, 1,
                     sh_w13[1], sh_w2[1], ln2_g[1], ln2_b[1], alpha)
    return out.reshape(B, L, D)
```

```python
import functools
import math

import numpy as np
import jax
import jax.numpy as jnp
from jax import lax
from jax.experimental import pallas as pl
from jax.experimental.pallas import tpu as pltpu

F32 = jnp.float32
BF16 = jnp.bfloat16
I32 = jnp.int32
U32 = jnp.uint32
HI = lax.Precision.HIGHEST

LN_EPS = 1e-5
GLA_HEADS, GLA_DK, GLA_DV = 4, 64, 128
GLA_QK, GLA_V = GLA_HEADS * GLA_DK, GLA_HEADS * GLA_DV
GLA_GATE_RANK = 16
GLA_GATE_NORM = 16.0
CHUNK = 64
HG_HEADS, HG_EXPAND = 4, 128
HG_W = HG_HEADS * HG_EXPAND
HY_CH, HY_ORDER, HY_SHORT, HY_EMB = 512, 2, 3, 33
HY_WIDTH = (HY_ORDER + 1) * HY_CH
HY_FAST_DECAY, HY_SLOW_DECAY, HY_TARGET = 0.3, 1.5, 1e-2
RW_HEADS, RW_HEAD_DIM = 8, 64
RW_W = RW_HEADS * RW_HEAD_DIM
RW_DECAY_LORA, RW_AAA_LORA, RW_GATE_LORA = 64, 64, 128
RW_GN_EPS = 64e-5
GRID_W = 64
N_EXPERTS, TOP_K, N_GROUPS, TOPK_GROUPS = 256, 8, 8, 4
PER_GROUP = N_EXPERTS // N_GROUPS
EXPERT_FF, SHARED_FF = 256, 256
ROUTED_SCALE = 2.5

LANES = 128
SUBLANES = 8
VMEM_LIMIT = 56 * 1024 * 1024
ROW_TILE = 256
MOE_BLK = 256
CMB_TILE = 128
RW_CHUNK = 64
FFT_N2 = 128
FFT_G = 4


def _cparams(sem):
    return pltpu.CompilerParams(dimension_semantics=sem, vmem_limit_bytes=VMEM_LIMIT)


def _bdot(a, b):
    return jnp.dot(a.astype(BF16), b.astype(BF16), preferred_element_type=F32)


def _bdot_nt(a, b):
    return lax.dot_general(a.astype(BF16), b.astype(BF16), (((1,), (1,)), ((), ())),
                           preferred_element_type=F32)


def _bdot_tn(a, b):
    return lax.dot_general(a.astype(BF16), b.astype(BF16), (((0,), (0,)), ((), ())),
                           preferred_element_type=F32)


def _bmm(a, b):
    return jnp.einsum('nij,njk->nik', a.astype(BF16), b.astype(BF16), preferred_element_type=F32)


def _bmm_nt(a, b):
    return jnp.einsum('nik,njk->nij', a.astype(BF16), b.astype(BF16), preferred_element_type=F32)


def _bmm_tn(a, b):
    return _bmm(jnp.swapaxes(a, 1, 2), b)


def _hdot(a, b):
    return jnp.dot(a, b, precision=HI, preferred_element_type=F32)


def _head_sums(x, ones_bd):
    hi = x.astype(BF16)
    lo = (x - hi.astype(F32)).astype(BF16)
    return (jnp.dot(hi, ones_bd, preferred_element_type=F32) + jnp.dot(lo, ones_bd, preferred_element_type=F32))


def _silu(x):
    return x * jax.nn.sigmoid(x)


def _layer_norm_rows(x, g, b, eps):
    mu = jnp.mean(x, axis=-1, keepdims=True)
    xc = x - mu
    var = jnp.mean(xc * xc, axis=-1, keepdims=True)
    return xc * lax.rsqrt(var + eps) * g + b


def _mod_kernel(c_ref, w_ref, b_ref, o_ref):
    o_ref[...] = _hdot(_silu(c_ref[...]), w_ref[...]) + b_ref[...]


def _modulation(cc, w, b):
    R, D = cc.shape
    out = pl.pallas_call(
        _mod_kernel,
        grid=(6,),
        in_specs=[pl.BlockSpec((R, D), lambda j: (0, 0)),
                  pl.BlockSpec((D, D), lambda j: (0, j)),
                  pl.BlockSpec((1, D), lambda j: (0, j))],
        out_specs=pl.BlockSpec((R, D), lambda j: (0, j)),
        out_shape=jax.ShapeDtypeStruct((R, 6 * D), F32),
        compiler_params=_cparams(("parallel",)),
        name="modulation",
    )(cc, w, b.reshape(1, 6 * D))
    return out.reshape(R, 6, D)


def _in0_kernel(x_ref, mod_ref, w_ref, wa_ref, w2_ref, gb_ref, lb_ref,
                gq_ref, gk_ref, gv_ref, gla_ref, r_ref, hq_ref, hk_ref, hla_ref, hv_ref, hg_ref):
    sh1 = mod_ref[0:1, :]
    sc1 = mod_ref[1:2, :]
    h = (x_ref[...] * (1.0 + sc1) + sh1).astype(BF16)

    def proj(off, width):
        return jnp.dot(h, w_ref[:, off:off + width], preferred_element_type=F32)

    gq_ref[...] = (proj(0, GLA_QK) * (GLA_DK ** -0.5)).astype(BF16)
    gk_ref[...] = proj(GLA_QK, GLA_QK).astype(BF16)
    gv_ref[...] = proj(2 * GLA_QK, GLA_V).astype(BF16)
    r_ref[...] = proj(2 * GLA_QK + GLA_V, GLA_V).astype(BF16)
    base = 2 * GLA_QK + 2 * GLA_V
    a = jnp.dot(h, wa_ref[...], preferred_element_type=F32)
    z = _bdot(a, w2_ref[...]) + gb_ref[...]
    ls = (jnp.minimum(z, 0.0) - jnp.log(1.0 + jnp.exp(-jnp.abs(z)))) * (1.0 / GLA_GATE_NORM)
    gla_ref[0] = ls[:, :GLA_QK]
    gla_ref[1] = ls[:, GLA_QK:]
    hq_ref[...] = _silu(proj(base, HG_W)).astype(BF16)
    for d in range(2):
        zf = proj(base + (1 + d) * HG_W, HG_W)
        lb = lb_ref[d:d + 1, :]
        f = lb + (1.0 - lb) * jax.nn.sigmoid(zf)
        hk_ref[d] = (1.0 - f).astype(BF16)
        hla_ref[d] = jnp.log(f)
    hv_ref[...] = proj(base + 3 * HG_W, HG_W).astype(BF16)
    hg_ref[...] = proj(base + 4 * HG_W, HG_W).astype(BF16)


def _in_proj0(xcat, mods, nctx_tiles, w_in, gate_w2, gate_b, lb):
    B, S, D = xcat.shape
    J = S // ROW_TILE
    T = B * S
    a_off = 2 * GLA_QK + 2 * GLA_V
    wmain = jnp.concatenate([w_in[:, :a_off], w_in[:, a_off + 2 * GLA_GATE_RANK:]], axis=1).astype(BF16)
    wa = jnp.pad(w_in[:, a_off:a_off + 2 * GLA_GATE_RANK], ((0, 0), (0, LANES - 2 * GLA_GATE_RANK))).astype(BF16)
    w2 = jnp.zeros((LANES, 2 * GLA_QK), F32)
    w2 = w2.at[:GLA_GATE_RANK, :GLA_QK].set(gate_w2[0]).at[GLA_GATE_RANK:2 * GLA_GATE_RANK, GLA_QK:].set(gate_w2[1])
    gb = gate_b.reshape(1, 2 * GLA_QK)
    WM = wmain.shape[1]

    def rows(w):
        return pl.BlockSpec((ROW_TILE, w), lambda b, j: (b * J + j, 0))

    def rows2(w):
        return pl.BlockSpec((2, ROW_TILE, w), lambda b, j: (0, b * J + j, 0))

    def const(shape):
        return pl.BlockSpec(shape, lambda b, j: (0,) * len(shape))

    sd = jax.ShapeDtypeStruct
    outs = pl.pallas_call(
        _in0_kernel,
        grid=(B, J),
        in_specs=[rows(D),
                  pl.BlockSpec((None, 6, D), lambda b, j: (jnp.where(j < nctx_tiles, B, b), 0, 0)),
                  const((D, WM)), const((D, LANES)), const((LANES, 2 * GLA_QK)), const((1, 2 * GLA_QK)),
                  const((2, HG_W))],
        out_specs=[rows(GLA_QK), rows(GLA_QK), rows(GLA_V), rows2(GLA_QK), rows(GLA_V),
                   rows(HG_W), rows2(HG_W), rows2(HG_W), rows(HG_W), rows(HG_W)],
        out_shape=[sd((T, GLA_QK), BF16), sd((T, GLA_QK), BF16), sd((T, GLA_V), BF16), sd((2, T, GLA_QK), F32),
                   sd((T, GLA_V), BF16), sd((T, HG_W), BF16), sd((2, T, HG_W), BF16), sd((2, T, HG_W), F32),
                   sd((T, HG_W), BF16), sd((T, HG_W), BF16)],
        compiler_params=_cparams(("parallel", "parallel")),
        name="in_proj0",
    )(xcat.reshape(T, D), mods, wmain, wa, w2, gb, lb)
    return outs


def _rec_kernel(q_ref, k_ref, v_ref, la_ref, o_ref, st_ref, *, rev, nh, dk, dv, nchunk):
    t = pl.program_id(1)

    @pl.when(t == 0)
    def _():
        st_ref[...] = jnp.zeros_like(st_ref)

    C = CHUNK
    row = lax.broadcasted_iota(I32, (C, C), 0)
    col = lax.broadcasted_iota(I32, (C, C), 1)
    incl = (col >= row) if rev else (col <= row)
    tri = jnp.where(incl, 1.0, 0.0).astype(BF16)
    ref_i = C // 2 - 1 if rev else C // 2
    last_i = 0 if rev else C - 1
    order = range(nchunk - 1, -1, -1) if rev else range(nchunk)

    def stack(x, hd):
        return jnp.stack([x[c * C:(c + 1) * C, h * hd:(h + 1) * hd] for c in range(nchunk) for h in range(nh)], 0)

    def rows(x, i):
        return jnp.concatenate([jnp.broadcast_to(x[c * C + i:c * C + i + 1], (C, x.shape[1]))
                                for c in range(nchunk)], axis=0)

    la = la_ref[...]
    q = q_ref[...].astype(F32)
    k = k_ref[...].astype(F32)
    b = jnp.concatenate([_split_dot(tri, la[c * C:(c + 1) * C]) for c in range(nchunk)], axis=0)
    b_mid = rows(b, ref_i)
    b_last = rows(b, last_i)
    v = stack(v_ref[...].astype(F32), dv)
    sc = _bmm_nt(stack(q * jnp.exp(b - b_mid), dk), stack(k * jnp.exp(b_mid - b), dk))
    o_intra = _bmm(jnp.where(incl[None], sc, 0.0), v)
    q_in = stack(q * jnp.exp(b), dk)
    kv_t = _bmm_tn(v, stack(k * jnp.exp(b_last - b), dk))
    dec = stack(jnp.exp(b_last), dk)[:, 0:1, :]
    s_t = st_ref[...]
    for c in order:
        sl = slice(c * nh, (c + 1) * nh)
        o_c = o_intra[sl] + _bmm_nt(q_in[sl], s_t)
        o_ref[c * C:(c + 1) * C, :] = jnp.concatenate([o_c[h] for h in range(nh)], axis=1)
        s_t = s_t * dec[sl] + kv_t[sl]
    st_ref[...] = s_t


def _gated_recurrence(q, k, v, la, kdir, B, S, nctx_blocks, rev, nh, dk, dv):
    T = B * S
    NB = S // ROW_TILE
    wk, wv = nh * dk, nh * dv
    assert q.shape[1] == wk
    d = 1 if rev else 0
    nc = nctx_blocks

    def blk(t):
        if not rev:
            return t
        return jnp.where(t < nc, nc - 1 - t, NB - 1 - (t - nc))

    kern = functools.partial(_rec_kernel, rev=rev, nh=nh, dk=dk, dv=dv, nchunk=ROW_TILE // CHUNK)
    return pl.pallas_call(
        kern,
        grid=(B, NB),
        in_specs=[pl.BlockSpec((ROW_TILE, wk), lambda b, t: (b * NB + blk(t), 0)),
                  pl.BlockSpec((None, ROW_TILE, wk), lambda b, t: (kdir, b * NB + blk(t), 0)),
                  pl.BlockSpec((ROW_TILE, wv), lambda b, t: (b * NB + blk(t), 0)),
                  pl.BlockSpec((None, ROW_TILE, wk), lambda b, t: (d, b * NB + blk(t), 0))],
        out_specs=pl.BlockSpec((ROW_TILE, wv), lambda b, t: (b * NB + blk(t), 0)),
        out_shape=jax.ShapeDtypeStruct((T, wv), F32),
        scratch_shapes=[pltpu.VMEM((nh, dv, dk), F32)],
        compiler_params=_cparams(("parallel", "arbitrary")),
        name="gated_rec_bwd" if rev else "gated_rec_fwd",
    )(q, k, v, la)


def _pack_rows(x):
    w = x.shape[1] // 2
    lo = lax.bitcast_convert_type(x[:, :w].astype(BF16).astype(F32), U32)
    hi = lax.bitcast_convert_type(x[:, w:].astype(BF16).astype(F32), U32)
    return (lo >> 16) | (hi & jnp.uint32(0xFFFF0000))


def _unpack_rows(p):
    lo = lax.bitcast_convert_type(p << 16, F32)
    hi = lax.bitcast_convert_type(p & jnp.uint32(0xFFFF0000), F32)
    return lo, hi


def _post_mix(y, x, mod_ref, wo_ref, lng_ref, lnb_ref, alpha, xl_ref, u_ref, up_ref):
    g1 = mod_ref[2:3, :]
    sh2 = mod_ref[3:4, :]
    sc2 = mod_ref[4:5, :]
    yo = jnp.dot(y.astype(BF16), wo_ref[...], preferred_element_type=F32)
    xl = _layer_norm_rows(alpha * x + g1 * yo, lng_ref[...], lnb_ref[...], LN_EPS)
    xl_ref[...] = xl
    u = xl * (1.0 + sc2) + sh2
    u_ref[...] = u
    up_ref[...] = _pack_rows(u)


def _out0_kernel(gf_ref, gb_ref, hf_ref, hb_ref, r_ref, hg_ref, x_ref, mod_ref, wo_ref,
                 gg_ref, hgg_ref, lng_ref, lnb_ref, xl_ref, u_ref, up_ref, *, alpha):
    def heads(o, g, gate):
        parts = []
        for hh in range(o.shape[1] // LANES):
            seg = o[:, hh * LANES:(hh + 1) * LANES]
            ms = jnp.mean(seg * seg, axis=-1, keepdims=True)
            parts.append(seg * lax.rsqrt(ms + 1e-6) * g)
        return jnp.concatenate(parts, axis=1) * _silu(gate.astype(F32))

    y = jnp.concatenate([heads(gf_ref[...] + gb_ref[...], gg_ref[...], r_ref[...]),
                         heads(hf_ref[...] + hb_ref[...], hgg_ref[...], hg_ref[...])], axis=1)
    _post_mix(y, x_ref[...], mod_ref, wo_ref, lng_ref, lnb_ref, alpha, xl_ref, u_ref, up_ref)


def _readout0(o_gf, o_gb, o_hf, o_hb, r, hgate, xcat, mods, nctx_tiles, w_out, gla_g, hg_g, ln_g, ln_b, alpha):
    B, S, D = xcat.shape
    J = S // ROW_TILE
    T = B * S

    def rows(w):
        return pl.BlockSpec((ROW_TILE, w), lambda b, j: (b * J + j, 0))

    def const(shape):
        return pl.BlockSpec(shape, lambda b, j: (0,) * len(shape))

    sd = jax.ShapeDtypeStruct
    return pl.pallas_call(
        functools.partial(_out0_kernel, alpha=alpha),
        grid=(B, J),
        in_specs=[rows(GLA_V), rows(GLA_V), rows(HG_W), rows(HG_W), rows(GLA_V), rows(HG_W), rows(D),
                  pl.BlockSpec((None, 6, D), lambda b, j: (jnp.where(j < nctx_tiles, B, b), 0, 0)),
                  const((GLA_V + HG_W, D)), const((1, GLA_DV)), const((1, HG_EXPAND)), const((1, D)), const((1, D))],
        out_specs=[rows(D), rows(D), rows(D // 2)],
        out_shape=[sd((T, D), F32), sd((T, D), F32), sd((T, D // 2), U32)],
        compiler_params=_cparams(("parallel", "parallel")),
        name="readout0",
    )(o_gf, o_gb, o_hf, o_hb, r, hgate, xcat.reshape(T, D), mods, w_out.astype(BF16),
      gla_g.reshape(1, -1), hg_g.reshape(1, -1), ln_g.reshape(1, D), ln_b.reshape(1, D))


def _route_kernel(u_ref, rwh_ref, rwl_ref, rb_ref, tri_ref, eidx_ref, gate_ref, rank_ref, cnt_ref, carry_ref):
    i = pl.program_id(0)

    @pl.when(i == 0)
    def _():
        carry_ref[...] = jnp.zeros_like(carry_ref)

    u = u_ref[...]
    TM = u.shape[0]
    E = N_EXPERTS
    u_hi = u.astype(BF16)
    u_lo = (u - u_hi.astype(F32)).astype(BF16)
    nt = (((1,), (1,)), ((), ()))
    logits = (lax.dot_general(rwh_ref[...], u_hi, nt, preferred_element_type=F32)
              + lax.dot_general(rwh_ref[...], u_lo, nt, preferred_element_type=F32)
              + lax.dot_general(rwl_ref[...], u_hi, nt, preferred_element_type=F32))
    scores = jax.nn.sigmoid(logits)
    sel = scores + rb_ref[...]
    neg = -jnp.inf
    sel3 = sel.reshape(N_GROUPS, PER_GROUP, TM)
    io_g = lax.broadcasted_iota(I32, sel3.shape, 1)
    m1 = jnp.max(sel3, axis=1, keepdims=True)
    i1 = jnp.min(jnp.where(sel3 == m1, io_g, PER_GROUP), axis=1, keepdims=True)
    m2 = jnp.max(jnp.where(io_g == i1, neg, sel3), axis=1, keepdims=True)
    grp = m1 + m2
    io_n = lax.broadcasted_iota(I32, grp.shape, 0)
    keep = jnp.zeros(grp.shape, jnp.bool_)
    for _ in range(TOPK_GROUPS):
        m = jnp.max(grp, axis=0, keepdims=True)
        idx = jnp.min(jnp.where(grp == m, io_n, N_GROUPS), axis=0, keepdims=True)
        hit = io_n == idx
        keep = jnp.logical_or(keep, hit)
        grp = jnp.where(hit, neg, grp)
    sel = jnp.where(keep, sel3, neg).reshape(E, TM)
    io_e = lax.broadcasted_iota(I32, (E, TM), 0)
    base = carry_ref[...]
    tri = tri_ref[...]
    e_rows, g_rows, r_rows = [], [], []
    for _ in range(TOP_K):
        m = jnp.max(sel, axis=0, keepdims=True)
        idx = jnp.min(jnp.where(sel == m, io_e, E), axis=0, keepdims=True)
        hit = io_e == idx
        hit_f = jnp.where(hit, 1.0, 0.0)
        g_rows.append(jnp.sum(jnp.where(hit, scores, 0.0), axis=0, keepdims=True))
        prefix = jnp.dot(hit_f.astype(BF16), tri, preferred_element_type=F32)
        r_rows.append(jnp.sum(jnp.where(hit, prefix + base, 0.0), axis=0, keepdims=True))
        base = base + jnp.sum(hit_f, axis=1, keepdims=True)
        e_rows.append(idx)
        sel = jnp.where(hit, neg, sel)
    carry_ref[...] = base
    cnt_ref[...] = base
    g = jnp.concatenate(g_rows, axis=0)
    gate_ref[...] = g / jnp.sum(g, axis=0, keepdims=True) * ROUTED_SCALE
    eidx_ref[...] = jnp.concatenate(e_rows, axis=0)
    rank_ref[...] = jnp.concatenate(r_rows, axis=0).astype(I32)


def _route(u, router_w, router_bias):
    T, D = u.shape
    E = N_EXPERTS
    n = T // ROW_TILE
    tri = jnp.asarray(np.triu(np.ones((ROW_TILE, ROW_TILE), np.float32), 1), BF16)
    sd = jax.ShapeDtypeStruct
    cols = pl.BlockSpec((TOP_K, ROW_TILE), lambda i: (0, i))
    rwt = router_w.T
    rw_hi = rwt.astype(BF16)
    rw_lo = (rwt - rw_hi.astype(F32)).astype(BF16)
    return pl.pallas_call(
        _route_kernel,
        grid=(n,),
        in_specs=[pl.BlockSpec((ROW_TILE, D), lambda i: (i, 0)),
                  pl.BlockSpec((E, D), lambda i: (0, 0)),
                  pl.BlockSpec((E, D), lambda i: (0, 0)),
                  pl.BlockSpec((E, 1), lambda i: (0, 0)),
                  pl.BlockSpec((ROW_TILE, ROW_TILE), lambda i: (0, 0))],
        out_specs=[cols, cols, cols, pl.BlockSpec((E, 1), lambda i: (0, 0))],
        out_shape=[sd((TOP_K, T), I32), sd((TOP_K, T), F32), sd((TOP_K, T), I32), sd((E, 1), F32)],
        scratch_shapes=[pltpu.VMEM((E, 1), F32)],
        compiler_params=_cparams(("arbitrary",)),
        name="moe_route",
    )(u, rw_hi, rw_lo, router_bias.reshape(E, 1), tri)


def _dest_kernel(e_ref, r_ref, ps_ref, d_ref):
    e = e_ref[...]
    TM = e.shape[1]
    io_e = lax.broadcasted_iota(I32, (N_EXPERTS, TM), 0)
    ps = ps_ref[...]
    rows = []
    for k in range(TOP_K):
        rows.append(jnp.sum(jnp.where(io_e == e[k:k + 1, :], ps, 0.0), axis=0, keepdims=True))
    d_ref[...] = jnp.concatenate(rows, axis=0).astype(I32) + r_ref[...]


def _slot_of(eidx, rank, pad_start):
    T = eidx.shape[1]
    cols = pl.BlockSpec((TOP_K, ROW_TILE), lambda i: (0, i))
    return pl.pallas_call(
        _dest_kernel,
        grid=(T // ROW_TILE,),
        in_specs=[cols, cols, pl.BlockSpec((N_EXPERTS, 1), lambda i: (0, 0))],
        out_specs=cols,
        out_shape=jax.ShapeDtypeStruct((TOP_K, T), I32),
        compiler_params=_cparams(("parallel",)),
        name="moe_slot",
    )(eidx, rank, pad_start.astype(F32).reshape(N_EXPERTS, 1))


def _dispatch_kernel(z0_ref, zn_ref, dest_ref, u_ref, xs_ref, zero_scr, sem, zsem):
    TM = u_ref.shape[0]

    @pl.when(pl.program_id(0) == 0)
    def _():
        zero_scr[...] = jnp.zeros_like(zero_scr)

        def zcopy(e):
            row0 = pl.multiple_of(z0_ref[e], MOE_BLK)
            return pltpu.make_async_copy(zero_scr, xs_ref.at[pl.ds(row0, MOE_BLK)], zsem)

        def zstart(e, c):
            @pl.when(zn_ref[e] > 0)
            def _():
                zcopy(e).start()
            return c

        def zwait(e, c):
            @pl.when(zn_ref[e] > 0)
            def _():
                zcopy(e).wait()
            return c

        lax.fori_loop(0, N_EXPERTS, zstart, 0)
        lax.fori_loop(0, N_EXPERTS, zwait, 0)

    def copy(t, k):
        return pltpu.make_async_copy(u_ref.at[pl.ds(t, 1)], xs_ref.at[pl.ds(dest_ref[k, t], 1)], sem)

    def issue(t, c):
        for k in range(TOP_K):
            copy(t, k).start(priority=k % 2)
        return c

    lax.fori_loop(0, TM, issue, 0)
    for k in range(TOP_K):
        pltpu.make_async_copy(u_ref, xs_ref.at[pl.ds(0, TM)], sem).wait()


def _dispatch(up, dest, n_slots, pad_row0, pad_rows):
    T, W = up.shape
    return pl.pallas_call(
        _dispatch_kernel,
        grid_spec=pltpu.PrefetchScalarGridSpec(
            num_scalar_prefetch=2,
            grid=(T // ROW_TILE,),
            in_specs=[pl.BlockSpec((TOP_K, ROW_TILE), lambda i, z0, zn: (0, i), memory_space=pltpu.SMEM),
                      pl.BlockSpec((ROW_TILE, W), lambda i, z0, zn: (i, 0))],
            out_specs=pl.BlockSpec(memory_space=pl.ANY),
            scratch_shapes=[pltpu.VMEM((MOE_BLK, W), U32), pltpu.SemaphoreType.DMA, pltpu.SemaphoreType.DMA]),
        out_shape=jax.ShapeDtypeStruct((n_slots, W), U32),
        compiler_params=_cparams(("arbitrary",)),
        name="moe_dispatch",
    )(pad_row0, pad_rows, dest, up)


def _expert_kernel(be_ref, nu_ref, first_ref, nxt_ref, par_ref, xs_ref, w13_hbm, w2_hbm, ys_ref,
                   w13_buf, w2_buf, w13_bf, w2_bf, sem, *, layer):
    i = pl.program_id(0)

    def fetch(e, slot):
        return (pltpu.make_async_copy(w13_hbm.at[layer, e], w13_buf.at[slot], sem.at[0, slot]),
                pltpu.make_async_copy(w2_hbm.at[layer, e], w2_buf.at[slot], sem.at[1, slot]))

    @pl.when(i == 0)
    def _():
        for c in fetch(be_ref[0], 0):
            c.start()

    @pl.when(jnp.logical_and(i < nu_ref[0], first_ref[i] == 1))
    def _():
        slot = par_ref[i]
        for c in fetch(be_ref[i], slot):
            c.wait()

        @pl.when(nxt_ref[i] >= 0)
        def _():
            for c in fetch(nxt_ref[i], 1 - slot):
                c.start()

        w13_bf[...] = w13_buf[slot].astype(BF16)
        w2_bf[...] = w2_buf[slot].astype(BF16)

    @pl.when(i < nu_ref[0])
    def _():
        lo, hi = _unpack_rows(xs_ref[...])
        x = jnp.concatenate([lo, hi], axis=1).astype(BF16)
        h = jnp.dot(x, w13_bf[...], preferred_element_type=F32)
        a = _silu(h[:, :EXPERT_FF]) * h[:, EXPERT_FF:]
        ys_ref[...] = _pack_rows(jnp.dot(a.astype(BF16), w2_bf[...], preferred_element_type=F32))


def _expert_gemm(xs, blk_exp, n_used, cnt, w13, w2, layer):
    NP, W = xs.shape
    D = 2 * W
    nblk = NP // MOE_BLK
    F2 = w13.shape[3]
    E = cnt.shape[0]
    blk = jnp.arange(nblk, dtype=I32)
    first = jnp.concatenate([jnp.ones((1,), I32), (blk_exp[1:] != blk_exp[:-1]).astype(I32)])
    first = jnp.where(blk < n_used[0], first, 0)
    par = (jnp.cumsum(first) - 1) % 2
    ids = jnp.where(cnt > 0, jnp.arange(E, dtype=I32), E)
    suffix_min = lax.cummin(ids, axis=0, reverse=True)
    next_active = jnp.concatenate([suffix_min[1:], jnp.full((1,), E, I32)])
    next_active = jnp.where(next_active >= E, -1, next_active)
    nxt = jnp.take(next_active, blk_exp)

    def xmap(i, be, nu, fi, nx, pa):
        return (jnp.minimum(i, nu[0] - 1), 0)

    return pl.pallas_call(
        functools.partial(_expert_kernel, layer=layer),
        grid_spec=pltpu.PrefetchScalarGridSpec(
            num_scalar_prefetch=5,
            grid=(nblk,),
            in_specs=[pl.BlockSpec((MOE_BLK, W), xmap),
                      pl.BlockSpec(memory_space=pl.ANY), pl.BlockSpec(memory_space=pl.ANY)],
            out_specs=pl.BlockSpec((MOE_BLK, W), xmap),
            scratch_shapes=[pltpu.VMEM((2, D, F2), F32), pltpu.VMEM((2, F2 // 2, D), F32),
                            pltpu.VMEM((D, F2), BF16), pltpu.VMEM((F2 // 2, D), BF16),
                            pltpu.SemaphoreType.DMA((2, 2))]),
        out_shape=jax.ShapeDtypeStruct((NP, W), U32),
        compiler_params=_cparams(("arbitrary",)),
        name="moe_experts",
    )(blk_exp, n_used, first, nxt.astype(I32), par.astype(I32), xs, w13, w2)


def _combine_kernel(mt_ref, dcur_ref, dnxt_ref, gate_ref, up_ref, xl_ref, mod_ref, s13_ref, s2_ref, lng_ref, lnb_ref,
                    ys_ref, o_ref, rows_a, rows_b, sem_a, sem_b, *, alpha):
    del mt_ref
    i = pl.program_id(0)
    H = CMB_TILE
    W = up_ref.shape[1]

    def issue(dref, col0, rows, sem):
        for t in range(H):
            for k in range(TOP_K):
                pltpu.make_async_copy(ys_ref.at[pl.ds(dref[k, col0 + t], 1)], rows.at[k, pl.ds(t, 1)],
                                      sem).start(priority=k % 2)

    def wait(rows, sem):
        for k in range(TOP_K):
            pltpu.make_async_copy(ys_ref.at[pl.ds(0, H)], rows.at[k], sem).wait()

    def compute(rows, r0):
        lo, hi = _unpack_rows(up_ref[r0:r0 + H])
        hs = _bdot(jnp.concatenate([lo, hi], axis=1), s13_ref[...])
        sh = _bdot(_silu(hs[:, :SHARED_FF]) * hs[:, SHARED_FF:], s2_ref[...])
        acc_lo, acc_hi = sh[:, :W], sh[:, W:]
        gate = gate_ref[r0:r0 + H]
        for k in range(TOP_K):
            lo, hi = _unpack_rows(rows[k])
            g = gate[:, k:k + 1]
            acc_lo = acc_lo + lo * g
            acc_hi = acc_hi + hi * g
        acc = jnp.concatenate([acc_lo, acc_hi], axis=1)
        g2 = mod_ref[5:6, :]
        o_ref[r0:r0 + H] = _layer_norm_rows(alpha * xl_ref[r0:r0 + H] + g2 * acc, lng_ref[...], lnb_ref[...], LN_EPS)

    @pl.when(i == 0)
    def _():
        issue(dcur_ref, 0, rows_a, sem_a)

    wait(rows_a, sem_a)
    issue(dcur_ref, H, rows_b, sem_b)
    compute(rows_a, 0)
    wait(rows_b, sem_b)
    issue(dnxt_ref, 0, rows_a, sem_a)
    compute(rows_b, H)

    @pl.when(i == pl.num_programs(0) - 1)
    def _():
        wait(rows_a, sem_a)


def _combine(dest, gate_t, up, xl, ys, mods, mod_of_tile, sh_w13, sh_w2, ln_g, ln_b, alpha):
    T, D = xl.shape
    W = D // 2
    TM = 2 * CMB_TILE
    n = T // TM

    def rows(w):
        return pl.BlockSpec((TM, w), lambda i, mt: (i, 0))

    def const(shape):
        return pl.BlockSpec(shape, lambda i, mt: (0,) * len(shape))

    return pl.pallas_call(
        functools.partial(_combine_kernel, alpha=alpha),
        grid_spec=pltpu.PrefetchScalarGridSpec(
            num_scalar_prefetch=1,
            grid=(n,),
            in_specs=[pl.BlockSpec((TOP_K, TM), lambda i, mt: (0, i), memory_space=pltpu.SMEM),
                      pl.BlockSpec((TOP_K, TM), lambda i, mt: (0, jnp.minimum(i + 1, n - 1)), memory_space=pltpu.SMEM),
                      rows(TOP_K), rows(W), rows(D),
                      pl.BlockSpec((None, 6, D), lambda i, mt: (mt[i], 0, 0)),
                      const((D, 2 * SHARED_FF)), const((SHARED_FF, D)), const((1, D)), const((1, D)),
                      pl.BlockSpec(memory_space=pl.ANY)],
            out_specs=rows(D),
            scratch_shapes=[pltpu.VMEM((TOP_K, CMB_TILE, W), U32), pltpu.VMEM((TOP_K, CMB_TILE, W), U32),
                            pltpu.SemaphoreType.DMA, pltpu.SemaphoreType.DMA]),
        out_shape=jax.ShapeDtypeStruct((T, D), F32),
        compiler_params=_cparams(("arbitrary",)),
        name="moe_combine",
    )(mod_of_tile, dest, dest, gate_t, up, xl, mods, sh_w13.astype(BF16), sh_w2.astype(BF16),
      ln_g.reshape(1, D), ln_b.reshape(1, D), ys)


def _moe_block(u, up, xl, mods, mod_of_tile, router_w, router_bias, w13, w2, layer, sh_w13, sh_w2, ln_g, ln_b, alpha):
    T, D = u.shape
    eidx, gate, rank, counts = _route(u, router_w, router_bias)
    cnt = counts.reshape(N_EXPERTS).astype(I32)
    padded = (cnt + MOE_BLK - 1) // MOE_BLK * MOE_BLK
    pad_end = jnp.cumsum(padded)
    pad_start = pad_end - padded
    nblk = T * TOP_K // MOE_BLK + N_EXPERTS
    n_used = (pad_end[-1] // MOE_BLK).astype(I32)
    blk_row0 = jnp.arange(nblk, dtype=I32) * MOE_BLK
    blk_exp = jnp.sum((pad_end[None, :] <= blk_row0[:, None]).astype(I32), axis=1)
    last_e = jnp.max(jnp.where(cnt > 0, jnp.arange(N_EXPERTS, dtype=I32), 0))
    blk_exp = jnp.minimum(blk_exp, last_e)
    dest = _slot_of(eidx, rank, pad_start)
    xs = _dispatch(up, dest, nblk * MOE_BLK, jnp.maximum(pad_end - MOE_BLK, 0), padded)
    ys = _expert_gemm(xs, blk_exp, n_used.reshape(1), cnt, w13, w2, layer)
    return _combine(dest, gate.T, up, xl, ys, mods, mod_of_tile, sh_w13, sh_w2, ln_g, ln_b, alpha)


def _in1_kernel(x_ref, mod_ref, wh_ref, wr_ref, ph_ref, pr_ref, *, nctx_tiles):
    sh1 = mod_ref[0:1, :]
    sc1 = mod_ref[1:2, :]
    h = (x_ref[...] * (1.0 + sc1) + sh1).astype(BF16)
    pr_ref[...] = jnp.dot(h, wr_ref[...], preferred_element_type=F32)

    @pl.when(pl.program_id(1) >= nctx_tiles)
    def _():
        ph_ref[...] = jnp.dot(h, wh_ref[...], preferred_element_type=F32)


RW_PW = 1920


def _rw_reorder(t):
    a = 3 * RW_W
    lo = 2 * RW_DECAY_LORA + RW_AAA_LORA
    out = jnp.concatenate([t[..., :a], t[..., a + lo:], t[..., a:a + lo]], axis=-1)
    pad = [(0, 0)] * (t.ndim - 1) + [(0, RW_PW - out.shape[-1])]
    return jnp.pad(out, pad)


def _in_proj1(xcat, mods, nctx_tiles, w_in):
    B, S, D = xcat.shape
    J = S // ROW_TILE
    T = B * S
    wh = w_in[:, :HY_WIDTH].astype(BF16)
    wr = _rw_reorder(w_in[:, HY_WIDTH:]).astype(BF16)

    def rows(w):
        return pl.BlockSpec((ROW_TILE, w), lambda b, j: (b * J + j, 0))

    def const(shape):
        return pl.BlockSpec(shape, lambda b, j: (0,) * len(shape))

    sd = jax.ShapeDtypeStruct
    JL = J - nctx_tiles
    return pl.pallas_call(
        functools.partial(_in1_kernel, nctx_tiles=nctx_tiles),
        grid=(B, J),
        in_specs=[rows(D), pl.BlockSpec((None, 6, D), lambda b, j: (jnp.where(j < nctx_tiles, B, b), 0, 0)),
                  const((D, HY_WIDTH)), const((D, RW_PW))],
        out_specs=[pl.BlockSpec((ROW_TILE, HY_WIDTH), lambda b, j: (b * JL + jnp.maximum(j - nctx_tiles, 0), 0)),
                   rows(RW_PW)],
        out_shape=[sd((B * JL * ROW_TILE, HY_WIDTH), F32), sd((T, RW_PW), F32)],
        compiler_params=_cparams(("parallel", "arbitrary")),
        name="in_proj1",
    )(xcat.reshape(T, D), mods, wh, wr)


def _rw_streams_kernel(prev_ref, main_ref, next_ref, mu_ref, w0_ref, w2_ref, a0_ref, a2_ref, g2_ref,
                       kk_ref, ka_ref, bd_ref,
                       r_ref, k_ref, v_ref, kkn_ref, bb_ref, ld_ref, g_ref, sh_scr, *, nctx_tiles, n_lat_tiles):
    j = pl.program_id(1)
    TM = main_ref.shape[0]
    W = main_ref.shape[1]
    H = GRID_W
    p = main_ref[...]
    ext = jnp.concatenate([prev_ref[...], p, next_ref[...]], axis=0)
    left = ext[H - 1:H - 1 + TM]
    right = ext[H + 1:H + 1 + TM]
    up = ext[0:TM]
    down = ext[2 * H:2 * H + TM]
    i = lax.broadcasted_iota(I32, (TM, W), 0)
    lane = lax.broadcasted_iota(I32, (TM, W), 1)
    even = (lane & 1) == 0
    c4 = lane & 3
    jl = j - nctx_tiles

    @pl.when(j < nctx_tiles)
    def _():
        lo = jnp.where(j == 0, 1, 0)
        hi = jnp.where(j == nctx_tiles - 1, TM - 1, TM)
        sh_scr[...] = jnp.where(even, jnp.where(i >= lo, left, 0.0), jnp.where(i < hi, right, 0.0))

    @pl.when(j >= nctx_tiles)
    def _():
        col = i & (H - 1)
        up_lo = jnp.where(jl == 0, H, 0)
        down_hi = jnp.where(jl == n_lat_tiles - 1, TM - H, TM)
        l_v = jnp.where(col != 0, left, 0.0)
        r_v = jnp.where(col != H - 1, right, 0.0)
        u_v = jnp.where(i >= up_lo, up, 0.0)
        d_v = jnp.where(i < down_hi, down, 0.0)
        sh_scr[...] = jnp.where(c4 == 0, l_v, jnp.where(c4 == 1, r_v, jnp.where(c4 == 2, u_v, d_v)))

    pm = p + mu_ref[...] * (sh_scr[...] - p)
    r = pm[:, 0:RW_W]
    k = pm[:, RW_W:2 * RW_W]
    v = pm[:, 2 * RW_W:3 * RW_W]
    o = 3 * RW_W
    gl = pm[:, o:o + RW_GATE_LORA]
    o += RW_GATE_LORA
    wl_f = pm[:, o:o + RW_DECAY_LORA]
    wl_b = pm[:, o + RW_DECAY_LORA:o + 2 * RW_DECAY_LORA]
    al = pm[:, o + 2 * RW_DECAY_LORA:o + 2 * RW_DECAY_LORA + RW_AAA_LORA]
    for d, wl in enumerate((wl_f, wl_b)):
        z = w0_ref[d:d + 1, :] + _bdot(jnp.tanh(wl), w2_ref[d])
        w = -(jnp.maximum(-z, 0.0) + jnp.log(1.0 + jnp.exp(-jnp.abs(z)))) - 0.5
        ld_ref[d] = -jnp.exp(w)
    a = jax.nn.sigmoid(a0_ref[...] + _bdot(al, a2_ref[...]))
    g_ref[...] = _bdot(jax.nn.sigmoid(gl), g2_ref[...])
    kk = k * kk_ref[...]
    n2 = _head_sums(kk * kk, bd_ref[...])
    kkn = kk / jnp.maximum(jnp.sqrt(n2), 1e-12)
    r_ref[...] = r
    k_ref[...] = k * (1.0 + (a - 1.0) * ka_ref[...])
    v_ref[...] = v
    kkn_ref[...] = kkn
    bb_ref[...] = kkn * a


def _head_sum_matrix(width, hd):
    i = np.arange(width)
    return jnp.asarray((i[:, None] // hd == i[None, :] // hd).astype(np.float32), BF16)


def _rw_streams(p_rw, B, S, nctx_tiles, mu, w0, w2, a0, a2, g2, k_k, k_a):
    T = B * S
    J = S // ROW_TILE
    HB = ROW_TILE // GRID_W
    NH = S // GRID_W

    def rows(w):
        return pl.BlockSpec((ROW_TILE, w), lambda b, j: (b * J + j, 0))

    def rows2(w):
        return pl.BlockSpec((2, ROW_TILE, w), lambda b, j: (0, b * J + j, 0))

    def const(shape):
        return pl.BlockSpec(shape, lambda b, j: (0,) * len(shape))

    sd = jax.ShapeDtypeStruct
    kern = functools.partial(_rw_streams_kernel, nctx_tiles=nctx_tiles, n_lat_tiles=J - nctx_tiles)
    return pl.pallas_call(
        kern,
        grid=(B, J),
        in_specs=[pl.BlockSpec((GRID_W, RW_PW), lambda b, j: (b * NH + jnp.maximum(j * HB - 1, 0), 0)),
                  rows(RW_PW),
                  pl.BlockSpec((GRID_W, RW_PW), lambda b, j: (b * NH + jnp.minimum((j + 1) * HB, NH - 1), 0)),
                  const((1, RW_PW)), const((2, RW_W)), const((2, RW_DECAY_LORA, RW_W)), const((1, RW_W)),
                  const((RW_AAA_LORA, RW_W)), const((RW_GATE_LORA, RW_W)), const((1, RW_W)), const((1, RW_W)),
                  const((RW_W, RW_W))],
        out_specs=[rows(RW_W), rows(RW_W), rows(RW_W), rows(RW_W), rows(RW_W), rows2(RW_W), rows(RW_W)],
        out_shape=[sd((T, RW_W), F32)] * 5 + [sd((2, T, RW_W), F32), sd((T, RW_W), F32)],
        scratch_shapes=[pltpu.VMEM((ROW_TILE, RW_PW), F32)],
        compiler_params=_cparams(("parallel", "parallel")),
        name="rwkv_streams",
    )(p_rw, p_rw, p_rw, _rw_reorder(mu).reshape(1, RW_PW), w0, w2, a0.reshape(1, RW_W), a2, g2,
      k_k.reshape(1, RW_W), k_a.reshape(1, RW_W), _head_sum_matrix(RW_W, RW_HEAD_DIM))


def _rwkv_kernel(r_ref, k_ref, v_ref, kk_ref, bb_ref, ld_ref, o_ref, st_ref, *, rev, nchunk):
    t = pl.program_id(1)

    @pl.when(t == 0)
    def _():
        st_ref[...] = jnp.zeros_like(st_ref)

    C = RW_CHUNK
    hd = RW_HEAD_DIM
    NH = RW_HEADS
    row = lax.broadcasted_iota(I32, (C, C), 0)
    col = lax.broadcasted_iota(I32, (C, C), 1)
    incl = ((col >= row) if rev else (col <= row))[None]
    strict = ((col > row) if rev else (col < row))[None]
    tri = jnp.where(incl[0], 1.0, 0.0).astype(BF16)
    eye = (row == col)[None]
    last_i = 0 if rev else C - 1
    order = range(nchunk - 1, -1, -1) if rev else range(nchunk)
    n_double = int(math.log2(C)) - 1

    def stack(x):
        return jnp.stack([x[c * C:(c + 1) * C, h * hd:(h + 1) * hd] for c in range(nchunk) for h in range(NH)], 0)

    ld = ld_ref[...]
    g = jnp.concatenate([_split_dot(tri, ld[c * C:(c + 1) * C]) for c in range(nchunk)], axis=0)
    g_last = jnp.concatenate([jnp.broadcast_to(g[c * C + last_i:c * C + last_i + 1], (C, NH * hd))
                              for c in range(nchunk)], axis=0)
    k = k_ref[...]
    bb = bb_ref[...]
    eng = jnp.exp(-g)
    e_end = jnp.exp(g_last - g)
    kk_t = stack(kk_ref[...] * jnp.exp(g - ld))
    b_t = stack(bb * eng)
    k_t = stack(k * eng)
    r_t = stack(r_ref[...] * jnp.exp(g))
    b_bar = stack(bb * e_end)
    k_bar = stack(k * e_end)
    dec = stack(jnp.exp(g_last))
    v = stack(v_ref[...])
    m_b = jnp.where(strict, _bmm_nt(kk_t, b_t), 0.0)
    m_k = jnp.where(strict, _bmm_nt(kk_t, k_t), 0.0)
    a_rb = jnp.where(incl, _bmm_nt(r_t, b_t), 0.0)
    a_rk = jnp.where(incl, _bmm_nt(r_t, k_t), 0.0)
    pw = -m_b
    tinv = jnp.where(eye, 1.0, 0.0) + pw
    for _ in range(n_double):
        pw = _bmm(pw, pw)
        tinv = tinv + _bmm(tinv, pw)
    a_t = _bmm(tinv, kk_t)
    u_b = _bmm(tinv, _bmm(m_k, v))
    q_h = r_t - _bmm(a_rb, a_t)
    o_h = _bmm(a_rk, v) - _bmm(a_rb, u_b)
    g_m = jnp.where(eye, dec, 0.0) - _bmm_tn(a_t, b_bar)
    h_m = _bmm_tn(v, k_bar) - _bmm_tn(u_b, b_bar)
    s = st_ref[...]
    for c in order:
        sl = slice(c * NH, (c + 1) * NH)
        o_c = _bmm_nt(q_h[sl], s) + o_h[sl]
        o_ref[c * C:(c + 1) * C, :] = jnp.concatenate([o_c[h] for h in range(NH)], axis=1)
        s = _bmm(s, g_m[sl]) + h_m[sl]
    st_ref[...] = s


def _rwkv_scan(r, k, v, kk, bb, ld, B, S, nctx_blocks, rev):
    T = B * S
    NB = S // ROW_TILE
    d = 1 if rev else 0
    nc = nctx_blocks

    def blk(t):
        if not rev:
            return t
        return jnp.where(t < nc, nc - 1 - t, NB - 1 - (t - nc))

    spec = pl.BlockSpec((ROW_TILE, RW_W), lambda b, t: (b * NB + blk(t), 0))
    kern = functools.partial(_rwkv_kernel, rev=rev, nchunk=ROW_TILE // RW_CHUNK)
    return pl.pallas_call(
        kern,
        grid=(B, NB),
        in_specs=[spec, spec, spec, spec, spec,
                  pl.BlockSpec((None, ROW_TILE, RW_W), lambda b, t: (d, b * NB + blk(t), 0))],
        out_specs=spec,
        out_shape=jax.ShapeDtypeStruct((T, RW_W), F32),
        scratch_shapes=[pltpu.VMEM((RW_HEADS, RW_HEAD_DIM, RW_HEAD_DIM), F32)],
        compiler_params=_cparams(("parallel", "arbitrary")),
        name="rwkv_scan_bwd" if rev else "rwkv_scan_fwd",
    )(r, k, v, kk, bb, ld)


def _filter_mlp_kernel(z_ref, w1_ref, b1_ref, w2_ref, b2_ref, w3_ref, sf_ref, win_ref, f_ref):
    h = jnp.sin(sf_ref[0:1, :] * (_hdot(z_ref[...], w1_ref[...]) + b1_ref[...]))
    h = jnp.sin(sf_ref[1:2, :] * (_hdot(h, w2_ref[...]) + b2_ref[...]))
    f_ref[...] = _hdot(h, w3_ref[...]) * win_ref[...]


def _hyena_filters(L, w1, b1, w2, b2, w3, sin_freq):
    t = np.linspace(0.0, 1.0, L, dtype=np.float32)[:, None]
    bands = (HY_EMB - 1) // 2
    wpos = (2.0 * math.pi * np.arange(L, dtype=np.float32)[:, None] / L).astype(np.float32)
    fr = np.linspace(1e-4, bands - 1, bands, dtype=np.float32)[None, :]
    z = np.concatenate([t, np.cos(fr * wpos), -np.sin(fr * wpos)], -1).astype(np.float32)
    max_decay = math.log(HY_TARGET) / HY_FAST_DECAY
    min_decay = math.log(HY_TARGET) / HY_SLOW_DECAY
    deltas = np.linspace(min_decay, max_decay, HY_CH, dtype=np.float32)
    window = np.exp(-t * np.abs(deltas)).astype(np.float32)
    dist = np.concatenate([np.arange(L), [0], np.arange(L - 1, 0, -1)])
    z2 = np.pad(z[dist], ((0, 0), (0, LANES - HY_EMB)))
    win2 = np.tile(window[dist], (1, HY_ORDER))
    win2[L] = 0.0
    w1p = jnp.pad(w1, ((0, LANES - HY_EMB), (0, 0)))
    Hd = w1.shape[1]
    FW = HY_ORDER * HY_CH
    w3s = jnp.transpose(w3.reshape(Hd, HY_ORDER, 2, HY_CH), (2, 0, 1, 3)).reshape(2, Hd, FW)
    TR = min(L, 512)
    nl = L // TR
    return pl.pallas_call(
        _filter_mlp_kernel,
        grid=(2 * nl,),
        in_specs=[pl.BlockSpec((TR, LANES), lambda i: (i, 0)),
                  pl.BlockSpec((LANES, Hd), lambda i: (0, 0)), pl.BlockSpec((1, Hd), lambda i: (0, 0)),
                  pl.BlockSpec((Hd, Hd), lambda i: (0, 0)), pl.BlockSpec((1, Hd), lambda i: (0, 0)),
                  pl.BlockSpec((None, Hd, FW), lambda i: (i // nl, 0, 0)), pl.BlockSpec((2, Hd), lambda i: (0, 0)),
                  pl.BlockSpec((TR, FW), lambda i: (i, 0))],
        out_specs=pl.BlockSpec((TR, FW), lambda i: (i, 0)),
        out_shape=jax.ShapeDtypeStruct((2 * L, FW), F32),
        compiler_params=_cparams(("parallel",)),
        name="hyena_filter_mlp",
    )(jnp.asarray(z2), w1p, b1.reshape(1, Hd), w2, b2.reshape(1, Hd), w3s, sin_freq, jnp.asarray(win2))


def _sconv_kernel(p_ref, w_ref, b_ref, o_ref):
    p = p_ref[...]
    L = p.shape[0]
    i = lax.broadcasted_iota(I32, p.shape, 0)
    prev = jnp.where(i == 0, 0.0, pltpu.roll(p, 1, 0))
    nxt = jnp.where(i == L - 1, 0.0, pltpu.roll(p, L - 1, 0))
    o_ref[...] = prev * w_ref[0:1, :] + p * w_ref[1:2, :] + nxt * w_ref[2:3, :] + b_ref[...]


def _short_conv(p_hy, B, S, Lc, conv_w, conv_b):
    L = S - Lc
    p3 = p_hy.reshape(B, L, HY_WIDTH)
    return pl.pallas_call(
        _sconv_kernel,
        grid=(B, HY_WIDTH // LANES),
        in_specs=[pl.BlockSpec((None, L, LANES), lambda b, c: (b, 0, c)),
                  pl.BlockSpec((HY_SHORT, LANES), lambda b, c: (0, c)),
                  pl.BlockSpec((1, LANES), lambda b, c: (0, c))],
        out_specs=pl.BlockSpec((None, L, LANES), lambda b, c: (b, 0, c)),
        out_shape=jax.ShapeDtypeStruct((B, L, HY_WIDTH), F32),
        compiler_params=_cparams(("parallel", "parallel")),
        name="hyena_short_conv",
    )(p3, conv_w, conv_b.reshape(1, HY_WIDTH))


def _dft_constants(L):
    n = 2 * L
    n2 = FFT_N2
    n1 = n // n2
    na = n1 // 2
    g8 = SUBLANES
    w1 = np.exp(-2j * np.pi * np.outer(np.arange(n1), np.arange(n1)) / n1)
    eye8 = np.eye(g8)

    def kron_fwd(w, real_in):
        k1n, an = w.shape
        blocks = np.stack([np.stack([w.real, -w.imag], 1), np.stack([w.imag, w.real], 1)], 1)
        if real_in:
            blocks = blocks[:, :, 0:1, :]
        m = np.einsum('kria,bc->krbiac', blocks, eye8)
        return m.reshape(k1n * 2 * g8, blocks.shape[2] * an * g8)

    m1 = kron_fwd(w1[:, :na], False)
    m1f = kron_fwd(w1, True)
    cw = np.conj(w1[:, :na]).T / n
    blocks = np.stack([np.stack([cw.real, -cw.imag], 1), np.stack([cw.imag, cw.real], 1)], 0)
    m1inv = np.einsum('iark,bc->iabkrc', blocks, eye8).reshape(2 * na * g8, n1 * 2 * g8)
    w2 = np.exp(-2j * np.pi * np.outer(np.arange(n2), np.arange(n2)) / n2)
    w2big = np.block([[w2.real, -w2.imag], [w2.imag, w2.real]])
    w2c = np.conj(w2)
    iw2big = np.block([[w2c.real, -w2c.imag], [w2c.imag, w2c.real]])
    tw = np.exp(-2j * np.pi * np.arange(n2) / n)

    def lanes(cols):
        z = np.concatenate([np.broadcast_to(col[:, None], (n2, LANES)) for col in cols], axis=1)
        return np.stack([z.real, z.imag], 0)

    tw0 = lanes([tw ** q for q in range(FFT_G)])
    twg = lanes([tw ** FFT_G] * FFT_G)
    c = lambda x, dt: jnp.asarray(np.ascontiguousarray(x), dt)
    return dict(m1=c(m1, BF16), m1f=c(m1f, BF16), m1inv=c(m1inv, BF16), w2=c(w2big, BF16), iw2=c(iw2big, BF16),
                tw0=c(tw0, F32), twg=c(twg, F32), n1=n1, na=na)


def _cplx_rows(z, twr, twi, conj):
    n2 = z.shape[0] // 2
    zr, zi = z[:n2], z[n2:]
    if conj:
        return jnp.concatenate([zr * twr + zi * twi, zi * twr - zr * twi], axis=0)
    return jnp.concatenate([zr * twr - zi * twi, zi * twr + zr * twi], axis=0)


def _split_dot(m, x):
    hi = x.astype(BF16)
    lo = (x - hi.astype(F32)).astype(BF16)
    return jnp.dot(m, hi, preferred_element_type=F32) + jnp.dot(m, lo, preferred_element_type=F32)


def _next_twiddle(tw, twg_ref):
    twr, twi = tw
    b_r, b_i = twg_ref[0], twg_ref[1]
    return twr * b_r - twi * b_i, twr * b_i + twi * b_r


def _load_group(scr, j, C):
    return jnp.concatenate([scr[j * FFT_G + q].reshape(2 * FFT_N2, C) for q in range(FFT_G)], axis=1)


def _filt_fft_kernel(f_ref, m1_ref, w2_ref, tw0_ref, twg_ref, o_ref, y_scr, *, n1):
    nbg = FFT_N2 // SUBLANES
    C = f_ref.shape[-1]
    for bg in range(nbg):
        yg = _split_dot(m1_ref[...], f_ref[:, bg].reshape(n1 * SUBLANES, C))
        y_scr[:, :, bg] = yg.reshape(n1, 2, SUBLANES, C)

    def body(j, tw):
        y = _cplx_rows(_load_group(y_scr, j, C), tw[0], tw[1], False)
        z = _split_dot(w2_ref[...], y)
        for q in range(FFT_G):
            o_ref[j * FFT_G + q] = z[:, q * C:(q + 1) * C]
        return _next_twiddle(tw, twg_ref)

    lax.fori_loop(0, n1 // FFT_G, body, (tw0_ref[0], tw0_ref[1]))


def _filter_spectrum(filt, L, dc):
    n1 = dc['n1']
    nbg = FFT_N2 // SUBLANES
    ncb = HY_CH // LANES
    full = filt.reshape(n1, nbg, SUBLANES, HY_ORDER * HY_CH)
    return pl.pallas_call(
        functools.partial(_filt_fft_kernel, n1=n1),
        grid=(HY_ORDER, ncb),
        in_specs=[pl.BlockSpec((n1, nbg, SUBLANES, LANES), lambda o, c: (0, 0, 0, o * ncb + c)),
                  pl.BlockSpec(dc['m1f'].shape, lambda o, c: (0, 0)),
                  pl.BlockSpec(dc['w2'].shape, lambda o, c: (0, 0)),
                  pl.BlockSpec(dc['tw0'].shape, lambda o, c: (0, 0, 0)),
                  pl.BlockSpec(dc['twg'].shape, lambda o, c: (0, 0, 0))],
        out_specs=pl.BlockSpec((None, n1, 2 * FFT_N2, LANES), lambda o, c: (o, 0, 0, c)),
        out_shape=jax.ShapeDtypeStruct((HY_ORDER, n1, 2 * FFT_N2, HY_CH), F32),
        scratch_shapes=[pltpu.VMEM((n1, 2, nbg, SUBLANES, LANES), F32)],
        compiler_params=_cparams(("parallel", "parallel")),
        name="hyena_filter_fft",
    )(full, dc['m1f'], dc['w2'], dc['tw0'], dc['twg'])


def _hyconv_kernel(za_ref, zb_ref, ga_ref, gb_ref, ff_ref, bias_ref, m1_ref, m1i_ref, w2_ref, iw2_ref,
                   tw0_ref, twg_ref, oa_ref, ob_ref, y_scr, *, n1, na):
    nbg = FFT_N2 // SUBLANES
    C = za_ref.shape[-1]
    half = na * SUBLANES
    for bg in range(nbg):
        xg = jnp.concatenate([za_ref[:, bg].reshape(half, C), zb_ref[:, bg].reshape(half, C)], axis=0)
        yg = jnp.dot(m1_ref[...], xg.astype(BF16), preferred_element_type=F32)
        y_scr[:, :, bg] = yg.reshape(n1, 2, SUBLANES, C)

    def body(j, tw):
        y = _cplx_rows(_load_group(y_scr, j, C), tw[0], tw[1], False)
        z = jnp.dot(w2_ref[...], y.astype(BF16), preferred_element_type=F32)
        f = _load_group(ff_ref, j, C)
        zr, zi = z[:FFT_N2], z[FFT_N2:]
        fr, fi = f[:FFT_N2], f[FFT_N2:]
        p = jnp.concatenate([zr * fr - zi * fi, zr * fi + zi * fr], axis=0)
        v = jnp.dot(iw2_ref[...], p.astype(BF16), preferred_element_type=F32)
        v = _cplx_rows(v, tw[0], tw[1], True)
        for q in range(FFT_G):
            y_scr[j * FFT_G + q] = v[:, q * C:(q + 1) * C].reshape(2, nbg, SUBLANES, C)
        return _next_twiddle(tw, twg_ref)

    lax.fori_loop(0, n1 // FFT_G, body, (tw0_ref[0], tw0_ref[1]))
    bias = bias_ref[...]
    for bg in range(nbg):
        vg = y_scr[:, :, bg].reshape(n1 * 2 * SUBLANES, C)
        out = jnp.dot(m1i_ref[...], vg.astype(BF16), preferred_element_type=F32)
        ya = out[:half].reshape(na, SUBLANES, C)
        yb = out[half:].reshape(na, SUBLANES, C)
        za = za_ref[:, bg]
        zb = zb_ref[:, bg]
        oa_ref[:, bg] = ga_ref[:, bg] * (ya + za * bias)
        ob_ref[:, bg] = gb_ref[:, bg] * (yb + zb * bias)


def _hyena_conv(z, z_col0, gates, g_col0, spec, bias, dc, B, L):
    n1, na = dc['n1'], dc['na']
    nbg = FFT_N2 // SUBLANES
    NCB = HY_CH // LANES

    def view(a):
        return a.reshape(B, na, nbg, SUBLANES, a.shape[-1])

    def seq(col0, which):
        return pl.BlockSpec((None, na, nbg, SUBLANES, LANES),
                            lambda c, p: (2 * p + which, 0, 0, 0, col0 + c))

    def const(a):
        nd = a.ndim
        return pl.BlockSpec(a.shape, lambda c, p: (0,) * nd)

    out_a, out_b = pl.pallas_call(
        functools.partial(_hyconv_kernel, n1=n1, na=na),
        grid=(NCB, B // 2),
        in_specs=[seq(z_col0, 0), seq(z_col0, 1), seq(g_col0, 0), seq(g_col0, 1),
                  pl.BlockSpec((n1, 2 * FFT_N2, LANES), lambda c, p: (0, 0, c), pipeline_mode=pl.Buffered(1)),
                  pl.BlockSpec((1, LANES), lambda c, p: (0, c)),
                  const(dc['m1']), const(dc['m1inv']), const(dc['w2']), const(dc['iw2']),
                  const(dc['tw0']), const(dc['twg'])],
        out_specs=[pl.BlockSpec((None, na, nbg, SUBLANES, LANES), lambda c, p: (p, 0, 0, 0, c)),
                   pl.BlockSpec((None, na, nbg, SUBLANES, LANES), lambda c, p: (p, 0, 0, 0, c))],
        out_shape=[jax.ShapeDtypeStruct((B // 2, na, nbg, SUBLANES, HY_CH), F32)] * 2,
        scratch_shapes=[pltpu.VMEM((n1, 2, nbg, SUBLANES, LANES), F32)],
        compiler_params=_cparams(("parallel", "arbitrary")),
        name="hyena_long_conv",
    )(view(z), view(z), view(gates), view(gates), spec, bias.reshape(1, HY_CH),
      dc['m1'], dc['m1inv'], dc['w2'], dc['iw2'], dc['tw0'], dc['twg'])
    out = jnp.stack([out_a, out_b], axis=1)
    return out.reshape(B, L, HY_CH)


def _out1_kernel(hy_ref, of_ref, ob_ref, r_ref, k_ref, v_ref, g_ref, x_ref, mod_ref, wo_ref,
                 rk_ref, lg_ref, lb_ref, bd_ref, lng_ref, lnb_ref, xl_ref, u_ref, up_ref, *, alpha):
    o = of_ref[...] + ob_ref[...]
    bd = bd_ref[...]
    inv = 1.0 / RW_HEAD_DIM
    mu = _head_sums(o, bd) * inv
    oc = o - mu
    var = _head_sums(oc * oc, bd) * inv
    on = oc * lax.rsqrt(var + RW_GN_EPS) * lg_ref[...] + lb_ref[...]
    bonus = _head_sums(r_ref[...] * k_ref[...] * rk_ref[...], bd) * v_ref[...]
    y = jnp.concatenate([hy_ref[...], (on + bonus) * g_ref[...]], axis=1)
    _post_mix(y, x_ref[...], mod_ref, wo_ref, lng_ref, lnb_ref, alpha, xl_ref, u_ref, up_ref)


def _readout1(hy, o_f, o_b, r, k, v, g, xcat, mods, B, S, Lc, w_out, r_k, ln_g, ln_b, dn_g, dn_b, alpha):
    D = xcat.shape[-1]
    L = S - Lc
    J = S // ROW_TILE
    JL = L // ROW_TILE
    JC = Lc // ROW_TILE

    def cat_rows(w):
        return pl.BlockSpec((ROW_TILE, w), lambda b, j: (b * J + JC + j, 0))

    def lat_rows(w):
        return pl.BlockSpec((ROW_TILE, w), lambda b, j: (b * JL + j, 0))

    def const(shape):
        return pl.BlockSpec(shape, lambda b, j: (0,) * len(shape))

    sd = jax.ShapeDtypeStruct
    return pl.pallas_call(
        functools.partial(_out1_kernel, alpha=alpha),
        grid=(B, JL),
        in_specs=[lat_rows(HY_CH), cat_rows(RW_W), cat_rows(RW_W), cat_rows(RW_W), cat_rows(RW_W), cat_rows(RW_W),
                  cat_rows(RW_W), cat_rows(D),
                  pl.BlockSpec((None, 6, D), lambda b, j: (b, 0, 0)),
                  const((HY_CH + RW_W, D)), const((1, RW_W)), const((1, RW_W)), const((1, RW_W)),
                  const((RW_W, RW_W)), const((1, D)), const((1, D))],
        out_specs=[lat_rows(D), lat_rows(D), lat_rows(D // 2)],
        out_shape=[sd((B * L, D), F32), sd((B * L, D), F32), sd((B * L, D // 2), U32)],
        compiler_params=_cparams(("parallel", "parallel")),
        name="readout1",
    )(hy, o_f, o_b, r, k, v, g, xcat.reshape(B * S, D), mods, w_out.astype(BF16),
      r_k.reshape(1, RW_W), ln_g.reshape(1, RW_W), ln_b.reshape(1, RW_W),
      _head_sum_matrix(RW_W, RW_HEAD_DIM), dn_g.reshape(1, D), dn_b.reshape(1, D))


def kernel(x, c, ctx, c_ctx, mod_w, mod_b, ln1_g, ln1_b, ln2_g, ln2_b, ev_w_in, ev_w_out, gla_gate_w2, gla_gate_b, gla_norm_g, hg_lb_logits, hg_norm_g, od_w_in, od_w_out, hy_conv_w, hy_conv_b, hy_ffn_w1, hy_ffn_b1, hy_ffn_w2, hy_ffn_b2, hy_ffn_w3, hy_sin_freq, hy_bias, rw_mu, rw_w0, rw_w2, rw_a0, rw_a2, rw_g2, rw_k_k, rw_k_a, rw_r_k, rw_ln_g, rw_ln_b, router_w, router_bias, exp_w13, exp_w2, sh_w13, sh_w2):
    B, L, D = x.shape
    Lc = ctx.shape[1]
    S = Lc + L
    depth = mod_w.shape[0]
    assert depth == 2 and L % ROW_TILE == 0 and Lc % ROW_TILE == 0 and B % 2 == 0
    assert L % (FFT_N2 * 2) == 0 and L % GRID_W == 0
    alpha = (2 * depth) ** 0.25
    nctx = Lc // ROW_TILE
    J = S // ROW_TILE
    T = B * S

    cc = jnp.concatenate([c, c_ctx[None, :]], axis=0)
    cc = jnp.pad(cc, ((0, (-cc.shape[0]) % SUBLANES), (0, 0)))
    hg_lb = jnp.cumsum(jax.nn.softmax(hg_lb_logits.astype(F32), axis=0), axis=0)
    xcat = jnp.concatenate([ctx, x], axis=1)

    mods = _modulation(cc, mod_w[0], mod_b[0])
    gq, gk, gv, gla, r, hq, hk, hla, hv, hgate = _in_proj0(xcat, mods, nctx, ev_w_in[0], gla_gate_w2[0],
                                                            gla_gate_b[0], hg_lb[0])
    gk3 = gk.reshape(1, T, GLA_QK)
    o_gf = _gated_recurrence(gq, gk3, gv, gla, 0, B, S, nctx, False, GLA_HEADS, GLA_DK, GLA_DV)
    o_gb = _gated_recurrence(gq, gk3, gv, gla, 0, B, S, nctx, True, GLA_HEADS, GLA_DK, GLA_DV)
    o_hf = _gated_recurrence(hq, hk, hv, hla, 0, B, S, nctx, False, HG_HEADS, HG_EXPAND, HG_EXPAND)
    o_hb = _gated_recurrence(hq, hk, hv, hla, 1, B, S, nctx, True, HG_HEADS, HG_EXPAND, HG_EXPAND)
    xl, u, up = _readout0(o_gf, o_gb, o_hf, o_hb, r, hgate, xcat, mods, nctx, ev_w_out[0], gla_norm_g[0],
                          hg_norm_g[0], ln1_g[0], ln1_b[0], alpha)
    cmb_rows = 2 * CMB_TILE
    tiles_per_b = S // cmb_rows
    tile_in_b = jnp.arange(B * tiles_per_b, dtype=I32) % tiles_per_b
    mod_of_tile = jnp.where(tile_in_b < Lc // cmb_rows, B, jnp.arange(B * tiles_per_b, dtype=I32) // tiles_per_b)
    x1 = _moe_block(u, up, xl, mods, mod_of_tile.astype(I32), router_w[0], router_bias[0], exp_w13, exp_w2, 0,
                    sh_w13[0], sh_w2[0], ln2_g[0], ln2_b[0], alpha)
    xcat = x1.reshape(B, S, D)

    mods = _modulation(cc, mod_w[1], mod_b[1])
    p_hy, p_rw = _in_proj1(xcat, mods, nctx, od_w_in[0])
    rr, rk, rv, rkk, rbb, rld, rg = _rw_streams(p_rw, B, S, nctx, rw_mu[0], rw_w0[0], rw_w2[0], rw_a0[0],
                                                 rw_a2[0], rw_g2[0], rw_k_k[0], rw_k_a[0])
    o_f = _rwkv_scan(rr, rk, rv, rkk, rbb, rld, B, S, nctx, False)
    o_b = _rwkv_scan(rr, rk, rv, rkk, rbb, rld, B, S, nctx, True)
    dc = _dft_constants(L)
    filt = _hyena_filters(L, hy_ffn_w1[0], hy_ffn_b1[0], hy_ffn_w2[0], hy_ffn_b2[0], hy_ffn_w3[0], hy_sin_freq[0])
    spec = _filter_spectrum(filt, L, dc)
    uu = _short_conv(p_hy, B, S, Lc, hy_conv_w[0], hy_conv_b[0])
    NCB = HY_CH // LANES
    z1 = _hyena_conv(uu, 0, uu, NCB, spec[0], hy_bias[0, 0], dc, B, L)
    z2 = _hyena_conv(z1, 0, uu, 2 * NCB, spec[1], hy_bias[0, 1], dc, B, L)
    xl, u, up = _readout1(z2.reshape(B * L, HY_CH), o_f, o_b, rr, rk, rv, rg, xcat, mods, B, S, Lc, od_w_out[0],
                          rw_r_k[0], rw_ln_g[0], rw_ln_b[0], ln1_g[1], ln1_b[1], alpha)
    mod_of_tile = (jnp.arange(B * (L // cmb_rows), dtype=I32) // (L // cmb_rows)).astype(I32)
    out = _moe_block(u, up, xl, mods, mod_of_tile, router_w[1], router_bias[1], exp_w13, exp_w2, 1,
                     sh_w13[1], sh_w2[1], ln2_g[1], ln2_b[1], alpha)
    return out.reshape(B, L, D)
```

```python
import functools
import math

import numpy as np
import jax
import jax.numpy as jnp
from jax import lax
from jax.experimental import pallas as pl
from jax.experimental.pallas import tpu as pltpu

F32 = jnp.float32
BF16 = jnp.bfloat16
I32 = jnp.int32
U32 = jnp.uint32
HI = lax.Precision.HIGHEST

LN_EPS = 1e-5
GLA_HEADS, GLA_DK, GLA_DV = 4, 64, 128
GLA_QK, GLA_V = GLA_HEADS * GLA_DK, GLA_HEADS * GLA_DV
GLA_GATE_RANK = 16
GLA_GATE_NORM = 16.0
CHUNK = 64
HG_HEADS, HG_EXPAND = 4, 128
HG_W = HG_HEADS * HG_EXPAND
HY_CH, HY_ORDER, HY_SHORT, HY_EMB = 512, 2, 3, 33
HY_WIDTH = (HY_ORDER + 1) * HY_CH
HY_FAST_DECAY, HY_SLOW_DECAY, HY_TARGET = 0.3, 1.5, 1e-2
RW_HEADS, RW_HEAD_DIM = 8, 64
RW_W = RW_HEADS * RW_HEAD_DIM
RW_DECAY_LORA, RW_AAA_LORA, RW_GATE_LORA = 64, 64, 128
RW_GN_EPS = 64e-5
GRID_W = 64
N_EXPERTS, TOP_K, N_GROUPS, TOPK_GROUPS = 256, 8, 8, 4
PER_GROUP = N_EXPERTS // N_GROUPS
EXPERT_FF, SHARED_FF = 256, 256
ROUTED_SCALE = 2.5

LANES = 128
SUBLANES = 8
VMEM_LIMIT = 56 * 1024 * 1024
ROW_TILE = 256
MOE_BLK = 256
CMB_TILE = 128
RW_CHUNK = 64
FFT_N2 = 128
FFT_G = 4


def _cparams(sem):
    return pltpu.CompilerParams(dimension_semantics=sem, vmem_limit_bytes=VMEM_LIMIT)


def _bdot(a, b):
    return jnp.dot(a.astype(BF16), b.astype(BF16), preferred_element_type=F32)


def _bdot_nt(a, b):
    return lax.dot_general(a.astype(BF16), b.astype(BF16), (((1,), (1,)), ((), ())),
                           preferred_element_type=F32)


def _bdot_tn(a, b):
    return lax.dot_general(a.astype(BF16), b.astype(BF16), (((0,), (0,)), ((), ())),
                           preferred_element_type=F32)


def _bmm(a, b):
    return jnp.einsum('nij,njk->nik', a.astype(BF16), b.astype(BF16), preferred_element_type=F32)


def _bmm_nt(a, b):
    return jnp.einsum('nik,njk->nij', a.astype(BF16), b.astype(BF16), preferred_element_type=F32)


def _bmm_tn(a, b):
    return _bmm(jnp.swapaxes(a, 1, 2), b)


def _hdot(a, b):
    return jnp.dot(a, b, precision=HI, preferred_element_type=F32)


def _head_sums(x, ones_bd):
    hi = x.astype(BF16)
    lo = (x - hi.astype(F32)).astype(BF16)
    return (jnp.dot(hi, ones_bd, preferred_element_type=F32) + jnp.dot(lo, ones_bd, preferred_element_type=F32))


def _silu(x):
    return x * jax.nn.sigmoid(x)


def _layer_norm_rows(x, g, b, eps):
    mu = jnp.mean(x, axis=-1, keepdims=True)
    xc = x - mu
    var = jnp.mean(xc * xc, axis=-1, keepdims=True)
    return xc * lax.rsqrt(var + eps) * g + b


def _mod_kernel(c_ref, w_ref, b_ref, o_ref):
    o_ref[...] = _hdot(_silu(c_ref[...]), w_ref[...]) + b_ref[...]


def _modulation(cc, w, b):
    R, D = cc.shape
    out = pl.pallas_call(
        _mod_kernel,
        grid=(6,),
        in_specs=[pl.BlockSpec((R, D), lambda j: (0, 0)),
                  pl.BlockSpec((D, D), lambda j: (0, j)),
                  pl.BlockSpec((1, D), lambda j: (0, j))],
        out_specs=pl.BlockSpec((R, D), lambda j: (0, j)),
        out_shape=jax.ShapeDtypeStruct((R, 6 * D), F32),
        compiler_params=_cparams(("parallel",)),
        name="modulation",
    )(cc, w, b.reshape(1, 6 * D))
    return out.reshape(R, 6, D)


def _in0_kernel(x_ref, mod_ref, w_ref, wa_ref, w2_ref, gb_ref, lb_ref,
                gq_ref, gk_ref, gv_ref, gla_ref, r_ref, hq_ref, hk_ref, hla_ref, hv_ref, hg_ref):
    sh1 = mod_ref[0:1, :]
    sc1 = mod_ref[1:2, :]
    h = (x_ref[...] * (1.0 + sc1) + sh1).astype(BF16)

    def proj(off, width):
        return jnp.dot(h, w_ref[:, off:off + width], preferred_element_type=F32)

    gq_ref[...] = (proj(0, GLA_QK) * (GLA_DK ** -0.5)).astype(BF16)
    gk_ref[...] = proj(GLA_QK, GLA_QK).astype(BF16)
    gv_ref[...] = proj(2 * GLA_QK, GLA_V).astype(BF16)
    r_ref[...] = proj(2 * GLA_QK + GLA_V, GLA_V).astype(BF16)
    base = 2 * GLA_QK + 2 * GLA_V
    a = jnp.dot(h, wa_ref[...], preferred_element_type=F32)
    z = _bdot(a, w2_ref[...]) + gb_ref[...]
    ls = (jnp.minimum(z, 0.0) - jnp.log(1.0 + jnp.exp(-jnp.abs(z)))) * (1.0 / GLA_GATE_NORM)
    gla_ref[0] = ls[:, :GLA_QK]
    gla_ref[1] = ls[:, GLA_QK:]
    hq_ref[...] = _silu(proj(base, HG_W)).astype(BF16)
    for d in range(2):
        zf = proj(base + (1 + d) * HG_W, HG_W)
        lb = lb_ref[d:d + 1, :]
        f = lb + (1.0 - lb) * jax.nn.sigmoid(zf)
        hk_ref[d] = (1.0 - f).astype(BF16)
        hla_ref[d] = jnp.log(f)
    hv_ref[...] = proj(base + 3 * HG_W, HG_W).astype(BF16)
    hg_ref[...] = proj(base + 4 * HG_W, HG_W).astype(BF16)


def _in_proj0(xcat, mods, nctx_tiles, w_in, gate_w2, gate_b, lb):
    B, S, D = xcat.shape
    J = S // ROW_TILE
    T = B * S
    a_off = 2 * GLA_QK + 2 * GLA_V
    wmain = jnp.concatenate([w_in[:, :a_off], w_in[:, a_off + 2 * GLA_GATE_RANK:]], axis=1).astype(BF16)
    wa = jnp.pad(w_in[:, a_off:a_off + 2 * GLA_GATE_RANK], ((0, 0), (0, LANES - 2 * GLA_GATE_RANK))).astype(BF16)
    w2 = jnp.zeros((LANES, 2 * GLA_QK), F32)
    w2 = w2.at[:GLA_GATE_RANK, :GLA_QK].set(gate_w2[0]).at[GLA_GATE_RANK:2 * GLA_GATE_RANK, GLA_QK:].set(gate_w2[1])
    gb = gate_b.reshape(1, 2 * GLA_QK)
    WM = wmain.shape[1]

    def rows(w):
        return pl.BlockSpec((ROW_TILE, w), lambda b, j: (b * J + j, 0))

    def rows2(w):
        return pl.BlockSpec((2, ROW_TILE, w), lambda b, j: (0, b * J + j, 0))

    def const(shape):
        return pl.BlockSpec(shape, lambda b, j: (0,) * len(shape))

    sd = jax.ShapeDtypeStruct
    outs = pl.pallas_call(
        _in0_kernel,
        grid=(B, J),
        in_specs=[rows(D),
                  pl.BlockSpec((None, 6, D), lambda b, j: (jnp.where(j < nctx_tiles, B, b), 0, 0)),
                  const((D, WM)), const((D, LANES)), const((LANES, 2 * GLA_QK)), const((1, 2 * GLA_QK)),
                  const((2, HG_W))],
        out_specs=[rows(GLA_QK), rows(GLA_QK), rows(GLA_V), rows2(GLA_QK), rows(GLA_V),
                   rows(HG_W), rows2(HG_W), rows2(HG_W), rows(HG_W), rows(HG_W)],
        out_shape=[sd((T, GLA_QK), BF16), sd((T, GLA_QK), BF16), sd((T, GLA_V), BF16), sd((2, T, GLA_QK), F32),
                   sd((T, GLA_V), BF16), sd((T, HG_W), BF16), sd((2, T, HG_W), BF16), sd((2, T, HG_W), F32),
                   sd((T, HG_W), BF16), sd((T, HG_W), BF16)],
        compiler_params=_cparams(("parallel", "parallel")),
        name="in_proj0",
    )(xcat.reshape(T, D), mods, wmain, wa, w2, gb, lb)
    return outs


def _rec_kernel(q_ref, k_ref, v_ref, la_ref, o_ref, st_ref, *, rev, nh, dk, dv, nchunk):
    t = pl.program_id(1)

    @pl.when(t == 0)
    def _():
        st_ref[...] = jnp.zeros_like(st_ref)

    C = CHUNK
    row = lax.broadcasted_iota(I32, (C, C), 0)
    col = lax.broadcasted_iota(I32, (C, C), 1)
    incl = (col >= row) if rev else (col <= row)
    tri = jnp.where(incl, 1.0, 0.0).astype(BF16)
    ref_i = C // 2 - 1 if rev else C // 2
    last_i = 0 if rev else C - 1
    order = range(nchunk - 1, -1, -1) if rev else range(nchunk)

    def stack(x, hd):
        return jnp.stack([x[c * C:(c + 1) * C, h * hd:(h + 1) * hd] for c in range(nchunk) for h in range(nh)], 0)

    def rows(x, i):
        return jnp.concatenate([jnp.broadcast_to(x[c * C + i:c * C + i + 1], (C, x.shape[1]))
                                for c in range(nchunk)], axis=0)

    la = la_ref[...]
    q = q_ref[...].astype(F32)
    k = k_ref[...].astype(F32)
    b = jnp.concatenate([_split_dot(tri, la[c * C:(c + 1) * C]) for c in range(nchunk)], axis=0)
    b_mid = rows(b, ref_i)
    b_last = rows(b, last_i)
    v = stack(v_ref[...].astype(F32), dv)
    sc = _bmm_nt(stack(q * jnp.exp(b - b_mid), dk), stack(k * jnp.exp(b_mid - b), dk))
    o_intra = _bmm(jnp.where(incl[None], sc, 0.0), v)
    q_in = stack(q * jnp.exp(b), dk)
    kv_t = _bmm_tn(v, stack(k * jnp.exp(b_last - b), dk))
    dec = stack(jnp.exp(b_last), dk)[:, 0:1, :]
    s_t = st_ref[...]
    for c in order:
        sl = slice(c * nh, (c + 1) * nh)
        o_c = o_intra[sl] + _bmm_nt(q_in[sl], s_t)
        o_ref[c * C:(c + 1) * C, :] = jnp.concatenate([o_c[h] for h in range(nh)], axis=1)
        s_t = s_t * dec[sl] + kv_t[sl]
    st_ref[...] = s_t


def _gated_recurrence(q, k, v, la, kdir, B, S, nctx_blocks, rev, nh, dk, dv):
    T = B * S
    NB = S // ROW_TILE
    wk, wv = nh * dk, nh * dv
    assert q.shape[1] == wk
    d = 1 if rev else 0
    nc = nctx_blocks

    def blk(t):
        if not rev:
            return t
        return jnp.where(t < nc, nc - 1 - t, NB - 1 - (t - nc))

    kern = functools.partial(_rec_kernel, rev=rev, nh=nh, dk=dk, dv=dv, nchunk=ROW_TILE // CHUNK)
    return pl.pallas_call(
        kern,
        grid=(B, NB),
        in_specs=[pl.BlockSpec((ROW_TILE, wk), lambda b, t: (b * NB + blk(t), 0)),
                  pl.BlockSpec((None, ROW_TILE, wk), lambda b, t: (kdir, b * NB + blk(t), 0)),
                  pl.BlockSpec((ROW_TILE, wv), lambda b, t: (b * NB + blk(t), 0)),
                  pl.BlockSpec((None, ROW_TILE, wk), lambda b, t: (d, b * NB + blk(t), 0))],
        out_specs=pl.BlockSpec((ROW_TILE, wv), lambda b, t: (b * NB + blk(t), 0)),
        out_shape=jax.ShapeDtypeStruct((T, wv), F32),
        scratch_shapes=[pltpu.VMEM((nh, dv, dk), F32)],
        compiler_params=_cparams(("parallel", "arbitrary")),
        name="gated_rec_bwd" if rev else "gated_rec_fwd",
    )(q, k, v, la)


def _pack_rows(x):
    w = x.shape[1] // 2
    lo = lax.bitcast_convert_type(x[:, :w].astype(BF16).astype(F32), U32)
    hi = lax.bitcast_convert_type(x[:, w:].astype(BF16).astype(F32), U32)
    return (lo >> 16) | (hi & jnp.uint32(0xFFFF0000))


def _unpack_rows(p):
    lo = lax.bitcast_convert_type(p << 16, F32)
    hi = lax.bitcast_convert_type(p & jnp.uint32(0xFFFF0000), F32)
    return lo, hi


ROW_SUB = 4


def _store_row_slabs(ref, val, r0=0):
    m = val.shape[0]
    for j in range(ROW_SUB):
        ref[r0:r0 + m, j, :] = val[:, j * LANES:(j + 1) * LANES]


def _load_row_slabs(ref, r0=0, m=None):
    m = ref.shape[0] - r0 if m is None else m
    return jnp.concatenate([ref[r0:r0 + m, j, :] for j in range(ROW_SUB)], axis=1)


def _post_mix(y, x, mod_ref, wo_ref, lng_ref, lnb_ref, alpha, xl_ref, u_ref, up_ref):
    g1 = mod_ref[2:3, :]
    sh2 = mod_ref[3:4, :]
    sc2 = mod_ref[4:5, :]
    yo = jnp.dot(y.astype(BF16), wo_ref[...], preferred_element_type=F32)
    xl = _layer_norm_rows(alpha * x + g1 * yo, lng_ref[...], lnb_ref[...], LN_EPS)
    xl_ref[...] = xl
    u = xl * (1.0 + sc2) + sh2
    u_ref[...] = u
    _store_row_slabs(up_ref, _pack_rows(u))


def _out0_kernel(gf_ref, gb_ref, hf_ref, hb_ref, r_ref, hg_ref, x_ref, mod_ref, wo_ref,
                 gg_ref, hgg_ref, lng_ref, lnb_ref, xl_ref, u_ref, up_ref, *, alpha):
    def heads(o, g, gate):
        parts = []
        for hh in range(o.shape[1] // LANES):
            seg = o[:, hh * LANES:(hh + 1) * LANES]
            ms = jnp.mean(seg * seg, axis=-1, keepdims=True)
            parts.append(seg * lax.rsqrt(ms + 1e-6) * g)
        return jnp.concatenate(parts, axis=1) * _silu(gate.astype(F32))

    y = jnp.concatenate([heads(gf_ref[...] + gb_ref[...], gg_ref[...], r_ref[...]),
                         heads(hf_ref[...] + hb_ref[...], hgg_ref[...], hg_ref[...])], axis=1)
    _post_mix(y, x_ref[...], mod_ref, wo_ref, lng_ref, lnb_ref, alpha, xl_ref, u_ref, up_ref)


def _readout0(o_gf, o_gb, o_hf, o_hb, r, hgate, xcat, mods, nctx_tiles, w_out, gla_g, hg_g, ln_g, ln_b, alpha):
    B, S, D = xcat.shape
    J = S // ROW_TILE
    T = B * S

    def rows(w):
        return pl.BlockSpec((ROW_TILE, w), lambda b, j: (b * J + j, 0))

    def const(shape):
        return pl.BlockSpec(shape, lambda b, j: (0,) * len(shape))

    sd = jax.ShapeDtypeStruct
    return pl.pallas_call(
        functools.partial(_out0_kernel, alpha=alpha),
        grid=(B, J),
        in_specs=[rows(GLA_V), rows(GLA_V), rows(HG_W), rows(HG_W), rows(GLA_V), rows(HG_W), rows(D),
                  pl.BlockSpec((None, 6, D), lambda b, j: (jnp.where(j < nctx_tiles, B, b), 0, 0)),
                  const((GLA_V + HG_W, D)), const((1, GLA_DV)), const((1, HG_EXPAND)), const((1, D)), const((1, D))],
        out_specs=[rows(D), rows(D), pl.BlockSpec((ROW_TILE, ROW_SUB, LANES), lambda b, j: (b * J + j, 0, 0))],
        out_shape=[sd((T, D), F32), sd((T, D), F32), sd((T, ROW_SUB, LANES), U32)],
        compiler_params=_cparams(("parallel", "parallel")),
        name="readout0",
    )(o_gf, o_gb, o_hf, o_hb, r, hgate, xcat.reshape(T, D), mods, w_out.astype(BF16),
      gla_g.reshape(1, -1), hg_g.reshape(1, -1), ln_g.reshape(1, D), ln_b.reshape(1, D))


def _route_kernel(u_ref, rwh_ref, rwl_ref, rb_ref, tri_ref, eidx_ref, gate_ref, rank_ref, cnt_ref, carry_ref):
    i = pl.program_id(0)

    @pl.when(i == 0)
    def _():
        carry_ref[...] = jnp.zeros_like(carry_ref)

    u = u_ref[...]
    TM = u.shape[0]
    E = N_EXPERTS
    u_hi = u.astype(BF16)
    u_lo = (u - u_hi.astype(F32)).astype(BF16)
    nt = (((1,), (1,)), ((), ()))
    logits = (lax.dot_general(rwh_ref[...], u_hi, nt, preferred_element_type=F32)
              + lax.dot_general(rwh_ref[...], u_lo, nt, preferred_element_type=F32)
              + lax.dot_general(rwl_ref[...], u_hi, nt, preferred_element_type=F32))
    scores = jax.nn.sigmoid(logits)
    sel = scores + rb_ref[...]
    neg = -jnp.inf
    sel3 = sel.reshape(N_GROUPS, PER_GROUP, TM)
    io_g = lax.broadcasted_iota(I32, sel3.shape, 1)
    m1 = jnp.max(sel3, axis=1, keepdims=True)
    i1 = jnp.min(jnp.where(sel3 == m1, io_g, PER_GROUP), axis=1, keepdims=True)
    m2 = jnp.max(jnp.where(io_g == i1, neg, sel3), axis=1, keepdims=True)
    grp = m1 + m2
    io_n = lax.broadcasted_iota(I32, grp.shape, 0)
    keep = jnp.zeros(grp.shape, jnp.bool_)
    for _ in range(TOPK_GROUPS):
        m = jnp.max(grp, axis=0, keepdims=True)
        idx = jnp.min(jnp.where(grp == m, io_n, N_GROUPS), axis=0, keepdims=True)
        hit = io_n == idx
        keep = jnp.logical_or(keep, hit)
        grp = jnp.where(hit, neg, grp)
    sel = jnp.where(keep, sel3, neg).reshape(E, TM)
    io_e = lax.broadcasted_iota(I32, (E, TM), 0)
    base = carry_ref[...]
    tri = tri_ref[...]
    e_rows, g_rows, r_rows = [], [], []
    for _ in range(TOP_K):
        m = jnp.max(sel, axis=0, keepdims=True)
        idx = jnp.min(jnp.where(sel == m, io_e, E), axis=0, keepdims=True)
        hit = io_e == idx
        hit_f = jnp.where(hit, 1.0, 0.0)
        g_rows.append(jnp.sum(jnp.where(hit, scores, 0.0), axis=0, keepdims=True))
        prefix = jnp.dot(hit_f.astype(BF16), tri, preferred_element_type=F32)
        r_rows.append(jnp.sum(jnp.where(hit, prefix + base, 0.0), axis=0, keepdims=True))
        base = base + jnp.sum(hit_f, axis=1, keepdims=True)
        e_rows.append(idx)
        sel = jnp.where(hit, neg, sel)
    carry_ref[...] = base
    cnt_ref[...] = base
    g = jnp.concatenate(g_rows, axis=0)
    gate_ref[...] = g / jnp.sum(g, axis=0, keepdims=True) * ROUTED_SCALE
    eidx_ref[...] = jnp.concatenate(e_rows, axis=0)
    rank_ref[...] = jnp.concatenate(r_rows, axis=0).astype(I32)


def _route(u, router_w, router_bias):
    T, D = u.shape
    E = N_EXPERTS
    n = T // ROW_TILE
    tri = jnp.asarray(np.triu(np.ones((ROW_TILE, ROW_TILE), np.float32), 1), BF16)
    sd = jax.ShapeDtypeStruct
    cols = pl.BlockSpec((TOP_K, ROW_TILE), lambda i: (0, i))
    rwt = router_w.T
    rw_hi = rwt.astype(BF16)
    rw_lo = (rwt - rw_hi.astype(F32)).astype(BF16)
    return pl.pallas_call(
        _route_kernel,
        grid=(n,),
        in_specs=[pl.BlockSpec((ROW_TILE, D), lambda i: (i, 0)),
                  pl.BlockSpec((E, D), lambda i: (0, 0)),
                  pl.BlockSpec((E, D), lambda i: (0, 0)),
                  pl.BlockSpec((E, 1), lambda i: (0, 0)),
                  pl.BlockSpec((ROW_TILE, ROW_TILE), lambda i: (0, 0))],
        out_specs=[cols, cols, cols, pl.BlockSpec((E, 1), lambda i: (0, 0))],
        out_shape=[sd((TOP_K, T), I32), sd((TOP_K, T), F32), sd((TOP_K, T), I32), sd((E, 1), F32)],
        scratch_shapes=[pltpu.VMEM((E, 1), F32)],
        compiler_params=_cparams(("arbitrary",)),
        name="moe_route",
    )(u, rw_hi, rw_lo, router_bias.reshape(E, 1), tri)


def _dest_kernel(e_ref, r_ref, ps_ref, d_ref):
    e = e_ref[...]
    TM = e.shape[1]
    io_e = lax.broadcasted_iota(I32, (N_EXPERTS, TM), 0)
    ps = ps_ref[...]
    rows = []
    for k in range(TOP_K):
        rows.append(jnp.sum(jnp.where(io_e == e[k:k + 1, :], ps, 0.0), axis=0, keepdims=True))
    d_ref[...] = jnp.concatenate(rows, axis=0).astype(I32) + r_ref[...]


def _slot_of(eidx, rank, pad_start):
    T = eidx.shape[1]
    cols = pl.BlockSpec((TOP_K, ROW_TILE), lambda i: (0, i))
    return pl.pallas_call(
        _dest_kernel,
        grid=(T // ROW_TILE,),
        in_specs=[cols, cols, pl.BlockSpec((N_EXPERTS, 1), lambda i: (0, 0))],
        out_specs=cols,
        out_shape=jax.ShapeDtypeStruct((TOP_K, T), I32),
        compiler_params=_cparams(("parallel",)),
        name="moe_slot",
    )(eidx, rank, pad_start.astype(F32).reshape(N_EXPERTS, 1))


def _dispatch_kernel(z0_ref, zn_ref, dest_ref, u_ref, xs_ref, zero_scr, sem, zsem):
    TM = u_ref.shape[0]

    @pl.when(pl.program_id(0) == 0)
    def _():
        zero_scr[...] = jnp.zeros_like(zero_scr)

        def zcopy(e):
            row0 = pl.multiple_of(z0_ref[e], MOE_BLK)
            return pltpu.make_async_copy(zero_scr, xs_ref.at[pl.ds(row0, MOE_BLK)], zsem)

        def zstart(e, c):
            @pl.when(zn_ref[e] > 0)
            def _():
                zcopy(e).start()
            return c

        def zwait(e, c):
            @pl.when(zn_ref[e] > 0)
            def _():
                zcopy(e).wait()
            return c

        lax.fori_loop(0, N_EXPERTS, zstart, 0)
        lax.fori_loop(0, N_EXPERTS, zwait, 0)

    def copy(t, k):
        return pltpu.make_async_copy(u_ref.at[t], xs_ref.at[dest_ref[k, t]], sem)

    def issue(t, c):
        for k in range(TOP_K):
            copy(t, k).start(priority=k % 2)
        return c

    lax.fori_loop(0, TM, issue, 0)
    for k in range(TOP_K):
        pltpu.make_async_copy(u_ref, xs_ref.at[pl.ds(0, TM)], sem).wait()


def _dispatch(up, dest, n_slots, pad_row0, pad_rows):
    T = up.shape[0]
    slab = up.shape[1:]
    return pl.pallas_call(
        _dispatch_kernel,
        grid_spec=pltpu.PrefetchScalarGridSpec(
            num_scalar_prefetch=2,
            grid=(T // ROW_TILE,),
            in_specs=[pl.BlockSpec((TOP_K, ROW_TILE), lambda i, z0, zn: (0, i), memory_space=pltpu.SMEM),
                      pl.BlockSpec((ROW_TILE,) + slab, lambda i, z0, zn: (i, 0, 0))],
            out_specs=pl.BlockSpec(memory_space=pl.ANY),
            scratch_shapes=[pltpu.VMEM((MOE_BLK,) + slab, U32), pltpu.SemaphoreType.DMA, pltpu.SemaphoreType.DMA]),
        out_shape=jax.ShapeDtypeStruct((n_slots,) + slab, U32),
        compiler_params=_cparams(("arbitrary",)),
        name="moe_dispatch",
    )(pad_row0, pad_rows, dest, up)


def _expert_kernel(be_ref, nu_ref, first_ref, nxt_ref, par_ref, xs_ref, w13_hbm, w2_hbm, ys_ref,
                   w13_buf, w2_buf, w13_bf, w2_bf, sem, *, layer):
    i = pl.program_id(0)

    def fetch(e, slot):
        return (pltpu.make_async_copy(w13_hbm.at[layer, e], w13_buf.at[slot], sem.at[0, slot]),
                pltpu.make_async_copy(w2_hbm.at[layer, e], w2_buf.at[slot], sem.at[1, slot]))

    @pl.when(i == 0)
    def _():
        for c in fetch(be_ref[0], 0):
            c.start()

    @pl.when(jnp.logical_and(i < nu_ref[0], first_ref[i] == 1))
    def _():
        slot = par_ref[i]
        for c in fetch(be_ref[i], slot):
            c.wait()

        @pl.when(nxt_ref[i] >= 0)
        def _():
            for c in fetch(nxt_ref[i], 1 - slot):
                c.start()

        w13_bf[...] = w13_buf[slot].astype(BF16)
        w2_bf[...] = w2_buf[slot].astype(BF16)

    @pl.when(i < nu_ref[0])
    def _():
        lo, hi = _unpack_rows(_load_row_slabs(xs_ref))
        x = jnp.concatenate([lo, hi], axis=1).astype(BF16)
        h = jnp.dot(x, w13_bf[...], preferred_element_type=F32)
        a = _silu(h[:, :EXPERT_FF]) * h[:, EXPERT_FF:]
        _store_row_slabs(ys_ref, _pack_rows(jnp.dot(a.astype(BF16), w2_bf[...], preferred_element_type=F32)))


def _expert_gemm(xs, blk_exp, n_used, cnt, w13, w2, layer):
    NP = xs.shape[0]
    slab = xs.shape[1:]
    D = 2 * slab[0] * slab[1]
    nblk = NP // MOE_BLK
    F2 = w13.shape[3]
    E = cnt.shape[0]
    blk = jnp.arange(nblk, dtype=I32)
    first = jnp.concatenate([jnp.ones((1,), I32), (blk_exp[1:] != blk_exp[:-1]).astype(I32)])
    first = jnp.where(blk < n_used[0], first, 0)
    par = (jnp.cumsum(first) - 1) % 2
    ids = jnp.where(cnt > 0, jnp.arange(E, dtype=I32), E)
    suffix_min = lax.cummin(ids, axis=0, reverse=True)
    next_active = jnp.concatenate([suffix_min[1:], jnp.full((1,), E, I32)])
    next_active = jnp.where(next_active >= E, -1, next_active)
    nxt = jnp.take(next_active, blk_exp)

    def xmap(i, be, nu, fi, nx, pa):
        return (jnp.minimum(i, nu[0] - 1), 0, 0)

    return pl.pallas_call(
        functools.partial(_expert_kernel, layer=layer),
        grid_spec=pltpu.PrefetchScalarGridSpec(
            num_scalar_prefetch=5,
            grid=(nblk,),
            in_specs=[pl.BlockSpec((MOE_BLK,) + slab, xmap),
                      pl.BlockSpec(memory_space=pl.ANY), pl.BlockSpec(memory_space=pl.ANY)],
            out_specs=pl.BlockSpec((MOE_BLK,) + slab, xmap),
            scratch_shapes=[pltpu.VMEM((2, D, F2), F32), pltpu.VMEM((2, F2 // 2, D), F32),
                            pltpu.VMEM((D, F2), BF16), pltpu.VMEM((F2 // 2, D), BF16),
                            pltpu.SemaphoreType.DMA((2, 2))]),
        out_shape=jax.ShapeDtypeStruct((NP,) + slab, U32),
        compiler_params=_cparams(("arbitrary",)),
        name="moe_experts",
    )(blk_exp, n_used, first, nxt.astype(I32), par.astype(I32), xs, w13, w2)


def _combine_kernel(mt_ref, dcur_ref, dnxt_ref, gate_ref, up_ref, xl_ref, mod_ref, s13_ref, s2_ref, lng_ref, lnb_ref,
                    ys_ref, o_ref, rows_a, rows_b, sem_a, sem_b, *, alpha):
    del mt_ref
    i = pl.program_id(0)
    H = CMB_TILE
    W = ROW_SUB * LANES

    def issue(dref, col0, rows, sem):
        for t in range(H):
            for k in range(TOP_K):
                pltpu.make_async_copy(ys_ref.at[dref[k, col0 + t]], rows.at[k, t],
                                      sem).start(priority=k % 2)

    def wait(rows, sem):
        for k in range(TOP_K):
            pltpu.make_async_copy(ys_ref.at[pl.ds(0, H)], rows.at[k], sem).wait()

    def compute(rows, r0):
        lo, hi = _unpack_rows(_load_row_slabs(up_ref, r0, H))
        hs = _bdot(jnp.concatenate([lo, hi], axis=1), s13_ref[...])
        sh = _bdot(_silu(hs[:, :SHARED_FF]) * hs[:, SHARED_FF:], s2_ref[...])
        acc_lo, acc_hi = sh[:, :W], sh[:, W:]
        gate = gate_ref[r0:r0 + H]
        for k in range(TOP_K):
            lo, hi = _unpack_rows(_load_row_slabs(rows.at[k]))
            g = gate[:, k:k + 1]
            acc_lo = acc_lo + lo * g
            acc_hi = acc_hi + hi * g
        acc = jnp.concatenate([acc_lo, acc_hi], axis=1)
        g2 = mod_ref[5:6, :]
        o_ref[r0:r0 + H] = _layer_norm_rows(alpha * xl_ref[r0:r0 + H] + g2 * acc, lng_ref[...], lnb_ref[...], LN_EPS)

    @pl.when(i == 0)
    def _():
        issue(dcur_ref, 0, rows_a, sem_a)

    wait(rows_a, sem_a)
    issue(dcur_ref, H, rows_b, sem_b)
    compute(rows_a, 0)
    wait(rows_b, sem_b)
    issue(dnxt_ref, 0, rows_a, sem_a)
    compute(rows_b, H)

    @pl.when(i == pl.num_programs(0) - 1)
    def _():
        wait(rows_a, sem_a)


def _combine(dest, gate_t, up, xl, ys, mods, mod_of_tile, sh_w13, sh_w2, ln_g, ln_b, alpha):
    T, D = xl.shape
    W = D // 2
    TM = 2 * CMB_TILE
    n = T // TM

    def rows(w):
        return pl.BlockSpec((TM, w), lambda i, mt: (i, 0))

    def const(shape):
        return pl.BlockSpec(shape, lambda i, mt: (0,) * len(shape))

    return pl.pallas_call(
        functools.partial(_combine_kernel, alpha=alpha),
        grid_spec=pltpu.PrefetchScalarGridSpec(
            num_scalar_prefetch=1,
            grid=(n,),
            in_specs=[pl.BlockSpec((TOP_K, TM), lambda i, mt: (0, i), memory_space=pltpu.SMEM),
                      pl.BlockSpec((TOP_K, TM), lambda i, mt: (0, jnp.minimum(i + 1, n - 1)), memory_space=pltpu.SMEM),
                      rows(TOP_K), pl.BlockSpec((TM, ROW_SUB, LANES), lambda i, mt: (i, 0, 0)), rows(D),
                      pl.BlockSpec((None, 6, D), lambda i, mt: (mt[i], 0, 0)),
                      const((D, 2 * SHARED_FF)), const((SHARED_FF, D)), const((1, D)), const((1, D)),
                      pl.BlockSpec(memory_space=pl.ANY)],
            out_specs=rows(D),
            scratch_shapes=[pltpu.VMEM((TOP_K, CMB_TILE, ROW_SUB, LANES), U32),
                            pltpu.VMEM((TOP_K, CMB_TILE, ROW_SUB, LANES), U32),
                            pltpu.SemaphoreType.DMA, pltpu.SemaphoreType.DMA]),
        out_shape=jax.ShapeDtypeStruct((T, D), F32),
        compiler_params=_cparams(("arbitrary",)),
        name="moe_combine",
    )(mod_of_tile, dest, dest, gate_t, up, xl, mods, sh_w13.astype(BF16), sh_w2.astype(BF16),
      ln_g.reshape(1, D), ln_b.reshape(1, D), ys)


def _moe_block(u, up, xl, mods, mod_of_tile, router_w, router_bias, w13, w2, layer, sh_w13, sh_w2, ln_g, ln_b, alpha):
    T, D = u.shape
    eidx, gate, rank, counts = _route(u, router_w, router_bias)
    cnt = counts.reshape(N_EXPERTS).astype(I32)
    padded = (cnt + MOE_BLK - 1) // MOE_BLK * MOE_BLK
    pad_end = jnp.cumsum(padded)
    pad_start = pad_end - padded
    nblk = T * TOP_K // MOE_BLK + N_EXPERTS
    n_used = (pad_end[-1] // MOE_BLK).astype(I32)
    blk_row0 = jnp.arange(nblk, dtype=I32) * MOE_BLK
    blk_exp = jnp.sum((pad_end[None, :] <= blk_row0[:, None]).astype(I32), axis=1)
    last_e = jnp.max(jnp.where(cnt > 0, jnp.arange(N_EXPERTS, dtype=I32), 0))
    blk_exp = jnp.minimum(blk_exp, last_e)
    dest = _slot_of(eidx, rank, pad_start)
    xs = _dispatch(up, dest, nblk * MOE_BLK, jnp.maximum(pad_end - MOE_BLK, 0), padded)
    ys = _expert_gemm(xs, blk_exp, n_used.reshape(1), cnt, w13, w2, layer)
    return _combine(dest, gate.T, up, xl, ys, mods, mod_of_tile, sh_w13, sh_w2, ln_g, ln_b, alpha)


def _in1_kernel(x_ref, mod_ref, wh_ref, wr_ref, ph_ref, pr_ref, *, nctx_tiles):
    sh1 = mod_ref[0:1, :]
    sc1 = mod_ref[1:2, :]
    h = (x_ref[...] * (1.0 + sc1) + sh1).astype(BF16)
    pr_ref[...] = jnp.dot(h, wr_ref[...], preferred_element_type=F32)

    @pl.when(pl.program_id(1) >= nctx_tiles)
    def _():
        ph_ref[...] = jnp.dot(h, wh_ref[...], preferred_element_type=F32)


RW_PW = 1920


def _rw_reorder(t):
    a = 3 * RW_W
    lo = 2 * RW_DECAY_LORA + RW_AAA_LORA
    out = jnp.concatenate([t[..., :a], t[..., a + lo:], t[..., a:a + lo]], axis=-1)
    pad = [(0, 0)] * (t.ndim - 1) + [(0, RW_PW - out.shape[-1])]
    return jnp.pad(out, pad)


def _in_proj1(xcat, mods, nctx_tiles, w_in):
    B, S, D = xcat.shape
    J = S // ROW_TILE
    T = B * S
    wh = w_in[:, :HY_WIDTH].astype(BF16)
    wr = _rw_reorder(w_in[:, HY_WIDTH:]).astype(BF16)

    def rows(w):
        return pl.BlockSpec((ROW_TILE, w), lambda b, j: (b * J + j, 0))

    def const(shape):
        return pl.BlockSpec(shape, lambda b, j: (0,) * len(shape))

    sd = jax.ShapeDtypeStruct
    JL = J - nctx_tiles
    return pl.pallas_call(
        functools.partial(_in1_kernel, nctx_tiles=nctx_tiles),
        grid=(B, J),
        in_specs=[rows(D), pl.BlockSpec((None, 6, D), lambda b, j: (jnp.where(j < nctx_tiles, B, b), 0, 0)),
                  const((D, HY_WIDTH)), const((D, RW_PW))],
        out_specs=[pl.BlockSpec((ROW_TILE, HY_WIDTH), lambda b, j: (b * JL + jnp.maximum(j - nctx_tiles, 0), 0)),
                   rows(RW_PW)],
        out_shape=[sd((B * JL * ROW_TILE, HY_WIDTH), F32), sd((T, RW_PW), F32)],
        compiler_params=_cparams(("parallel", "arbitrary")),
        name="in_proj1",
    )(xcat.reshape(T, D), mods, wh, wr)


def _rw_streams_kernel(prev_ref, main_ref, next_ref, mu_ref, w0_ref, w2_ref, a0_ref, a2_ref, g2_ref,
                       kk_ref, ka_ref, bd_ref,
                       r_ref, k_ref, v_ref, kkn_ref, bb_ref, ld_ref, g_ref, sh_scr, *, nctx_tiles, n_lat_tiles):
    j = pl.program_id(1)
    TM = main_ref.shape[0]
    W = main_ref.shape[1]
    H = GRID_W
    p = main_ref[...]
    ext = jnp.concatenate([prev_ref[...], p, next_ref[...]], axis=0)
    left = ext[H - 1:H - 1 + TM]
    right = ext[H + 1:H + 1 + TM]
    up = ext[0:TM]
    down = ext[2 * H:2 * H + TM]
    i = lax.broadcasted_iota(I32, (TM, W), 0)
    lane = lax.broadcasted_iota(I32, (TM, W), 1)
    even = (lane & 1) == 0
    c4 = lane & 3
    jl = j - nctx_tiles

    @pl.when(j < nctx_tiles)
    def _():
        lo = jnp.where(j == 0, 1, 0)
        hi = jnp.where(j == nctx_tiles - 1, TM - 1, TM)
        sh_scr[...] = jnp.where(even, jnp.where(i >= lo, left, 0.0), jnp.where(i < hi, right, 0.0))

    @pl.when(j >= nctx_tiles)
    def _():
        col = i & (H - 1)
        up_lo = jnp.where(jl == 0, H, 0)
        down_hi = jnp.where(jl == n_lat_tiles - 1, TM - H, TM)
        l_v = jnp.where(col != 0, left, 0.0)
        r_v = jnp.where(col != H - 1, right, 0.0)
        u_v = jnp.where(i >= up_lo, up, 0.0)
        d_v = jnp.where(i < down_hi, down, 0.0)
        sh_scr[...] = jnp.where(c4 == 0, l_v, jnp.where(c4 == 1, r_v, jnp.where(c4 == 2, u_v, d_v)))

    pm = p + mu_ref[...] * (sh_scr[...] - p)
    r = pm[:, 0:RW_W]
    k = pm[:, RW_W:2 * RW_W]
    v = pm[:, 2 * RW_W:3 * RW_W]
    o = 3 * RW_W
    gl = pm[:, o:o + RW_GATE_LORA]
    o += RW_GATE_LORA
    wl_f = pm[:, o:o + RW_DECAY_LORA]
    wl_b = pm[:, o + RW_DECAY_LORA:o + 2 * RW_DECAY_LORA]
    al = pm[:, o + 2 * RW_DECAY_LORA:o + 2 * RW_DECAY_LORA + RW_AAA_LORA]
    for d, wl in enumerate((wl_f, wl_b)):
        z = w0_ref[d:d + 1, :] + _bdot(jnp.tanh(wl), w2_ref[d])
        w = -(jnp.maximum(-z, 0.0) + jnp.log(1.0 + jnp.exp(-jnp.abs(z)))) - 0.5
        ld_ref[d] = -jnp.exp(w)
    a = jax.nn.sigmoid(a0_ref[...] + _bdot(al, a2_ref[...]))
    g_ref[...] = _bdot(jax.nn.sigmoid(gl), g2_ref[...])
    kk = k * kk_ref[...]
    n2 = _head_sums(kk * kk, bd_ref[...])
    kkn = kk / jnp.maximum(jnp.sqrt(n2), 1e-12)
    r_ref[...] = r
    k_ref[...] = k * (1.0 + (a - 1.0) * ka_ref[...])
    v_ref[...] = v
    kkn_ref[...] = kkn
    bb_ref[...] = kkn * a


def _head_sum_matrix(width, hd):
    i = np.arange(width)
    return jnp.asarray((i[:, None] // hd == i[None, :] // hd).astype(np.float32), BF16)


def _rw_streams(p_rw, B, S, nctx_tiles, mu, w0, w2, a0, a2, g2, k_k, k_a):
    T = B * S
    J = S // ROW_TILE
    HB = ROW_TILE // GRID_W
    NH = S // GRID_W

    def rows(w):
        return pl.BlockSpec((ROW_TILE, w), lambda b, j: (b * J + j, 0))

    def rows2(w):
        return pl.BlockSpec((2, ROW_TILE, w), lambda b, j: (0, b * J + j, 0))

    def const(shape):
        return pl.BlockSpec(shape, lambda b, j: (0,) * len(shape))

    sd = jax.ShapeDtypeStruct
    kern = functools.partial(_rw_streams_kernel, nctx_tiles=nctx_tiles, n_lat_tiles=J - nctx_tiles)
    return pl.pallas_call(
        kern,
        grid=(B, J),
        in_specs=[pl.BlockSpec((GRID_W, RW_PW), lambda b, j: (b * NH + jnp.maximum(j * HB - 1, 0), 0)),
                  rows(RW_PW),
                  pl.BlockSpec((GRID_W, RW_PW), lambda b, j: (b * NH + jnp.minimum((j + 1) * HB, NH - 1), 0)),
                  const((1, RW_PW)), const((2, RW_W)), const((2, RW_DECAY_LORA, RW_W)), const((1, RW_W)),
                  const((RW_AAA_LORA, RW_W)), const((RW_GATE_LORA, RW_W)), const((1, RW_W)), const((1, RW_W)),
                  const((RW_W, RW_W))],
        out_specs=[rows(RW_W), rows(RW_W), rows(RW_W), rows(RW_W), rows(RW_W), rows2(RW_W), rows(RW_W)],
        out_shape=[sd((T, RW_W), F32)] * 5 + [sd((2, T, RW_W), F32), sd((T, RW_W), F32)],
        scratch_shapes=[pltpu.VMEM((ROW_TILE, RW_PW), F32)],
        compiler_params=_cparams(("parallel", "parallel")),
        name="rwkv_streams",
    )(p_rw, p_rw, p_rw, _rw_reorder(mu).reshape(1, RW_PW), w0, w2, a0.reshape(1, RW_W), a2, g2,
      k_k.reshape(1, RW_W), k_a.reshape(1, RW_W), _head_sum_matrix(RW_W, RW_HEAD_DIM))


def _rwkv_kernel(r_ref, k_ref, v_ref, kk_ref, bb_ref, ld_ref, o_ref, st_ref, *, rev, nchunk):
    t = pl.program_id(1)

    @pl.when(t == 0)
    def _():
        st_ref[...] = jnp.zeros_like(st_ref)

    C = RW_CHUNK
    hd = RW_HEAD_DIM
    NH = RW_HEADS
    row = lax.broadcasted_iota(I32, (C, C), 0)
    col = lax.broadcasted_iota(I32, (C, C), 1)
    incl = ((col >= row) if rev else (col <= row))[None]
    strict = ((col > row) if rev else (col < row))[None]
    tri = jnp.where(incl[0], 1.0, 0.0).astype(BF16)
    eye = (row == col)[None]
    last_i = 0 if rev else C - 1
    order = range(nchunk - 1, -1, -1) if rev else range(nchunk)
    n_double = int(math.log2(C)) - 1

    def stack(x):
        return jnp.stack([x[c * C:(c + 1) * C, h * hd:(h + 1) * hd] for c in range(nchunk) for h in range(NH)], 0)

    ld = ld_ref[...]
    g = jnp.concatenate([_split_dot(tri, ld[c * C:(c + 1) * C]) for c in range(nchunk)], axis=0)
    g_last = jnp.concatenate([jnp.broadcast_to(g[c * C + last_i:c * C + last_i + 1], (C, NH * hd))
                              for c in range(nchunk)], axis=0)
    k = k_ref[...]
    bb = bb_ref[...]
    eng = jnp.exp(-g)
    e_end = jnp.exp(g_last - g)
    kk_t = stack(kk_ref[...] * jnp.exp(g - ld))
    b_t = stack(bb * eng)
    k_t = stack(k * eng)
    r_t = stack(r_ref[...] * jnp.exp(g))
    b_bar = stack(bb * e_end)
    k_bar = stack(k * e_end)
    dec = stack(jnp.exp(g_last))
    v = stack(v_ref[...])
    m_b = jnp.where(strict, _bmm_nt(kk_t, b_t), 0.0)
    m_k = jnp.where(strict, _bmm_nt(kk_t, k_t), 0.0)
    a_rb = jnp.where(incl, _bmm_nt(r_t, b_t), 0.0)
    a_rk = jnp.where(incl, _bmm_nt(r_t, k_t), 0.0)
    pw = -m_b
    tinv = jnp.where(eye, 1.0, 0.0) + pw
    for _ in range(n_double):
        pw = _bmm(pw, pw)
        tinv = tinv + _bmm(tinv, pw)
    a_t = _bmm(tinv, kk_t)
    u_b = _bmm(tinv, _bmm(m_k, v))
    q_h = r_t - _bmm(a_rb, a_t)
    o_h = _bmm(a_rk, v) - _bmm(a_rb, u_b)
    g_m = jnp.where(eye, dec, 0.0) - _bmm_tn(a_t, b_bar)
    h_m = _bmm_tn(v, k_bar) - _bmm_tn(u_b, b_bar)
    s = st_ref[...]
    for c in order:
        sl = slice(c * NH, (c + 1) * NH)
        o_c = _bmm_nt(q_h[sl], s) + o_h[sl]
        o_ref[c * C:(c + 1) * C, :] = jnp.concatenate([o_c[h] for h in range(NH)], axis=1)
        s = _bmm(s, g_m[sl]) + h_m[sl]
    st_ref[...] = s


def _rwkv_scan(r, k, v, kk, bb, ld, B, S, nctx_blocks, rev):
    T = B * S
    NB = S // ROW_TILE
    d = 1 if rev else 0
    nc = nctx_blocks

    def blk(t):
        if not rev:
            return t
        return jnp.where(t < nc, nc - 1 - t, NB - 1 - (t - nc))

    spec = pl.BlockSpec((ROW_TILE, RW_W), lambda b, t: (b * NB + blk(t), 0))
    kern = functools.partial(_rwkv_kernel, rev=rev, nchunk=ROW_TILE // RW_CHUNK)
    return pl.pallas_call(
        kern,
        grid=(B, NB),
        in_specs=[spec, spec, spec, spec, spec,
                  pl.BlockSpec((None, ROW_TILE, RW_W), lambda b, t: (d, b * NB + blk(t), 0))],
        out_specs=spec,
        out_shape=jax.ShapeDtypeStruct((T, RW_W), F32),
        scratch_shapes=[pltpu.VMEM((RW_HEADS, RW_HEAD_DIM, RW_HEAD_DIM), F32)],
        compiler_params=_cparams(("parallel", "arbitrary")),
        name="rwkv_scan_bwd" if rev else "rwkv_scan_fwd",
    )(r, k, v, kk, bb, ld)


def _filter_mlp_kernel(z_ref, w1_ref, b1_ref, w2_ref, b2_ref, w3_ref, sf_ref, win_ref, f_ref):
    h = jnp.sin(sf_ref[0:1, :] * (_hdot(z_ref[...], w1_ref[...]) + b1_ref[...]))
    h = jnp.sin(sf_ref[1:2, :] * (_hdot(h, w2_ref[...]) + b2_ref[...]))
    f_ref[...] = _hdot(h, w3_ref[...]) * win_ref[...]


def _hyena_filters(L, w1, b1, w2, b2, w3, sin_freq):
    t = np.linspace(0.0, 1.0, L, dtype=np.float32)[:, None]
    bands = (HY_EMB - 1) // 2
    wpos = (2.0 * math.pi * np.arange(L, dtype=np.float32)[:, None] / L).astype(np.float32)
    fr = np.linspace(1e-4, bands - 1, bands, dtype=np.float32)[None, :]
    z = np.concatenate([t, np.cos(fr * wpos), -np.sin(fr * wpos)], -1).astype(np.float32)
    max_decay = math.log(HY_TARGET) / HY_FAST_DECAY
    min_decay = math.log(HY_TARGET) / HY_SLOW_DECAY
    deltas = np.linspace(min_decay, max_decay, HY_CH, dtype=np.float32)
    window = np.exp(-t * np.abs(deltas)).astype(np.float32)
    dist = np.concatenate([np.arange(L), [0], np.arange(L - 1, 0, -1)])
    z2 = np.pad(z[dist], ((0, 0), (0, LANES - HY_EMB)))
    win2 = np.tile(window[dist], (1, HY_ORDER))
    win2[L] = 0.0
    w1p = jnp.pad(w1, ((0, LANES - HY_EMB), (0, 0)))
    Hd = w1.shape[1]
    FW = HY_ORDER * HY_CH
    w3s = jnp.transpose(w3.reshape(Hd, HY_ORDER, 2, HY_CH), (2, 0, 1, 3)).reshape(2, Hd, FW)
    TR = min(L, 512)
    nl = L // TR
    return pl.pallas_call(
        _filter_mlp_kernel,
        grid=(2 * nl,),
        in_specs=[pl.BlockSpec((TR, LANES), lambda i: (i, 0)),
                  pl.BlockSpec((LANES, Hd), lambda i: (0, 0)), pl.BlockSpec((1, Hd), lambda i: (0, 0)),
                  pl.BlockSpec((Hd, Hd), lambda i: (0, 0)), pl.BlockSpec((1, Hd), lambda i: (0, 0)),
                  pl.BlockSpec((None, Hd, FW), lambda i: (i // nl, 0, 0)), pl.BlockSpec((2, Hd), lambda i: (0, 0)),
                  pl.BlockSpec((TR, FW), lambda i: (i, 0))],
        out_specs=pl.BlockSpec((TR, FW), lambda i: (i, 0)),
        out_shape=jax.ShapeDtypeStruct((2 * L, FW), F32),
        compiler_params=_cparams(("parallel",)),
        name="hyena_filter_mlp",
    )(jnp.asarray(z2), w1p, b1.reshape(1, Hd), w2, b2.reshape(1, Hd), w3s, sin_freq, jnp.asarray(win2))


def _sconv_kernel(p_ref, w_ref, b_ref, o_ref):
    p = p_ref[...]
    L = p.shape[0]
    i = lax.broadcasted_iota(I32, p.shape, 0)
    prev = jnp.where(i == 0, 0.0, pltpu.roll(p, 1, 0))
    nxt = jnp.where(i == L - 1, 0.0, pltpu.roll(p, L - 1, 0))
    o_ref[...] = prev * w_ref[0:1, :] + p * w_ref[1:2, :] + nxt * w_ref[2:3, :] + b_ref[...]


def _short_conv(p_hy, B, S, Lc, conv_w, conv_b):
    L = S - Lc
    p3 = p_hy.reshape(B, L, HY_WIDTH)
    return pl.pallas_call(
        _sconv_kernel,
        grid=(B, HY_WIDTH // LANES),
        in_specs=[pl.BlockSpec((None, L, LANES), lambda b, c: (b, 0, c)),
                  pl.BlockSpec((HY_SHORT, LANES), lambda b, c: (0, c)),
                  pl.BlockSpec((1, LANES), lambda b, c: (0, c))],
        out_specs=pl.BlockSpec((None, L, LANES), lambda b, c: (b, 0, c)),
        out_shape=jax.ShapeDtypeStruct((B, L, HY_WIDTH), F32),
        compiler_params=_cparams(("parallel", "parallel")),
        name="hyena_short_conv",
    )(p3, conv_w, conv_b.reshape(1, HY_WIDTH))


def _dft_constants(L):
    n = 2 * L
    n2 = FFT_N2
    n1 = n // n2
    na = n1 // 2
    g8 = SUBLANES
    w1 = np.exp(-2j * np.pi * np.outer(np.arange(n1), np.arange(n1)) / n1)
    eye8 = np.eye(g8)

    def kron_fwd(w, real_in):
        k1n, an = w.shape
        blocks = np.stack([np.stack([w.real, -w.imag], 1), np.stack([w.imag, w.real], 1)], 1)
        if real_in:
            blocks = blocks[:, :, 0:1, :]
        m = np.einsum('kria,bc->krbiac', blocks, eye8)
        return m.reshape(k1n * 2 * g8, blocks.shape[2] * an * g8)

    m1 = kron_fwd(w1[:, :na], False)
    m1f = kron_fwd(w1, True)
    cw = np.conj(w1[:, :na]).T / n
    blocks = np.stack([np.stack([cw.real, -cw.imag], 1), np.stack([cw.imag, cw.real], 1)], 0)
    m1inv = np.einsum('iark,bc->iabkrc', blocks, eye8).reshape(2 * na * g8, n1 * 2 * g8)
    w2 = np.exp(-2j * np.pi * np.outer(np.arange(n2), np.arange(n2)) / n2)
    w2big = np.block([[w2.real, -w2.imag], [w2.imag, w2.real]])
    w2c = np.conj(w2)
    iw2big = np.block([[w2c.real, -w2c.imag], [w2c.imag, w2c.real]])
    tw = np.exp(-2j * np.pi * np.arange(n2) / n)

    def lanes(cols):
        z = np.concatenate([np.broadcast_to(col[:, None], (n2, LANES)) for col in cols], axis=1)
        return np.stack([z.real, z.imag], 0)

    tw0 = lanes([tw ** q for q in range(FFT_G)])
    twg = lanes([tw ** FFT_G] * FFT_G)
    c = lambda x, dt: jnp.asarray(np.ascontiguousarray(x), dt)
    return dict(m1=c(m1, BF16), m1f=c(m1f, BF16), m1inv=c(m1inv, BF16), w2=c(w2big, BF16), iw2=c(iw2big, BF16),
                tw0=c(tw0, F32), twg=c(twg, F32), n1=n1, na=na)


def _cplx_rows(z, twr, twi, conj):
    n2 = z.shape[0] // 2
    zr, zi = z[:n2], z[n2:]
    if conj:
        return jnp.concatenate([zr * twr + zi * twi, zi * twr - zr * twi], axis=0)
    return jnp.concatenate([zr * twr - zi * twi, zi * twr + zr * twi], axis=0)


def _split_dot(m, x):
    hi = x.astype(BF16)
    lo = (x - hi.astype(F32)).astype(BF16)
    return jnp.dot(m, hi, preferred_element_type=F32) + jnp.dot(m, lo, preferred_element_type=F32)


def _next_twiddle(tw, twg_ref):
    twr, twi = tw
    b_r, b_i = twg_ref[0], twg_ref[1]
    return twr * b_r - twi * b_i, twr * b_i + twi * b_r


def _load_group(scr, j, C):
    return jnp.concatenate([scr[j * FFT_G + q].reshape(2 * FFT_N2, C) for q in range(FFT_G)], axis=1)


def _filt_fft_kernel(f_ref, m1_ref, w2_ref, tw0_ref, twg_ref, o_ref, y_scr, *, n1):
    nbg = FFT_N2 // SUBLANES
    C = f_ref.shape[-1]
    for bg in range(nbg):
        yg = _split_dot(m1_ref[...], f_ref[:, bg].reshape(n1 * SUBLANES, C))
        y_scr[:, :, bg] = yg.reshape(n1, 2, SUBLANES, C)

    def body(j, tw):
        y = _cplx_rows(_load_group(y_scr, j, C), tw[0], tw[1], False)
        z = _split_dot(w2_ref[...], y)
        for q in range(FFT_G):
            o_ref[j * FFT_G + q] = z[:, q * C:(q + 1) * C]
        return _next_twiddle(tw, twg_ref)

    lax.fori_loop(0, n1 // FFT_G, body, (tw0_ref[0], tw0_ref[1]))


def _filter_spectrum(filt, L, dc):
    n1 = dc['n1']
    nbg = FFT_N2 // SUBLANES
    ncb = HY_CH // LANES
    full = filt.reshape(n1, nbg, SUBLANES, HY_ORDER * HY_CH)
    return pl.pallas_call(
        functools.partial(_filt_fft_kernel, n1=n1),
        grid=(HY_ORDER, ncb),
        in_specs=[pl.BlockSpec((n1, nbg, SUBLANES, LANES), lambda o, c: (0, 0, 0, o * ncb + c)),
                  pl.BlockSpec(dc['m1f'].shape, lambda o, c: (0, 0)),
                  pl.BlockSpec(dc['w2'].shape, lambda o, c: (0, 0)),
                  pl.BlockSpec(dc['tw0'].shape, lambda o, c: (0, 0, 0)),
                  pl.BlockSpec(dc['twg'].shape, lambda o, c: (0, 0, 0))],
        out_specs=pl.BlockSpec((None, n1, 2 * FFT_N2, LANES), lambda o, c: (o, 0, 0, c)),
        out_shape=jax.ShapeDtypeStruct((HY_ORDER, n1, 2 * FFT_N2, HY_CH), F32),
        scratch_shapes=[pltpu.VMEM((n1, 2, nbg, SUBLANES, LANES), F32)],
        compiler_params=_cparams(("parallel", "parallel")),
        name="hyena_filter_fft",
    )(full, dc['m1f'], dc['w2'], dc['tw0'], dc['twg'])


def _hyconv_kernel(za_ref, zb_ref, ga_ref, gb_ref, ff_ref, bias_ref, m1_ref, m1i_ref, w2_ref, iw2_ref,
                   tw0_ref, twg_ref, oa_ref, ob_ref, y_scr, *, n1, na):
    nbg = FFT_N2 // SUBLANES
    C = za_ref.shape[-1]
    half = na * SUBLANES
    for bg in range(nbg):
        xg = jnp.concatenate([za_ref[:, bg].reshape(half, C), zb_ref[:, bg].reshape(half, C)], axis=0)
        yg = jnp.dot(m1_ref[...], xg.astype(BF16), preferred_element_type=F32)
        y_scr[:, :, bg] = yg.reshape(n1, 2, SUBLANES, C)

    def body(j, tw):
        y = _cplx_rows(_load_group(y_scr, j, C), tw[0], tw[1], False)
        z = jnp.dot(w2_ref[...], y.astype(BF16), preferred_element_type=F32)
        f = _load_group(ff_ref, j, C)
        zr, zi = z[:FFT_N2], z[FFT_N2:]
        fr, fi = f[:FFT_N2], f[FFT_N2:]
        p = jnp.concatenate([zr * fr - zi * fi, zr * fi + zi * fr], axis=0)
        v = jnp.dot(iw2_ref[...], p.astype(BF16), preferred_element_type=F32)
        v = _cplx_rows(v, tw[0], tw[1], True)
        for q in range(FFT_G):
            y_scr[j * FFT_G + q] = v[:, q * C:(q + 1) * C].reshape(2, nbg, SUBLANES, C)
        return _next_twiddle(tw, twg_ref)

    lax.fori_loop(0, n1 // FFT_G, body, (tw0_ref[0], tw0_ref[1]))
    bias = bias_ref[...]
    for bg in range(nbg):
        vg = y_scr[:, :, bg].reshape(n1 * 2 * SUBLANES, C)
        out = jnp.dot(m1i_ref[...], vg.astype(BF16), preferred_element_type=F32)
        ya = out[:half].reshape(na, SUBLANES, C)
        yb = out[half:].reshape(na, SUBLANES, C)
        za = za_ref[:, bg]
        zb = zb_ref[:, bg]
        oa_ref[:, bg] = ga_ref[:, bg] * (ya + za * bias)
        ob_ref[:, bg] = gb_ref[:, bg] * (yb + zb * bias)


def _hyena_conv(z, z_col0, gates, g_col0, spec, bias, dc, B, L):
    n1, na = dc['n1'], dc['na']
    nbg = FFT_N2 // SUBLANES
    NCB = HY_CH // LANES

    def view(a):
        return a.reshape(B, na, nbg, SUBLANES, a.shape[-1])

    def seq(col0, which):
        return pl.BlockSpec((None, na, nbg, SUBLANES, LANES),
                            lambda c, p: (2 * p + which, 0, 0, 0, col0 + c))

    def const(a):
        nd = a.ndim
        return pl.BlockSpec(a.shape, lambda c, p: (0,) * nd)

    out_a, out_b = pl.pallas_call(
        functools.partial(_hyconv_kernel, n1=n1, na=na),
        grid=(NCB, B // 2),
        in_specs=[seq(z_col0, 0), seq(z_col0, 1), seq(g_col0, 0), seq(g_col0, 1),
                  pl.BlockSpec((n1, 2 * FFT_N2, LANES), lambda c, p: (0, 0, c), pipeline_mode=pl.Buffered(1)),
                  pl.BlockSpec((1, LANES), lambda c, p: (0, c)),
                  const(dc['m1']), const(dc['m1inv']), const(dc['w2']), const(dc['iw2']),
                  const(dc['tw0']), const(dc['twg'])],
        out_specs=[pl.BlockSpec((None, na, nbg, SUBLANES, LANES), lambda c, p: (p, 0, 0, 0, c)),
                   pl.BlockSpec((None, na, nbg, SUBLANES, LANES), lambda c, p: (p, 0, 0, 0, c))],
        out_shape=[jax.ShapeDtypeStruct((B // 2, na, nbg, SUBLANES, HY_CH), F32)] * 2,
        scratch_shapes=[pltpu.VMEM((n1, 2, nbg, SUBLANES, LANES), F32)],
        compiler_params=_cparams(("parallel", "arbitrary")),
        name="hyena_long_conv",
    )(view(z), view(z), view(gates), view(gates), spec, bias.reshape(1, HY_CH),
      dc['m1'], dc['m1inv'], dc['w2'], dc['iw2'], dc['tw0'], dc['twg'])
    out = jnp.stack([out_a, out_b], axis=1)
    return out.reshape(B, L, HY_CH)


def _out1_kernel(hy_ref, of_ref, ob_ref, r_ref, k_ref, v_ref, g_ref, x_ref, mod_ref, wo_ref,
                 rk_ref, lg_ref, lb_ref, bd_ref, lng_ref, lnb_ref, xl_ref, u_ref, up_ref, *, alpha):
    o = of_ref[...] + ob_ref[...]
    bd = bd_ref[...]
    inv = 1.0 / RW_HEAD_DIM
    mu = _head_sums(o, bd) * inv
    oc = o - mu
    var = _head_sums(oc * oc, bd) * inv
    on = oc * lax.rsqrt(var + RW_GN_EPS) * lg_ref[...] + lb_ref[...]
    bonus = _head_sums(r_ref[...] * k_ref[...] * rk_ref[...], bd) * v_ref[...]
    y = jnp.concatenate([hy_ref[...], (on + bonus) * g_ref[...]], axis=1)
    _post_mix(y, x_ref[...], mod_ref, wo_ref, lng_ref, lnb_ref, alpha, xl_ref, u_ref, up_ref)


def _readout1(hy, o_f, o_b, r, k, v, g, xcat, mods, B, S, Lc, w_out, r_k, ln_g, ln_b, dn_g, dn_b, alpha):
    D = xcat.shape[-1]
    L = S - Lc
    J = S // ROW_TILE
    JL = L // ROW_TILE
    JC = Lc // ROW_TILE

    def cat_rows(w):
        return pl.BlockSpec((ROW_TILE, w), lambda b, j: (b * J + JC + j, 0))

    def lat_rows(w):
        return pl.BlockSpec((ROW_TILE, w), lambda b, j: (b * JL + j, 0))

    def const(shape):
        return pl.BlockSpec(shape, lambda b, j: (0,) * len(shape))

    sd = jax.ShapeDtypeStruct
    return pl.pallas_call(
        functools.partial(_out1_kernel, alpha=alpha),
        grid=(B, JL),
        in_specs=[lat_rows(HY_CH), cat_rows(RW_W), cat_rows(RW_W), cat_rows(RW_W), cat_rows(RW_W), cat_rows(RW_W),
                  cat_rows(RW_W), cat_rows(D),
                  pl.BlockSpec((None, 6, D), lambda b, j: (b, 0, 0)),
                  const((HY_CH + RW_W, D)), const((1, RW_W)), const((1, RW_W)), const((1, RW_W)),
                  const((RW_W, RW_W)), const((1, D)), const((1, D))],
        out_specs=[lat_rows(D), lat_rows(D),
                   pl.BlockSpec((ROW_TILE, ROW_SUB, LANES), lambda b, j: (b * JL + j, 0, 0))],
        out_shape=[sd((B * L, D), F32), sd((B * L, D), F32), sd((B * L, ROW_SUB, LANES), U32)],
        compiler_params=_cparams(("parallel", "parallel")),
        name="readout1",
    )(hy, o_f, o_b, r, k, v, g, xcat.reshape(B * S, D), mods, w_out.astype(BF16),
      r_k.reshape(1, RW_W), ln_g.reshape(1, RW_W), ln_b.reshape(1, RW_W),
      _head_sum_matrix(RW_W, RW_HEAD_DIM), dn_g.reshape(1, D), dn_b.reshape(1, D))


def kernel(x, c, ctx, c_ctx, mod_w, mod_b, ln1_g, ln1_b, ln2_g, ln2_b, ev_w_in, ev_w_out, gla_gate_w2, gla_gate_b, gla_norm_g, hg_lb_logits, hg_norm_g, od_w_in, od_w_out, hy_conv_w, hy_conv_b, hy_ffn_w1, hy_ffn_b1, hy_ffn_w2, hy_ffn_b2, hy_ffn_w3, hy_sin_freq, hy_bias, rw_mu, rw_w0, rw_w2, rw_a0, rw_a2, rw_g2, rw_k_k, rw_k_a, rw_r_k, rw_ln_g, rw_ln_b, router_w, router_bias, exp_w13, exp_w2, sh_w13, sh_w2):
    B, L, D = x.shape
    Lc = ctx.shape[1]
    S = Lc + L
    depth = mod_w.shape[0]
    assert depth == 2 and L % ROW_TILE == 0 and Lc % ROW_TILE == 0 and B % 2 == 0
    assert L % (FFT_N2 * 2) == 0 and L % GRID_W == 0
    alpha = (2 * depth) ** 0.25
    nctx = Lc // ROW_TILE
    J = S // ROW_TILE
    T = B * S

    cc = jnp.concatenate([c, c_ctx[None, :]], axis=0)
    cc = jnp.pad(cc, ((0, (-cc.shape[0]) % SUBLANES), (0, 0)))
    hg_lb = jnp.cumsum(jax.nn.softmax(hg_lb_logits.astype(F32), axis=0), axis=0)
    xcat = jnp.concatenate([ctx, x], axis=1)

    mods = _modulation(cc, mod_w[0], mod_b[0])
    gq, gk, gv, gla, r, hq, hk, hla, hv, hgate = _in_proj0(xcat, mods, nctx, ev_w_in[0], gla_gate_w2[0],
                                                            gla_gate_b[0], hg_lb[0])
    gk3 = gk.reshape(1, T, GLA_QK)
    o_gf = _gated_recurrence(gq, gk3, gv, gla, 0, B, S, nctx, False, GLA_HEADS, GLA_DK, GLA_DV)
    o_gb = _gated_recurrence(gq, gk3, gv, gla, 0, B, S, nctx, True, GLA_HEADS, GLA_DK, GLA_DV)
    o_hf = _gated_recurrence(hq, hk, hv, hla, 0, B, S, nctx, False, HG_HEADS, HG_EXPAND, HG_EXPAND)
    o_hb = _gated_recurrence(hq, hk, hv, hla, 1, B, S, nctx, True, HG_HEADS, HG_EXPAND, HG_EXPAND)
    xl, u, up = _readout0(o_gf, o_gb, o_hf, o_hb, r, hgate, xcat, mods, nctx, ev_w_out[0], gla_norm_g[0],
                          hg_norm_g[0], ln1_g[0], ln1_b[0], alpha)
    cmb_rows = 2 * CMB_TILE
    tiles_per_b = S // cmb_rows
    tile_in_b = jnp.arange(B * tiles_per_b, dtype=I32) % tiles_per_b
    mod_of_tile = jnp.where(tile_in_b < Lc // cmb_rows, B, jnp.arange(B * tiles_per_b, dtype=I32) // tiles_per_b)
    x1 = _moe_block(u, up, xl, mods, mod_of_tile.astype(I32), router_w[0], router_bias[0], exp_w13, exp_w2, 0,
                    sh_w13[0], sh_w2[0], ln2_g[0], ln2_b[0], alpha)
    xcat = x1.reshape(B, S, D)

    mods = _modulation(cc, mod_w[1], mod_b[1])
    p_hy, p_rw = _in_proj1(xcat, mods, nctx, od_w_in[0])
    rr, rk, rv, rkk, rbb, rld, rg = _rw_streams(p_rw, B, S, nctx, rw_mu[0], rw_w0[0], rw_w2[0], rw_a0[0],
                                                 rw_a2[0], rw_g2[0], rw_k_k[0], rw_k_a[0])
    o_f = _rwkv_scan(rr, rk, rv, rkk, rbb, rld, B, S, nctx, False)
    o_b = _rwkv_scan(rr, rk, rv, rkk, rbb, rld, B, S, nctx, True)
    dc = _dft_constants(L)
    filt = _hyena_filters(L, hy_ffn_w1[0], hy_ffn_b1[0], hy_ffn_w2[0], hy_ffn_b2[0], hy_ffn_w3[0], hy_sin_freq[0])
    spec = _filter_spectrum(filt, L, dc)
    uu = _short_conv(p_hy, B, S, Lc, hy_conv_w[0], hy_conv_b[0])
    NCB = HY_CH // LANES
    z1 = _hyena_conv(uu, 0, uu, NCB, spec[0], hy_bias[0, 0], dc, B, L)
    z2 = _hyena_conv(z1, 0, uu, 2 * NCB, spec[1], hy_bias[0, 1], dc, B, L)
    xl, u, up = _readout1(z2.reshape(B * L, HY_CH), o_f, o_b, rr, rk, rv, rg, xcat, mods, B, S, Lc, od_w_out[0],
                          rw_r_k[0], rw_ln_g[0], rw_ln_b[0], ln1_g[1], ln1_b[1], alpha)
    mod_of_tile = (jnp.arange(B * (L // cmb_rows), dtype=I32) // (L // cmb_rows)).astype(I32)
    out = _moe_block(u, up, xl, mods, mod_of_tile, router_w[1], router_bias[1], exp_w13, exp_w2, 1,
                     sh_w13[1], sh_w2[1], ln2_g[1], ln2_b[1], alpha)
    return out.reshape(B, L, D)
```

```python
import functools
import math

import numpy as np
import jax
import jax.numpy as jnp
from jax import lax
from jax.experimental import pallas as pl
from jax.experimental.pallas import tpu as pltpu

F32 = jnp.float32
BF16 = jnp.bfloat16
I32 = jnp.int32
U32 = jnp.uint32
HI = lax.Precision.HIGHEST

LN_EPS = 1e-5
GLA_HEADS, GLA_DK, GLA_DV = 4, 64, 128
GLA_QK, GLA_V = GLA_HEADS * GLA_DK, GLA_HEADS * GLA_DV
GLA_GATE_RANK = 16
GLA_GATE_NORM = 16.0
CHUNK = 64
HG_HEADS, HG_EXPAND = 4, 128
HG_W = HG_HEADS * HG_EXPAND
HY_CH, HY_ORDER, HY_SHORT, HY_EMB = 512, 2, 3, 33
HY_WIDTH = (HY_ORDER + 1) * HY_CH
HY_FAST_DECAY, HY_SLOW_DECAY, HY_TARGET = 0.3, 1.5, 1e-2
RW_HEADS, RW_HEAD_DIM = 8, 64
RW_W = RW_HEADS * RW_HEAD_DIM
RW_DECAY_LORA, RW_AAA_LORA, RW_GATE_LORA = 64, 64, 128
RW_GN_EPS = 64e-5
GRID_W = 64
N_EXPERTS, TOP_K, N_GROUPS, TOPK_GROUPS = 256, 8, 8, 4
PER_GROUP = N_EXPERTS // N_GROUPS
EXPERT_FF, SHARED_FF = 256, 256
ROUTED_SCALE = 2.5

LANES = 128
SUBLANES = 8
VMEM_LIMIT = 56 * 1024 * 1024
ROW_TILE = 256
MOE_BLK = 256
CMB_TILE = 128
RW_CHUNK = 64
FFT_N2 = 128
FFT_G = 4


def _cparams(sem):
    return pltpu.CompilerParams(dimension_semantics=sem, vmem_limit_bytes=VMEM_LIMIT)


def _bdot(a, b):
    return jnp.dot(a.astype(BF16), b.astype(BF16), preferred_element_type=F32)


def _bdot_nt(a, b):
    return lax.dot_general(a.astype(BF16), b.astype(BF16), (((1,), (1,)), ((), ())),
                           preferred_element_type=F32)


def _bdot_tn(a, b):
    return lax.dot_general(a.astype(BF16), b.astype(BF16), (((0,), (0,)), ((), ())),
                           preferred_element_type=F32)


def _bmm(a, b):
    return jnp.einsum('nij,njk->nik', a.astype(BF16), b.astype(BF16), preferred_element_type=F32)


def _bmm_nt(a, b):
    return jnp.einsum('nik,njk->nij', a.astype(BF16), b.astype(BF16), preferred_element_type=F32)


def _bmm_tn(a, b):
    return _bmm(jnp.swapaxes(a, 1, 2), b)


def _hdot(a, b):
    return jnp.dot(a, b, precision=HI, preferred_element_type=F32)


def _head_sums(x, ones_bd):
    hi = x.astype(BF16)
    lo = (x - hi.astype(F32)).astype(BF16)
    return (jnp.dot(hi, ones_bd, preferred_element_type=F32) + jnp.dot(lo, ones_bd, preferred_element_type=F32))


def _silu(x):
    return x * jax.nn.sigmoid(x)


def _layer_norm_rows(x, g, b, eps):
    mu = jnp.mean(x, axis=-1, keepdims=True)
    xc = x - mu
    var = jnp.mean(xc * xc, axis=-1, keepdims=True)
    return xc * lax.rsqrt(var + eps) * g + b


def _mod_kernel(c_ref, w_ref, b_ref, o_ref):
    o_ref[...] = _hdot(_silu(c_ref[...]), w_ref[...]) + b_ref[...]


def _modulation(cc, w, b):
    R, D = cc.shape
    out = pl.pallas_call(
        _mod_kernel,
        grid=(6,),
        in_specs=[pl.BlockSpec((R, D), lambda j: (0, 0)),
                  pl.BlockSpec((D, D), lambda j: (0, j)),
                  pl.BlockSpec((1, D), lambda j: (0, j))],
        out_specs=pl.BlockSpec((R, D), lambda j: (0, j)),
        out_shape=jax.ShapeDtypeStruct((R, 6 * D), F32),
        compiler_params=_cparams(("parallel",)),
        name="modulation",
    )(cc, w, b.reshape(1, 6 * D))
    return out.reshape(R, 6, D)


def _in0_kernel(x_ref, mod_ref, w_ref, wa_ref, w2_ref, gb_ref, lb_ref,
                gq_ref, gk_ref, gv_ref, gla_ref, r_ref, hq_ref, hk_ref, hla_ref, hv_ref, hg_ref):
    sh1 = mod_ref[0:1, :]
    sc1 = mod_ref[1:2, :]
    h = (x_ref[...] * (1.0 + sc1) + sh1).astype(BF16)

    def proj(off, width):
        return jnp.dot(h, w_ref[:, off:off + width], preferred_element_type=F32)

    gq_ref[...] = (proj(0, GLA_QK) * (GLA_DK ** -0.5)).astype(BF16)
    gk_ref[...] = proj(GLA_QK, GLA_QK).astype(BF16)
    gv_ref[...] = proj(2 * GLA_QK, GLA_V).astype(BF16)
    r_ref[...] = proj(2 * GLA_QK + GLA_V, GLA_V).astype(BF16)
    base = 2 * GLA_QK + 2 * GLA_V
    a = jnp.dot(h, wa_ref[...], preferred_element_type=F32)
    z = _bdot(a, w2_ref[...]) + gb_ref[...]
    ls = (jnp.minimum(z, 0.0) - jnp.log(1.0 + jnp.exp(-jnp.abs(z)))) * (1.0 / GLA_GATE_NORM)
    gla_ref[0] = ls[:, :GLA_QK]
    gla_ref[1] = ls[:, GLA_QK:]
    hq_ref[...] = _silu(proj(base, HG_W)).astype(BF16)
    for d in range(2):
        zf = proj(base + (1 + d) * HG_W, HG_W)
        lb = lb_ref[d:d + 1, :]
        f = lb + (1.0 - lb) * jax.nn.sigmoid(zf)
        hk_ref[d] = (1.0 - f).astype(BF16)
        hla_ref[d] = jnp.log(f)
    hv_ref[...] = proj(base + 3 * HG_W, HG_W).astype(BF16)
    hg_ref[...] = proj(base + 4 * HG_W, HG_W).astype(BF16)


def _in_proj0(xcat, mods, nctx_tiles, w_in, gate_w2, gate_b, lb):
    B, S, D = xcat.shape
    J = S // ROW_TILE
    T = B * S
    a_off = 2 * GLA_QK + 2 * GLA_V
    wmain = jnp.concatenate([w_in[:, :a_off], w_in[:, a_off + 2 * GLA_GATE_RANK:]], axis=1).astype(BF16)
    wa = jnp.pad(w_in[:, a_off:a_off + 2 * GLA_GATE_RANK], ((0, 0), (0, LANES - 2 * GLA_GATE_RANK))).astype(BF16)
    w2 = jnp.zeros((LANES, 2 * GLA_QK), F32)
    w2 = w2.at[:GLA_GATE_RANK, :GLA_QK].set(gate_w2[0]).at[GLA_GATE_RANK:2 * GLA_GATE_RANK, GLA_QK:].set(gate_w2[1])
    gb = gate_b.reshape(1, 2 * GLA_QK)
    WM = wmain.shape[1]

    def rows(w):
        return pl.BlockSpec((ROW_TILE, w), lambda b, j: (b * J + j, 0))

    def rows2(w):
        return pl.BlockSpec((2, ROW_TILE, w), lambda b, j: (0, b * J + j, 0))

    def const(shape):
        return pl.BlockSpec(shape, lambda b, j: (0,) * len(shape))

    sd = jax.ShapeDtypeStruct
    outs = pl.pallas_call(
        _in0_kernel,
        grid=(B, J),
        in_specs=[rows(D),
                  pl.BlockSpec((None, 6, D), lambda b, j: (jnp.where(j < nctx_tiles, B, b), 0, 0)),
                  const((D, WM)), const((D, LANES)), const((LANES, 2 * GLA_QK)), const((1, 2 * GLA_QK)),
                  const((2, HG_W))],
        out_specs=[rows(GLA_QK), rows(GLA_QK), rows(GLA_V), rows2(GLA_QK), rows(GLA_V),
                   rows(HG_W), rows2(HG_W), rows2(HG_W), rows(HG_W), rows(HG_W)],
        out_shape=[sd((T, GLA_QK), BF16), sd((T, GLA_QK), BF16), sd((T, GLA_V), BF16), sd((2, T, GLA_QK), F32),
                   sd((T, GLA_V), BF16), sd((T, HG_W), BF16), sd((2, T, HG_W), BF16), sd((2, T, HG_W), F32),
                   sd((T, HG_W), BF16), sd((T, HG_W), BF16)],
        compiler_params=_cparams(("parallel", "parallel")),
        name="in_proj0",
    )(xcat.reshape(T, D), mods, wmain, wa, w2, gb, lb)
    return outs


def _rec_kernel(q_ref, k_ref, v_ref, la_ref, o_ref, st_ref, *, rev, nh, dk, dv, nchunk):
    t = pl.program_id(1)

    @pl.when(t == 0)
    def _():
        st_ref[...] = jnp.zeros_like(st_ref)

    C = CHUNK
    row = lax.broadcasted_iota(I32, (C, C), 0)
    col = lax.broadcasted_iota(I32, (C, C), 1)
    incl = (col >= row) if rev else (col <= row)
    tri = jnp.where(incl, 1.0, 0.0).astype(BF16)
    ref_i = C // 2 - 1 if rev else C // 2
    last_i = 0 if rev else C - 1
    order = range(nchunk - 1, -1, -1) if rev else range(nchunk)

    def stack(x, hd):
        return jnp.stack([x[c * C:(c + 1) * C, h * hd:(h + 1) * hd] for c in range(nchunk) for h in range(nh)], 0)

    def rows(x, i):
        return jnp.concatenate([jnp.broadcast_to(x[c * C + i:c * C + i + 1], (C, x.shape[1]))
                                for c in range(nchunk)], axis=0)

    la = la_ref[...]
    q = q_ref[...].astype(F32)
    k = k_ref[...].astype(F32)
    b = jnp.concatenate([_split_dot(tri, la[c * C:(c + 1) * C]) for c in range(nchunk)], axis=0)
    b_mid = rows(b, ref_i)
    b_last = rows(b, last_i)
    v = stack(v_ref[...].astype(F32), dv)
    sc = _bmm_nt(stack(q * jnp.exp(b - b_mid), dk), stack(k * jnp.exp(b_mid - b), dk))
    o_intra = _bmm(jnp.where(incl[None], sc, 0.0), v)
    q_in = stack(q * jnp.exp(b), dk)
    kv_t = _bmm_tn(v, stack(k * jnp.exp(b_last - b), dk))
    dec = stack(jnp.exp(b_last), dk)[:, 0:1, :]
    s_t = st_ref[...]
    for c in order:
        sl = slice(c * nh, (c + 1) * nh)
        o_c = o_intra[sl] + _bmm_nt(q_in[sl], s_t)
        o_ref[c * C:(c + 1) * C, :] = jnp.concatenate([o_c[h] for h in range(nh)], axis=1)
        s_t = s_t * dec[sl] + kv_t[sl]
    st_ref[...] = s_t


def _gated_recurrence(q, k, v, la, kdir, B, S, nctx_blocks, rev, nh, dk, dv):
    T = B * S
    NB = S // ROW_TILE
    wk, wv = nh * dk, nh * dv
    assert q.shape[1] == wk
    d = 1 if rev else 0
    nc = nctx_blocks

    def blk(t):
        if not rev:
            return t
        return jnp.where(t < nc, nc - 1 - t, NB - 1 - (t - nc))

    kern = functools.partial(_rec_kernel, rev=rev, nh=nh, dk=dk, dv=dv, nchunk=ROW_TILE // CHUNK)
    return pl.pallas_call(
        kern,
        grid=(B, NB),
        in_specs=[pl.BlockSpec((ROW_TILE, wk), lambda b, t: (b * NB + blk(t), 0)),
                  pl.BlockSpec((None, ROW_TILE, wk), lambda b, t: (kdir, b * NB + blk(t), 0)),
                  pl.BlockSpec((ROW_TILE, wv), lambda b, t: (b * NB + blk(t), 0)),
                  pl.BlockSpec((None, ROW_TILE, wk), lambda b, t: (d, b * NB + blk(t), 0))],
        out_specs=pl.BlockSpec((ROW_TILE, wv), lambda b, t: (b * NB + blk(t), 0)),
        out_shape=jax.ShapeDtypeStruct((T, wv), F32),
        scratch_shapes=[pltpu.VMEM((nh, dv, dk), F32)],
        compiler_params=_cparams(("parallel", "arbitrary")),
        name="gated_rec_bwd" if rev else "gated_rec_fwd",
    )(q, k, v, la)


def _pack_rows(x):
    w = x.shape[1] // 2
    lo = lax.bitcast_convert_type(x[:, :w].astype(BF16).astype(F32), U32)
    hi = lax.bitcast_convert_type(x[:, w:].astype(BF16).astype(F32), U32)
    return (lo >> 16) | (hi & jnp.uint32(0xFFFF0000))


def _unpack_rows(p):
    lo = lax.bitcast_convert_type(p << 16, F32)
    hi = lax.bitcast_convert_type(p & jnp.uint32(0xFFFF0000), F32)
    return lo, hi


ROW_SUB = 4


def _store_row_slabs(ref, val):
    m = val.shape[0]
    for j in range(ROW_SUB):
        ref[pl.ds(j, m, stride=ROW_SUB), :] = val[:, j * LANES:(j + 1) * LANES]


def _load_row_slabs(ref, r0=0, m=None):
    m = ref.shape[0] - r0 if m is None else m
    return jnp.concatenate([ref[r0:r0 + m, j, :] for j in range(ROW_SUB)], axis=1)


def _post_mix(y, x, mod_ref, wo_ref, lng_ref, lnb_ref, alpha, xl_ref, u_ref, up_ref):
    g1 = mod_ref[2:3, :]
    sh2 = mod_ref[3:4, :]
    sc2 = mod_ref[4:5, :]
    yo = jnp.dot(y.astype(BF16), wo_ref[...], preferred_element_type=F32)
    xl = _layer_norm_rows(alpha * x + g1 * yo, lng_ref[...], lnb_ref[...], LN_EPS)
    xl_ref[...] = xl
    u = xl * (1.0 + sc2) + sh2
    u_ref[...] = u
    _store_row_slabs(up_ref, _pack_rows(u))


def _out0_kernel(gf_ref, gb_ref, hf_ref, hb_ref, r_ref, hg_ref, x_ref, mod_ref, wo_ref,
                 gg_ref, hgg_ref, lng_ref, lnb_ref, xl_ref, u_ref, up_ref, *, alpha):
    def heads(o, g, gate):
        parts = []
        for hh in range(o.shape[1] // LANES):
            seg = o[:, hh * LANES:(hh + 1) * LANES]
            ms = jnp.mean(seg * seg, axis=-1, keepdims=True)
            parts.append(seg * lax.rsqrt(ms + 1e-6) * g)
        return jnp.concatenate(parts, axis=1) * _silu(gate.astype(F32))

    y = jnp.concatenate([heads(gf_ref[...] + gb_ref[...], gg_ref[...], r_ref[...]),
                         heads(hf_ref[...] + hb_ref[...], hgg_ref[...], hg_ref[...])], axis=1)
    _post_mix(y, x_ref[...], mod_ref, wo_ref, lng_ref, lnb_ref, alpha, xl_ref, u_ref, up_ref)


def _readout0(o_gf, o_gb, o_hf, o_hb, r, hgate, xcat, mods, nctx_tiles, w_out, gla_g, hg_g, ln_g, ln_b, alpha):
    B, S, D = xcat.shape
    J = S // ROW_TILE
    T = B * S

    def rows(w):
        return pl.BlockSpec((ROW_TILE, w), lambda b, j: (b * J + j, 0))

    def const(shape):
        return pl.BlockSpec(shape, lambda b, j: (0,) * len(shape))

    sd = jax.ShapeDtypeStruct
    return pl.pallas_call(
        functools.partial(_out0_kernel, alpha=alpha),
        grid=(B, J),
        in_specs=[rows(GLA_V), rows(GLA_V), rows(HG_W), rows(HG_W), rows(GLA_V), rows(HG_W), rows(D),
                  pl.BlockSpec((None, 6, D), lambda b, j: (jnp.where(j < nctx_tiles, B, b), 0, 0)),
                  const((GLA_V + HG_W, D)), const((1, GLA_DV)), const((1, HG_EXPAND)), const((1, D)), const((1, D))],
        out_specs=[rows(D), rows(D), pl.BlockSpec((ROW_TILE * ROW_SUB, LANES), lambda b, j: (b * J + j, 0))],
        out_shape=[sd((T, D), F32), sd((T, D), F32), sd((T * ROW_SUB, LANES), U32)],
        compiler_params=_cparams(("parallel", "parallel")),
        name="readout0",
    )(o_gf, o_gb, o_hf, o_hb, r, hgate, xcat.reshape(T, D), mods, w_out.astype(BF16),
      gla_g.reshape(1, -1), hg_g.reshape(1, -1), ln_g.reshape(1, D), ln_b.reshape(1, D))


def _route_kernel(u_ref, rwh_ref, rwl_ref, rb_ref, tri_ref, eidx_ref, gate_ref, rank_ref, cnt_ref, carry_ref):
    i = pl.program_id(0)

    @pl.when(i == 0)
    def _():
        carry_ref[...] = jnp.zeros_like(carry_ref)

    u = u_ref[...]
    TM = u.shape[0]
    E = N_EXPERTS
    u_hi = u.astype(BF16)
    u_lo = (u - u_hi.astype(F32)).astype(BF16)
    nt = (((1,), (1,)), ((), ()))
    logits = (lax.dot_general(rwh_ref[...], u_hi, nt, preferred_element_type=F32)
              + lax.dot_general(rwh_ref[...], u_lo, nt, preferred_element_type=F32)
              + lax.dot_general(rwl_ref[...], u_hi, nt, preferred_element_type=F32))
    scores = jax.nn.sigmoid(logits)
    sel = scores + rb_ref[...]
    neg = -jnp.inf
    sel3 = sel.reshape(N_GROUPS, PER_GROUP, TM)
    io_g = lax.broadcasted_iota(I32, sel3.shape, 1)
    m1 = jnp.max(sel3, axis=1, keepdims=True)
    i1 = jnp.min(jnp.where(sel3 == m1, io_g, PER_GROUP), axis=1, keepdims=True)
    m2 = jnp.max(jnp.where(io_g == i1, neg, sel3), axis=1, keepdims=True)
    grp = m1 + m2
    io_n = lax.broadcasted_iota(I32, grp.shape, 0)
    keep = jnp.zeros(grp.shape, jnp.bool_)
    for _ in range(TOPK_GROUPS):
        m = jnp.max(grp, axis=0, keepdims=True)
        idx = jnp.min(jnp.where(grp == m, io_n, N_GROUPS), axis=0, keepdims=True)
        hit = io_n == idx
        keep = jnp.logical_or(keep, hit)
        grp = jnp.where(hit, neg, grp)
    sel = jnp.where(keep, sel3, neg).reshape(E, TM)
    io_e = lax.broadcasted_iota(I32, (E, TM), 0)
    base = carry_ref[...]
    tri = tri_ref[...]
    e_rows, g_rows, r_rows = [], [], []
    for _ in range(TOP_K):
        m = jnp.max(sel, axis=0, keepdims=True)
        idx = jnp.min(jnp.where(sel == m, io_e, E), axis=0, keepdims=True)
        hit = io_e == idx
        hit_f = jnp.where(hit, 1.0, 0.0)
        g_rows.append(jnp.sum(jnp.where(hit, scores, 0.0), axis=0, keepdims=True))
        prefix = jnp.dot(hit_f.astype(BF16), tri, preferred_element_type=F32)
        r_rows.append(jnp.sum(jnp.where(hit, prefix + base, 0.0), axis=0, keepdims=True))
        base = base + jnp.sum(hit_f, axis=1, keepdims=True)
        e_rows.append(idx)
        sel = jnp.where(hit, neg, sel)
    carry_ref[...] = base
    cnt_ref[...] = base
    g = jnp.concatenate(g_rows, axis=0)
    gate_ref[...] = g / jnp.sum(g, axis=0, keepdims=True) * ROUTED_SCALE
    eidx_ref[...] = jnp.concatenate(e_rows, axis=0)
    rank_ref[...] = jnp.concatenate(r_rows, axis=0).astype(I32)


def _route(u, router_w, router_bias):
    T, D = u.shape
    E = N_EXPERTS
    n = T // ROW_TILE
    tri = jnp.asarray(np.triu(np.ones((ROW_TILE, ROW_TILE), np.float32), 1), BF16)
    sd = jax.ShapeDtypeStruct
    cols = pl.BlockSpec((TOP_K, ROW_TILE), lambda i: (0, i))
    rwt = router_w.T
    rw_hi = rwt.astype(BF16)
    rw_lo = (rwt - rw_hi.astype(F32)).astype(BF16)
    return pl.pallas_call(
        _route_kernel,
        grid=(n,),
        in_specs=[pl.BlockSpec((ROW_TILE, D), lambda i: (i, 0)),
                  pl.BlockSpec((E, D), lambda i: (0, 0)),
                  pl.BlockSpec((E, D), lambda i: (0, 0)),
                  pl.BlockSpec((E, 1), lambda i: (0, 0)),
                  pl.BlockSpec((ROW_TILE, ROW_TILE), lambda i: (0, 0))],
        out_specs=[cols, cols, cols, pl.BlockSpec((E, 1), lambda i: (0, 0))],
        out_shape=[sd((TOP_K, T), I32), sd((TOP_K, T), F32), sd((TOP_K, T), I32), sd((E, 1), F32)],
        scratch_shapes=[pltpu.VMEM((E, 1), F32)],
        compiler_params=_cparams(("arbitrary",)),
        name="moe_route",
    )(u, rw_hi, rw_lo, router_bias.reshape(E, 1), tri)


def _dest_kernel(e_ref, r_ref, ps_ref, d_ref):
    e = e_ref[...]
    TM = e.shape[1]
    io_e = lax.broadcasted_iota(I32, (N_EXPERTS, TM), 0)
    ps = ps_ref[...]
    rows = []
    for k in range(TOP_K):
        rows.append(jnp.sum(jnp.where(io_e == e[k:k + 1, :], ps, 0.0), axis=0, keepdims=True))
    d_ref[...] = jnp.concatenate(rows, axis=0).astype(I32) + r_ref[...]


def _slot_of(eidx, rank, pad_start):
    T = eidx.shape[1]
    cols = pl.BlockSpec((TOP_K, ROW_TILE), lambda i: (0, i))
    return pl.pallas_call(
        _dest_kernel,
        grid=(T // ROW_TILE,),
        in_specs=[cols, cols, pl.BlockSpec((N_EXPERTS, 1), lambda i: (0, 0))],
        out_specs=cols,
        out_shape=jax.ShapeDtypeStruct((TOP_K, T), I32),
        compiler_params=_cparams(("parallel",)),
        name="moe_slot",
    )(eidx, rank, pad_start.astype(F32).reshape(N_EXPERTS, 1))


def _dispatch_kernel(z0_ref, zn_ref, dest_ref, u_ref, xs_ref, zero_scr, sem, zsem):
    TM = u_ref.shape[0]

    @pl.when(pl.program_id(0) == 0)
    def _():
        zero_scr[...] = jnp.zeros_like(zero_scr)

        def zcopy(e):
            row0 = pl.multiple_of(z0_ref[e], MOE_BLK)
            return pltpu.make_async_copy(zero_scr, xs_ref.at[pl.ds(row0, MOE_BLK)], zsem)

        def zstart(e, c):
            @pl.when(zn_ref[e] > 0)
            def _():
                zcopy(e).start()
            return c

        def zwait(e, c):
            @pl.when(zn_ref[e] > 0)
            def _():
                zcopy(e).wait()
            return c

        lax.fori_loop(0, N_EXPERTS, zstart, 0)
        lax.fori_loop(0, N_EXPERTS, zwait, 0)

    def copy(t, k):
        return pltpu.make_async_copy(u_ref.at[t], xs_ref.at[dest_ref[k, t]], sem)

    def issue(t, c):
        for k in range(TOP_K):
            copy(t, k).start(priority=k % 2)
        return c

    lax.fori_loop(0, TM, issue, 0)
    for k in range(TOP_K):
        pltpu.make_async_copy(u_ref, xs_ref.at[pl.ds(0, TM)], sem).wait()


def _dispatch(up, dest, n_slots, pad_row0, pad_rows):
    T = up.shape[0]
    slab = up.shape[1:]
    return pl.pallas_call(
        _dispatch_kernel,
        grid_spec=pltpu.PrefetchScalarGridSpec(
            num_scalar_prefetch=2,
            grid=(T // ROW_TILE,),
            in_specs=[pl.BlockSpec((TOP_K, ROW_TILE), lambda i, z0, zn: (0, i), memory_space=pltpu.SMEM),
                      pl.BlockSpec((ROW_TILE,) + slab, lambda i, z0, zn: (i, 0, 0))],
            out_specs=pl.BlockSpec(memory_space=pl.ANY),
            scratch_shapes=[pltpu.VMEM((MOE_BLK,) + slab, U32), pltpu.SemaphoreType.DMA, pltpu.SemaphoreType.DMA]),
        out_shape=jax.ShapeDtypeStruct((n_slots,) + slab, U32),
        compiler_params=_cparams(("arbitrary",)),
        name="moe_dispatch",
    )(pad_row0, pad_rows, dest, up)


def _expert_kernel(be_ref, nu_ref, first_ref, nxt_ref, par_ref, xs_ref, w13_hbm, w2_hbm,
                   ys_ref, w13_buf, w2_buf, w13_bf, w2_bf, sem, *, layer):
    i = pl.program_id(0)

    def fetch(e, slot):
        return (pltpu.make_async_copy(w13_hbm.at[layer, e], w13_buf.at[slot], sem.at[0, slot]),
                pltpu.make_async_copy(w2_hbm.at[layer, e], w2_buf.at[slot], sem.at[1, slot]))

    @pl.when(i == 0)
    def _():
        for c in fetch(be_ref[0], 0):
            c.start()

    @pl.when(jnp.logical_and(i < nu_ref[0], first_ref[i] == 1))
    def _():
        slot = par_ref[i]
        for c in fetch(be_ref[i], slot):
            c.wait()

        @pl.when(nxt_ref[i] >= 0)
        def _():
            for c in fetch(nxt_ref[i], 1 - slot):
                c.start()

        w13_bf[...] = w13_buf[slot].astype(BF16)
        w2_bf[...] = w2_buf[slot].astype(BF16)

    @pl.when(i < nu_ref[0])
    def _():
        m = xs_ref.shape[0] // ROW_SUB
        lo, hi = _unpack_rows(jnp.concatenate(
            [xs_ref[pl.ds(j, m, stride=ROW_SUB), :] for j in range(ROW_SUB)], axis=1))
        x = jnp.concatenate([lo, hi], axis=1).astype(BF16)
        h = jnp.dot(x, w13_bf[...], preferred_element_type=F32)
        a = _silu(h[:, :EXPERT_FF]) * h[:, EXPERT_FF:]
        y = _pack_rows(jnp.dot(a.astype(BF16), w2_bf[...], preferred_element_type=F32))
        for j in range(ROW_SUB):
            ys_ref[pl.ds(j, m, stride=ROW_SUB), :] = y[:, j * LANES:(j + 1) * LANES]


def _expert_gemm(xs, blk_exp, n_used, cnt, w13, w2, layer):
    NP = xs.shape[0]
    slab = xs.shape[1:]
    D = 2 * slab[0] * slab[1]
    nblk = NP // MOE_BLK
    F2 = w13.shape[3]
    E = cnt.shape[0]
    blk = jnp.arange(nblk, dtype=I32)
    first = jnp.concatenate([jnp.ones((1,), I32), (blk_exp[1:] != blk_exp[:-1]).astype(I32)])
    first = jnp.where(blk < n_used[0], first, 0)
    par = (jnp.cumsum(first) - 1) % 2
    ids = jnp.where(cnt > 0, jnp.arange(E, dtype=I32), E)
    suffix_min = lax.cummin(ids, axis=0, reverse=True)
    next_active = jnp.concatenate([suffix_min[1:], jnp.full((1,), E, I32)])
    next_active = jnp.where(next_active >= E, -1, next_active)
    nxt = jnp.take(next_active, blk_exp)

    def xmap(i, be, nu, fi, nx, pa):
        return (jnp.minimum(i, nu[0] - 1), 0)

    flat = (NP * slab[0], slab[1])
    ys = pl.pallas_call(
        functools.partial(_expert_kernel, layer=layer),
        grid_spec=pltpu.PrefetchScalarGridSpec(
            num_scalar_prefetch=5,
            grid=(nblk,),
            in_specs=[pl.BlockSpec((MOE_BLK * slab[0], slab[1]), xmap),
                      pl.BlockSpec(memory_space=pl.ANY), pl.BlockSpec(memory_space=pl.ANY)],
            out_specs=pl.BlockSpec((MOE_BLK * slab[0], slab[1]), xmap),
            scratch_shapes=[pltpu.VMEM((2, D, F2), F32), pltpu.VMEM((2, F2 // 2, D), F32),
                            pltpu.VMEM((D, F2), BF16), pltpu.VMEM((F2 // 2, D), BF16),
                            pltpu.SemaphoreType.DMA((2, 2))]),
        out_shape=jax.ShapeDtypeStruct(flat, U32),
        compiler_params=_cparams(("arbitrary",)),
        name="moe_experts",
    )(blk_exp, n_used, first, nxt.astype(I32), par.astype(I32), xs.reshape(flat), w13, w2)
    return ys.reshape((NP,) + slab)


def _combine_kernel(mt_ref, dcur_ref, dnxt_ref, gate_ref, up_ref, xl_ref, mod_ref, s13_ref, s2_ref, lng_ref, lnb_ref,
                    ys_ref, o_ref, rows_a, rows_b, sem_a, sem_b, *, alpha):
    del mt_ref
    i = pl.program_id(0)
    H = CMB_TILE
    W = ROW_SUB * LANES

    def issue(dref, col0, rows, sem):
        for t in range(H):
            for k in range(TOP_K):
                pltpu.make_async_copy(ys_ref.at[dref[k, col0 + t]], rows.at[k, t],
                                      sem).start(priority=k % 2)

    def wait(rows, sem):
        for k in range(TOP_K):
            pltpu.make_async_copy(ys_ref.at[pl.ds(0, H)], rows.at[k], sem).wait()

    def compute(rows, r0):
        lo, hi = _unpack_rows(_load_row_slabs(up_ref, r0, H))
        hs = _bdot(jnp.concatenate([lo, hi], axis=1), s13_ref[...])
        sh = _bdot(_silu(hs[:, :SHARED_FF]) * hs[:, SHARED_FF:], s2_ref[...])
        acc_lo, acc_hi = sh[:, :W], sh[:, W:]
        gate = gate_ref[r0:r0 + H]
        for k in range(TOP_K):
            lo, hi = _unpack_rows(_load_row_slabs(rows.at[k]))
            g = gate[:, k:k + 1]
            acc_lo = acc_lo + lo * g
            acc_hi = acc_hi + hi * g
        acc = jnp.concatenate([acc_lo, acc_hi], axis=1)
        g2 = mod_ref[5:6, :]
        o_ref[r0:r0 + H] = _layer_norm_rows(alpha * xl_ref[r0:r0 + H] + g2 * acc, lng_ref[...], lnb_ref[...], LN_EPS)

    @pl.when(i == 0)
    def _():
        issue(dcur_ref, 0, rows_a, sem_a)

    wait(rows_a, sem_a)
    issue(dcur_ref, H, rows_b, sem_b)
    compute(rows_a, 0)
    wait(rows_b, sem_b)
    issue(dnxt_ref, 0, rows_a, sem_a)
    compute(rows_b, H)

    @pl.when(i == pl.num_programs(0) - 1)
    def _():
        wait(rows_a, sem_a)


def _combine(dest, gate_t, up, xl, ys, mods, mod_of_tile, sh_w13, sh_w2, ln_g, ln_b, alpha):
    T, D = xl.shape
    W = D // 2
    TM = 2 * CMB_TILE
    n = T // TM

    def rows(w):
        return pl.BlockSpec((TM, w), lambda i, mt: (i, 0))

    def const(shape):
        return pl.BlockSpec(shape, lambda i, mt: (0,) * len(shape))

    return pl.pallas_call(
        functools.partial(_combine_kernel, alpha=alpha),
        grid_spec=pltpu.PrefetchScalarGridSpec(
            num_scalar_prefetch=1,
            grid=(n,),
            in_specs=[pl.BlockSpec((TOP_K, TM), lambda i, mt: (0, i), memory_space=pltpu.SMEM),
                      pl.BlockSpec((TOP_K, TM), lambda i, mt: (0, jnp.minimum(i + 1, n - 1)), memory_space=pltpu.SMEM),
                      rows(TOP_K), pl.BlockSpec((TM, ROW_SUB, LANES), lambda i, mt: (i, 0, 0)), rows(D),
                      pl.BlockSpec((None, 6, D), lambda i, mt: (mt[i], 0, 0)),
                      const((D, 2 * SHARED_FF)), const((SHARED_FF, D)), const((1, D)), const((1, D)),
                      pl.BlockSpec(memory_space=pl.ANY)],
            out_specs=rows(D),
            scratch_shapes=[pltpu.VMEM((TOP_K, CMB_TILE, ROW_SUB, LANES), U32),
                            pltpu.VMEM((TOP_K, CMB_TILE, ROW_SUB, LANES), U32),
                            pltpu.SemaphoreType.DMA, pltpu.SemaphoreType.DMA]),
        out_shape=jax.ShapeDtypeStruct((T, D), F32),
        compiler_params=_cparams(("arbitrary",)),
        name="moe_combine",
    )(mod_of_tile, dest, dest, gate_t, up, xl, mods, sh_w13.astype(BF16), sh_w2.astype(BF16),
      ln_g.reshape(1, D), ln_b.reshape(1, D), ys)


def _moe_block(u, up, xl, mods, mod_of_tile, router_w, router_bias, w13, w2, layer, sh_w13, sh_w2, ln_g, ln_b, alpha):
    T, D = u.shape
    up = up.reshape(T, ROW_SUB, LANES)
    eidx, gate, rank, counts = _route(u, router_w, router_bias)
    cnt = counts.reshape(N_EXPERTS).astype(I32)
    padded = (cnt + MOE_BLK - 1) // MOE_BLK * MOE_BLK
    pad_end = jnp.cumsum(padded)
    pad_start = pad_end - padded
    nblk = T * TOP_K // MOE_BLK + N_EXPERTS
    n_used = (pad_end[-1] // MOE_BLK).astype(I32)
    blk_row0 = jnp.arange(nblk, dtype=I32) * MOE_BLK
    blk_exp = jnp.sum((pad_end[None, :] <= blk_row0[:, None]).astype(I32), axis=1)
    last_e = jnp.max(jnp.where(cnt > 0, jnp.arange(N_EXPERTS, dtype=I32), 0))
    blk_exp = jnp.minimum(blk_exp, last_e)
    dest = _slot_of(eidx, rank, pad_start)
    xs = _dispatch(up, dest, nblk * MOE_BLK, jnp.maximum(pad_end - MOE_BLK, 0), padded)
    ys = _expert_gemm(xs, blk_exp, n_used.reshape(1), cnt, w13, w2, layer)
    return _combine(dest, gate.T, up, xl, ys, mods, mod_of_tile, sh_w13, sh_w2, ln_g, ln_b, alpha)


def _in1_kernel(x_ref, mod_ref, wh_ref, wr_ref, ph_ref, pr_ref, *, nctx_tiles):
    sh1 = mod_ref[0:1, :]
    sc1 = mod_ref[1:2, :]
    h = (x_ref[...] * (1.0 + sc1) + sh1).astype(BF16)
    pr_ref[...] = jnp.dot(h, wr_ref[...], preferred_element_type=F32)

    @pl.when(pl.program_id(1) >= nctx_tiles)
    def _():
        ph_ref[...] = jnp.dot(h, wh_ref[...], preferred_element_type=F32)


RW_PW = 1920


def _rw_reorder(t):
    a = 3 * RW_W
    lo = 2 * RW_DECAY_LORA + RW_AAA_LORA
    out = jnp.concatenate([t[..., :a], t[..., a + lo:], t[..., a:a + lo]], axis=-1)
    pad = [(0, 0)] * (t.ndim - 1) + [(0, RW_PW - out.shape[-1])]
    return jnp.pad(out, pad)


def _in_proj1(xcat, mods, nctx_tiles, w_in):
    B, S, D = xcat.shape
    J = S // ROW_TILE
    T = B * S
    wh = w_in[:, :HY_WIDTH].astype(BF16)
    wr = _rw_reorder(w_in[:, HY_WIDTH:]).astype(BF16)

    def rows(w):
        return pl.BlockSpec((ROW_TILE, w), lambda b, j: (b * J + j, 0))

    def const(shape):
        return pl.BlockSpec(shape, lambda b, j: (0,) * len(shape))

    sd = jax.ShapeDtypeStruct
    JL = J - nctx_tiles
    return pl.pallas_call(
        functools.partial(_in1_kernel, nctx_tiles=nctx_tiles),
        grid=(B, J),
        in_specs=[rows(D), pl.BlockSpec((None, 6, D), lambda b, j: (jnp.where(j < nctx_tiles, B, b), 0, 0)),
                  const((D, HY_WIDTH)), const((D, RW_PW))],
        out_specs=[pl.BlockSpec((ROW_TILE, HY_WIDTH), lambda b, j: (b * JL + jnp.maximum(j - nctx_tiles, 0), 0)),
                   rows(RW_PW)],
        out_shape=[sd((B * JL * ROW_TILE, HY_WIDTH), F32), sd((T, RW_PW), F32)],
        compiler_params=_cparams(("parallel", "arbitrary")),
        name="in_proj1",
    )(xcat.reshape(T, D), mods, wh, wr)


def _rw_streams_kernel(prev_ref, main_ref, next_ref, mu_ref, w0_ref, w2_ref, a0_ref, a2_ref, g2_ref,
                       kk_ref, ka_ref, bd_ref,
                       r_ref, k_ref, v_ref, kkn_ref, bb_ref, ld_ref, g_ref, sh_scr, *, nctx_tiles, n_lat_tiles):
    j = pl.program_id(1)
    TM = main_ref.shape[0]
    W = main_ref.shape[1]
    H = GRID_W
    p = main_ref[...]
    ext = jnp.concatenate([prev_ref[...], p, next_ref[...]], axis=0)
    left = ext[H - 1:H - 1 + TM]
    right = ext[H + 1:H + 1 + TM]
    up = ext[0:TM]
    down = ext[2 * H:2 * H + TM]
    i = lax.broadcasted_iota(I32, (TM, W), 0)
    lane = lax.broadcasted_iota(I32, (TM, W), 1)
    even = (lane & 1) == 0
    c4 = lane & 3
    jl = j - nctx_tiles

    @pl.when(j < nctx_tiles)
    def _():
        lo = jnp.where(j == 0, 1, 0)
        hi = jnp.where(j == nctx_tiles - 1, TM - 1, TM)
        sh_scr[...] = jnp.where(even, jnp.where(i >= lo, left, 0.0), jnp.where(i < hi, right, 0.0))

    @pl.when(j >= nctx_tiles)
    def _():
        col = i & (H - 1)
        up_lo = jnp.where(jl == 0, H, 0)
        down_hi = jnp.where(jl == n_lat_tiles - 1, TM - H, TM)
        l_v = jnp.where(col != 0, left, 0.0)
        r_v = jnp.where(col != H - 1, right, 0.0)
        u_v = jnp.where(i >= up_lo, up, 0.0)
        d_v = jnp.where(i < down_hi, down, 0.0)
        sh_scr[...] = jnp.where(c4 == 0, l_v, jnp.where(c4 == 1, r_v, jnp.where(c4 == 2, u_v, d_v)))

    pm = p + mu_ref[...] * (sh_scr[...] - p)
    r = pm[:, 0:RW_W]
    k = pm[:, RW_W:2 * RW_W]
    v = pm[:, 2 * RW_W:3 * RW_W]
    o = 3 * RW_W
    gl = pm[:, o:o + RW_GATE_LORA]
    o += RW_GATE_LORA
    wl_f = pm[:, o:o + RW_DECAY_LORA]
    wl_b = pm[:, o + RW_DECAY_LORA:o + 2 * RW_DECAY_LORA]
    al = pm[:, o + 2 * RW_DECAY_LORA:o + 2 * RW_DECAY_LORA + RW_AAA_LORA]
    for d, wl in enumerate((wl_f, wl_b)):
        z = w0_ref[d:d + 1, :] + _bdot(jnp.tanh(wl), w2_ref[d])
        w = -(jnp.maximum(-z, 0.0) + jnp.log(1.0 + jnp.exp(-jnp.abs(z)))) - 0.5
        ld_ref[d] = -jnp.exp(w)
    a = jax.nn.sigmoid(a0_ref[...] + _bdot(al, a2_ref[...]))
    g_ref[...] = _bdot(jax.nn.sigmoid(gl), g2_ref[...])
    kk = k * kk_ref[...]
    n2 = _head_sums(kk * kk, bd_ref[...])
    kkn = kk / jnp.maximum(jnp.sqrt(n2), 1e-12)
    r_ref[...] = r
    k_ref[...] = k * (1.0 + (a - 1.0) * ka_ref[...])
    v_ref[...] = v
    kkn_ref[...] = kkn
    bb_ref[...] = kkn * a


def _head_sum_matrix(width, hd):
    i = np.arange(width)
    return jnp.asarray((i[:, None] // hd == i[None, :] // hd).astype(np.float32), BF16)


def _rw_streams(p_rw, B, S, nctx_tiles, mu, w0, w2, a0, a2, g2, k_k, k_a):
    T = B * S
    J = S // ROW_TILE
    HB = ROW_TILE // GRID_W
    NH = S // GRID_W

    def rows(w):
        return pl.BlockSpec((ROW_TILE, w), lambda b, j: (b * J + j, 0))

    def rows2(w):
        return pl.BlockSpec((2, ROW_TILE, w), lambda b, j: (0, b * J + j, 0))

    def const(shape):
        return pl.BlockSpec(shape, lambda b, j: (0,) * len(shape))

    sd = jax.ShapeDtypeStruct
    kern = functools.partial(_rw_streams_kernel, nctx_tiles=nctx_tiles, n_lat_tiles=J - nctx_tiles)
    return pl.pallas_call(
        kern,
        grid=(B, J),
        in_specs=[pl.BlockSpec((GRID_W, RW_PW), lambda b, j: (b * NH + jnp.maximum(j * HB - 1, 0), 0)),
                  rows(RW_PW),
                  pl.BlockSpec((GRID_W, RW_PW), lambda b, j: (b * NH + jnp.minimum((j + 1) * HB, NH - 1), 0)),
                  const((1, RW_PW)), const((2, RW_W)), const((2, RW_DECAY_LORA, RW_W)), const((1, RW_W)),
                  const((RW_AAA_LORA, RW_W)), const((RW_GATE_LORA, RW_W)), const((1, RW_W)), const((1, RW_W)),
                  const((RW_W, RW_W))],
        out_specs=[rows(RW_W), rows(RW_W), rows(RW_W), rows(RW_W), rows(RW_W), rows2(RW_W), rows(RW_W)],
        out_shape=[sd((T, RW_W), F32)] * 5 + [sd((2, T, RW_W), F32), sd((T, RW_W), F32)],
        scratch_shapes=[pltpu.VMEM((ROW_TILE, RW_PW), F32)],
        compiler_params=_cparams(("parallel", "parallel")),
        name="rwkv_streams",
    )(p_rw, p_rw, p_rw, _rw_reorder(mu).reshape(1, RW_PW), w0, w2, a0.reshape(1, RW_W), a2, g2,
      k_k.reshape(1, RW_W), k_a.reshape(1, RW_W), _head_sum_matrix(RW_W, RW_HEAD_DIM))


def _rwkv_kernel(r_ref, k_ref, v_ref, kk_ref, bb_ref, ld_ref, o_ref, st_ref, *, rev, nchunk):
    t = pl.program_id(1)

    @pl.when(t == 0)
    def _():
        st_ref[...] = jnp.zeros_like(st_ref)

    C = RW_CHUNK
    hd = RW_HEAD_DIM
    NH = RW_HEADS
    row = lax.broadcasted_iota(I32, (C, C), 0)
    col = lax.broadcasted_iota(I32, (C, C), 1)
    incl = ((col >= row) if rev else (col <= row))[None]
    strict = ((col > row) if rev else (col < row))[None]
    tri = jnp.where(incl[0], 1.0, 0.0).astype(BF16)
    eye = (row == col)[None]
    last_i = 0 if rev else C - 1
    order = range(nchunk - 1, -1, -1) if rev else range(nchunk)
    n_double = int(math.log2(C)) - 1

    def stack(x):
        return jnp.stack([x[c * C:(c + 1) * C, h * hd:(h + 1) * hd] for c in range(nchunk) for h in range(NH)], 0)

    ld = ld_ref[...]
    g = jnp.concatenate([_split_dot(tri, ld[c * C:(c + 1) * C]) for c in range(nchunk)], axis=0)
    g_last = jnp.concatenate([jnp.broadcast_to(g[c * C + last_i:c * C + last_i + 1], (C, NH * hd))
                              for c in range(nchunk)], axis=0)
    k = k_ref[...]
    bb = bb_ref[...]
    eng = jnp.exp(-g)
    e_end = jnp.exp(g_last - g)
    kk_t = stack(kk_ref[...] * jnp.exp(g - ld))
    b_t = stack(bb * eng)
    k_t = stack(k * eng)
    r_t = stack(r_ref[...] * jnp.exp(g))
    b_bar = stack(bb * e_end)
    k_bar = stack(k * e_end)
    dec = stack(jnp.exp(g_last))
    v = stack(v_ref[...])
    m_b = jnp.where(strict, _bmm_nt(kk_t, b_t), 0.0)
    m_k = jnp.where(strict, _bmm_nt(kk_t, k_t), 0.0)
    a_rb = jnp.where(incl, _bmm_nt(r_t, b_t), 0.0)
    a_rk = jnp.where(incl, _bmm_nt(r_t, k_t), 0.0)
    pw = -m_b
    tinv = jnp.where(eye, 1.0, 0.0) + pw
    for _ in range(n_double):
        pw = _bmm(pw, pw)
        tinv = tinv + _bmm(tinv, pw)
    a_t = _bmm(tinv, kk_t)
    u_b = _bmm(tinv, _bmm(m_k, v))
    q_h = r_t - _bmm(a_rb, a_t)
    o_h = _bmm(a_rk, v) - _bmm(a_rb, u_b)
    g_m = jnp.where(eye, dec, 0.0) - _bmm_tn(a_t, b_bar)
    h_m = _bmm_tn(v, k_bar) - _bmm_tn(u_b, b_bar)
    s = st_ref[...]
    for c in order:
        sl = slice(c * NH, (c + 1) * NH)
        o_c = _bmm_nt(q_h[sl], s) + o_h[sl]
        o_ref[c * C:(c + 1) * C, :] = jnp.concatenate([o_c[h] for h in range(NH)], axis=1)
        s = _bmm(s, g_m[sl]) + h_m[sl]
    st_ref[...] = s


def _rwkv_scan(r, k, v, kk, bb, ld, B, S, nctx_blocks, rev):
    T = B * S
    NB = S // ROW_TILE
    d = 1 if rev else 0
    nc = nctx_blocks

    def blk(t):
        if not rev:
            return t
        return jnp.where(t < nc, nc - 1 - t, NB - 1 - (t - nc))

    spec = pl.BlockSpec((ROW_TILE, RW_W), lambda b, t: (b * NB + blk(t), 0))
    kern = functools.partial(_rwkv_kernel, rev=rev, nchunk=ROW_TILE // RW_CHUNK)
    return pl.pallas_call(
        kern,
        grid=(B, NB),
        in_specs=[spec, spec, spec, spec, spec,
                  pl.BlockSpec((None, ROW_TILE, RW_W), lambda b, t: (d, b * NB + blk(t), 0))],
        out_specs=spec,
        out_shape=jax.ShapeDtypeStruct((T, RW_W), F32),
        scratch_shapes=[pltpu.VMEM((RW_HEADS, RW_HEAD_DIM, RW_HEAD_DIM), F32)],
        compiler_params=_cparams(("parallel", "arbitrary")),
        name="rwkv_scan_bwd" if rev else "rwkv_scan_fwd",
    )(r, k, v, kk, bb, ld)


def _filter_mlp_kernel(z_ref, w1_ref, b1_ref, w2_ref, b2_ref, w3_ref, sf_ref, win_ref, f_ref):
    h = jnp.sin(sf_ref[0:1, :] * (_hdot(z_ref[...], w1_ref[...]) + b1_ref[...]))
    h = jnp.sin(sf_ref[1:2, :] * (_hdot(h, w2_ref[...]) + b2_ref[...]))
    f_ref[...] = _hdot(h, w3_ref[...]) * win_ref[...]


def _hyena_filters(L, w1, b1, w2, b2, w3, sin_freq):
    t = np.linspace(0.0, 1.0, L, dtype=np.float32)[:, None]
    bands = (HY_EMB - 1) // 2
    wpos = (2.0 * math.pi * np.arange(L, dtype=np.float32)[:, None] / L).astype(np.float32)
    fr = np.linspace(1e-4, bands - 1, bands, dtype=np.float32)[None, :]
    z = np.concatenate([t, np.cos(fr * wpos), -np.sin(fr * wpos)], -1).astype(np.float32)
    max_decay = math.log(HY_TARGET) / HY_FAST_DECAY
    min_decay = math.log(HY_TARGET) / HY_SLOW_DECAY
    deltas = np.linspace(min_decay, max_decay, HY_CH, dtype=np.float32)
    window = np.exp(-t * np.abs(deltas)).astype(np.float32)
    dist = np.concatenate([np.arange(L), [0], np.arange(L - 1, 0, -1)])
    z2 = np.pad(z[dist], ((0, 0), (0, LANES - HY_EMB)))
    win2 = np.tile(window[dist], (1, HY_ORDER))
    win2[L] = 0.0
    w1p = jnp.pad(w1, ((0, LANES - HY_EMB), (0, 0)))
    Hd = w1.shape[1]
    FW = HY_ORDER * HY_CH
    w3s = jnp.transpose(w3.reshape(Hd, HY_ORDER, 2, HY_CH), (2, 0, 1, 3)).reshape(2, Hd, FW)
    TR = min(L, 512)
    nl = L // TR
    return pl.pallas_call(
        _filter_mlp_kernel,
        grid=(2 * nl,),
        in_specs=[pl.BlockSpec((TR, LANES), lambda i: (i, 0)),
                  pl.BlockSpec((LANES, Hd), lambda i: (0, 0)), pl.BlockSpec((1, Hd), lambda i: (0, 0)),
                  pl.BlockSpec((Hd, Hd), lambda i: (0, 0)), pl.BlockSpec((1, Hd), lambda i: (0, 0)),
                  pl.BlockSpec((None, Hd, FW), lambda i: (i // nl, 0, 0)), pl.BlockSpec((2, Hd), lambda i: (0, 0)),
                  pl.BlockSpec((TR, FW), lambda i: (i, 0))],
        out_specs=pl.BlockSpec((TR, FW), lambda i: (i, 0)),
        out_shape=jax.ShapeDtypeStruct((2 * L, FW), F32),
        compiler_params=_cparams(("parallel",)),
        name="hyena_filter_mlp",
    )(jnp.asarray(z2), w1p, b1.reshape(1, Hd), w2, b2.reshape(1, Hd), w3s, sin_freq, jnp.asarray(win2))


def _sconv_kernel(p_ref, w_ref, b_ref, o_ref):
    p = p_ref[...]
    L = p.shape[0]
    i = lax.broadcasted_iota(I32, p.shape, 0)
    prev = jnp.where(i == 0, 0.0, pltpu.roll(p, 1, 0))
    nxt = jnp.where(i == L - 1, 0.0, pltpu.roll(p, L - 1, 0))
    o_ref[...] = prev * w_ref[0:1, :] + p * w_ref[1:2, :] + nxt * w_ref[2:3, :] + b_ref[...]


def _short_conv(p_hy, B, S, Lc, conv_w, conv_b):
    L = S - Lc
    p3 = p_hy.reshape(B, L, HY_WIDTH)
    return pl.pallas_call(
        _sconv_kernel,
        grid=(B, HY_WIDTH // LANES),
        in_specs=[pl.BlockSpec((None, L, LANES), lambda b, c: (b, 0, c)),
                  pl.BlockSpec((HY_SHORT, LANES), lambda b, c: (0, c)),
                  pl.BlockSpec((1, LANES), lambda b, c: (0, c))],
        out_specs=pl.BlockSpec((None, L, LANES), lambda b, c: (b, 0, c)),
        out_shape=jax.ShapeDtypeStruct((B, L, HY_WIDTH), F32),
        compiler_params=_cparams(("parallel", "parallel")),
        name="hyena_short_conv",
    )(p3, conv_w, conv_b.reshape(1, HY_WIDTH))


def _dft_constants(L):
    n = 2 * L
    n2 = FFT_N2
    n1 = n // n2
    na = n1 // 2
    g8 = SUBLANES
    w1 = np.exp(-2j * np.pi * np.outer(np.arange(n1), np.arange(n1)) / n1)
    eye8 = np.eye(g8)

    def kron_fwd(w, real_in):
        k1n, an = w.shape
        blocks = np.stack([np.stack([w.real, -w.imag], 1), np.stack([w.imag, w.real], 1)], 1)
        if real_in:
            blocks = blocks[:, :, 0:1, :]
        m = np.einsum('kria,bc->krbiac', blocks, eye8)
        return m.reshape(k1n * 2 * g8, blocks.shape[2] * an * g8)

    m1 = kron_fwd(w1[:, :na], False)
    m1f = kron_fwd(w1, True)
    cw = np.conj(w1[:, :na]).T / n
    blocks = np.stack([np.stack([cw.real, -cw.imag], 1), np.stack([cw.imag, cw.real], 1)], 0)
    m1inv = np.einsum('iark,bc->iabkrc', blocks, eye8).reshape(2 * na * g8, n1 * 2 * g8)
    w2 = np.exp(-2j * np.pi * np.outer(np.arange(n2), np.arange(n2)) / n2)
    w2big = np.block([[w2.real, -w2.imag], [w2.imag, w2.real]])
    w2c = np.conj(w2)
    iw2big = np.block([[w2c.real, -w2c.imag], [w2c.imag, w2c.real]])
    tw = np.exp(-2j * np.pi * np.arange(n2) / n)

    def lanes(cols):
        z = np.concatenate([np.broadcast_to(col[:, None], (n2, LANES)) for col in cols], axis=1)
        return np.stack([z.real, z.imag], 0)

    tw0 = lanes([tw ** q for q in range(FFT_G)])
    twg = lanes([tw ** FFT_G] * FFT_G)
    c = lambda x, dt: jnp.asarray(np.ascontiguousarray(x), dt)
    return dict(m1=c(m1, BF16), m1f=c(m1f, BF16), m1inv=c(m1inv, BF16), w2=c(w2big, BF16), iw2=c(iw2big, BF16),
                tw0=c(tw0, F32), twg=c(twg, F32), n1=n1, na=na)


def _cplx_rows(z, twr, twi, conj):
    n2 = z.shape[0] // 2
    zr, zi = z[:n2], z[n2:]
    if conj:
        return jnp.concatenate([zr * twr + zi * twi, zi * twr - zr * twi], axis=0)
    return jnp.concatenate([zr * twr - zi * twi, zi * twr + zr * twi], axis=0)


def _split_dot(m, x):
    hi = x.astype(BF16)
    lo = (x - hi.astype(F32)).astype(BF16)
    return jnp.dot(m, hi, preferred_element_type=F32) + jnp.dot(m, lo, preferred_element_type=F32)


def _next_twiddle(tw, twg_ref):
    twr, twi = tw
    b_r, b_i = twg_ref[0], twg_ref[1]
    return twr * b_r - twi * b_i, twr * b_i + twi * b_r


def _load_group(scr, j, C):
    return jnp.concatenate([scr[j * FFT_G + q].reshape(2 * FFT_N2, C) for q in range(FFT_G)], axis=1)


def _filt_fft_kernel(f_ref, m1_ref, w2_ref, tw0_ref, twg_ref, o_ref, y_scr, *, n1):
    nbg = FFT_N2 // SUBLANES
    C = f_ref.shape[-1]
    for bg in range(nbg):
        yg = _split_dot(m1_ref[...], f_ref[:, bg].reshape(n1 * SUBLANES, C))
        y_scr[:, :, bg] = yg.reshape(n1, 2, SUBLANES, C)

    def body(j, tw):
        y = _cplx_rows(_load_group(y_scr, j, C), tw[0], tw[1], False)
        z = _split_dot(w2_ref[...], y)
        for q in range(FFT_G):
            o_ref[j * FFT_G + q] = z[:, q * C:(q + 1) * C]
        return _next_twiddle(tw, twg_ref)

    lax.fori_loop(0, n1 // FFT_G, body, (tw0_ref[0], tw0_ref[1]))


def _filter_spectrum(filt, L, dc):
    n1 = dc['n1']
    nbg = FFT_N2 // SUBLANES
    ncb = HY_CH // LANES
    full = filt.reshape(n1, nbg, SUBLANES, HY_ORDER * HY_CH)
    return pl.pallas_call(
        functools.partial(_filt_fft_kernel, n1=n1),
        grid=(HY_ORDER, ncb),
        in_specs=[pl.BlockSpec((n1, nbg, SUBLANES, LANES), lambda o, c: (0, 0, 0, o * ncb + c)),
                  pl.BlockSpec(dc['m1f'].shape, lambda o, c: (0, 0)),
                  pl.BlockSpec(dc['w2'].shape, lambda o, c: (0, 0)),
                  pl.BlockSpec(dc['tw0'].shape, lambda o, c: (0, 0, 0)),
                  pl.BlockSpec(dc['twg'].shape, lambda o, c: (0, 0, 0))],
        out_specs=pl.BlockSpec((None, n1, 2 * FFT_N2, LANES), lambda o, c: (o, 0, 0, c)),
        out_shape=jax.ShapeDtypeStruct((HY_ORDER, n1, 2 * FFT_N2, HY_CH), F32),
        scratch_shapes=[pltpu.VMEM((n1, 2, nbg, SUBLANES, LANES), F32)],
        compiler_params=_cparams(("parallel", "parallel")),
        name="hyena_filter_fft",
    )(full, dc['m1f'], dc['w2'], dc['tw0'], dc['twg'])


def _hyconv_kernel(za_ref, zb_ref, ga_ref, gb_ref, ff_ref, bias_ref, m1_ref, m1i_ref, w2_ref, iw2_ref,
                   tw0_ref, twg_ref, oa_ref, ob_ref, y_scr, *, n1, na):
    nbg = FFT_N2 // SUBLANES
    C = za_ref.shape[-1]
    half = na * SUBLANES
    for bg in range(nbg):
        xg = jnp.concatenate([za_ref[:, bg].reshape(half, C), zb_ref[:, bg].reshape(half, C)], axis=0)
        yg = jnp.dot(m1_ref[...], xg.astype(BF16), preferred_element_type=F32)
        y_scr[:, :, bg] = yg.reshape(n1, 2, SUBLANES, C)

    def body(j, tw):
        y = _cplx_rows(_load_group(y_scr, j, C), tw[0], tw[1], False)
        z = jnp.dot(w2_ref[...], y.astype(BF16), preferred_element_type=F32)
        f = _load_group(ff_ref, j, C)
        zr, zi = z[:FFT_N2], z[FFT_N2:]
        fr, fi = f[:FFT_N2], f[FFT_N2:]
        p = jnp.concatenate([zr * fr - zi * fi, zr * fi + zi * fr], axis=0)
        v = jnp.dot(iw2_ref[...], p.astype(BF16), preferred_element_type=F32)
        v = _cplx_rows(v, tw[0], tw[1], True)
        for q in range(FFT_G):
            y_scr[j * FFT_G + q] = v[:, q * C:(q + 1) * C].reshape(2, nbg, SUBLANES, C)
        return _next_twiddle(tw, twg_ref)

    lax.fori_loop(0, n1 // FFT_G, body, (tw0_ref[0], tw0_ref[1]))
    bias = bias_ref[...]
    for bg in range(nbg):
        vg = y_scr[:, :, bg].reshape(n1 * 2 * SUBLANES, C)
        out = jnp.dot(m1i_ref[...], vg.astype(BF16), preferred_element_type=F32)
        ya = out[:half].reshape(na, SUBLANES, C)
        yb = out[half:].reshape(na, SUBLANES, C)
        za = za_ref[:, bg]
        zb = zb_ref[:, bg]
        oa_ref[:, bg] = ga_ref[:, bg] * (ya + za * bias)
        ob_ref[:, bg] = gb_ref[:, bg] * (yb + zb * bias)


def _hyena_conv(z, z_col0, gates, g_col0, spec, bias, dc, B, L):
    n1, na = dc['n1'], dc['na']
    nbg = FFT_N2 // SUBLANES
    NCB = HY_CH // LANES

    def view(a):
        return a.reshape(B, na, nbg, SUBLANES, a.shape[-1])

    def seq(col0, which):
        return pl.BlockSpec((None, na, nbg, SUBLANES, LANES),
                            lambda c, p: (2 * p + which, 0, 0, 0, col0 + c))

    def const(a):
        nd = a.ndim
        return pl.BlockSpec(a.shape, lambda c, p: (0,) * nd)

    out_a, out_b = pl.pallas_call(
        functools.partial(_hyconv_kernel, n1=n1, na=na),
        grid=(NCB, B // 2),
        in_specs=[seq(z_col0, 0), seq(z_col0, 1), seq(g_col0, 0), seq(g_col0, 1),
                  pl.BlockSpec((n1, 2 * FFT_N2, LANES), lambda c, p: (0, 0, c), pipeline_mode=pl.Buffered(1)),
                  pl.BlockSpec((1, LANES), lambda c, p: (0, c)),
                  const(dc['m1']), const(dc['m1inv']), const(dc['w2']), const(dc['iw2']),
                  const(dc['tw0']), const(dc['twg'])],
        out_specs=[pl.BlockSpec((None, na, nbg, SUBLANES, LANES), lambda c, p: (p, 0, 0, 0, c)),
                   pl.BlockSpec((None, na, nbg, SUBLANES, LANES), lambda c, p: (p, 0, 0, 0, c))],
        out_shape=[jax.ShapeDtypeStruct((B // 2, na, nbg, SUBLANES, HY_CH), F32)] * 2,
        scratch_shapes=[pltpu.VMEM((n1, 2, nbg, SUBLANES, LANES), F32)],
        compiler_params=_cparams(("parallel", "arbitrary")),
        name="hyena_long_conv",
    )(view(z), view(z), view(gates), view(gates), spec, bias.reshape(1, HY_CH),
      dc['m1'], dc['m1inv'], dc['w2'], dc['iw2'], dc['tw0'], dc['twg'])
    out = jnp.stack([out_a, out_b], axis=1)
    return out.reshape(B, L, HY_CH)


def _out1_kernel(hy_ref, of_ref, ob_ref, r_ref, k_ref, v_ref, g_ref, x_ref, mod_ref, wo_ref,
                 rk_ref, lg_ref, lb_ref, bd_ref, lng_ref, lnb_ref, xl_ref, u_ref, up_ref, *, alpha):
    o = of_ref[...] + ob_ref[...]
    bd = bd_ref[...]
    inv = 1.0 / RW_HEAD_DIM
    mu = _head_sums(o, bd) * inv
    oc = o - mu
    var = _head_sums(oc * oc, bd) * inv
    on = oc * lax.rsqrt(var + RW_GN_EPS) * lg_ref[...] + lb_ref[...]
    bonus = _head_sums(r_ref[...] * k_ref[...] * rk_ref[...], bd) * v_ref[...]
    y = jnp.concatenate([hy_ref[...], (on + bonus) * g_ref[...]], axis=1)
    _post_mix(y, x_ref[...], mod_ref, wo_ref, lng_ref, lnb_ref, alpha, xl_ref, u_ref, up_ref)


def _readout1(hy, o_f, o_b, r, k, v, g, xcat, mods, B, S, Lc, w_out, r_k, ln_g, ln_b, dn_g, dn_b, alpha):
    D = xcat.shape[-1]
    L = S - Lc
    J = S // ROW_TILE
    JL = L // ROW_TILE
    JC = Lc // ROW_TILE

    def cat_rows(w):
        return pl.BlockSpec((ROW_TILE, w), lambda b, j: (b * J + JC + j, 0))

    def lat_rows(w):
        return pl.BlockSpec((ROW_TILE, w), lambda b, j: (b * JL + j, 0))

    def const(shape):
        return pl.BlockSpec(shape, lambda b, j: (0,) * len(shape))

    sd = jax.ShapeDtypeStruct
    return pl.pallas_call(
        functools.partial(_out1_kernel, alpha=alpha),
        grid=(B, JL),
        in_specs=[lat_rows(HY_CH), cat_rows(RW_W), cat_rows(RW_W), cat_rows(RW_W), cat_rows(RW_W), cat_rows(RW_W),
                  cat_rows(RW_W), cat_rows(D),
                  pl.BlockSpec((None, 6, D), lambda b, j: (b, 0, 0)),
                  const((HY_CH + RW_W, D)), const((1, RW_W)), const((1, RW_W)), const((1, RW_W)),
                  const((RW_W, RW_W)), const((1, D)), const((1, D))],
        out_specs=[lat_rows(D), lat_rows(D),
                   pl.BlockSpec((ROW_TILE * ROW_SUB, LANES), lambda b, j: (b * JL + j, 0))],
        out_shape=[sd((B * L, D), F32), sd((B * L, D), F32), sd((B * L * ROW_SUB, LANES), U32)],
        compiler_params=_cparams(("parallel", "parallel")),
        name="readout1",
    )(hy, o_f, o_b, r, k, v, g, xcat.reshape(B * S, D), mods, w_out.astype(BF16),
      r_k.reshape(1, RW_W), ln_g.reshape(1, RW_W), ln_b.reshape(1, RW_W),
      _head_sum_matrix(RW_W, RW_HEAD_DIM), dn_g.reshape(1, D), dn_b.reshape(1, D))


def kernel(x, c, ctx, c_ctx, mod_w, mod_b, ln1_g, ln1_b, ln2_g, ln2_b, ev_w_in, ev_w_out, gla_gate_w2, gla_gate_b, gla_norm_g, hg_lb_logits, hg_norm_g, od_w_in, od_w_out, hy_conv_w, hy_conv_b, hy_ffn_w1, hy_ffn_b1, hy_ffn_w2, hy_ffn_b2, hy_ffn_w3, hy_sin_freq, hy_bias, rw_mu, rw_w0, rw_w2, rw_a0, rw_a2, rw_g2, rw_k_k, rw_k_a, rw_r_k, rw_ln_g, rw_ln_b, router_w, router_bias, exp_w13, exp_w2, sh_w13, sh_w2):
    B, L, D = x.shape
    Lc = ctx.shape[1]
    S = Lc + L
    depth = mod_w.shape[0]
    assert depth == 2 and L % ROW_TILE == 0 and Lc % ROW_TILE == 0 and B % 2 == 0
    assert L % (FFT_N2 * 2) == 0 and L % GRID_W == 0
    alpha = (2 * depth) ** 0.25
    nctx = Lc // ROW_TILE
    J = S // ROW_TILE
    T = B * S

    cc = jnp.concatenate([c, c_ctx[None, :]], axis=0)
    cc = jnp.pad(cc, ((0, (-cc.shape[0]) % SUBLANES), (0, 0)))
    hg_lb = jnp.cumsum(jax.nn.softmax(hg_lb_logits.astype(F32), axis=0), axis=0)
    xcat = jnp.concatenate([ctx, x], axis=1)

    mods = _modulation(cc, mod_w[0], mod_b[0])
    gq, gk, gv, gla, r, hq, hk, hla, hv, hgate = _in_proj0(xcat, mods, nctx, ev_w_in[0], gla_gate_w2[0],
                                                            gla_gate_b[0], hg_lb[0])
    gk3 = gk.reshape(1, T, GLA_QK)
    o_gf = _gated_recurrence(gq, gk3, gv, gla, 0, B, S, nctx, False, GLA_HEADS, GLA_DK, GLA_DV)
    o_gb = _gated_recurrence(gq, gk3, gv, gla, 0, B, S, nctx, True, GLA_HEADS, GLA_DK, GLA_DV)
    o_hf = _gated_recurrence(hq, hk, hv, hla, 0, B, S, nctx, False, HG_HEADS, HG_EXPAND, HG_EXPAND)
    o_hb = _gated_recurrence(hq, hk, hv, hla, 1, B, S, nctx, True, HG_HEADS, HG_EXPAND, HG_EXPAND)
    xl, u, up = _readout0(o_gf, o_gb, o_hf, o_hb, r, hgate, xcat, mods, nctx, ev_w_out[0], gla_norm_g[0],
                          hg_norm_g[0], ln1_g[0], ln1_b[0], alpha)
    cmb_rows = 2 * CMB_TILE
    tiles_per_b = S // cmb_rows
    tile_in_b = jnp.arange(B * tiles_per_b, dtype=I32) % tiles_per_b
    mod_of_tile = jnp.where(tile_in_b < Lc // cmb_rows, B, jnp.arange(B * tiles_per_b, dtype=I32) // tiles_per_b)
    x1 = _moe_block(u, up, xl, mods, mod_of_tile.astype(I32), router_w[0], router_bias[0], exp_w13, exp_w2, 0,
                    sh_w13[0], sh_w2[0], ln2_g[0], ln2_b[0], alpha)
    xcat = x1.reshape(B, S, D)

    mods = _modulation(cc, mod_w[1], mod_b[1])
    p_hy, p_rw = _in_proj1(xcat, mods, nctx, od_w_in[0])
    rr, rk, rv, rkk, rbb, rld, rg = _rw_streams(p_rw, B, S, nctx, rw_mu[0], rw_w0[0], rw_w2[0], rw_a0[0],
                                                 rw_a2[0], rw_g2[0], rw_k_k[0], rw_k_a[0])
    o_f = _rwkv_scan(rr, rk, rv, rkk, rbb, rld, B, S, nctx, False)
    o_b = _rwkv_scan(rr, rk, rv, rkk, rbb, rld, B, S, nctx, True)
    dc = _dft_constants(L)
    filt = _hyena_filters(L, hy_ffn_w1[0], hy_ffn_b1[0], hy_ffn_w2[0], hy_ffn_b2[0], hy_ffn_w3[0], hy_sin_freq[0])
    spec = _filter_spectrum(filt, L, dc)
    uu = _short_conv(p_hy, B, S, Lc, hy_conv_w[0], hy_conv_b[0])
    NCB = HY_CH // LANES
    z1 = _hyena_conv(uu, 0, uu, NCB, spec[0], hy_bias[0, 0], dc, B, L)
    z2 = _hyena_conv(z1, 0, uu, 2 * NCB, spec[1], hy_bias[0, 1], dc, B, L)
    xl, u, up = _readout1(z2.reshape(B * L, HY_CH), o_f, o_b, rr, rk, rv, rg, xcat, mods, B, S, Lc, od_w_out[0],
                          rw_r_k[0], rw_ln_g[0], rw_ln_b[0], ln1_g[1], ln1_b[1], alpha)
    mod_of_tile = (jnp.arange(B * (L // cmb_rows), dtype=I32) // (L // cmb_rows)).astype(I32)
    out = _moe_block(u, up, xl, mods, mod_of_tile, router_w[1], router_bias[1], exp_w13, exp_w2, 1,
                     sh_w13[1], sh_w2[1], ln2_g[1], ln2_b[1], alpha)
    return out.reshape(B, L, D)
```

```python
import functools
import math

import numpy as np
import jax
import jax.numpy as jnp
from jax import lax
from jax.experimental import pallas as pl
from jax.experimental.pallas import tpu as pltpu

F32 = jnp.float32
BF16 = jnp.bfloat16
I32 = jnp.int32
U32 = jnp.uint32
HI = lax.Precision.HIGHEST

LN_EPS = 1e-5
GLA_HEADS, GLA_DK, GLA_DV = 4, 64, 128
GLA_QK, GLA_V = GLA_HEADS * GLA_DK, GLA_HEADS * GLA_DV
GLA_GATE_RANK = 16
GLA_GATE_NORM = 16.0
CHUNK = 64
HG_HEADS, HG_EXPAND = 4, 128
HG_W = HG_HEADS * HG_EXPAND
HY_CH, HY_ORDER, HY_SHORT, HY_EMB = 512, 2, 3, 33
HY_WIDTH = (HY_ORDER + 1) * HY_CH
HY_FAST_DECAY, HY_SLOW_DECAY, HY_TARGET = 0.3, 1.5, 1e-2
RW_HEADS, RW_HEAD_DIM = 8, 64
RW_W = RW_HEADS * RW_HEAD_DIM
RW_DECAY_LORA, RW_AAA_LORA, RW_GATE_LORA = 64, 64, 128
RW_GN_EPS = 64e-5
GRID_W = 64
N_EXPERTS, TOP_K, N_GROUPS, TOPK_GROUPS = 256, 8, 8, 4
PER_GROUP = N_EXPERTS // N_GROUPS
EXPERT_FF, SHARED_FF = 256, 256
ROUTED_SCALE = 2.5

LANES = 128
SUBLANES = 8
VMEM_LIMIT = 56 * 1024 * 1024
ROW_TILE = 256
MOE_BLK = 256
CMB_TILE = 128
RW_CHUNK = 64
FFT_N2 = 128
FFT_G = 4


def _cparams(sem):
    return pltpu.CompilerParams(dimension_semantics=sem, vmem_limit_bytes=VMEM_LIMIT)


def _bdot(a, b):
    return jnp.dot(a.astype(BF16), b.astype(BF16), preferred_element_type=F32)


def _bdot_nt(a, b):
    return lax.dot_general(a.astype(BF16), b.astype(BF16), (((1,), (1,)), ((), ())),
                           preferred_element_type=F32)


def _bdot_tn(a, b):
    return lax.dot_general(a.astype(BF16), b.astype(BF16), (((0,), (0,)), ((), ())),
                           preferred_element_type=F32)


def _bmm(a, b):
    return jnp.einsum('nij,njk->nik', a.astype(BF16), b.astype(BF16), preferred_element_type=F32)


def _bmm_nt(a, b):
    return jnp.einsum('nik,njk->nij', a.astype(BF16), b.astype(BF16), preferred_element_type=F32)


def _bmm_tn(a, b):
    return _bmm(jnp.swapaxes(a, 1, 2), b)


def _hdot(a, b):
    return jnp.dot(a, b, precision=HI, preferred_element_type=F32)


def _head_sums(x, ones_bd):
    hi = x.astype(BF16)
    lo = (x - hi.astype(F32)).astype(BF16)
    return (jnp.dot(hi, ones_bd, preferred_element_type=F32) + jnp.dot(lo, ones_bd, preferred_element_type=F32))


def _silu(x):
    return x * jax.nn.sigmoid(x)


def _layer_norm_rows(x, g, b, eps):
    mu = jnp.mean(x, axis=-1, keepdims=True)
    xc = x - mu
    var = jnp.mean(xc * xc, axis=-1, keepdims=True)
    return xc * lax.rsqrt(var + eps) * g + b


def _mod_kernel(c_ref, w_ref, b_ref, o_ref):
    o_ref[...] = _hdot(_silu(c_ref[...]), w_ref[...]) + b_ref[...]


def _modulation(cc, w, b):
    R, D = cc.shape
    out = pl.pallas_call(
        _mod_kernel,
        grid=(6,),
        in_specs=[pl.BlockSpec((R, D), lambda j: (0, 0)),
                  pl.BlockSpec((D, D), lambda j: (0, j)),
                  pl.BlockSpec((1, D), lambda j: (0, j))],
        out_specs=pl.BlockSpec((R, D), lambda j: (0, j)),
        out_shape=jax.ShapeDtypeStruct((R, 6 * D), F32),
        compiler_params=_cparams(("parallel",)),
        name="modulation",
    )(cc, w, b.reshape(1, 6 * D))
    return out.reshape(R, 6, D)


def _in0_kernel(x_ref, mod_ref, w_ref, wa_ref, w2_ref, gb_ref, lb_ref,
                gq_ref, gk_ref, gv_ref, gla_ref, r_ref, hq_ref, hk_ref, hla_ref, hv_ref, hg_ref):
    sh1 = mod_ref[0:1, :]
    sc1 = mod_ref[1:2, :]
    h = (x_ref[...] * (1.0 + sc1) + sh1).astype(BF16)

    def proj(off, width):
        return jnp.dot(h, w_ref[:, off:off + width], preferred_element_type=F32)

    gq_ref[...] = (proj(0, GLA_QK) * (GLA_DK ** -0.5)).astype(BF16)
    gk_ref[...] = proj(GLA_QK, GLA_QK).astype(BF16)
    gv_ref[...] = proj(2 * GLA_QK, GLA_V).astype(BF16)
    r_ref[...] = proj(2 * GLA_QK + GLA_V, GLA_V).astype(BF16)
    base = 2 * GLA_QK + 2 * GLA_V
    a = jnp.dot(h, wa_ref[...], preferred_element_type=F32)
    z = _bdot(a, w2_ref[...]) + gb_ref[...]
    ls = (jnp.minimum(z, 0.0) - jnp.log(1.0 + jnp.exp(-jnp.abs(z)))) * (1.0 / GLA_GATE_NORM)
    gla_ref[0] = ls[:, :GLA_QK]
    gla_ref[1] = ls[:, GLA_QK:]
    hq_ref[...] = _silu(proj(base, HG_W)).astype(BF16)
    for d in range(2):
        zf = proj(base + (1 + d) * HG_W, HG_W)
        lb = lb_ref[d:d + 1, :]
        f = lb + (1.0 - lb) * jax.nn.sigmoid(zf)
        hk_ref[d] = (1.0 - f).astype(BF16)
        hla_ref[d] = jnp.log(f)
    hv_ref[...] = proj(base + 3 * HG_W, HG_W).astype(BF16)
    hg_ref[...] = proj(base + 4 * HG_W, HG_W).astype(BF16)


def _in_proj0(xcat, mods, nctx_tiles, w_in, gate_w2, gate_b, lb):
    B, S, D = xcat.shape
    J = S // ROW_TILE
    T = B * S
    a_off = 2 * GLA_QK + 2 * GLA_V
    wmain = jnp.concatenate([w_in[:, :a_off], w_in[:, a_off + 2 * GLA_GATE_RANK:]], axis=1).astype(BF16)
    wa = jnp.pad(w_in[:, a_off:a_off + 2 * GLA_GATE_RANK], ((0, 0), (0, LANES - 2 * GLA_GATE_RANK))).astype(BF16)
    w2 = jnp.zeros((LANES, 2 * GLA_QK), F32)
    w2 = w2.at[:GLA_GATE_RANK, :GLA_QK].set(gate_w2[0]).at[GLA_GATE_RANK:2 * GLA_GATE_RANK, GLA_QK:].set(gate_w2[1])
    gb = gate_b.reshape(1, 2 * GLA_QK)
    WM = wmain.shape[1]

    def rows(w):
        return pl.BlockSpec((ROW_TILE, w), lambda b, j: (b * J + j, 0))

    def rows2(w):
        return pl.BlockSpec((2, ROW_TILE, w), lambda b, j: (0, b * J + j, 0))

    def const(shape):
        return pl.BlockSpec(shape, lambda b, j: (0,) * len(shape))

    sd = jax.ShapeDtypeStruct
    outs = pl.pallas_call(
        _in0_kernel,
        grid=(B, J),
        in_specs=[rows(D),
                  pl.BlockSpec((None, 6, D), lambda b, j: (jnp.where(j < nctx_tiles, B, b), 0, 0)),
                  const((D, WM)), const((D, LANES)), const((LANES, 2 * GLA_QK)), const((1, 2 * GLA_QK)),
                  const((2, HG_W))],
        out_specs=[rows(GLA_QK), rows(GLA_QK), rows(GLA_V), rows2(GLA_QK), rows(GLA_V),
                   rows(HG_W), rows2(HG_W), rows2(HG_W), rows(HG_W), rows(HG_W)],
        out_shape=[sd((T, GLA_QK), BF16), sd((T, GLA_QK), BF16), sd((T, GLA_V), BF16), sd((2, T, GLA_QK), F32),
                   sd((T, GLA_V), BF16), sd((T, HG_W), BF16), sd((2, T, HG_W), BF16), sd((2, T, HG_W), F32),
                   sd((T, HG_W), BF16), sd((T, HG_W), BF16)],
        compiler_params=_cparams(("parallel", "parallel")),
        name="in_proj0",
    )(xcat.reshape(T, D), mods, wmain, wa, w2, gb, lb)
    return outs


def _rec_kernel(q_ref, k_ref, v_ref, la_ref, o_ref, st_ref, *, rev, nh, dk, dv, nchunk):
    t = pl.program_id(1)

    @pl.when(t == 0)
    def _():
        st_ref[...] = jnp.zeros_like(st_ref)

    C = CHUNK
    row = lax.broadcasted_iota(I32, (C, C), 0)
    col = lax.broadcasted_iota(I32, (C, C), 1)
    incl = (col >= row) if rev else (col <= row)
    tri = jnp.where(incl, 1.0, 0.0).astype(BF16)
    ref_i = C // 2 - 1 if rev else C // 2
    last_i = 0 if rev else C - 1
    order = range(nchunk - 1, -1, -1) if rev else range(nchunk)

    def stack(x, hd):
        return jnp.stack([x[c * C:(c + 1) * C, h * hd:(h + 1) * hd] for c in range(nchunk) for h in range(nh)], 0)

    def rows(x, i):
        return jnp.concatenate([jnp.broadcast_to(x[c * C + i:c * C + i + 1], (C, x.shape[1]))
                                for c in range(nchunk)], axis=0)

    la = la_ref[...]
    q = q_ref[...].astype(F32)
    k = k_ref[...].astype(F32)
    b = jnp.concatenate([_split_dot(tri, la[c * C:(c + 1) * C]) for c in range(nchunk)], axis=0)
    b_mid = rows(b, ref_i)
    b_last = rows(b, last_i)
    v = stack(v_ref[...].astype(F32), dv)
    sc = _bmm_nt(stack(q * jnp.exp(b - b_mid), dk), stack(k * jnp.exp(b_mid - b), dk))
    o_intra = _bmm(jnp.where(incl[None], sc, 0.0), v)
    q_in = stack(q * jnp.exp(b), dk)
    kv_t = _bmm_tn(v, stack(k * jnp.exp(b_last - b), dk))
    dec = stack(jnp.exp(b_last), dk)[:, 0:1, :]
    s_t = st_ref[...]
    for c in order:
        sl = slice(c * nh, (c + 1) * nh)
        o_c = o_intra[sl] + _bmm_nt(q_in[sl], s_t)
        o_ref[c * C:(c + 1) * C, :] = jnp.concatenate([o_c[h] for h in range(nh)], axis=1)
        s_t = s_t * dec[sl] + kv_t[sl]
    st_ref[...] = s_t


def _gated_recurrence(q, k, v, la, kdir, B, S, nctx_blocks, rev, nh, dk, dv):
    T = B * S
    NB = S // ROW_TILE
    wk, wv = nh * dk, nh * dv
    assert q.shape[1] == wk
    d = 1 if rev else 0
    nc = nctx_blocks

    def blk(t):
        if not rev:
            return t
        return jnp.where(t < nc, nc - 1 - t, NB - 1 - (t - nc))

    kern = functools.partial(_rec_kernel, rev=rev, nh=nh, dk=dk, dv=dv, nchunk=ROW_TILE // CHUNK)
    return pl.pallas_call(
        kern,
        grid=(B, NB),
        in_specs=[pl.BlockSpec((ROW_TILE, wk), lambda b, t: (b * NB + blk(t), 0)),
                  pl.BlockSpec((None, ROW_TILE, wk), lambda b, t: (kdir, b * NB + blk(t), 0)),
                  pl.BlockSpec((ROW_TILE, wv), lambda b, t: (b * NB + blk(t), 0)),
                  pl.BlockSpec((None, ROW_TILE, wk), lambda b, t: (d, b * NB + blk(t), 0))],
        out_specs=pl.BlockSpec((ROW_TILE, wv), lambda b, t: (b * NB + blk(t), 0)),
        out_shape=jax.ShapeDtypeStruct((T, wv), F32),
        scratch_shapes=[pltpu.VMEM((nh, dv, dk), F32)],
        compiler_params=_cparams(("parallel", "arbitrary")),
        name="gated_rec_bwd" if rev else "gated_rec_fwd",
    )(q, k, v, la)


def _pack_rows(x):
    w = x.shape[1] // 2
    lo = lax.bitcast_convert_type(x[:, :w].astype(BF16).astype(F32), U32)
    hi = lax.bitcast_convert_type(x[:, w:].astype(BF16).astype(F32), U32)
    return (lo >> 16) | (hi & jnp.uint32(0xFFFF0000))


def _unpack_rows(p):
    lo = lax.bitcast_convert_type(p << 16, F32)
    hi = lax.bitcast_convert_type(p & jnp.uint32(0xFFFF0000), F32)
    return lo, hi


ROW_SUB = 4


def _store_row_slabs(ref, val):
    m = val.shape[0]
    for j in range(ROW_SUB):
        ref[pl.ds(j, m, stride=ROW_SUB), :] = val[:, j * LANES:(j + 1) * LANES]


def _load_row_slabs(ref, r0=0, m=None):
    m = ref.shape[0] // ROW_SUB - r0 if m is None else m
    return jnp.concatenate([ref[pl.ds(r0 * ROW_SUB + j, m, stride=ROW_SUB), :] for j in range(ROW_SUB)], axis=1)


def _post_mix(y, x, mod_ref, wo_ref, lng_ref, lnb_ref, alpha, xl_ref, u_ref, up_ref):
    g1 = mod_ref[2:3, :]
    sh2 = mod_ref[3:4, :]
    sc2 = mod_ref[4:5, :]
    yo = jnp.dot(y.astype(BF16), wo_ref[...], preferred_element_type=F32)
    xl = _layer_norm_rows(alpha * x + g1 * yo, lng_ref[...], lnb_ref[...], LN_EPS)
    xl_ref[...] = xl
    u = xl * (1.0 + sc2) + sh2
    u_ref[...] = u
    _store_row_slabs(up_ref, _pack_rows(u))


def _out0_kernel(gf_ref, gb_ref, hf_ref, hb_ref, r_ref, hg_ref, x_ref, mod_ref, wo_ref,
                 gg_ref, hgg_ref, lng_ref, lnb_ref, xl_ref, u_ref, up_ref, *, alpha):
    def heads(o, g, gate):
        parts = []
        for hh in range(o.shape[1] // LANES):
            seg = o[:, hh * LANES:(hh + 1) * LANES]
            ms = jnp.mean(seg * seg, axis=-1, keepdims=True)
            parts.append(seg * lax.rsqrt(ms + 1e-6) * g)
        return jnp.concatenate(parts, axis=1) * _silu(gate.astype(F32))

    y = jnp.concatenate([heads(gf_ref[...] + gb_ref[...], gg_ref[...], r_ref[...]),
                         heads(hf_ref[...] + hb_ref[...], hgg_ref[...], hg_ref[...])], axis=1)
    _post_mix(y, x_ref[...], mod_ref, wo_ref, lng_ref, lnb_ref, alpha, xl_ref, u_ref, up_ref)


def _readout0(o_gf, o_gb, o_hf, o_hb, r, hgate, xcat, mods, nctx_tiles, w_out, gla_g, hg_g, ln_g, ln_b, alpha):
    B, S, D = xcat.shape
    J = S // ROW_TILE
    T = B * S

    def rows(w):
        return pl.BlockSpec((ROW_TILE, w), lambda b, j: (b * J + j, 0))

    def const(shape):
        return pl.BlockSpec(shape, lambda b, j: (0,) * len(shape))

    sd = jax.ShapeDtypeStruct
    return pl.pallas_call(
        functools.partial(_out0_kernel, alpha=alpha),
        grid=(B, J),
        in_specs=[rows(GLA_V), rows(GLA_V), rows(HG_W), rows(HG_W), rows(GLA_V), rows(HG_W), rows(D),
                  pl.BlockSpec((None, 6, D), lambda b, j: (jnp.where(j < nctx_tiles, B, b), 0, 0)),
                  const((GLA_V + HG_W, D)), const((1, GLA_DV)), const((1, HG_EXPAND)), const((1, D)), const((1, D))],
        out_specs=[rows(D), rows(D), pl.BlockSpec((ROW_TILE * ROW_SUB, LANES), lambda b, j: (b * J + j, 0))],
        out_shape=[sd((T, D), F32), sd((T, D), F32), sd((T * ROW_SUB, LANES), U32)],
        compiler_params=_cparams(("parallel", "parallel")),
        name="readout0",
    )(o_gf, o_gb, o_hf, o_hb, r, hgate, xcat.reshape(T, D), mods, w_out.astype(BF16),
      gla_g.reshape(1, -1), hg_g.reshape(1, -1), ln_g.reshape(1, D), ln_b.reshape(1, D))


def _route_kernel(u_ref, rwh_ref, rwl_ref, rb_ref, tri_ref, eidx_ref, gate_ref, rank_ref, cnt_ref, carry_ref):
    i = pl.program_id(0)

    @pl.when(i == 0)
    def _():
        carry_ref[...] = jnp.zeros_like(carry_ref)

    u = u_ref[...]
    TM = u.shape[0]
    E = N_EXPERTS
    u_hi = u.astype(BF16)
    u_lo = (u - u_hi.astype(F32)).astype(BF16)
    nt = (((1,), (1,)), ((), ()))
    logits = (lax.dot_general(rwh_ref[...], u_hi, nt, preferred_element_type=F32)
              + lax.dot_general(rwh_ref[...], u_lo, nt, preferred_element_type=F32)
              + lax.dot_general(rwl_ref[...], u_hi, nt, preferred_element_type=F32))
    scores = jax.nn.sigmoid(logits)
    sel = scores + rb_ref[...]
    neg = -jnp.inf
    sel3 = sel.reshape(N_GROUPS, PER_GROUP, TM)
    io_g = lax.broadcasted_iota(I32, sel3.shape, 1)
    m1 = jnp.max(sel3, axis=1, keepdims=True)
    i1 = jnp.min(jnp.where(sel3 == m1, io_g, PER_GROUP), axis=1, keepdims=True)
    m2 = jnp.max(jnp.where(io_g == i1, neg, sel3), axis=1, keepdims=True)
    grp = m1 + m2
    io_n = lax.broadcasted_iota(I32, grp.shape, 0)
    keep = jnp.zeros(grp.shape, jnp.bool_)
    for _ in range(TOPK_GROUPS):
        m = jnp.max(grp, axis=0, keepdims=True)
        idx = jnp.min(jnp.where(grp == m, io_n, N_GROUPS), axis=0, keepdims=True)
        hit = io_n == idx
        keep = jnp.logical_or(keep, hit)
        grp = jnp.where(hit, neg, grp)
    sel = jnp.where(keep, sel3, neg).reshape(E, TM)
    io_e = lax.broadcasted_iota(I32, (E, TM), 0)
    base = carry_ref[...]
    tri = tri_ref[...]
    e_rows, g_rows, r_rows = [], [], []
    for _ in range(TOP_K):
        m = jnp.max(sel, axis=0, keepdims=True)
        idx = jnp.min(jnp.where(sel == m, io_e, E), axis=0, keepdims=True)
        hit = io_e == idx
        hit_f = jnp.where(hit, 1.0, 0.0)
        g_rows.append(jnp.sum(jnp.where(hit, scores, 0.0), axis=0, keepdims=True))
        prefix = jnp.dot(hit_f.astype(BF16), tri, preferred_element_type=F32)
        r_rows.append(jnp.sum(jnp.where(hit, prefix + base, 0.0), axis=0, keepdims=True))
        base = base + jnp.sum(hit_f, axis=1, keepdims=True)
        e_rows.append(idx)
        sel = jnp.where(hit, neg, sel)
    carry_ref[...] = base
    cnt_ref[...] = base
    g = jnp.concatenate(g_rows, axis=0)
    gate_ref[...] = g / jnp.sum(g, axis=0, keepdims=True) * ROUTED_SCALE
    eidx_ref[...] = jnp.concatenate(e_rows, axis=0)
    rank_ref[...] = jnp.concatenate(r_rows, axis=0).astype(I32)


def _route(u, router_w, router_bias):
    T, D = u.shape
    E = N_EXPERTS
    n = T // ROW_TILE
    tri = jnp.asarray(np.triu(np.ones((ROW_TILE, ROW_TILE), np.float32), 1), BF16)
    sd = jax.ShapeDtypeStruct
    cols = pl.BlockSpec((TOP_K, ROW_TILE), lambda i: (0, i))
    rwt = router_w.T
    rw_hi = rwt.astype(BF16)
    rw_lo = (rwt - rw_hi.astype(F32)).astype(BF16)
    return pl.pallas_call(
        _route_kernel,
        grid=(n,),
        in_specs=[pl.BlockSpec((ROW_TILE, D), lambda i: (i, 0)),
                  pl.BlockSpec((E, D), lambda i: (0, 0)),
                  pl.BlockSpec((E, D), lambda i: (0, 0)),
                  pl.BlockSpec((E, 1), lambda i: (0, 0)),
                  pl.BlockSpec((ROW_TILE, ROW_TILE), lambda i: (0, 0))],
        out_specs=[cols, cols, cols, pl.BlockSpec((E, 1), lambda i: (0, 0))],
        out_shape=[sd((TOP_K, T), I32), sd((TOP_K, T), F32), sd((TOP_K, T), I32), sd((E, 1), F32)],
        scratch_shapes=[pltpu.VMEM((E, 1), F32)],
        compiler_params=_cparams(("arbitrary",)),
        name="moe_route",
    )(u, rw_hi, rw_lo, router_bias.reshape(E, 1), tri)


def _dest_kernel(e_ref, r_ref, ps_ref, d_ref):
    e = e_ref[...]
    TM = e.shape[1]
    io_e = lax.broadcasted_iota(I32, (N_EXPERTS, TM), 0)
    ps = ps_ref[...]
    rows = []
    for k in range(TOP_K):
        rows.append(jnp.sum(jnp.where(io_e == e[k:k + 1, :], ps, 0.0), axis=0, keepdims=True))
    d_ref[...] = jnp.concatenate(rows, axis=0).astype(I32) + r_ref[...]


def _slot_of(eidx, rank, pad_start):
    T = eidx.shape[1]
    cols = pl.BlockSpec((TOP_K, ROW_TILE), lambda i: (0, i))
    return pl.pallas_call(
        _dest_kernel,
        grid=(T // ROW_TILE,),
        in_specs=[cols, cols, pl.BlockSpec((N_EXPERTS, 1), lambda i: (0, 0))],
        out_specs=cols,
        out_shape=jax.ShapeDtypeStruct((TOP_K, T), I32),
        compiler_params=_cparams(("parallel",)),
        name="moe_slot",
    )(eidx, rank, pad_start.astype(F32).reshape(N_EXPERTS, 1))


def _dispatch_kernel(z0_ref, zn_ref, dest_ref, u_ref, xs_ref, zero_scr, sem, zsem):
    TM = u_ref.shape[0]

    @pl.when(pl.program_id(0) == 0)
    def _():
        zero_scr[...] = jnp.zeros_like(zero_scr)

        def zcopy(e):
            row0 = pl.multiple_of(z0_ref[e], MOE_BLK)
            return pltpu.make_async_copy(zero_scr, xs_ref.at[pl.ds(row0, MOE_BLK)], zsem)

        def zstart(e, c):
            @pl.when(zn_ref[e] > 0)
            def _():
                zcopy(e).start()
            return c

        def zwait(e, c):
            @pl.when(zn_ref[e] > 0)
            def _():
                zcopy(e).wait()
            return c

        lax.fori_loop(0, N_EXPERTS, zstart, 0)
        lax.fori_loop(0, N_EXPERTS, zwait, 0)

    def copy(t, k):
        return pltpu.make_async_copy(u_ref.at[t], xs_ref.at[dest_ref[k, t]], sem)

    def issue(t, c):
        for k in range(TOP_K):
            copy(t, k).start(priority=k % 2)
        return c

    lax.fori_loop(0, TM, issue, 0)
    for k in range(TOP_K):
        pltpu.make_async_copy(u_ref, xs_ref.at[pl.ds(0, TM)], sem).wait()


def _dispatch(up, dest, n_slots, pad_row0, pad_rows):
    T = up.shape[0]
    slab = up.shape[1:]
    return pl.pallas_call(
        _dispatch_kernel,
        grid_spec=pltpu.PrefetchScalarGridSpec(
            num_scalar_prefetch=2,
            grid=(T // ROW_TILE,),
            in_specs=[pl.BlockSpec((TOP_K, ROW_TILE), lambda i, z0, zn: (0, i), memory_space=pltpu.SMEM),
                      pl.BlockSpec((ROW_TILE,) + slab, lambda i, z0, zn: (i, 0, 0))],
            out_specs=pl.BlockSpec(memory_space=pl.ANY),
            scratch_shapes=[pltpu.VMEM((MOE_BLK,) + slab, U32), pltpu.SemaphoreType.DMA, pltpu.SemaphoreType.DMA]),
        out_shape=jax.ShapeDtypeStruct((n_slots,) + slab, U32),
        compiler_params=_cparams(("arbitrary",)),
        name="moe_dispatch",
    )(pad_row0, pad_rows, dest, up)


def _expert_kernel(be_ref, nu_ref, first_ref, nxt_ref, par_ref, xs_ref, w13_hbm, w2_hbm,
                   ys_ref, w13_buf, w2_buf, w13_bf, w2_bf, sem, *, layer):
    i = pl.program_id(0)

    def fetch(e, slot):
        return (pltpu.make_async_copy(w13_hbm.at[layer, e], w13_buf.at[slot], sem.at[0, slot]),
                pltpu.make_async_copy(w2_hbm.at[layer, e], w2_buf.at[slot], sem.at[1, slot]))

    @pl.when(i == 0)
    def _():
        for c in fetch(be_ref[0], 0):
            c.start()

    @pl.when(jnp.logical_and(i < nu_ref[0], first_ref[i] == 1))
    def _():
        slot = par_ref[i]
        for c in fetch(be_ref[i], slot):
            c.wait()

        @pl.when(nxt_ref[i] >= 0)
        def _():
            for c in fetch(nxt_ref[i], 1 - slot):
                c.start()

        w13_bf[...] = w13_buf[slot].astype(BF16)
        w2_bf[...] = w2_buf[slot].astype(BF16)

    @pl.when(i < nu_ref[0])
    def _():
        m = xs_ref.shape[0] // ROW_SUB
        lo, hi = _unpack_rows(jnp.concatenate(
            [xs_ref[pl.ds(j, m, stride=ROW_SUB), :] for j in range(ROW_SUB)], axis=1))
        x = jnp.concatenate([lo, hi], axis=1).astype(BF16)
        h = jnp.dot(x, w13_bf[...], preferred_element_type=F32)
        a = _silu(h[:, :EXPERT_FF]) * h[:, EXPERT_FF:]
        y = _pack_rows(jnp.dot(a.astype(BF16), w2_bf[...], preferred_element_type=F32))
        for j in range(ROW_SUB):
            ys_ref[pl.ds(j, m, stride=ROW_SUB), :] = y[:, j * LANES:(j + 1) * LANES]


def _expert_gemm(xs, blk_exp, n_used, cnt, w13, w2, layer):
    NP = xs.shape[0]
    slab = xs.shape[1:]
    D = 2 * slab[0] * slab[1]
    nblk = NP // MOE_BLK
    F2 = w13.shape[3]
    E = cnt.shape[0]
    blk = jnp.arange(nblk, dtype=I32)
    first = jnp.concatenate([jnp.ones((1,), I32), (blk_exp[1:] != blk_exp[:-1]).astype(I32)])
    first = jnp.where(blk < n_used[0], first, 0)
    par = (jnp.cumsum(first) - 1) % 2
    ids = jnp.where(cnt > 0, jnp.arange(E, dtype=I32), E)
    suffix_min = lax.cummin(ids, axis=0, reverse=True)
    next_active = jnp.concatenate([suffix_min[1:], jnp.full((1,), E, I32)])
    next_active = jnp.where(next_active >= E, -1, next_active)
    nxt = jnp.take(next_active, blk_exp)

    def xmap(i, be, nu, fi, nx, pa):
        return (jnp.minimum(i, nu[0] - 1), 0)

    flat = (NP * slab[0], slab[1])
    ys = pl.pallas_call(
        functools.partial(_expert_kernel, layer=layer),
        grid_spec=pltpu.PrefetchScalarGridSpec(
            num_scalar_prefetch=5,
            grid=(nblk,),
            in_specs=[pl.BlockSpec((MOE_BLK * slab[0], slab[1]), xmap),
                      pl.BlockSpec(memory_space=pl.ANY), pl.BlockSpec(memory_space=pl.ANY)],
            out_specs=pl.BlockSpec((MOE_BLK * slab[0], slab[1]), xmap),
            scratch_shapes=[pltpu.VMEM((2, D, F2), F32), pltpu.VMEM((2, F2 // 2, D), F32),
                            pltpu.VMEM((D, F2), BF16), pltpu.VMEM((F2 // 2, D), BF16),
                            pltpu.SemaphoreType.DMA((2, 2))]),
        out_shape=jax.ShapeDtypeStruct(flat, U32),
        compiler_params=_cparams(("arbitrary",)),
        name="moe_experts",
    )(blk_exp, n_used, first, nxt.astype(I32), par.astype(I32), xs.reshape(flat), w13, w2)
    return ys.reshape((NP,) + slab)


def _combine_kernel(mt_ref, dcur_ref, dnxt_ref, gate_ref, up_ref, xl_ref, mod_ref, s13_ref, s2_ref, lng_ref, lnb_ref,
                    ys_ref, o_ref, rows_a, rows_b, rows_c, rows_d, sem_a, sem_b, sem_c, sem_d, *, alpha, n_steps):
    del mt_ref
    i = pl.program_id(0)
    H = CMB_TILE
    W = ROW_SUB * LANES

    def issue(dref, col0, rows, sem):
        for t in range(H):
            for k in range(TOP_K):
                pltpu.make_async_copy(ys_ref.at[dref[k, col0 + t]], rows.at[k, pl.ds(t * ROW_SUB, ROW_SUB)],
                                      sem).start(priority=k % 2)

    def wait(rows, sem):
        for k in range(TOP_K):
            pltpu.make_async_copy(rows.at[k], rows.at[k], sem).wait()

    def compute(rows, r0):
        lo, hi = _unpack_rows(_load_row_slabs(up_ref, r0, H))
        hs = _bdot(jnp.concatenate([lo, hi], axis=1), s13_ref[...])
        sh = _bdot(_silu(hs[:, :SHARED_FF]) * hs[:, SHARED_FF:], s2_ref[...])
        acc_lo, acc_hi = sh[:, :W], sh[:, W:]
        gate = gate_ref[r0:r0 + H]
        for k in range(TOP_K):
            lo, hi = _unpack_rows(_load_row_slabs(rows.at[k]))
            g = gate[:, k:k + 1]
            acc_lo = acc_lo + lo * g
            acc_hi = acc_hi + hi * g
        acc = jnp.concatenate([acc_lo, acc_hi], axis=1)
        g2 = mod_ref[5:6, :]
        o_ref[r0:r0 + H] = _layer_norm_rows(alpha * xl_ref[r0:r0 + H] + g2 * acc, lng_ref[...], lnb_ref[...], LN_EPS)

    pair0 = ((rows_a, sem_a), (rows_b, sem_b))
    pair1 = ((rows_c, sem_c), (rows_d, sem_d))

    @pl.when(i == 0)
    def _():
        issue(dcur_ref, 0, *pair0[0])
        issue(dcur_ref, H, *pair0[1])

    def step(cur, nxt):
        wait(*cur[0])
        issue(dnxt_ref, 0, *nxt[0])
        compute(cur[0][0], 0)
        wait(*cur[1])
        issue(dnxt_ref, H, *nxt[1])
        compute(cur[1][0], H)

    @pl.when(i % 2 == 0)
    def _():
        step(pair0, pair1)

    @pl.when(i % 2 == 1)
    def _():
        step(pair1, pair0)

    @pl.when(i == n_steps - 1)
    def _():
        for rows, sem in (pair1 if (n_steps - 1) % 2 == 0 else pair0):
            wait(rows, sem)


def _combine(dest, gate_t, up, xl, ys, mods, mod_of_tile, sh_w13, sh_w2, ln_g, ln_b, alpha):
    T, D = xl.shape
    TM = 2 * CMB_TILE
    n = T // TM

    def rows(w):
        return pl.BlockSpec((TM, w), lambda i, mt: (i, 0))

    def const(shape):
        return pl.BlockSpec(shape, lambda i, mt: (0,) * len(shape))

    return pl.pallas_call(
        functools.partial(_combine_kernel, alpha=alpha, n_steps=n),
        grid_spec=pltpu.PrefetchScalarGridSpec(
            num_scalar_prefetch=1,
            grid=(n,),
            in_specs=[pl.BlockSpec((TOP_K, TM), lambda i, mt: (0, i), memory_space=pltpu.SMEM),
                      pl.BlockSpec((TOP_K, TM), lambda i, mt: (0, jnp.minimum(i + 1, n - 1)), memory_space=pltpu.SMEM),
                      rows(TOP_K), pl.BlockSpec((TM * ROW_SUB, LANES), lambda i, mt: (i, 0)), rows(D),
                      pl.BlockSpec((None, 6, D), lambda i, mt: (mt[i], 0, 0)),
                      const((D, 2 * SHARED_FF)), const((SHARED_FF, D)), const((1, D)), const((1, D)),
                      pl.BlockSpec(memory_space=pl.ANY)],
            out_specs=rows(D),
            scratch_shapes=[pltpu.VMEM((TOP_K, CMB_TILE * ROW_SUB, LANES), U32)] * 4
                           + [pltpu.SemaphoreType.DMA] * 4),
        out_shape=jax.ShapeDtypeStruct((T, D), F32),
        compiler_params=_cparams(("arbitrary",)),
        name="moe_combine",
    )(mod_of_tile, dest, dest, gate_t, up, xl, mods, sh_w13.astype(BF16), sh_w2.astype(BF16),
      ln_g.reshape(1, D), ln_b.reshape(1, D), ys)


def _moe_block(u, up, xl, mods, mod_of_tile, router_w, router_bias, w13, w2, layer, sh_w13, sh_w2, ln_g, ln_b, alpha):
    T, D = u.shape
    up = up.reshape(T, ROW_SUB, LANES)
    eidx, gate, rank, counts = _route(u, router_w, router_bias)
    cnt = counts.reshape(N_EXPERTS).astype(I32)
    padded = (cnt + MOE_BLK - 1) // MOE_BLK * MOE_BLK
    pad_end = jnp.cumsum(padded)
    pad_start = pad_end - padded
    nblk = T * TOP_K // MOE_BLK + N_EXPERTS
    n_used = (pad_end[-1] // MOE_BLK).astype(I32)
    blk_row0 = jnp.arange(nblk, dtype=I32) * MOE_BLK
    blk_exp = jnp.sum((pad_end[None, :] <= blk_row0[:, None]).astype(I32), axis=1)
    last_e = jnp.max(jnp.where(cnt > 0, jnp.arange(N_EXPERTS, dtype=I32), 0))
    blk_exp = jnp.minimum(blk_exp, last_e)
    dest = _slot_of(eidx, rank, pad_start)
    xs = _dispatch(up, dest, nblk * MOE_BLK, jnp.maximum(pad_end - MOE_BLK, 0), padded)
    ys = _expert_gemm(xs, blk_exp, n_used.reshape(1), cnt, w13, w2, layer)
    return _combine(dest, gate.T, up.reshape(T * ROW_SUB, LANES), xl, ys, mods, mod_of_tile, sh_w13, sh_w2,
                    ln_g, ln_b, alpha)


def _in1_kernel(x_ref, mod_ref, wh_ref, wr_ref, ph_ref, pr_ref, *, nctx_tiles):
    sh1 = mod_ref[0:1, :]
    sc1 = mod_ref[1:2, :]
    h = (x_ref[...] * (1.0 + sc1) + sh1).astype(BF16)
    pr_ref[...] = jnp.dot(h, wr_ref[...], preferred_element_type=F32)

    @pl.when(pl.program_id(1) >= nctx_tiles)
    def _():
        ph_ref[...] = jnp.dot(h, wh_ref[...], preferred_element_type=F32)


RW_PW = 1920


def _rw_reorder(t):
    a = 3 * RW_W
    lo = 2 * RW_DECAY_LORA + RW_AAA_LORA
    out = jnp.concatenate([t[..., :a], t[..., a + lo:], t[..., a:a + lo]], axis=-1)
    pad = [(0, 0)] * (t.ndim - 1) + [(0, RW_PW - out.shape[-1])]
    return jnp.pad(out, pad)


def _in_proj1(xcat, mods, nctx_tiles, w_in):
    B, S, D = xcat.shape
    J = S // ROW_TILE
    T = B * S
    wh = w_in[:, :HY_WIDTH].astype(BF16)
    wr = _rw_reorder(w_in[:, HY_WIDTH:]).astype(BF16)

    def rows(w):
        return pl.BlockSpec((ROW_TILE, w), lambda b, j: (b * J + j, 0))

    def const(shape):
        return pl.BlockSpec(shape, lambda b, j: (0,) * len(shape))

    sd = jax.ShapeDtypeStruct
    JL = J - nctx_tiles
    return pl.pallas_call(
        functools.partial(_in1_kernel, nctx_tiles=nctx_tiles),
        grid=(B, J),
        in_specs=[rows(D), pl.BlockSpec((None, 6, D), lambda b, j: (jnp.where(j < nctx_tiles, B, b), 0, 0)),
                  const((D, HY_WIDTH)), const((D, RW_PW))],
        out_specs=[pl.BlockSpec((ROW_TILE, HY_WIDTH), lambda b, j: (b * JL + jnp.maximum(j - nctx_tiles, 0), 0)),
                   rows(RW_PW)],
        out_shape=[sd((B * JL * ROW_TILE, HY_WIDTH), F32), sd((T, RW_PW), F32)],
        compiler_params=_cparams(("parallel", "arbitrary")),
        name="in_proj1",
    )(xcat.reshape(T, D), mods, wh, wr)


def _rw_streams_kernel(prev_ref, main_ref, next_ref, mu_ref, w0_ref, w2_ref, a0_ref, a2_ref, g2_ref,
                       kk_ref, ka_ref, bd_ref,
                       r_ref, k_ref, v_ref, kkn_ref, bb_ref, ld_ref, g_ref, sh_scr, *, nctx_tiles, n_lat_tiles):
    j = pl.program_id(1)
    TM = main_ref.shape[0]
    W = main_ref.shape[1]
    H = GRID_W
    p = main_ref[...]
    ext = jnp.concatenate([prev_ref[...], p, next_ref[...]], axis=0)
    left = ext[H - 1:H - 1 + TM]
    right = ext[H + 1:H + 1 + TM]
    up = ext[0:TM]
    down = ext[2 * H:2 * H + TM]
    i = lax.broadcasted_iota(I32, (TM, W), 0)
    lane = lax.broadcasted_iota(I32, (TM, W), 1)
    even = (lane & 1) == 0
    c4 = lane & 3
    jl = j - nctx_tiles

    @pl.when(j < nctx_tiles)
    def _():
        lo = jnp.where(j == 0, 1, 0)
        hi = jnp.where(j == nctx_tiles - 1, TM - 1, TM)
        sh_scr[...] = jnp.where(even, jnp.where(i >= lo, left, 0.0), jnp.where(i < hi, right, 0.0))

    @pl.when(j >= nctx_tiles)
    def _():
        col = i & (H - 1)
        up_lo = jnp.where(jl == 0, H, 0)
        down_hi = jnp.where(jl == n_lat_tiles - 1, TM - H, TM)
        l_v = jnp.where(col != 0, left, 0.0)
        r_v = jnp.where(col != H - 1, right, 0.0)
        u_v = jnp.where(i >= up_lo, up, 0.0)
        d_v = jnp.where(i < down_hi, down, 0.0)
        sh_scr[...] = jnp.where(c4 == 0, l_v, jnp.where(c4 == 1, r_v, jnp.where(c4 == 2, u_v, d_v)))

    pm = p + mu_ref[...] * (sh_scr[...] - p)
    r = pm[:, 0:RW_W]
    k = pm[:, RW_W:2 * RW_W]
    v = pm[:, 2 * RW_W:3 * RW_W]
    o = 3 * RW_W
    gl = pm[:, o:o + RW_GATE_LORA]
    o += RW_GATE_LORA
    wl_f = pm[:, o:o + RW_DECAY_LORA]
    wl_b = pm[:, o + RW_DECAY_LORA:o + 2 * RW_DECAY_LORA]
    al = pm[:, o + 2 * RW_DECAY_LORA:o + 2 * RW_DECAY_LORA + RW_AAA_LORA]
    for d, wl in enumerate((wl_f, wl_b)):
        z = w0_ref[d:d + 1, :] + _bdot(jnp.tanh(wl), w2_ref[d])
        w = -(jnp.maximum(-z, 0.0) + jnp.log(1.0 + jnp.exp(-jnp.abs(z)))) - 0.5
        ld_ref[d] = -jnp.exp(w)
    a = jax.nn.sigmoid(a0_ref[...] + _bdot(al, a2_ref[...]))
    g_ref[...] = _bdot(jax.nn.sigmoid(gl), g2_ref[...])
    kk = k * kk_ref[...]
    n2 = _head_sums(kk * kk, bd_ref[...])
    kkn = kk / jnp.maximum(jnp.sqrt(n2), 1e-12)
    r_ref[...] = r
    k_ref[...] = k * (1.0 + (a - 1.0) * ka_ref[...])
    v_ref[...] = v
    kkn_ref[...] = kkn
    bb_ref[...] = kkn * a


def _head_sum_matrix(width, hd):
    i = np.arange(width)
    return jnp.asarray((i[:, None] // hd == i[None, :] // hd).astype(np.float32), BF16)


def _rw_streams(p_rw, B, S, nctx_tiles, mu, w0, w2, a0, a2, g2, k_k, k_a):
    T = B * S
    J = S // ROW_TILE
    HB = ROW_TILE // GRID_W
    NH = S // GRID_W

    def rows(w):
        return pl.BlockSpec((ROW_TILE, w), lambda b, j: (b * J + j, 0))

    def rows2(w):
        return pl.BlockSpec((2, ROW_TILE, w), lambda b, j: (0, b * J + j, 0))

    def const(shape):
        return pl.BlockSpec(shape, lambda b, j: (0,) * len(shape))

    sd = jax.ShapeDtypeStruct
    kern = functools.partial(_rw_streams_kernel, nctx_tiles=nctx_tiles, n_lat_tiles=J - nctx_tiles)
    return pl.pallas_call(
        kern,
        grid=(B, J),
        in_specs=[pl.BlockSpec((GRID_W, RW_PW), lambda b, j: (b * NH + jnp.maximum(j * HB - 1, 0), 0)),
                  rows(RW_PW),
                  pl.BlockSpec((GRID_W, RW_PW), lambda b, j: (b * NH + jnp.minimum((j + 1) * HB, NH - 1), 0)),
                  const((1, RW_PW)), const((2, RW_W)), const((2, RW_DECAY_LORA, RW_W)), const((1, RW_W)),
                  const((RW_AAA_LORA, RW_W)), const((RW_GATE_LORA, RW_W)), const((1, RW_W)), const((1, RW_W)),
                  const((RW_W, RW_W))],
        out_specs=[rows(RW_W), rows(RW_W), rows(RW_W), rows(RW_W), rows(RW_W), rows2(RW_W), rows(RW_W)],
        out_shape=[sd((T, RW_W), F32)] * 5 + [sd((2, T, RW_W), F32), sd((T, RW_W), F32)],
        scratch_shapes=[pltpu.VMEM((ROW_TILE, RW_PW), F32)],
        compiler_params=_cparams(("parallel", "parallel")),
        name="rwkv_streams",
    )(p_rw, p_rw, p_rw, _rw_reorder(mu).reshape(1, RW_PW), w0, w2, a0.reshape(1, RW_W), a2, g2,
      k_k.reshape(1, RW_W), k_a.reshape(1, RW_W), _head_sum_matrix(RW_W, RW_HEAD_DIM))


def _rwkv_kernel(r_ref, k_ref, v_ref, kk_ref, bb_ref, ld_ref, o_ref, st_ref, *, rev, nchunk):
    t = pl.program_id(1)

    @pl.when(t == 0)
    def _():
        st_ref[...] = jnp.zeros_like(st_ref)

    C = RW_CHUNK
    hd = RW_HEAD_DIM
    NH = RW_HEADS
    row = lax.broadcasted_iota(I32, (C, C), 0)
    col = lax.broadcasted_iota(I32, (C, C), 1)
    incl = ((col >= row) if rev else (col <= row))[None]
    strict = ((col > row) if rev else (col < row))[None]
    tri = jnp.where(incl[0], 1.0, 0.0).astype(BF16)
    eye = (row == col)[None]
    last_i = 0 if rev else C - 1
    order = range(nchunk - 1, -1, -1) if rev else range(nchunk)
    n_double = int(math.log2(C)) - 1

    def stack(x):
        return jnp.stack([x[c * C:(c + 1) * C, h * hd:(h + 1) * hd] for c in range(nchunk) for h in range(NH)], 0)

    ld = ld_ref[...]
    g = jnp.concatenate([_split_dot(tri, ld[c * C:(c + 1) * C]) for c in range(nchunk)], axis=0)
    g_last = jnp.concatenate([jnp.broadcast_to(g[c * C + last_i:c * C + last_i + 1], (C, NH * hd))
                              for c in range(nchunk)], axis=0)
    k = k_ref[...]
    bb = bb_ref[...]
    eng = jnp.exp(-g)
    e_end = jnp.exp(g_last - g)
    kk_t = stack(kk_ref[...] * jnp.exp(g - ld))
    b_t = stack(bb * eng)
    k_t = stack(k * eng)
    r_t = stack(r_ref[...] * jnp.exp(g))
    b_bar = stack(bb * e_end)
    k_bar = stack(k * e_end)
    dec = stack(jnp.exp(g_last))
    v = stack(v_ref[...])
    m_b = jnp.where(strict, _bmm_nt(kk_t, b_t), 0.0)
    m_k = jnp.where(strict, _bmm_nt(kk_t, k_t), 0.0)
    a_rb = jnp.where(incl, _bmm_nt(r_t, b_t), 0.0)
    a_rk = jnp.where(incl, _bmm_nt(r_t, k_t), 0.0)
    pw = -m_b
    tinv = jnp.where(eye, 1.0, 0.0) + pw
    for _ in range(n_double):
        pw = _bmm(pw, pw)
        tinv = tinv + _bmm(tinv, pw)
    a_t = _bmm(tinv, kk_t)
    u_b = _bmm(tinv, _bmm(m_k, v))
    q_h = r_t - _bmm(a_rb, a_t)
    o_h = _bmm(a_rk, v) - _bmm(a_rb, u_b)
    g_m = jnp.where(eye, dec, 0.0) - _bmm_tn(a_t, b_bar)
    h_m = _bmm_tn(v, k_bar) - _bmm_tn(u_b, b_bar)
    s = st_ref[...]
    for c in order:
        sl = slice(c * NH, (c + 1) * NH)
        o_c = _bmm_nt(q_h[sl], s) + o_h[sl]
        o_ref[c * C:(c + 1) * C, :] = jnp.concatenate([o_c[h] for h in range(NH)], axis=1)
        s = _bmm(s, g_m[sl]) + h_m[sl]
    st_ref[...] = s


def _rwkv_scan(r, k, v, kk, bb, ld, B, S, nctx_blocks, rev):
    T = B * S
    NB = S // ROW_TILE
    d = 1 if rev else 0
    nc = nctx_blocks

    def blk(t):
        if not rev:
            return t
        return jnp.where(t < nc, nc - 1 - t, NB - 1 - (t - nc))

    spec = pl.BlockSpec((ROW_TILE, RW_W), lambda b, t: (b * NB + blk(t), 0))
    kern = functools.partial(_rwkv_kernel, rev=rev, nchunk=ROW_TILE // RW_CHUNK)
    return pl.pallas_call(
        kern,
        grid=(B, NB),
        in_specs=[spec, spec, spec, spec, spec,
                  pl.BlockSpec((None, ROW_TILE, RW_W), lambda b, t: (d, b * NB + blk(t), 0))],
        out_specs=spec,
        out_shape=jax.ShapeDtypeStruct((T, RW_W), F32),
        scratch_shapes=[pltpu.VMEM((RW_HEADS, RW_HEAD_DIM, RW_HEAD_DIM), F32)],
        compiler_params=_cparams(("parallel", "arbitrary")),
        name="rwkv_scan_bwd" if rev else "rwkv_scan_fwd",
    )(r, k, v, kk, bb, ld)


def _filter_mlp_kernel(z_ref, w1_ref, b1_ref, w2_ref, b2_ref, w3_ref, sf_ref, win_ref, f_ref):
    h = jnp.sin(sf_ref[0:1, :] * (_hdot(z_ref[...], w1_ref[...]) + b1_ref[...]))
    h = jnp.sin(sf_ref[1:2, :] * (_hdot(h, w2_ref[...]) + b2_ref[...]))
    f_ref[...] = _hdot(h, w3_ref[...]) * win_ref[...]


def _hyena_filters(L, w1, b1, w2, b2, w3, sin_freq):
    t = np.linspace(0.0, 1.0, L, dtype=np.float32)[:, None]
    bands = (HY_EMB - 1) // 2
    wpos = (2.0 * math.pi * np.arange(L, dtype=np.float32)[:, None] / L).astype(np.float32)
    fr = np.linspace(1e-4, bands - 1, bands, dtype=np.float32)[None, :]
    z = np.concatenate([t, np.cos(fr * wpos), -np.sin(fr * wpos)], -1).astype(np.float32)
    max_decay = math.log(HY_TARGET) / HY_FAST_DECAY
    min_decay = math.log(HY_TARGET) / HY_SLOW_DECAY
    deltas = np.linspace(min_decay, max_decay, HY_CH, dtype=np.float32)
    window = np.exp(-t * np.abs(deltas)).astype(np.float32)
    dist = np.concatenate([np.arange(L), [0], np.arange(L - 1, 0, -1)])
    z2 = np.pad(z[dist], ((0, 0), (0, LANES - HY_EMB)))
    win2 = np.tile(window[dist], (1, HY_ORDER))
    win2[L] = 0.0
    w1p = jnp.pad(w1, ((0, LANES - HY_EMB), (0, 0)))
    Hd = w1.shape[1]
    FW = HY_ORDER * HY_CH
    w3s = jnp.transpose(w3.reshape(Hd, HY_ORDER, 2, HY_CH), (2, 0, 1, 3)).reshape(2, Hd, FW)
    TR = min(L, 512)
    nl = L // TR
    return pl.pallas_call(
        _filter_mlp_kernel,
        grid=(2 * nl,),
        in_specs=[pl.BlockSpec((TR, LANES), lambda i: (i, 0)),
                  pl.BlockSpec((LANES, Hd), lambda i: (0, 0)), pl.BlockSpec((1, Hd), lambda i: (0, 0)),
                  pl.BlockSpec((Hd, Hd), lambda i: (0, 0)), pl.BlockSpec((1, Hd), lambda i: (0, 0)),
                  pl.BlockSpec((None, Hd, FW), lambda i: (i // nl, 0, 0)), pl.BlockSpec((2, Hd), lambda i: (0, 0)),
                  pl.BlockSpec((TR, FW), lambda i: (i, 0))],
        out_specs=pl.BlockSpec((TR, FW), lambda i: (i, 0)),
        out_shape=jax.ShapeDtypeStruct((2 * L, FW), F32),
        compiler_params=_cparams(("parallel",)),
        name="hyena_filter_mlp",
    )(jnp.asarray(z2), w1p, b1.reshape(1, Hd), w2, b2.reshape(1, Hd), w3s, sin_freq, jnp.asarray(win2))


def _sconv_kernel(p_ref, w_ref, b_ref, o_ref):
    p = p_ref[...]
    L = p.shape[0]
    i = lax.broadcasted_iota(I32, p.shape, 0)
    prev = jnp.where(i == 0, 0.0, pltpu.roll(p, 1, 0))
    nxt = jnp.where(i == L - 1, 0.0, pltpu.roll(p, L - 1, 0))
    o_ref[...] = prev * w_ref[0:1, :] + p * w_ref[1:2, :] + nxt * w_ref[2:3, :] + b_ref[...]


def _short_conv(p_hy, B, S, Lc, conv_w, conv_b):
    L = S - Lc
    p3 = p_hy.reshape(B, L, HY_WIDTH)
    return pl.pallas_call(
        _sconv_kernel,
        grid=(B, HY_WIDTH // LANES),
        in_specs=[pl.BlockSpec((None, L, LANES), lambda b, c: (b, 0, c)),
                  pl.BlockSpec((HY_SHORT, LANES), lambda b, c: (0, c)),
                  pl.BlockSpec((1, LANES), lambda b, c: (0, c))],
        out_specs=pl.BlockSpec((None, L, LANES), lambda b, c: (b, 0, c)),
        out_shape=jax.ShapeDtypeStruct((B, L, HY_WIDTH), F32),
        compiler_params=_cparams(("parallel", "parallel")),
        name="hyena_short_conv",
    )(p3, conv_w, conv_b.reshape(1, HY_WIDTH))


def _dft_constants(L):
    n = 2 * L
    n2 = FFT_N2
    n1 = n // n2
    na = n1 // 2
    g8 = SUBLANES
    w1 = np.exp(-2j * np.pi * np.outer(np.arange(n1), np.arange(n1)) / n1)
    eye8 = np.eye(g8)

    def kron_fwd(w, real_in):
        k1n, an = w.shape
        blocks = np.stack([np.stack([w.real, -w.imag], 1), np.stack([w.imag, w.real], 1)], 1)
        if real_in:
            blocks = blocks[:, :, 0:1, :]
        m = np.einsum('kria,bc->krbiac', blocks, eye8)
        return m.reshape(k1n * 2 * g8, blocks.shape[2] * an * g8)

    m1 = kron_fwd(w1[:, :na], False)
    m1f = kron_fwd(w1, True)
    cw = np.conj(w1[:, :na]).T / n
    blocks = np.stack([np.stack([cw.real, -cw.imag], 1), np.stack([cw.imag, cw.real], 1)], 0)
    m1inv = np.einsum('iark,bc->iabkrc', blocks, eye8).reshape(2 * na * g8, n1 * 2 * g8)
    w2 = np.exp(-2j * np.pi * np.outer(np.arange(n2), np.arange(n2)) / n2)
    w2big = np.block([[w2.real, -w2.imag], [w2.imag, w2.real]])
    w2c = np.conj(w2)
    iw2big = np.block([[w2c.real, -w2c.imag], [w2c.imag, w2c.real]])
    tw = np.exp(-2j * np.pi * np.arange(n2) / n)

    def lanes(cols):
        z = np.concatenate([np.broadcast_to(col[:, None], (n2, LANES)) for col in cols], axis=1)
        return np.stack([z.real, z.imag], 0)

    tw0 = lanes([tw ** q for q in range(FFT_G)])
    twg = lanes([tw ** FFT_G] * FFT_G)
    c = lambda x, dt: jnp.asarray(np.ascontiguousarray(x), dt)
    return dict(m1=c(m1, BF16), m1f=c(m1f, BF16), m1inv=c(m1inv, BF16), w2=c(w2big, BF16), iw2=c(iw2big, BF16),
                tw0=c(tw0, F32), twg=c(twg, F32), n1=n1, na=na)


def _cplx_rows(z, twr, twi, conj):
    n2 = z.shape[0] // 2
    zr, zi = z[:n2], z[n2:]
    if conj:
        return jnp.concatenate([zr * twr + zi * twi, zi * twr - zr * twi], axis=0)
    return jnp.concatenate([zr * twr - zi * twi, zi * twr + zr * twi], axis=0)


def _split_dot(m, x):
    hi = x.astype(BF16)
    lo = (x - hi.astype(F32)).astype(BF16)
    return jnp.dot(m, hi, preferred_element_type=F32) + jnp.dot(m, lo, preferred_element_type=F32)


def _next_twiddle(tw, twg_ref):
    twr, twi = tw
    b_r, b_i = twg_ref[0], twg_ref[1]
    return twr * b_r - twi * b_i, twr * b_i + twi * b_r


def _load_group(scr, j, C):
    return jnp.concatenate([scr[j * FFT_G + q].reshape(2 * FFT_N2, C) for q in range(FFT_G)], axis=1)


def _filt_fft_kernel(f_ref, m1_ref, w2_ref, tw0_ref, twg_ref, o_ref, y_scr, *, n1):
    nbg = FFT_N2 // SUBLANES
    C = f_ref.shape[-1]
    for bg in range(nbg):
        yg = _split_dot(m1_ref[...], f_ref[:, bg].reshape(n1 * SUBLANES, C))
        y_scr[:, :, bg] = yg.reshape(n1, 2, SUBLANES, C)

    def body(j, tw):
        y = _cplx_rows(_load_group(y_scr, j, C), tw[0], tw[1], False)
        z = _split_dot(w2_ref[...], y)
        for q in range(FFT_G):
            o_ref[j * FFT_G + q] = z[:, q * C:(q + 1) * C]
        return _next_twiddle(tw, twg_ref)

    lax.fori_loop(0, n1 // FFT_G, body, (tw0_ref[0], tw0_ref[1]))


def _filter_spectrum(filt, L, dc):
    n1 = dc['n1']
    nbg = FFT_N2 // SUBLANES
    ncb = HY_CH // LANES
    full = filt.reshape(n1, nbg, SUBLANES, HY_ORDER * HY_CH)
    return pl.pallas_call(
        functools.partial(_filt_fft_kernel, n1=n1),
        grid=(HY_ORDER, ncb),
        in_specs=[pl.BlockSpec((n1, nbg, SUBLANES, LANES), lambda o, c: (0, 0, 0, o * ncb + c)),
                  pl.BlockSpec(dc['m1f'].shape, lambda o, c: (0, 0)),
                  pl.BlockSpec(dc['w2'].shape, lambda o, c: (0, 0)),
                  pl.BlockSpec(dc['tw0'].shape, lambda o, c: (0, 0, 0)),
                  pl.BlockSpec(dc['twg'].shape, lambda o, c: (0, 0, 0))],
        out_specs=pl.BlockSpec((None, n1, 2 * FFT_N2, LANES), lambda o, c: (o, 0, 0, c)),
        out_shape=jax.ShapeDtypeStruct((HY_ORDER, n1, 2 * FFT_N2, HY_CH), F32),
        scratch_shapes=[pltpu.VMEM((n1, 2, nbg, SUBLANES, LANES), F32)],
        compiler_params=_cparams(("parallel", "parallel")),
        name="hyena_filter_fft",
    )(full, dc['m1f'], dc['w2'], dc['tw0'], dc['twg'])


def _hyconv_kernel(za_ref, zb_ref, ga_ref, gb_ref, ff_ref, bias_ref, m1_ref, m1i_ref, w2_ref, iw2_ref,
                   tw0_ref, twg_ref, oa_ref, ob_ref, y_scr, *, n1, na):
    nbg = FFT_N2 // SUBLANES
    C = za_ref.shape[-1]
    half = na * SUBLANES
    for bg in range(nbg):
        xg = jnp.concatenate([za_ref[:, bg].reshape(half, C), zb_ref[:, bg].reshape(half, C)], axis=0)
        yg = jnp.dot(m1_ref[...], xg.astype(BF16), preferred_element_type=F32)
        y_scr[:, :, bg] = yg.reshape(n1, 2, SUBLANES, C)

    def body(j, tw):
        y = _cplx_rows(_load_group(y_scr, j, C), tw[0], tw[1], False)
        z = jnp.dot(w2_ref[...], y.astype(BF16), preferred_element_type=F32)
        f = _load_group(ff_ref, j, C)
        zr, zi = z[:FFT_N2], z[FFT_N2:]
        fr, fi = f[:FFT_N2], f[FFT_N2:]
        p = jnp.concatenate([zr * fr - zi * fi, zr * fi + zi * fr], axis=0)
        v = jnp.dot(iw2_ref[...], p.astype(BF16), preferred_element_type=F32)
        v = _cplx_rows(v, tw[0], tw[1], True)
        for q in range(FFT_G):
            y_scr[j * FFT_G + q] = v[:, q * C:(q + 1) * C].reshape(2, nbg, SUBLANES, C)
        return _next_twiddle(tw, twg_ref)

    lax.fori_loop(0, n1 // FFT_G, body, (tw0_ref[0], tw0_ref[1]))
    bias = bias_ref[...]
    for bg in range(nbg):
        vg = y_scr[:, :, bg].reshape(n1 * 2 * SUBLANES, C)
        out = jnp.dot(m1i_ref[...], vg.astype(BF16), preferred_element_type=F32)
        ya = out[:half].reshape(na, SUBLANES, C)
        yb = out[half:].reshape(na, SUBLANES, C)
        za = za_ref[:, bg]
        zb = zb_ref[:, bg]
        oa_ref[:, bg] = ga_ref[:, bg] * (ya + za * bias)
        ob_ref[:, bg] = gb_ref[:, bg] * (yb + zb * bias)


def _hyena_conv(z, z_col0, gates, g_col0, spec, bias, dc, B, L):
    n1, na = dc['n1'], dc['na']
    nbg = FFT_N2 // SUBLANES
    NCB = HY_CH // LANES

    def view(a):
        return a.reshape(B, na, nbg, SUBLANES, a.shape[-1])

    def seq(col0, which):
        return pl.BlockSpec((None, na, nbg, SUBLANES, LANES),
                            lambda c, p: (2 * p + which, 0, 0, 0, col0 + c))

    def const(a):
        nd = a.ndim
        return pl.BlockSpec(a.shape, lambda c, p: (0,) * nd)

    out_a, out_b = pl.pallas_call(
        functools.partial(_hyconv_kernel, n1=n1, na=na),
        grid=(NCB, B // 2),
        in_specs=[seq(z_col0, 0), seq(z_col0, 1), seq(g_col0, 0), seq(g_col0, 1),
                  pl.BlockSpec((n1, 2 * FFT_N2, LANES), lambda c, p: (0, 0, c), pipeline_mode=pl.Buffered(1)),
                  pl.BlockSpec((1, LANES), lambda c, p: (0, c)),
                  const(dc['m1']), const(dc['m1inv']), const(dc['w2']), const(dc['iw2']),
                  const(dc['tw0']), const(dc['twg'])],
        out_specs=[pl.BlockSpec((None, na, nbg, SUBLANES, LANES), lambda c, p: (p, 0, 0, 0, c)),
                   pl.BlockSpec((None, na, nbg, SUBLANES, LANES), lambda c, p: (p, 0, 0, 0, c))],
        out_shape=[jax.ShapeDtypeStruct((B // 2, na, nbg, SUBLANES, HY_CH), F32)] * 2,
        scratch_shapes=[pltpu.VMEM((n1, 2, nbg, SUBLANES, LANES), F32)],
        compiler_params=_cparams(("parallel", "arbitrary")),
        name="hyena_long_conv",
    )(view(z), view(z), view(gates), view(gates), spec, bias.reshape(1, HY_CH),
      dc['m1'], dc['m1inv'], dc['w2'], dc['iw2'], dc['tw0'], dc['twg'])
    out = jnp.stack([out_a, out_b], axis=1)
    return out.reshape(B, L, HY_CH)


def _out1_kernel(hy_ref, of_ref, ob_ref, r_ref, k_ref, v_ref, g_ref, x_ref, mod_ref, wo_ref,
                 rk_ref, lg_ref, lb_ref, bd_ref, lng_ref, lnb_ref, xl_ref, u_ref, up_ref, *, alpha):
    o = of_ref[...] + ob_ref[...]
    bd = bd_ref[...]
    inv = 1.0 / RW_HEAD_DIM
    mu = _head_sums(o, bd) * inv
    oc = o - mu
    var = _head_sums(oc * oc, bd) * inv
    on = oc * lax.rsqrt(var + RW_GN_EPS) * lg_ref[...] + lb_ref[...]
    bonus = _head_sums(r_ref[...] * k_ref[...] * rk_ref[...], bd) * v_ref[...]
    y = jnp.concatenate([hy_ref[...], (on + bonus) * g_ref[...]], axis=1)
    _post_mix(y, x_ref[...], mod_ref, wo_ref, lng_ref, lnb_ref, alpha, xl_ref, u_ref, up_ref)


def _readout1(hy, o_f, o_b, r, k, v, g, xcat, mods, B, S, Lc, w_out, r_k, ln_g, ln_b, dn_g, dn_b, alpha):
    D = xcat.shape[-1]
    L = S - Lc
    J = S // ROW_TILE
    JL = L // ROW_TILE
    JC = Lc // ROW_TILE

    def cat_rows(w):
        return pl.BlockSpec((ROW_TILE, w), lambda b, j: (b * J + JC + j, 0))

    def lat_rows(w):
        return pl.BlockSpec((ROW_TILE, w), lambda b, j: (b * JL + j, 0))

    def const(shape):
        return pl.BlockSpec(shape, lambda b, j: (0,) * len(shape))

    sd = jax.ShapeDtypeStruct
    return pl.pallas_call(
        functools.partial(_out1_kernel, alpha=alpha),
        grid=(B, JL),
        in_specs=[lat_rows(HY_CH), cat_rows(RW_W), cat_rows(RW_W), cat_rows(RW_W), cat_rows(RW_W), cat_rows(RW_W),
                  cat_rows(RW_W), cat_rows(D),
                  pl.BlockSpec((None, 6, D), lambda b, j: (b, 0, 0)),
                  const((HY_CH + RW_W, D)), const((1, RW_W)), const((1, RW_W)), const((1, RW_W)),
                  const((RW_W, RW_W)), const((1, D)), const((1, D))],
        out_specs=[lat_rows(D), lat_rows(D),
                   pl.BlockSpec((ROW_TILE * ROW_SUB, LANES), lambda b, j: (b * JL + j, 0))],
        out_shape=[sd((B * L, D), F32), sd((B * L, D), F32), sd((B * L * ROW_SUB, LANES), U32)],
        compiler_params=_cparams(("parallel", "parallel")),
        name="readout1",
    )(hy, o_f, o_b, r, k, v, g, xcat.reshape(B * S, D), mods, w_out.astype(BF16),
      r_k.reshape(1, RW_W), ln_g.reshape(1, RW_W), ln_b.reshape(1, RW_W),
      _head_sum_matrix(RW_W, RW_HEAD_DIM), dn_g.reshape(1, D), dn_b.reshape(1, D))


def kernel(x, c, ctx, c_ctx, mod_w, mod_b, ln1_g, ln1_b, ln2_g, ln2_b, ev_w_in, ev_w_out, gla_gate_w2, gla_gate_b, gla_norm_g, hg_lb_logits, hg_norm_g, od_w_in, od_w_out, hy_conv_w, hy_conv_b, hy_ffn_w1, hy_ffn_b1, hy_ffn_w2, hy_ffn_b2, hy_ffn_w3, hy_sin_freq, hy_bias, rw_mu, rw_w0, rw_w2, rw_a0, rw_a2, rw_g2, rw_k_k, rw_k_a, rw_r_k, rw_ln_g, rw_ln_b, router_w, router_bias, exp_w13, exp_w2, sh_w13, sh_w2):
    B, L, D = x.shape
    Lc = ctx.shape[1]
    S = Lc + L
    depth = mod_w.shape[0]
    assert depth == 2 and L % ROW_TILE == 0 and Lc % ROW_TILE == 0 and B % 2 == 0
    assert L % (FFT_N2 * 2) == 0 and L % GRID_W == 0
    alpha = (2 * depth) ** 0.25
    nctx = Lc // ROW_TILE
    J = S // ROW_TILE
    T = B * S

    cc = jnp.concatenate([c, c_ctx[None, :]], axis=0)
    cc = jnp.pad(cc, ((0, (-cc.shape[0]) % SUBLANES), (0, 0)))
    hg_lb = jnp.cumsum(jax.nn.softmax(hg_lb_logits.astype(F32), axis=0), axis=0)
    xcat = jnp.concatenate([ctx, x], axis=1)

    mods = _modulation(cc, mod_w[0], mod_b[0])
    gq, gk, gv, gla, r, hq, hk, hla, hv, hgate = _in_proj0(xcat, mods, nctx, ev_w_in[0], gla_gate_w2[0],
                                                            gla_gate_b[0], hg_lb[0])
    gk3 = gk.reshape(1, T, GLA_QK)
    o_gf = _gated_recurrence(gq, gk3, gv, gla, 0, B, S, nctx, False, GLA_HEADS, GLA_DK, GLA_DV)
    o_gb = _gated_recurrence(gq, gk3, gv, gla, 0, B, S, nctx, True, GLA_HEADS, GLA_DK, GLA_DV)
    o_hf = _gated_recurrence(hq, hk, hv, hla, 0, B, S, nctx, False, HG_HEADS, HG_EXPAND, HG_EXPAND)
    o_hb = _gated_recurrence(hq, hk, hv, hla, 1, B, S, nctx, True, HG_HEADS, HG_EXPAND, HG_EXPAND)
    xl, u, up = _readout0(o_gf, o_gb, o_hf, o_hb, r, hgate, xcat, mods, nctx, ev_w_out[0], gla_norm_g[0],
                          hg_norm_g[0], ln1_g[0], ln1_b[0], alpha)
    cmb_rows = 2 * CMB_TILE
    tiles_per_b = S // cmb_rows
    tile_in_b = jnp.arange(B * tiles_per_b, dtype=I32) % tiles_per_b
    mod_of_tile = jnp.where(tile_in_b < Lc // cmb_rows, B, jnp.arange(B * tiles_per_b, dtype=I32) // tiles_per_b)
    x1 = _moe_block(u, up, xl, mods, mod_of_tile.astype(I32), router_w[0], router_bias[0], exp_w13, exp_w2, 0,
                    sh_w13[0], sh_w2[0], ln2_g[0], ln2_b[0], alpha)
    xcat = x1.reshape(B, S, D)

    mods = _modulation(cc, mod_w[1], mod_b[1])
    p_hy, p_rw = _in_proj1(xcat, mods, nctx, od_w_in[0])
    rr, rk, rv, rkk, rbb, rld, rg = _rw_streams(p_rw, B, S, nctx, rw_mu[0], rw_w0[0], rw_w2[0], rw_a0[0],
                                                 rw_a2[0], rw_g2[0], rw_k_k[0], rw_k_a[0])
    o_f = _rwkv_scan(rr, rk, rv, rkk, rbb, rld, B, S, nctx, False)
    o_b = _rwkv_scan(rr, rk, rv, rkk, rbb, rld, B, S, nctx, True)
    dc = _dft_constants(L)
    filt = _hyena_filters(L, hy_ffn_w1[0], hy_ffn_b1[0], hy_ffn_w2[0], hy_ffn_b2[0], hy_ffn_w3[0], hy_sin_freq[0])
    spec = _filter_spectrum(filt, L, dc)
    uu = _short_conv(p_hy, B, S, Lc, hy_conv_w[0], hy_conv_b[0])
    NCB = HY_CH // LANES
    z1 = _hyena_conv(uu, 0, uu, NCB, spec[0], hy_bias[0, 0], dc, B, L)
    z2 = _hyena_conv(z1, 0, uu, 2 * NCB, spec[1], hy_bias[0, 1], dc, B, L)
    xl, u, up = _readout1(z2.reshape(B * L, HY_CH), o_f, o_b, rr, rk, rv, rg, xcat, mods, B, S, Lc, od_w_out[0],
                          rw_r_k[0], rw_ln_g[0], rw_ln_b[0], ln1_g[1], ln1_b[1], alpha)
    mod_of_tile = (jnp.arange(B * (L // cmb_rows), dtype=I32) // (L // cmb_rows)).astype(I32)
    out = _moe_block(u, up, xl, mods, mod_of_tile, router_w[1], router_bias[1], exp_w13, exp_w2, 1,
                     sh_w13[1], sh_w2[1], ln2_g[1], ln2_b[1], alpha)
    return out.reshape(B, L, D)
```

```python
import functools
import math

import numpy as np
import jax
import jax.numpy as jnp
from jax import lax
from jax.experimental import pallas as pl
from jax.experimental.pallas import tpu as pltpu

F32 = jnp.float32
BF16 = jnp.bfloat16
I32 = jnp.int32
U32 = jnp.uint32
HI = lax.Precision.HIGHEST

LN_EPS = 1e-5
GLA_HEADS, GLA_DK, GLA_DV = 4, 64, 128
GLA_QK, GLA_V = GLA_HEADS * GLA_DK, GLA_HEADS * GLA_DV
GLA_GATE_RANK = 16
GLA_GATE_NORM = 16.0
CHUNK = 64
HG_HEADS, HG_EXPAND = 4, 128
HG_W = HG_HEADS * HG_EXPAND
HY_CH, HY_ORDER, HY_SHORT, HY_EMB = 512, 2, 3, 33
HY_WIDTH = (HY_ORDER + 1) * HY_CH
HY_FAST_DECAY, HY_SLOW_DECAY, HY_TARGET = 0.3, 1.5, 1e-2
RW_HEADS, RW_HEAD_DIM = 8, 64
RW_W = RW_HEADS * RW_HEAD_DIM
RW_DECAY_LORA, RW_AAA_LORA, RW_GATE_LORA = 64, 64, 128
RW_GN_EPS = 64e-5
GRID_W = 64
N_EXPERTS, TOP_K, N_GROUPS, TOPK_GROUPS = 256, 8, 8, 4
PER_GROUP = N_EXPERTS // N_GROUPS
EXPERT_FF, SHARED_FF = 256, 256
ROUTED_SCALE = 2.5

LANES = 128
SUBLANES = 8
VMEM_LIMIT = 56 * 1024 * 1024
ROW_TILE = 256
MOE_BLK = 512
CMB_TILE = 128
RW_CHUNK = 64
FFT_N2 = 128
FFT_G = 4


def _cparams(sem):
    return pltpu.CompilerParams(dimension_semantics=sem, vmem_limit_bytes=VMEM_LIMIT)


def _bdot(a, b):
    return jnp.dot(a.astype(BF16), b.astype(BF16), preferred_element_type=F32)


def _bdot_nt(a, b):
    return lax.dot_general(a.astype(BF16), b.astype(BF16), (((1,), (1,)), ((), ())),
                           preferred_element_type=F32)


def _bdot_tn(a, b):
    return lax.dot_general(a.astype(BF16), b.astype(BF16), (((0,), (0,)), ((), ())),
                           preferred_element_type=F32)


def _bmm(a, b):
    return jnp.einsum('nij,njk->nik', a.astype(BF16), b.astype(BF16), preferred_element_type=F32)


def _bmm_nt(a, b):
    return jnp.einsum('nik,njk->nij', a.astype(BF16), b.astype(BF16), preferred_element_type=F32)


def _bmm_tn(a, b):
    return _bmm(jnp.swapaxes(a, 1, 2), b)


def _hdot(a, b):
    return jnp.dot(a, b, precision=HI, preferred_element_type=F32)


def _head_sums(x, ones_bd):
    hi = x.astype(BF16)
    lo = (x - hi.astype(F32)).astype(BF16)
    return (jnp.dot(hi, ones_bd, preferred_element_type=F32) + jnp.dot(lo, ones_bd, preferred_element_type=F32))


def _silu(x):
    return x * jax.nn.sigmoid(x)


def _layer_norm_rows(x, g, b, eps):
    mu = jnp.mean(x, axis=-1, keepdims=True)
    xc = x - mu
    var = jnp.mean(xc * xc, axis=-1, keepdims=True)
    return xc * lax.rsqrt(var + eps) * g + b


def _mod_kernel(c_ref, w_ref, b_ref, o_ref):
    o_ref[...] = _hdot(_silu(c_ref[...]), w_ref[...]) + b_ref[...]


def _modulation(cc, w, b):
    R, D = cc.shape
    out = pl.pallas_call(
        _mod_kernel,
        grid=(6,),
        in_specs=[pl.BlockSpec((R, D), lambda j: (0, 0)),
                  pl.BlockSpec((D, D), lambda j: (0, j)),
                  pl.BlockSpec((1, D), lambda j: (0, j))],
        out_specs=pl.BlockSpec((R, D), lambda j: (0, j)),
        out_shape=jax.ShapeDtypeStruct((R, 6 * D), F32),
        compiler_params=_cparams(("parallel",)),
        name="modulation",
    )(cc, w, b.reshape(1, 6 * D))
    return out.reshape(R, 6, D)


def _in0_kernel(x_ref, mod_ref, w_ref, wa_ref, w2_ref, gb_ref, lb_ref,
                gq_ref, gk_ref, gv_ref, gla_ref, r_ref, hq_ref, hk_ref, hla_ref, hv_ref, hg_ref):
    sh1 = mod_ref[0:1, :]
    sc1 = mod_ref[1:2, :]
    h = (x_ref[...] * (1.0 + sc1) + sh1).astype(BF16)

    def proj(off, width):
        return jnp.dot(h, w_ref[:, off:off + width], preferred_element_type=F32)

    gq_ref[...] = (proj(0, GLA_QK) * (GLA_DK ** -0.5)).astype(BF16)
    gk_ref[...] = proj(GLA_QK, GLA_QK).astype(BF16)
    gv_ref[...] = proj(2 * GLA_QK, GLA_V).astype(BF16)
    r_ref[...] = proj(2 * GLA_QK + GLA_V, GLA_V).astype(BF16)
    base = 2 * GLA_QK + 2 * GLA_V
    a = jnp.dot(h, wa_ref[...], preferred_element_type=F32)
    z = _bdot(a, w2_ref[...]) + gb_ref[...]
    ls = (jnp.minimum(z, 0.0) - jnp.log(1.0 + jnp.exp(-jnp.abs(z)))) * (1.0 / GLA_GATE_NORM)
    gla_ref[0] = ls[:, :GLA_QK]
    gla_ref[1] = ls[:, GLA_QK:]
    hq_ref[...] = _silu(proj(base, HG_W)).astype(BF16)
    for d in range(2):
        zf = proj(base + (1 + d) * HG_W, HG_W)
        lb = lb_ref[d:d + 1, :]
        f = lb + (1.0 - lb) * jax.nn.sigmoid(zf)
        hk_ref[d] = (1.0 - f).astype(BF16)
        hla_ref[d] = jnp.log(f)
    hv_ref[...] = proj(base + 3 * HG_W, HG_W).astype(BF16)
    hg_ref[...] = proj(base + 4 * HG_W, HG_W).astype(BF16)


def _in_proj0(xcat, mods, nctx_tiles, w_in, gate_w2, gate_b, lb):
    B, S, D = xcat.shape
    J = S // ROW_TILE
    T = B * S
    a_off = 2 * GLA_QK + 2 * GLA_V
    wmain = jnp.concatenate([w_in[:, :a_off], w_in[:, a_off + 2 * GLA_GATE_RANK:]], axis=1).astype(BF16)
    wa = jnp.pad(w_in[:, a_off:a_off + 2 * GLA_GATE_RANK], ((0, 0), (0, LANES - 2 * GLA_GATE_RANK))).astype(BF16)
    w2 = jnp.zeros((LANES, 2 * GLA_QK), F32)
    w2 = w2.at[:GLA_GATE_RANK, :GLA_QK].set(gate_w2[0]).at[GLA_GATE_RANK:2 * GLA_GATE_RANK, GLA_QK:].set(gate_w2[1])
    gb = gate_b.reshape(1, 2 * GLA_QK)
    WM = wmain.shape[1]

    def rows(w):
        return pl.BlockSpec((ROW_TILE, w), lambda b, j: (b * J + j, 0))

    def rows2(w):
        return pl.BlockSpec((2, ROW_TILE, w), lambda b, j: (0, b * J + j, 0))

    def const(shape):
        return pl.BlockSpec(shape, lambda b, j: (0,) * len(shape))

    sd = jax.ShapeDtypeStruct
    outs = pl.pallas_call(
        _in0_kernel,
        grid=(B, J),
        in_specs=[rows(D),
                  pl.BlockSpec((None, 6, D), lambda b, j: (jnp.where(j < nctx_tiles, B, b), 0, 0)),
                  const((D, WM)), const((D, LANES)), const((LANES, 2 * GLA_QK)), const((1, 2 * GLA_QK)),
                  const((2, HG_W))],
        out_specs=[rows(GLA_QK), rows(GLA_QK), rows(GLA_V), rows2(GLA_QK), rows(GLA_V),
                   rows(HG_W), rows2(HG_W), rows2(HG_W), rows(HG_W), rows(HG_W)],
        out_shape=[sd((T, GLA_QK), BF16), sd((T, GLA_QK), BF16), sd((T, GLA_V), BF16), sd((2, T, GLA_QK), F32),
                   sd((T, GLA_V), BF16), sd((T, HG_W), BF16), sd((2, T, HG_W), BF16), sd((2, T, HG_W), F32),
                   sd((T, HG_W), BF16), sd((T, HG_W), BF16)],
        compiler_params=_cparams(("parallel", "parallel")),
        name="in_proj0",
    )(xcat.reshape(T, D), mods, wmain, wa, w2, gb, lb)
    return outs


def _rec_kernel(q_ref, k_ref, v_ref, la_ref, o_ref, st_ref, *, rev, nh, dk, dv, nchunk):
    t = pl.program_id(1)

    @pl.when(t == 0)
    def _():
        st_ref[...] = jnp.zeros_like(st_ref)

    C = CHUNK
    row = lax.broadcasted_iota(I32, (C, C), 0)
    col = lax.broadcasted_iota(I32, (C, C), 1)
    incl = (col >= row) if rev else (col <= row)
    tri = jnp.where(incl, 1.0, 0.0).astype(BF16)
    ref_i = C // 2 - 1 if rev else C // 2
    last_i = 0 if rev else C - 1
    order = range(nchunk - 1, -1, -1) if rev else range(nchunk)

    def stack(x, hd):
        return jnp.stack([x[c * C:(c + 1) * C, h * hd:(h + 1) * hd] for c in range(nchunk) for h in range(nh)], 0)

    def rows(x, i):
        return jnp.concatenate([jnp.broadcast_to(x[c * C + i:c * C + i + 1], (C, x.shape[1]))
                                for c in range(nchunk)], axis=0)

    la = la_ref[...]
    q = q_ref[...].astype(F32)
    k = k_ref[...].astype(F32)
    b = jnp.concatenate([_split_dot(tri, la[c * C:(c + 1) * C]) for c in range(nchunk)], axis=0)
    b_mid = rows(b, ref_i)
    b_last = rows(b, last_i)
    v = stack(v_ref[...].astype(F32), dv)
    sc = _bmm_nt(stack(q * jnp.exp(b - b_mid), dk), stack(k * jnp.exp(b_mid - b), dk))
    o_intra = _bmm(jnp.where(incl[None], sc, 0.0), v)
    q_in = stack(q * jnp.exp(b), dk)
    kv_t = _bmm_tn(v, stack(k * jnp.exp(b_last - b), dk))
    dec = stack(jnp.exp(b_last), dk)[:, 0:1, :]
    s_t = st_ref[...]
    for c in order:
        sl = slice(c * nh, (c + 1) * nh)
        o_c = o_intra[sl] + _bmm_nt(q_in[sl], s_t)
        o_ref[c * C:(c + 1) * C, :] = jnp.concatenate([o_c[h] for h in range(nh)], axis=1)
        s_t = s_t * dec[sl] + kv_t[sl]
    st_ref[...] = s_t


def _gated_recurrence(q, k, v, la, kdir, B, S, nctx_blocks, rev, nh, dk, dv):
    T = B * S
    NB = S // ROW_TILE
    wk, wv = nh * dk, nh * dv
    assert q.shape[1] == wk
    d = 1 if rev else 0
    nc = nctx_blocks

    def blk(t):
        if not rev:
            return t
        return jnp.where(t < nc, nc - 1 - t, NB - 1 - (t - nc))

    kern = functools.partial(_rec_kernel, rev=rev, nh=nh, dk=dk, dv=dv, nchunk=ROW_TILE // CHUNK)
    return pl.pallas_call(
        kern,
        grid=(B, NB),
        in_specs=[pl.BlockSpec((ROW_TILE, wk), lambda b, t: (b * NB + blk(t), 0)),
                  pl.BlockSpec((None, ROW_TILE, wk), lambda b, t: (kdir, b * NB + blk(t), 0)),
                  pl.BlockSpec((ROW_TILE, wv), lambda b, t: (b * NB + blk(t), 0)),
                  pl.BlockSpec((None, ROW_TILE, wk), lambda b, t: (d, b * NB + blk(t), 0))],
        out_specs=pl.BlockSpec((ROW_TILE, wv), lambda b, t: (b * NB + blk(t), 0)),
        out_shape=jax.ShapeDtypeStruct((T, wv), F32),
        scratch_shapes=[pltpu.VMEM((nh, dv, dk), F32)],
        compiler_params=_cparams(("parallel", "arbitrary")),
        name="gated_rec_bwd" if rev else "gated_rec_fwd",
    )(q, k, v, la)


def _pack_rows(x):
    w = x.shape[1] // 2
    lo = lax.bitcast_convert_type(x[:, :w].astype(BF16).astype(F32), U32)
    hi = lax.bitcast_convert_type(x[:, w:].astype(BF16).astype(F32), U32)
    return (lo >> 16) | (hi & jnp.uint32(0xFFFF0000))


def _unpack_rows(p):
    lo = lax.bitcast_convert_type(p << 16, F32)
    hi = lax.bitcast_convert_type(p & jnp.uint32(0xFFFF0000), F32)
    return lo, hi


ROW_SUB = 4


def _store_row_slabs(ref, val):
    m = val.shape[0]
    for j in range(ROW_SUB):
        ref[pl.ds(j, m, stride=ROW_SUB), :] = val[:, j * LANES:(j + 1) * LANES]


def _load_row_slabs(ref, r0=0, m=None):
    m = ref.shape[0] // ROW_SUB - r0 if m is None else m
    return jnp.concatenate([ref[pl.ds(r0 * ROW_SUB + j, m, stride=ROW_SUB), :] for j in range(ROW_SUB)], axis=1)


def _post_mix(y, x, mod_ref, wo_ref, lng_ref, lnb_ref, alpha, xl_ref, u_ref, up_ref):
    g1 = mod_ref[2:3, :]
    sh2 = mod_ref[3:4, :]
    sc2 = mod_ref[4:5, :]
    yo = jnp.dot(y.astype(BF16), wo_ref[...], preferred_element_type=F32)
    xl = _layer_norm_rows(alpha * x + g1 * yo, lng_ref[...], lnb_ref[...], LN_EPS)
    xl_ref[...] = xl
    u = xl * (1.0 + sc2) + sh2
    u_ref[...] = u
    _store_row_slabs(up_ref, _pack_rows(u))


def _out0_kernel(gf_ref, gb_ref, hf_ref, hb_ref, r_ref, hg_ref, x_ref, mod_ref, wo_ref,
                 gg_ref, hgg_ref, lng_ref, lnb_ref, xl_ref, u_ref, up_ref, *, alpha):
    def heads(o, g, gate):
        parts = []
        for hh in range(o.shape[1] // LANES):
            seg = o[:, hh * LANES:(hh + 1) * LANES]
            ms = jnp.mean(seg * seg, axis=-1, keepdims=True)
            parts.append(seg * lax.rsqrt(ms + 1e-6) * g)
        return jnp.concatenate(parts, axis=1) * _silu(gate.astype(F32))

    y = jnp.concatenate([heads(gf_ref[...] + gb_ref[...], gg_ref[...], r_ref[...]),
                         heads(hf_ref[...] + hb_ref[...], hgg_ref[...], hg_ref[...])], axis=1)
    _post_mix(y, x_ref[...], mod_ref, wo_ref, lng_ref, lnb_ref, alpha, xl_ref, u_ref, up_ref)


def _readout0(o_gf, o_gb, o_hf, o_hb, r, hgate, xcat, mods, nctx_tiles, w_out, gla_g, hg_g, ln_g, ln_b, alpha):
    B, S, D = xcat.shape
    J = S // ROW_TILE
    T = B * S

    def rows(w):
        return pl.BlockSpec((ROW_TILE, w), lambda b, j: (b * J + j, 0))

    def const(shape):
        return pl.BlockSpec(shape, lambda b, j: (0,) * len(shape))

    sd = jax.ShapeDtypeStruct
    return pl.pallas_call(
        functools.partial(_out0_kernel, alpha=alpha),
        grid=(B, J),
        in_specs=[rows(GLA_V), rows(GLA_V), rows(HG_W), rows(HG_W), rows(GLA_V), rows(HG_W), rows(D),
                  pl.BlockSpec((None, 6, D), lambda b, j: (jnp.where(j < nctx_tiles, B, b), 0, 0)),
                  const((GLA_V + HG_W, D)), const((1, GLA_DV)), const((1, HG_EXPAND)), const((1, D)), const((1, D))],
        out_specs=[rows(D), rows(D), pl.BlockSpec((ROW_TILE * ROW_SUB, LANES), lambda b, j: (b * J + j, 0))],
        out_shape=[sd((T, D), F32), sd((T, D), F32), sd((T * ROW_SUB, LANES), U32)],
        compiler_params=_cparams(("parallel", "parallel")),
        name="readout0",
    )(o_gf, o_gb, o_hf, o_hb, r, hgate, xcat.reshape(T, D), mods, w_out.astype(BF16),
      gla_g.reshape(1, -1), hg_g.reshape(1, -1), ln_g.reshape(1, D), ln_b.reshape(1, D))


def _route_kernel(u_ref, rwh_ref, rwl_ref, rb_ref, tri_ref, eidx_ref, gate_ref, rank_ref, cnt_ref, carry_ref):
    i = pl.program_id(0)

    @pl.when(i == 0)
    def _():
        carry_ref[...] = jnp.zeros_like(carry_ref)

    u = u_ref[...]
    TM = u.shape[0]
    E = N_EXPERTS
    u_hi = u.astype(BF16)
    u_lo = (u - u_hi.astype(F32)).astype(BF16)
    nt = (((1,), (1,)), ((), ()))
    logits = (lax.dot_general(rwh_ref[...], u_hi, nt, preferred_element_type=F32)
              + lax.dot_general(rwh_ref[...], u_lo, nt, preferred_element_type=F32)
              + lax.dot_general(rwl_ref[...], u_hi, nt, preferred_element_type=F32))
    scores = jax.nn.sigmoid(logits)
    sel = scores + rb_ref[...]
    neg = -jnp.inf
    sel3 = sel.reshape(N_GROUPS, PER_GROUP, TM)
    io_g = lax.broadcasted_iota(I32, sel3.shape, 1)
    m1 = jnp.max(sel3, axis=1, keepdims=True)
    i1 = jnp.min(jnp.where(sel3 == m1, io_g, PER_GROUP), axis=1, keepdims=True)
    m2 = jnp.max(jnp.where(io_g == i1, neg, sel3), axis=1, keepdims=True)
    grp = m1 + m2
    io_n = lax.broadcasted_iota(I32, grp.shape, 0)
    keep = jnp.zeros(grp.shape, jnp.bool_)
    for _ in range(TOPK_GROUPS):
        m = jnp.max(grp, axis=0, keepdims=True)
        idx = jnp.min(jnp.where(grp == m, io_n, N_GROUPS), axis=0, keepdims=True)
        hit = io_n == idx
        keep = jnp.logical_or(keep, hit)
        grp = jnp.where(hit, neg, grp)
    sel = jnp.where(keep, sel3, neg).reshape(E, TM)
    io_e = lax.broadcasted_iota(I32, (E, TM), 0)
    base = carry_ref[...]
    tri = tri_ref[...]
    e_rows, g_rows, r_rows = [], [], []
    for _ in range(TOP_K):
        m = jnp.max(sel, axis=0, keepdims=True)
        idx = jnp.min(jnp.where(sel == m, io_e, E), axis=0, keepdims=True)
        hit = io_e == idx
        hit_f = jnp.where(hit, 1.0, 0.0)
        g_rows.append(jnp.sum(jnp.where(hit, scores, 0.0), axis=0, keepdims=True))
        prefix = jnp.dot(hit_f.astype(BF16), tri, preferred_element_type=F32)
        r_rows.append(jnp.sum(jnp.where(hit, prefix + base, 0.0), axis=0, keepdims=True))
        base = base + jnp.sum(hit_f, axis=1, keepdims=True)
        e_rows.append(idx)
        sel = jnp.where(hit, neg, sel)
    carry_ref[...] = base
    cnt_ref[...] = base
    g = jnp.concatenate(g_rows, axis=0)
    gate_ref[...] = g / jnp.sum(g, axis=0, keepdims=True) * ROUTED_SCALE
    eidx_ref[...] = jnp.concatenate(e_rows, axis=0)
    rank_ref[...] = jnp.concatenate(r_rows, axis=0).astype(I32)


def _route(u, router_w, router_bias):
    T, D = u.shape
    E = N_EXPERTS
    n = T // ROW_TILE
    tri = jnp.asarray(np.triu(np.ones((ROW_TILE, ROW_TILE), np.float32), 1), BF16)
    sd = jax.ShapeDtypeStruct
    cols = pl.BlockSpec((TOP_K, ROW_TILE), lambda i: (0, i))
    rwt = router_w.T
    rw_hi = rwt.astype(BF16)
    rw_lo = (rwt - rw_hi.astype(F32)).astype(BF16)
    return pl.pallas_call(
        _route_kernel,
        grid=(n,),
        in_specs=[pl.BlockSpec((ROW_TILE, D), lambda i: (i, 0)),
                  pl.BlockSpec((E, D), lambda i: (0, 0)),
                  pl.BlockSpec((E, D), lambda i: (0, 0)),
                  pl.BlockSpec((E, 1), lambda i: (0, 0)),
                  pl.BlockSpec((ROW_TILE, ROW_TILE), lambda i: (0, 0))],
        out_specs=[cols, cols, cols, pl.BlockSpec((E, 1), lambda i: (0, 0))],
        out_shape=[sd((TOP_K, T), I32), sd((TOP_K, T), F32), sd((TOP_K, T), I32), sd((E, 1), F32)],
        scratch_shapes=[pltpu.VMEM((E, 1), F32)],
        compiler_params=_cparams(("arbitrary",)),
        name="moe_route",
    )(u, rw_hi, rw_lo, router_bias.reshape(E, 1), tri)


def _dest_kernel(e_ref, r_ref, ps_ref, d_ref):
    e = e_ref[...]
    TM = e.shape[1]
    io_e = lax.broadcasted_iota(I32, (N_EXPERTS, TM), 0)
    ps = ps_ref[...]
    rows = []
    for k in range(TOP_K):
        rows.append(jnp.sum(jnp.where(io_e == e[k:k + 1, :], ps, 0.0), axis=0, keepdims=True))
    d_ref[...] = jnp.concatenate(rows, axis=0).astype(I32) + r_ref[...]


def _slot_of(eidx, rank, pad_start):
    T = eidx.shape[1]
    cols = pl.BlockSpec((TOP_K, ROW_TILE), lambda i: (0, i))
    return pl.pallas_call(
        _dest_kernel,
        grid=(T // ROW_TILE,),
        in_specs=[cols, cols, pl.BlockSpec((N_EXPERTS, 1), lambda i: (0, 0))],
        out_specs=cols,
        out_shape=jax.ShapeDtypeStruct((TOP_K, T), I32),
        compiler_params=_cparams(("parallel",)),
        name="moe_slot",
    )(eidx, rank, pad_start.astype(F32).reshape(N_EXPERTS, 1))


def _dispatch_kernel(z0_ref, zn_ref, dest_ref, u_ref, xs_ref, zero_scr, sem, zsem):
    TM = u_ref.shape[0]

    @pl.when(pl.program_id(0) == 0)
    def _():
        zero_scr[...] = jnp.zeros_like(zero_scr)

        def zcopy(e):
            row0 = pl.multiple_of(z0_ref[e], MOE_BLK)
            return pltpu.make_async_copy(zero_scr, xs_ref.at[pl.ds(row0, MOE_BLK)], zsem)

        def zstart(e, c):
            @pl.when(zn_ref[e] > 0)
            def _():
                zcopy(e).start()
            return c

        def zwait(e, c):
            @pl.when(zn_ref[e] > 0)
            def _():
                zcopy(e).wait()
            return c

        lax.fori_loop(0, N_EXPERTS, zstart, 0)
        lax.fori_loop(0, N_EXPERTS, zwait, 0)

    def copy(t, k):
        return pltpu.make_async_copy(u_ref.at[t], xs_ref.at[dest_ref[k, t]], sem)

    def issue(t, c):
        for k in range(TOP_K):
            copy(t, k).start(priority=k % 2)
        return c

    lax.fori_loop(0, TM, issue, 0)
    for k in range(TOP_K):
        pltpu.make_async_copy(u_ref, xs_ref.at[pl.ds(0, TM)], sem).wait()


def _dispatch(up, dest, n_slots, pad_row0, pad_rows):
    T = up.shape[0]
    slab = up.shape[1:]
    return pl.pallas_call(
        _dispatch_kernel,
        grid_spec=pltpu.PrefetchScalarGridSpec(
            num_scalar_prefetch=2,
            grid=(T // ROW_TILE,),
            in_specs=[pl.BlockSpec((TOP_K, ROW_TILE), lambda i, z0, zn: (0, i), memory_space=pltpu.SMEM),
                      pl.BlockSpec((ROW_TILE,) + slab, lambda i, z0, zn: (i, 0, 0))],
            out_specs=pl.BlockSpec(memory_space=pl.ANY),
            scratch_shapes=[pltpu.VMEM((MOE_BLK,) + slab, U32), pltpu.SemaphoreType.DMA, pltpu.SemaphoreType.DMA]),
        out_shape=jax.ShapeDtypeStruct((n_slots,) + slab, U32),
        compiler_params=_cparams(("arbitrary",)),
        name="moe_dispatch",
    )(pad_row0, pad_rows, dest, up)


def _expert_kernel(be_ref, nu_ref, first_ref, nxt_ref, par_ref, xs_ref, w13_hbm, w2_hbm,
                   ys_ref, w13_buf, w2_buf, w13_bf, w2_bf, sem, *, layer):
    i = pl.program_id(0)

    def fetch(e, slot):
        return (pltpu.make_async_copy(w13_hbm.at[layer, e], w13_buf.at[slot], sem.at[0, slot]),
                pltpu.make_async_copy(w2_hbm.at[layer, e], w2_buf.at[slot], sem.at[1, slot]))

    @pl.when(i == 0)
    def _():
        for c in fetch(be_ref[0], 0):
            c.start()

    @pl.when(jnp.logical_and(i < nu_ref[0], first_ref[i] == 1))
    def _():
        slot = par_ref[i]
        for c in fetch(be_ref[i], slot):
            c.wait()

        @pl.when(nxt_ref[i] >= 0)
        def _():
            for c in fetch(nxt_ref[i], 1 - slot):
                c.start()

        w13_bf[...] = w13_buf[slot].astype(BF16)
        w2_bf[...] = w2_buf[slot].astype(BF16)

    @pl.when(i < nu_ref[0])
    def _():
        m = xs_ref.shape[0] // ROW_SUB
        lo, hi = _unpack_rows(jnp.concatenate(
            [xs_ref[pl.ds(j, m, stride=ROW_SUB), :] for j in range(ROW_SUB)], axis=1))
        x = jnp.concatenate([lo, hi], axis=1).astype(BF16)
        h = jnp.dot(x, w13_bf[...], preferred_element_type=F32)
        a = _silu(h[:, :EXPERT_FF]) * h[:, EXPERT_FF:]
        y = _pack_rows(jnp.dot(a.astype(BF16), w2_bf[...], preferred_element_type=F32))
        for j in range(ROW_SUB):
            ys_ref[pl.ds(j, m, stride=ROW_SUB), :] = y[:, j * LANES:(j + 1) * LANES]


def _expert_gemm(xs, blk_exp, n_used, cnt, w13, w2, layer):
    NP = xs.shape[0]
    slab = xs.shape[1:]
    D = 2 * slab[0] * slab[1]
    nblk = NP // MOE_BLK
    F2 = w13.shape[3]
    E = cnt.shape[0]
    blk = jnp.arange(nblk, dtype=I32)
    first = jnp.concatenate([jnp.ones((1,), I32), (blk_exp[1:] != blk_exp[:-1]).astype(I32)])
    first = jnp.where(blk < n_used[0], first, 0)
    par = (jnp.cumsum(first) - 1) % 2
    ids = jnp.where(cnt > 0, jnp.arange(E, dtype=I32), E)
    suffix_min = lax.cummin(ids, axis=0, reverse=True)
    next_active = jnp.concatenate([suffix_min[1:], jnp.full((1,), E, I32)])
    next_active = jnp.where(next_active >= E, -1, next_active)
    nxt = jnp.take(next_active, blk_exp)

    def xmap(i, be, nu, fi, nx, pa):
        return (jnp.minimum(i, nu[0] - 1), 0)

    flat = (NP * slab[0], slab[1])
    ys = pl.pallas_call(
        functools.partial(_expert_kernel, layer=layer),
        grid_spec=pltpu.PrefetchScalarGridSpec(
            num_scalar_prefetch=5,
            grid=(nblk,),
            in_specs=[pl.BlockSpec((MOE_BLK * slab[0], slab[1]), xmap),
                      pl.BlockSpec(memory_space=pl.ANY), pl.BlockSpec(memory_space=pl.ANY)],
            out_specs=pl.BlockSpec((MOE_BLK * slab[0], slab[1]), xmap),
            scratch_shapes=[pltpu.VMEM((2, D, F2), F32), pltpu.VMEM((2, F2 // 2, D), F32),
                            pltpu.VMEM((D, F2), BF16), pltpu.VMEM((F2 // 2, D), BF16),
                            pltpu.SemaphoreType.DMA((2, 2))]),
        out_shape=jax.ShapeDtypeStruct(flat, U32),
        compiler_params=_cparams(("arbitrary",)),
        name="moe_experts",
    )(blk_exp, n_used, first, nxt.astype(I32), par.astype(I32), xs.reshape(flat), w13, w2)
    return ys.reshape((NP,) + slab)


def _combine_kernel(mt_ref, dcur_ref, dnxt_ref, gate_ref, up_ref, xl_ref, mod_ref, s13_ref, s2_ref, lng_ref, lnb_ref,
                    ys_ref, o_ref, rows_a, rows_b, rows_c, rows_d, sem_a, sem_b, sem_c, sem_d, *, alpha, n_steps):
    del mt_ref
    i = pl.program_id(0)
    H = CMB_TILE
    W = ROW_SUB * LANES

    def issue(dref, col0, rows, sem):
        for t in range(H):
            for k in range(TOP_K):
                pltpu.make_async_copy(ys_ref.at[dref[k, col0 + t]], rows.at[k, pl.ds(t * ROW_SUB, ROW_SUB)],
                                      sem).start(priority=k % 2)

    def wait(rows, sem):
        for k in range(TOP_K):
            pltpu.make_async_copy(rows.at[k], rows.at[k], sem).wait()

    def compute(rows, r0):
        lo, hi = _unpack_rows(_load_row_slabs(up_ref, r0, H))
        hs = _bdot(jnp.concatenate([lo, hi], axis=1), s13_ref[...])
        sh = _bdot(_silu(hs[:, :SHARED_FF]) * hs[:, SHARED_FF:], s2_ref[...])
        acc_lo, acc_hi = sh[:, :W], sh[:, W:]
        gate = gate_ref[r0:r0 + H]
        for k in range(TOP_K):
            lo, hi = _unpack_rows(_load_row_slabs(rows.at[k]))
            g = gate[:, k:k + 1]
            acc_lo = acc_lo + lo * g
            acc_hi = acc_hi + hi * g
        acc = jnp.concatenate([acc_lo, acc_hi], axis=1)
        g2 = mod_ref[5:6, :]
        o_ref[r0:r0 + H] = _layer_norm_rows(alpha * xl_ref[r0:r0 + H] + g2 * acc, lng_ref[...], lnb_ref[...], LN_EPS)

    pair0 = ((rows_a, sem_a), (rows_b, sem_b))
    pair1 = ((rows_c, sem_c), (rows_d, sem_d))

    @pl.when(i == 0)
    def _():
        issue(dcur_ref, 0, *pair0[0])
        issue(dcur_ref, H, *pair0[1])

    def step(cur, nxt):
        wait(*cur[0])
        issue(dnxt_ref, 0, *nxt[0])
        compute(cur[0][0], 0)
        wait(*cur[1])
        issue(dnxt_ref, H, *nxt[1])
        compute(cur[1][0], H)

    @pl.when(i % 2 == 0)
    def _():
        step(pair0, pair1)

    @pl.when(i % 2 == 1)
    def _():
        step(pair1, pair0)

    @pl.when(i == n_steps - 1)
    def _():
        for rows, sem in (pair1 if (n_steps - 1) % 2 == 0 else pair0):
            wait(rows, sem)


def _combine(dest, gate_t, up, xl, ys, mods, mod_of_tile, sh_w13, sh_w2, ln_g, ln_b, alpha):
    T, D = xl.shape
    TM = 2 * CMB_TILE
    n = T // TM

    def rows(w):
        return pl.BlockSpec((TM, w), lambda i, mt: (i, 0))

    def const(shape):
        return pl.BlockSpec(shape, lambda i, mt: (0,) * len(shape))

    return pl.pallas_call(
        functools.partial(_combine_kernel, alpha=alpha, n_steps=n),
        grid_spec=pltpu.PrefetchScalarGridSpec(
            num_scalar_prefetch=1,
            grid=(n,),
            in_specs=[pl.BlockSpec((TOP_K, TM), lambda i, mt: (0, i), memory_space=pltpu.SMEM),
                      pl.BlockSpec((TOP_K, TM), lambda i, mt: (0, jnp.minimum(i + 1, n - 1)), memory_space=pltpu.SMEM),
                      rows(TOP_K), pl.BlockSpec((TM * ROW_SUB, LANES), lambda i, mt: (i, 0)), rows(D),
                      pl.BlockSpec((None, 6, D), lambda i, mt: (mt[i], 0, 0)),
                      const((D, 2 * SHARED_FF)), const((SHARED_FF, D)), const((1, D)), const((1, D)),
                      pl.BlockSpec(memory_space=pl.ANY)],
            out_specs=rows(D),
            scratch_shapes=[pltpu.VMEM((TOP_K, CMB_TILE * ROW_SUB, LANES), U32)] * 4
                           + [pltpu.SemaphoreType.DMA] * 4),
        out_shape=jax.ShapeDtypeStruct((T, D), F32),
        compiler_params=_cparams(("arbitrary",)),
        name="moe_combine",
    )(mod_of_tile, dest, dest, gate_t, up, xl, mods, sh_w13.astype(BF16), sh_w2.astype(BF16),
      ln_g.reshape(1, D), ln_b.reshape(1, D), ys)


def _moe_block(u, up, xl, mods, mod_of_tile, router_w, router_bias, w13, w2, layer, sh_w13, sh_w2, ln_g, ln_b, alpha):
    T, D = u.shape
    up = up.reshape(T, ROW_SUB, LANES)
    eidx, gate, rank, counts = _route(u, router_w, router_bias)
    cnt = counts.reshape(N_EXPERTS).astype(I32)
    padded = (cnt + MOE_BLK - 1) // MOE_BLK * MOE_BLK
    pad_end = jnp.cumsum(padded)
    pad_start = pad_end - padded
    nblk = T * TOP_K // MOE_BLK + N_EXPERTS
    n_used = (pad_end[-1] // MOE_BLK).astype(I32)
    blk_row0 = jnp.arange(nblk, dtype=I32) * MOE_BLK
    blk_exp = jnp.sum((pad_end[None, :] <= blk_row0[:, None]).astype(I32), axis=1)
    last_e = jnp.max(jnp.where(cnt > 0, jnp.arange(N_EXPERTS, dtype=I32), 0))
    blk_exp = jnp.minimum(blk_exp, last_e)
    dest = _slot_of(eidx, rank, pad_start)
    xs = _dispatch(up, dest, nblk * MOE_BLK, jnp.maximum(pad_end - MOE_BLK, 0), padded)
    ys = _expert_gemm(xs, blk_exp, n_used.reshape(1), cnt, w13, w2, layer)
    return _combine(dest, gate.T, up.reshape(T * ROW_SUB, LANES), xl, ys, mods, mod_of_tile, sh_w13, sh_w2,
                    ln_g, ln_b, alpha)


def _in1_kernel(x_ref, mod_ref, wh_ref, wr_ref, ph_ref, pr_ref, *, nctx_tiles):
    sh1 = mod_ref[0:1, :]
    sc1 = mod_ref[1:2, :]
    h = (x_ref[...] * (1.0 + sc1) + sh1).astype(BF16)
    pr_ref[...] = jnp.dot(h, wr_ref[...], preferred_element_type=F32)

    @pl.when(pl.program_id(1) >= nctx_tiles)
    def _():
        ph_ref[...] = jnp.dot(h, wh_ref[...], preferred_element_type=F32)


RW_PW = 1920


def _rw_reorder(t):
    a = 3 * RW_W
    lo = 2 * RW_DECAY_LORA + RW_AAA_LORA
    out = jnp.concatenate([t[..., :a], t[..., a + lo:], t[..., a:a + lo]], axis=-1)
    pad = [(0, 0)] * (t.ndim - 1) + [(0, RW_PW - out.shape[-1])]
    return jnp.pad(out, pad)


def _in_proj1(xcat, mods, nctx_tiles, w_in):
    B, S, D = xcat.shape
    J = S // ROW_TILE
    T = B * S
    wh = w_in[:, :HY_WIDTH].astype(BF16)
    wr = _rw_reorder(w_in[:, HY_WIDTH:]).astype(BF16)

    def rows(w):
        return pl.BlockSpec((ROW_TILE, w), lambda b, j: (b * J + j, 0))

    def const(shape):
        return pl.BlockSpec(shape, lambda b, j: (0,) * len(shape))

    sd = jax.ShapeDtypeStruct
    JL = J - nctx_tiles
    return pl.pallas_call(
        functools.partial(_in1_kernel, nctx_tiles=nctx_tiles),
        grid=(B, J),
        in_specs=[rows(D), pl.BlockSpec((None, 6, D), lambda b, j: (jnp.where(j < nctx_tiles, B, b), 0, 0)),
                  const((D, HY_WIDTH)), const((D, RW_PW))],
        out_specs=[pl.BlockSpec((ROW_TILE, HY_WIDTH), lambda b, j: (b * JL + jnp.maximum(j - nctx_tiles, 0), 0)),
                   rows(RW_PW)],
        out_shape=[sd((B * JL * ROW_TILE, HY_WIDTH), F32), sd((T, RW_PW), F32)],
        compiler_params=_cparams(("parallel", "arbitrary")),
        name="in_proj1",
    )(xcat.reshape(T, D), mods, wh, wr)


def _rw_streams_kernel(prev_ref, main_ref, next_ref, mu_ref, w0_ref, w2_ref, a0_ref, a2_ref, g2_ref,
                       kk_ref, ka_ref, bd_ref,
                       r_ref, k_ref, v_ref, kkn_ref, bb_ref, ld_ref, g_ref, sh_scr, *, nctx_tiles, n_lat_tiles):
    j = pl.program_id(1)
    TM = main_ref.shape[0]
    W = main_ref.shape[1]
    H = GRID_W
    p = main_ref[...]
    ext = jnp.concatenate([prev_ref[...], p, next_ref[...]], axis=0)
    left = ext[H - 1:H - 1 + TM]
    right = ext[H + 1:H + 1 + TM]
    up = ext[0:TM]
    down = ext[2 * H:2 * H + TM]
    i = lax.broadcasted_iota(I32, (TM, W), 0)
    lane = lax.broadcasted_iota(I32, (TM, W), 1)
    even = (lane & 1) == 0
    c4 = lane & 3
    jl = j - nctx_tiles

    @pl.when(j < nctx_tiles)
    def _():
        lo = jnp.where(j == 0, 1, 0)
        hi = jnp.where(j == nctx_tiles - 1, TM - 1, TM)
        sh_scr[...] = jnp.where(even, jnp.where(i >= lo, left, 0.0), jnp.where(i < hi, right, 0.0))

    @pl.when(j >= nctx_tiles)
    def _():
        col = i & (H - 1)
        up_lo = jnp.where(jl == 0, H, 0)
        down_hi = jnp.where(jl == n_lat_tiles - 1, TM - H, TM)
        l_v = jnp.where(col != 0, left, 0.0)
        r_v = jnp.where(col != H - 1, right, 0.0)
        u_v = jnp.where(i >= up_lo, up, 0.0)
        d_v = jnp.where(i < down_hi, down, 0.0)
        sh_scr[...] = jnp.where(c4 == 0, l_v, jnp.where(c4 == 1, r_v, jnp.where(c4 == 2, u_v, d_v)))

    pm = p + mu_ref[...] * (sh_scr[...] - p)
    r = pm[:, 0:RW_W]
    k = pm[:, RW_W:2 * RW_W]
    v = pm[:, 2 * RW_W:3 * RW_W]
    o = 3 * RW_W
    gl = pm[:, o:o + RW_GATE_LORA]
    o += RW_GATE_LORA
    wl_f = pm[:, o:o + RW_DECAY_LORA]
    wl_b = pm[:, o + RW_DECAY_LORA:o + 2 * RW_DECAY_LORA]
    al = pm[:, o + 2 * RW_DECAY_LORA:o + 2 * RW_DECAY_LORA + RW_AAA_LORA]
    for d, wl in enumerate((wl_f, wl_b)):
        z = w0_ref[d:d + 1, :] + _bdot(jnp.tanh(wl), w2_ref[d])
        w = -(jnp.maximum(-z, 0.0) + jnp.log(1.0 + jnp.exp(-jnp.abs(z)))) - 0.5
        ld_ref[d] = -jnp.exp(w)
    a = jax.nn.sigmoid(a0_ref[...] + _bdot(al, a2_ref[...]))
    g_ref[...] = _bdot(jax.nn.sigmoid(gl), g2_ref[...])
    kk = k * kk_ref[...]
    n2 = _head_sums(kk * kk, bd_ref[...])
    kkn = kk / jnp.maximum(jnp.sqrt(n2), 1e-12)
    r_ref[...] = r.astype(BF16)
    k_ref[...] = (k * (1.0 + (a - 1.0) * ka_ref[...])).astype(BF16)
    v_ref[...] = v.astype(BF16)
    kkn_ref[...] = kkn.astype(BF16)
    bb_ref[...] = (kkn * a).astype(BF16)


def _head_sum_matrix(width, hd):
    i = np.arange(width)
    return jnp.asarray((i[:, None] // hd == i[None, :] // hd).astype(np.float32), BF16)


def _rw_streams(p_rw, B, S, nctx_tiles, mu, w0, w2, a0, a2, g2, k_k, k_a):
    T = B * S
    J = S // ROW_TILE
    HB = ROW_TILE // GRID_W
    NH = S // GRID_W

    def rows(w):
        return pl.BlockSpec((ROW_TILE, w), lambda b, j: (b * J + j, 0))

    def rows2(w):
        return pl.BlockSpec((2, ROW_TILE, w), lambda b, j: (0, b * J + j, 0))

    def const(shape):
        return pl.BlockSpec(shape, lambda b, j: (0,) * len(shape))

    sd = jax.ShapeDtypeStruct
    kern = functools.partial(_rw_streams_kernel, nctx_tiles=nctx_tiles, n_lat_tiles=J - nctx_tiles)
    return pl.pallas_call(
        kern,
        grid=(B, J),
        in_specs=[pl.BlockSpec((GRID_W, RW_PW), lambda b, j: (b * NH + jnp.maximum(j * HB - 1, 0), 0)),
                  rows(RW_PW),
                  pl.BlockSpec((GRID_W, RW_PW), lambda b, j: (b * NH + jnp.minimum((j + 1) * HB, NH - 1), 0)),
                  const((1, RW_PW)), const((2, RW_W)), const((2, RW_DECAY_LORA, RW_W)), const((1, RW_W)),
                  const((RW_AAA_LORA, RW_W)), const((RW_GATE_LORA, RW_W)), const((1, RW_W)), const((1, RW_W)),
                  const((RW_W, RW_W))],
        out_specs=[rows(RW_W), rows(RW_W), rows(RW_W), rows(RW_W), rows(RW_W), rows2(RW_W), rows(RW_W)],
        out_shape=[sd((T, RW_W), BF16)] * 5 + [sd((2, T, RW_W), F32), sd((T, RW_W), F32)],
        scratch_shapes=[pltpu.VMEM((ROW_TILE, RW_PW), F32)],
        compiler_params=_cparams(("parallel", "parallel")),
        name="rwkv_streams",
    )(p_rw, p_rw, p_rw, _rw_reorder(mu).reshape(1, RW_PW), w0, w2, a0.reshape(1, RW_W), a2, g2,
      k_k.reshape(1, RW_W), k_a.reshape(1, RW_W), _head_sum_matrix(RW_W, RW_HEAD_DIM))


def _rwkv_kernel(r_ref, k_ref, v_ref, kk_ref, bb_ref, ld_ref, o_ref, st_ref, *, rev, nchunk):
    t = pl.program_id(1)

    @pl.when(t == 0)
    def _():
        st_ref[...] = jnp.zeros_like(st_ref)

    C = RW_CHUNK
    hd = RW_HEAD_DIM
    NH = RW_HEADS
    row = lax.broadcasted_iota(I32, (C, C), 0)
    col = lax.broadcasted_iota(I32, (C, C), 1)
    incl = ((col >= row) if rev else (col <= row))[None]
    strict = ((col > row) if rev else (col < row))[None]
    tri = jnp.where(incl[0], 1.0, 0.0).astype(BF16)
    eye = (row == col)[None]
    last_i = 0 if rev else C - 1
    order = range(nchunk - 1, -1, -1) if rev else range(nchunk)
    n_double = int(math.log2(C)) - 1

    def stack(x):
        return jnp.stack([x[c * C:(c + 1) * C, h * hd:(h + 1) * hd] for c in range(nchunk) for h in range(NH)], 0)

    ld = ld_ref[...]
    g = jnp.concatenate([_split_dot(tri, ld[c * C:(c + 1) * C]) for c in range(nchunk)], axis=0)
    g_last = jnp.concatenate([jnp.broadcast_to(g[c * C + last_i:c * C + last_i + 1], (C, NH * hd))
                              for c in range(nchunk)], axis=0)
    k = k_ref[...].astype(F32)
    bb = bb_ref[...].astype(F32)
    eng = jnp.exp(-g)
    e_end = jnp.exp(g_last - g)
    kk_t = stack(kk_ref[...].astype(F32) * jnp.exp(g - ld))
    b_t = stack(bb * eng)
    k_t = stack(k * eng)
    r_t = stack(r_ref[...].astype(F32) * jnp.exp(g))
    b_bar = stack(bb * e_end)
    k_bar = stack(k * e_end)
    dec = stack(jnp.exp(g_last))
    v = stack(v_ref[...].astype(F32))
    m_b = jnp.where(strict, _bmm_nt(kk_t, b_t), 0.0)
    m_k = jnp.where(strict, _bmm_nt(kk_t, k_t), 0.0)
    a_rb = jnp.where(incl, _bmm_nt(r_t, b_t), 0.0)
    a_rk = jnp.where(incl, _bmm_nt(r_t, k_t), 0.0)
    pw = -m_b
    tinv = jnp.where(eye, 1.0, 0.0) + pw
    for _ in range(n_double):
        pw = _bmm(pw, pw)
        tinv = tinv + _bmm(tinv, pw)
    a_t = _bmm(tinv, kk_t)
    u_b = _bmm(tinv, _bmm(m_k, v))
    q_h = r_t - _bmm(a_rb, a_t)
    o_h = _bmm(a_rk, v) - _bmm(a_rb, u_b)
    g_m = jnp.where(eye, dec, 0.0) - _bmm_tn(a_t, b_bar)
    h_m = _bmm_tn(v, k_bar) - _bmm_tn(u_b, b_bar)
    s = st_ref[...]
    for c in order:
        sl = slice(c * NH, (c + 1) * NH)
        o_c = _bmm_nt(q_h[sl], s) + o_h[sl]
        o_ref[c * C:(c + 1) * C, :] = jnp.concatenate([o_c[h] for h in range(NH)], axis=1)
        s = _bmm(s, g_m[sl]) + h_m[sl]
    st_ref[...] = s


def _rwkv_scan(r, k, v, kk, bb, ld, B, S, nctx_blocks, rev):
    T = B * S
    NB = S // ROW_TILE
    d = 1 if rev else 0
    nc = nctx_blocks

    def blk(t):
        if not rev:
            return t
        return jnp.where(t < nc, nc - 1 - t, NB - 1 - (t - nc))

    spec = pl.BlockSpec((ROW_TILE, RW_W), lambda b, t: (b * NB + blk(t), 0))
    kern = functools.partial(_rwkv_kernel, rev=rev, nchunk=ROW_TILE // RW_CHUNK)
    return pl.pallas_call(
        kern,
        grid=(B, NB),
        in_specs=[spec, spec, spec, spec, spec,
                  pl.BlockSpec((None, ROW_TILE, RW_W), lambda b, t: (d, b * NB + blk(t), 0))],
        out_specs=spec,
        out_shape=jax.ShapeDtypeStruct((T, RW_W), F32),
        scratch_shapes=[pltpu.VMEM((RW_HEADS, RW_HEAD_DIM, RW_HEAD_DIM), F32)],
        compiler_params=_cparams(("parallel", "arbitrary")),
        name="rwkv_scan_bwd" if rev else "rwkv_scan_fwd",
    )(r, k, v, kk, bb, ld)


def _filter_mlp_kernel(z_ref, w1_ref, b1_ref, w2_ref, b2_ref, w3_ref, sf_ref, win_ref, f_ref):
    h = jnp.sin(sf_ref[0:1, :] * (_hdot(z_ref[...], w1_ref[...]) + b1_ref[...]))
    h = jnp.sin(sf_ref[1:2, :] * (_hdot(h, w2_ref[...]) + b2_ref[...]))
    f_ref[...] = _hdot(h, w3_ref[...]) * win_ref[...]


def _hyena_filters(L, w1, b1, w2, b2, w3, sin_freq):
    t = np.linspace(0.0, 1.0, L, dtype=np.float32)[:, None]
    bands = (HY_EMB - 1) // 2
    wpos = (2.0 * math.pi * np.arange(L, dtype=np.float32)[:, None] / L).astype(np.float32)
    fr = np.linspace(1e-4, bands - 1, bands, dtype=np.float32)[None, :]
    z = np.concatenate([t, np.cos(fr * wpos), -np.sin(fr * wpos)], -1).astype(np.float32)
    max_decay = math.log(HY_TARGET) / HY_FAST_DECAY
    min_decay = math.log(HY_TARGET) / HY_SLOW_DECAY
    deltas = np.linspace(min_decay, max_decay, HY_CH, dtype=np.float32)
    window = np.exp(-t * np.abs(deltas)).astype(np.float32)
    dist = np.concatenate([np.arange(L), [0], np.arange(L - 1, 0, -1)])
    z2 = np.pad(z[dist], ((0, 0), (0, LANES - HY_EMB)))
    win2 = np.tile(window[dist], (1, HY_ORDER))
    win2[L] = 0.0
    w1p = jnp.pad(w1, ((0, LANES - HY_EMB), (0, 0)))
    Hd = w1.shape[1]
    FW = HY_ORDER * HY_CH
    w3s = jnp.transpose(w3.reshape(Hd, HY_ORDER, 2, HY_CH), (2, 0, 1, 3)).reshape(2, Hd, FW)
    TR = min(L, 512)
    nl = L // TR
    return pl.pallas_call(
        _filter_mlp_kernel,
        grid=(2 * nl,),
        in_specs=[pl.BlockSpec((TR, LANES), lambda i: (i, 0)),
                  pl.BlockSpec((LANES, Hd), lambda i: (0, 0)), pl.BlockSpec((1, Hd), lambda i: (0, 0)),
                  pl.BlockSpec((Hd, Hd), lambda i: (0, 0)), pl.BlockSpec((1, Hd), lambda i: (0, 0)),
                  pl.BlockSpec((None, Hd, FW), lambda i: (i // nl, 0, 0)), pl.BlockSpec((2, Hd), lambda i: (0, 0)),
                  pl.BlockSpec((TR, FW), lambda i: (i, 0))],
        out_specs=pl.BlockSpec((TR, FW), lambda i: (i, 0)),
        out_shape=jax.ShapeDtypeStruct((2 * L, FW), F32),
        compiler_params=_cparams(("parallel",)),
        name="hyena_filter_mlp",
    )(jnp.asarray(z2), w1p, b1.reshape(1, Hd), w2, b2.reshape(1, Hd), w3s, sin_freq, jnp.asarray(win2))


def _sconv_kernel(p_ref, w_ref, b_ref, o_ref):
    p = p_ref[...]
    L = p.shape[0]
    i = lax.broadcasted_iota(I32, p.shape, 0)
    prev = jnp.where(i == 0, 0.0, pltpu.roll(p, 1, 0))
    nxt = jnp.where(i == L - 1, 0.0, pltpu.roll(p, L - 1, 0))
    o_ref[...] = prev * w_ref[0:1, :] + p * w_ref[1:2, :] + nxt * w_ref[2:3, :] + b_ref[...]


def _short_conv(p_hy, B, S, Lc, conv_w, conv_b):
    L = S - Lc
    p3 = p_hy.reshape(B, L, HY_WIDTH)
    return pl.pallas_call(
        _sconv_kernel,
        grid=(B, HY_WIDTH // LANES),
        in_specs=[pl.BlockSpec((None, L, LANES), lambda b, c: (b, 0, c)),
                  pl.BlockSpec((HY_SHORT, LANES), lambda b, c: (0, c)),
                  pl.BlockSpec((1, LANES), lambda b, c: (0, c))],
        out_specs=pl.BlockSpec((None, L, LANES), lambda b, c: (b, 0, c)),
        out_shape=jax.ShapeDtypeStruct((B, L, HY_WIDTH), F32),
        compiler_params=_cparams(("parallel", "parallel")),
        name="hyena_short_conv",
    )(p3, conv_w, conv_b.reshape(1, HY_WIDTH))


def _dft_constants(L):
    n = 2 * L
    n2 = FFT_N2
    n1 = n // n2
    na = n1 // 2
    g8 = SUBLANES
    w1 = np.exp(-2j * np.pi * np.outer(np.arange(n1), np.arange(n1)) / n1)
    eye8 = np.eye(g8)

    def kron_fwd(w, real_in):
        k1n, an = w.shape
        blocks = np.stack([np.stack([w.real, -w.imag], 1), np.stack([w.imag, w.real], 1)], 1)
        if real_in:
            blocks = blocks[:, :, 0:1, :]
        m = np.einsum('kria,bc->krbiac', blocks, eye8)
        return m.reshape(k1n * 2 * g8, blocks.shape[2] * an * g8)

    m1 = kron_fwd(w1[:, :na], False)
    m1f = kron_fwd(w1, True)
    cw = np.conj(w1[:, :na]).T / n
    blocks = np.stack([np.stack([cw.real, -cw.imag], 1), np.stack([cw.imag, cw.real], 1)], 0)
    m1inv = np.einsum('iark,bc->iabkrc', blocks, eye8).reshape(2 * na * g8, n1 * 2 * g8)
    w2 = np.exp(-2j * np.pi * np.outer(np.arange(n2), np.arange(n2)) / n2)
    w2big = np.block([[w2.real, -w2.imag], [w2.imag, w2.real]])
    w2c = np.conj(w2)
    iw2big = np.block([[w2c.real, -w2c.imag], [w2c.imag, w2c.real]])
    tw = np.exp(-2j * np.pi * np.arange(n2) / n)

    def lanes(cols):
        z = np.concatenate([np.broadcast_to(col[:, None], (n2, LANES)) for col in cols], axis=1)
        return np.stack([z.real, z.imag], 0)

    tw0 = lanes([tw ** q for q in range(FFT_G)])
    twg = lanes([tw ** FFT_G] * FFT_G)
    c = lambda x, dt: jnp.asarray(np.ascontiguousarray(x), dt)
    return dict(m1=c(m1, BF16), m1f=c(m1f, BF16), m1inv=c(m1inv, BF16), w2=c(w2big, BF16), iw2=c(iw2big, BF16),
                tw0=c(tw0, F32), twg=c(twg, F32), n1=n1, na=na)


def _cplx_rows(z, twr, twi, conj):
    n2 = z.shape[0] // 2
    zr, zi = z[:n2], z[n2:]
    if conj:
        return jnp.concatenate([zr * twr + zi * twi, zi * twr - zr * twi], axis=0)
    return jnp.concatenate([zr * twr - zi * twi, zi * twr + zr * twi], axis=0)


def _split_dot(m, x):
    hi = x.astype(BF16)
    lo = (x - hi.astype(F32)).astype(BF16)
    return jnp.dot(m, hi, preferred_element_type=F32) + jnp.dot(m, lo, preferred_element_type=F32)


def _next_twiddle(tw, twg_ref):
    twr, twi = tw
    b_r, b_i = twg_ref[0], twg_ref[1]
    return twr * b_r - twi * b_i, twr * b_i + twi * b_r


def _load_group(scr, j, C):
    return jnp.concatenate([scr[j * FFT_G + q].reshape(2 * FFT_N2, C) for q in range(FFT_G)], axis=1)


def _filt_fft_kernel(f_ref, m1_ref, w2_ref, tw0_ref, twg_ref, o_ref, y_scr, *, n1):
    nbg = FFT_N2 // SUBLANES
    C = f_ref.shape[-1]
    for bg in range(nbg):
        yg = _split_dot(m1_ref[...], f_ref[:, bg].reshape(n1 * SUBLANES, C))
        y_scr[:, :, bg] = yg.reshape(n1, 2, SUBLANES, C)

    def body(j, tw):
        y = _cplx_rows(_load_group(y_scr, j, C), tw[0], tw[1], False)
        z = _split_dot(w2_ref[...], y)
        for q in range(FFT_G):
            o_ref[j * FFT_G + q] = z[:, q * C:(q + 1) * C]
        return _next_twiddle(tw, twg_ref)

    lax.fori_loop(0, n1 // FFT_G, body, (tw0_ref[0], tw0_ref[1]))


def _filter_spectrum(filt, L, dc):
    n1 = dc['n1']
    nbg = FFT_N2 // SUBLANES
    ncb = HY_CH // LANES
    full = filt.reshape(n1, nbg, SUBLANES, HY_ORDER * HY_CH)
    return pl.pallas_call(
        functools.partial(_filt_fft_kernel, n1=n1),
        grid=(HY_ORDER, ncb),
        in_specs=[pl.BlockSpec((n1, nbg, SUBLANES, LANES), lambda o, c: (0, 0, 0, o * ncb + c)),
                  pl.BlockSpec(dc['m1f'].shape, lambda o, c: (0, 0)),
                  pl.BlockSpec(dc['w2'].shape, lambda o, c: (0, 0)),
                  pl.BlockSpec(dc['tw0'].shape, lambda o, c: (0, 0, 0)),
                  pl.BlockSpec(dc['twg'].shape, lambda o, c: (0, 0, 0))],
        out_specs=pl.BlockSpec((None, n1, 2 * FFT_N2, LANES), lambda o, c: (o, 0, 0, c)),
        out_shape=jax.ShapeDtypeStruct((HY_ORDER, n1, 2 * FFT_N2, HY_CH), F32),
        scratch_shapes=[pltpu.VMEM((n1, 2, nbg, SUBLANES, LANES), F32)],
        compiler_params=_cparams(("parallel", "parallel")),
        name="hyena_filter_fft",
    )(full, dc['m1f'], dc['w2'], dc['tw0'], dc['twg'])


def _hyconv_kernel(za_ref, zb_ref, ga_ref, gb_ref, ff_ref, bias_ref, m1_ref, m1i_ref, w2_ref, iw2_ref,
                   tw0_ref, twg_ref, oa_ref, ob_ref, y_scr, *, n1, na):
    nbg = FFT_N2 // SUBLANES
    C = za_ref.shape[-1]
    half = na * SUBLANES
    for bg in range(nbg):
        xg = jnp.concatenate([za_ref[:, bg].reshape(half, C), zb_ref[:, bg].reshape(half, C)], axis=0)
        yg = jnp.dot(m1_ref[...], xg.astype(BF16), preferred_element_type=F32)
        y_scr[:, :, bg] = yg.reshape(n1, 2, SUBLANES, C)

    def body(j, tw):
        y = _cplx_rows(_load_group(y_scr, j, C), tw[0], tw[1], False)
        z = jnp.dot(w2_ref[...], y.astype(BF16), preferred_element_type=F32)
        f = _load_group(ff_ref, j, C)
        zr, zi = z[:FFT_N2], z[FFT_N2:]
        fr, fi = f[:FFT_N2], f[FFT_N2:]
        p = jnp.concatenate([zr * fr - zi * fi, zr * fi + zi * fr], axis=0)
        v = jnp.dot(iw2_ref[...], p.astype(BF16), preferred_element_type=F32)
        v = _cplx_rows(v, tw[0], tw[1], True)
        for q in range(FFT_G):
            y_scr[j * FFT_G + q] = v[:, q * C:(q + 1) * C].reshape(2, nbg, SUBLANES, C)
        return _next_twiddle(tw, twg_ref)

    lax.fori_loop(0, n1 // FFT_G, body, (tw0_ref[0], tw0_ref[1]))
    bias = bias_ref[...]
    for bg in range(nbg):
        vg = y_scr[:, :, bg].reshape(n1 * 2 * SUBLANES, C)
        out = jnp.dot(m1i_ref[...], vg.astype(BF16), preferred_element_type=F32)
        ya = out[:half].reshape(na, SUBLANES, C)
        yb = out[half:].reshape(na, SUBLANES, C)
        za = za_ref[:, bg]
        zb = zb_ref[:, bg]
        oa_ref[:, bg] = ga_ref[:, bg] * (ya + za * bias)
        ob_ref[:, bg] = gb_ref[:, bg] * (yb + zb * bias)


def _hyena_conv(z, z_col0, gates, g_col0, spec, bias, dc, B, L):
    n1, na = dc['n1'], dc['na']
    nbg = FFT_N2 // SUBLANES
    NCB = HY_CH // LANES

    def view(a):
        return a.reshape(B, na, nbg, SUBLANES, a.shape[-1])

    def seq(col0, which):
        return pl.BlockSpec((None, na, nbg, SUBLANES, LANES),
                            lambda c, p: (2 * p + which, 0, 0, 0, col0 + c))

    def const(a):
        nd = a.ndim
        return pl.BlockSpec(a.shape, lambda c, p: (0,) * nd)

    out_a, out_b = pl.pallas_call(
        functools.partial(_hyconv_kernel, n1=n1, na=na),
        grid=(NCB, B // 2),
        in_specs=[seq(z_col0, 0), seq(z_col0, 1), seq(g_col0, 0), seq(g_col0, 1),
                  pl.BlockSpec((n1, 2 * FFT_N2, LANES), lambda c, p: (0, 0, c), pipeline_mode=pl.Buffered(1)),
                  pl.BlockSpec((1, LANES), lambda c, p: (0, c)),
                  const(dc['m1']), const(dc['m1inv']), const(dc['w2']), const(dc['iw2']),
                  const(dc['tw0']), const(dc['twg'])],
        out_specs=[pl.BlockSpec((None, na, nbg, SUBLANES, LANES), lambda c, p: (p, 0, 0, 0, c)),
                   pl.BlockSpec((None, na, nbg, SUBLANES, LANES), lambda c, p: (p, 0, 0, 0, c))],
        out_shape=[jax.ShapeDtypeStruct((B // 2, na, nbg, SUBLANES, HY_CH), F32)] * 2,
        scratch_shapes=[pltpu.VMEM((n1, 2, nbg, SUBLANES, LANES), F32)],
        compiler_params=_cparams(("parallel", "arbitrary")),
        name="hyena_long_conv",
    )(view(z), view(z), view(gates), view(gates), spec, bias.reshape(1, HY_CH),
      dc['m1'], dc['m1inv'], dc['w2'], dc['iw2'], dc['tw0'], dc['twg'])
    out = jnp.stack([out_a, out_b], axis=1)
    return out.reshape(B, L, HY_CH)


def _out1_kernel(hy_ref, of_ref, ob_ref, r_ref, k_ref, v_ref, g_ref, x_ref, mod_ref, wo_ref,
                 rk_ref, lg_ref, lb_ref, bd_ref, lng_ref, lnb_ref, xl_ref, u_ref, up_ref, *, alpha):
    o = of_ref[...] + ob_ref[...]
    bd = bd_ref[...]
    inv = 1.0 / RW_HEAD_DIM
    mu = _head_sums(o, bd) * inv
    oc = o - mu
    var = _head_sums(oc * oc, bd) * inv
    on = oc * lax.rsqrt(var + RW_GN_EPS) * lg_ref[...] + lb_ref[...]
    bonus = (_head_sums(r_ref[...].astype(F32) * k_ref[...].astype(F32) * rk_ref[...], bd)
             * v_ref[...].astype(F32))
    y = jnp.concatenate([hy_ref[...], (on + bonus) * g_ref[...]], axis=1)
    _post_mix(y, x_ref[...], mod_ref, wo_ref, lng_ref, lnb_ref, alpha, xl_ref, u_ref, up_ref)


def _readout1(hy, o_f, o_b, r, k, v, g, xcat, mods, B, S, Lc, w_out, r_k, ln_g, ln_b, dn_g, dn_b, alpha):
    D = xcat.shape[-1]
    L = S - Lc
    J = S // ROW_TILE
    JL = L // ROW_TILE
    JC = Lc // ROW_TILE

    def cat_rows(w):
        return pl.BlockSpec((ROW_TILE, w), lambda b, j: (b * J + JC + j, 0))

    def lat_rows(w):
        return pl.BlockSpec((ROW_TILE, w), lambda b, j: (b * JL + j, 0))

    def const(shape):
        return pl.BlockSpec(shape, lambda b, j: (0,) * len(shape))

    sd = jax.ShapeDtypeStruct
    return pl.pallas_call(
        functools.partial(_out1_kernel, alpha=alpha),
        grid=(B, JL),
        in_specs=[lat_rows(HY_CH), cat_rows(RW_W), cat_rows(RW_W), cat_rows(RW_W), cat_rows(RW_W), cat_rows(RW_W),
                  cat_rows(RW_W), cat_rows(D),
                  pl.BlockSpec((None, 6, D), lambda b, j: (b, 0, 0)),
                  const((HY_CH + RW_W, D)), const((1, RW_W)), const((1, RW_W)), const((1, RW_W)),
                  const((RW_W, RW_W)), const((1, D)), const((1, D))],
        out_specs=[lat_rows(D), lat_rows(D),
                   pl.BlockSpec((ROW_TILE * ROW_SUB, LANES), lambda b, j: (b * JL + j, 0))],
        out_shape=[sd((B * L, D), F32), sd((B * L, D), F32), sd((B * L * ROW_SUB, LANES), U32)],
        compiler_params=_cparams(("parallel", "parallel")),
        name="readout1",
    )(hy, o_f, o_b, r, k, v, g, xcat.reshape(B * S, D), mods, w_out.astype(BF16),
      r_k.reshape(1, RW_W), ln_g.reshape(1, RW_W), ln_b.reshape(1, RW_W),
      _head_sum_matrix(RW_W, RW_HEAD_DIM), dn_g.reshape(1, D), dn_b.reshape(1, D))


def kernel(x, c, ctx, c_ctx, mod_w, mod_b, ln1_g, ln1_b, ln2_g, ln2_b, ev_w_in, ev_w_out, gla_gate_w2, gla_gate_b, gla_norm_g, hg_lb_logits, hg_norm_g, od_w_in, od_w_out, hy_conv_w, hy_conv_b, hy_ffn_w1, hy_ffn_b1, hy_ffn_w2, hy_ffn_b2, hy_ffn_w3, hy_sin_freq, hy_bias, rw_mu, rw_w0, rw_w2, rw_a0, rw_a2, rw_g2, rw_k_k, rw_k_a, rw_r_k, rw_ln_g, rw_ln_b, router_w, router_bias, exp_w13, exp_w2, sh_w13, sh_w2):
    B, L, D = x.shape
    Lc = ctx.shape[1]
    S = Lc + L
    depth = mod_w.shape[0]
    assert depth == 2 and L % ROW_TILE == 0 and Lc % ROW_TILE == 0 and B % 2 == 0
    assert L % (FFT_N2 * 2) == 0 and L % GRID_W == 0
    alpha = (2 * depth) ** 0.25
    nctx = Lc // ROW_TILE
    J = S // ROW_TILE
    T = B * S

    cc = jnp.concatenate([c, c_ctx[None, :]], axis=0)
    cc = jnp.pad(cc, ((0, (-cc.shape[0]) % SUBLANES), (0, 0)))
    hg_lb = jnp.cumsum(jax.nn.softmax(hg_lb_logits.astype(F32), axis=0), axis=0)
    xcat = jnp.concatenate([ctx, x], axis=1)

    mods = _modulation(cc, mod_w[0], mod_b[0])
    gq, gk, gv, gla, r, hq, hk, hla, hv, hgate = _in_proj0(xcat, mods, nctx, ev_w_in[0], gla_gate_w2[0],
                                                            gla_gate_b[0], hg_lb[0])
    gk3 = gk.reshape(1, T, GLA_QK)
    o_gf = _gated_recurrence(gq, gk3, gv, gla, 0, B, S, nctx, False, GLA_HEADS, GLA_DK, GLA_DV)
    o_gb = _gated_recurrence(gq, gk3, gv, gla, 0, B, S, nctx, True, GLA_HEADS, GLA_DK, GLA_DV)
    o_hf = _gated_recurrence(hq, hk, hv, hla, 0, B, S, nctx, False, HG_HEADS, HG_EXPAND, HG_EXPAND)
    o_hb = _gated_recurrence(hq, hk, hv, hla, 1, B, S, nctx, True, HG_HEADS, HG_EXPAND, HG_EXPAND)
    xl, u, up = _readout0(o_gf, o_gb, o_hf, o_hb, r, hgate, xcat, mods, nctx, ev_w_out[0], gla_norm_g[0],
                          hg_norm_g[0], ln1_g[0], ln1_b[0], alpha)
    cmb_rows = 2 * CMB_TILE
    tiles_per_b = S // cmb_rows
    tile_in_b = jnp.arange(B * tiles_per_b, dtype=I32) % tiles_per_b
    mod_of_tile = jnp.where(tile_in_b < Lc // cmb_rows, B, jnp.arange(B * tiles_per_b, dtype=I32) // tiles_per_b)
    x1 = _moe_block(u, up, xl, mods, mod_of_tile.astype(I32), router_w[0], router_bias[0], exp_w13, exp_w2, 0,
                    sh_w13[0], sh_w2[0], ln2_g[0], ln2_b[0], alpha)
    xcat = x1.reshape(B, S, D)

    mods = _modulation(cc, mod_w[1], mod_b[1])
    p_hy, p_rw = _in_proj1(xcat, mods, nctx, od_w_in[0])
    rr, rk, rv, rkk, rbb, rld, rg = _rw_streams(p_rw, B, S, nctx, rw_mu[0], rw_w0[0], rw_w2[0], rw_a0[0],
                                                 rw_a2[0], rw_g2[0], rw_k_k[0], rw_k_a[0])
    o_f = _rwkv_scan(rr, rk, rv, rkk, rbb, rld, B, S, nctx, False)
    o_b = _rwkv_scan(rr, rk, rv, rkk, rbb, rld, B, S, nctx, True)
    dc = _dft_constants(L)
    filt = _hyena_filters(L, hy_ffn_w1[0], hy_ffn_b1[0], hy_ffn_w2[0], hy_ffn_b2[0], hy_ffn_w3[0], hy_sin_freq[0])
    spec = _filter_spectrum(filt, L, dc)
    uu = _short_conv(p_hy, B, S, Lc, hy_conv_w[0], hy_conv_b[0])
    NCB = HY_CH // LANES
    z1 = _hyena_conv(uu, 0, uu, NCB, spec[0], hy_bias[0, 0], dc, B, L)
    z2 = _hyena_conv(z1, 0, uu, 2 * NCB, spec[1], hy_bias[0, 1], dc, B, L)
    xl, u, up = _readout1(z2.reshape(B * L, HY_CH), o_f, o_b, rr, rk, rv, rg, xcat, mods, B, S, Lc, od_w_out[0],
                          rw_r_k[0], rw_ln_g[0], rw_ln_b[0], ln1_g[1], ln1_b[1], alpha)
    mod_of_tile = (jnp.arange(B * (L // cmb_rows), dtype=I32) // (L // cmb_rows)).astype(I32)
    out = _moe_block(u, up, xl, mods, mod_of_tile, router_w[1], router_bias[1], exp_w13, exp_w2, 1,
                     sh_w13[1], sh_w2[1], ln2_g[1], ln2_b[1], alpha)
    return out.reshape(B, L, D)
```

```python
import functools
import math

import numpy as np
import jax
import jax.numpy as jnp
from jax import lax
from jax.experimental import pallas as pl
from jax.experimental.pallas import tpu as pltpu

F32 = jnp.float32
BF16 = jnp.bfloat16
I32 = jnp.int32
U32 = jnp.uint32
HI = lax.Precision.HIGHEST

LN_EPS = 1e-5
GLA_HEADS, GLA_DK, GLA_DV = 4, 64, 128
GLA_QK, GLA_V = GLA_HEADS * GLA_DK, GLA_HEADS * GLA_DV
GLA_GATE_RANK = 16
GLA_GATE_NORM = 16.0
CHUNK = 64
HG_HEADS, HG_EXPAND = 4, 128
HG_W = HG_HEADS * HG_EXPAND
HY_CH, HY_ORDER, HY_SHORT, HY_EMB = 512, 2, 3, 33
HY_WIDTH = (HY_ORDER + 1) * HY_CH
HY_FAST_DECAY, HY_SLOW_DECAY, HY_TARGET = 0.3, 1.5, 1e-2
RW_HEADS, RW_HEAD_DIM = 8, 64
RW_W = RW_HEADS * RW_HEAD_DIM
RW_DECAY_LORA, RW_AAA_LORA, RW_GATE_LORA = 64, 64, 128
RW_GN_EPS = 64e-5
GRID_W = 64
N_EXPERTS, TOP_K, N_GROUPS, TOPK_GROUPS = 256, 8, 8, 4
PER_GROUP = N_EXPERTS // N_GROUPS
EXPERT_FF, SHARED_FF = 256, 256
ROUTED_SCALE = 2.5

LANES = 128
SUBLANES = 8
VMEM_LIMIT = 56 * 1024 * 1024
ROW_TILE = 256
MOE_BLK = 512
CMB_TILE = 128
RW_CHUNK = 64
FFT_N2 = 128
FFT_G = 8


def _cparams(sem):
    return pltpu.CompilerParams(dimension_semantics=sem, vmem_limit_bytes=VMEM_LIMIT)


def _bdot(a, b):
    return jnp.dot(a.astype(BF16), b.astype(BF16), preferred_element_type=F32)


def _bmm(a, b):
    return jnp.einsum('nij,njk->nik', a.astype(BF16), b.astype(BF16), preferred_element_type=F32)


def _bmm_nt(a, b):
    return jnp.einsum('nik,njk->nij', a.astype(BF16), b.astype(BF16), preferred_element_type=F32)


def _bmm_tn(a, b):
    return _bmm(jnp.swapaxes(a, 1, 2), b)


def _hdot(a, b):
    return jnp.dot(a, b, precision=HI, preferred_element_type=F32)


def _head_sums(x, ones_bd):
    hi = x.astype(BF16)
    lo = (x - hi.astype(F32)).astype(BF16)
    return (jnp.dot(hi, ones_bd, preferred_element_type=F32) + jnp.dot(lo, ones_bd, preferred_element_type=F32))


def _silu(x):
    return x * jax.nn.sigmoid(x)


def _layer_norm_rows(x, g, b, eps):
    mu = jnp.mean(x, axis=-1, keepdims=True)
    xc = x - mu
    var = jnp.mean(xc * xc, axis=-1, keepdims=True)
    return xc * lax.rsqrt(var + eps) * g + b


def _mod_kernel(c_ref, w_ref, b_ref, o_ref):
    o_ref[...] = _hdot(_silu(c_ref[...]), w_ref[...]) + b_ref[...]


def _modulation(cc, w, b):
    R, D = cc.shape
    out = pl.pallas_call(
        _mod_kernel,
        grid=(6,),
        in_specs=[pl.BlockSpec((R, D), lambda j: (0, 0)),
                  pl.BlockSpec((D, D), lambda j: (0, j)),
                  pl.BlockSpec((1, D), lambda j: (0, j))],
        out_specs=pl.BlockSpec((R, D), lambda j: (0, j)),
        out_shape=jax.ShapeDtypeStruct((R, 6 * D), F32),
        compiler_params=_cparams(("parallel",)),
        name="modulation",
    )(cc, w, b.reshape(1, 6 * D))
    return out.reshape(R, 6, D)


def _in0_kernel(x_ref, mod_ref, w_ref, wa_ref, w2_ref, gb_ref, lb_ref,
                gq_ref, gk_ref, gv_ref, gla_ref, r_ref, hq_ref, hk_ref, hla_ref, hv_ref, hg_ref):
    sh1 = mod_ref[0:1, :]
    sc1 = mod_ref[1:2, :]
    h = (x_ref[...] * (1.0 + sc1) + sh1).astype(BF16)

    def proj(off, width):
        return jnp.dot(h, w_ref[:, off:off + width], preferred_element_type=F32)

    gq_ref[...] = (proj(0, GLA_QK) * (GLA_DK ** -0.5)).astype(BF16)
    gk_ref[...] = proj(GLA_QK, GLA_QK).astype(BF16)
    gv_ref[...] = proj(2 * GLA_QK, GLA_V).astype(BF16)
    r_ref[...] = proj(2 * GLA_QK + GLA_V, GLA_V).astype(BF16)
    base = 2 * GLA_QK + 2 * GLA_V
    a = jnp.dot(h, wa_ref[...], preferred_element_type=F32)
    z = _bdot(a, w2_ref[...]) + gb_ref[...]
    ls = (jnp.minimum(z, 0.0) - jnp.log(1.0 + jnp.exp(-jnp.abs(z)))) * (1.0 / GLA_GATE_NORM)
    gla_ref[0] = ls[:, :GLA_QK]
    gla_ref[1] = ls[:, GLA_QK:]
    hq_ref[...] = _silu(proj(base, HG_W)).astype(BF16)
    for d in range(2):
        zf = proj(base + (1 + d) * HG_W, HG_W)
        lb = lb_ref[d:d + 1, :]
        f = lb + (1.0 - lb) * jax.nn.sigmoid(zf)
        hk_ref[d] = (1.0 - f).astype(BF16)
        hla_ref[d] = jnp.log(f)
    hv_ref[...] = proj(base + 3 * HG_W, HG_W).astype(BF16)
    hg_ref[...] = proj(base + 4 * HG_W, HG_W).astype(BF16)


def _in_proj0(xcat, mods, nctx_tiles, w_in, gate_w2, gate_b, lb):
    B, S, D = xcat.shape
    J = S // ROW_TILE
    T = B * S
    a_off = 2 * GLA_QK + 2 * GLA_V
    wmain = jnp.concatenate([w_in[:, :a_off], w_in[:, a_off + 2 * GLA_GATE_RANK:]], axis=1).astype(BF16)
    wa = jnp.pad(w_in[:, a_off:a_off + 2 * GLA_GATE_RANK], ((0, 0), (0, LANES - 2 * GLA_GATE_RANK))).astype(BF16)
    w2 = jnp.zeros((LANES, 2 * GLA_QK), F32)
    w2 = w2.at[:GLA_GATE_RANK, :GLA_QK].set(gate_w2[0]).at[GLA_GATE_RANK:2 * GLA_GATE_RANK, GLA_QK:].set(gate_w2[1])
    gb = gate_b.reshape(1, 2 * GLA_QK)
    WM = wmain.shape[1]

    def rows(w):
        return pl.BlockSpec((ROW_TILE, w), lambda b, j: (b * J + j, 0))

    def rows2(w):
        return pl.BlockSpec((2, ROW_TILE, w), lambda b, j: (0, b * J + j, 0))

    def const(shape):
        return pl.BlockSpec(shape, lambda b, j: (0,) * len(shape))

    sd = jax.ShapeDtypeStruct
    outs = pl.pallas_call(
        _in0_kernel,
        grid=(B, J),
        in_specs=[rows(D),
                  pl.BlockSpec((None, 6, D), lambda b, j: (jnp.where(j < nctx_tiles, B, b), 0, 0)),
                  const((D, WM)), const((D, LANES)), const((LANES, 2 * GLA_QK)), const((1, 2 * GLA_QK)),
                  const((2, HG_W))],
        out_specs=[rows(GLA_QK), rows(GLA_QK), rows(GLA_V), rows2(GLA_QK), rows(GLA_V),
                   rows(HG_W), rows2(HG_W), rows2(HG_W), rows(HG_W), rows(HG_W)],
        out_shape=[sd((T, GLA_QK), BF16), sd((T, GLA_QK), BF16), sd((T, GLA_V), BF16), sd((2, T, GLA_QK), F32),
                   sd((T, GLA_V), BF16), sd((T, HG_W), BF16), sd((2, T, HG_W), BF16), sd((2, T, HG_W), F32),
                   sd((T, HG_W), BF16), sd((T, HG_W), BF16)],
        compiler_params=_cparams(("parallel", "parallel")),
        name="in_proj0",
    )(xcat.reshape(T, D), mods, wmain, wa, w2, gb, lb)
    return outs


def _rec_kernel(q_ref, k_ref, v_ref, la_ref, o_ref, st_ref, *, rev, nh, dk, dv, nchunk):
    t = pl.program_id(1)

    @pl.when(t == 0)
    def _():
        st_ref[...] = jnp.zeros_like(st_ref)

    C = CHUNK
    row = lax.broadcasted_iota(I32, (C, C), 0)
    col = lax.broadcasted_iota(I32, (C, C), 1)
    incl = (col >= row) if rev else (col <= row)
    tri = jnp.where(incl, 1.0, 0.0).astype(BF16)
    ref_i = C // 2 - 1 if rev else C // 2
    last_i = 0 if rev else C - 1
    order = range(nchunk - 1, -1, -1) if rev else range(nchunk)

    def stack(x, hd):
        return jnp.stack([x[c * C:(c + 1) * C, h * hd:(h + 1) * hd] for c in range(nchunk) for h in range(nh)], 0)

    def rows(x, i):
        return jnp.concatenate([jnp.broadcast_to(x[c * C + i:c * C + i + 1], (C, x.shape[1]))
                                for c in range(nchunk)], axis=0)

    la = la_ref[...]
    q = q_ref[...].astype(F32)
    k = k_ref[...].astype(F32)
    b = jnp.concatenate([_split_dot(tri, la[c * C:(c + 1) * C]) for c in range(nchunk)], axis=0)
    b_mid = rows(b, ref_i)
    b_last = rows(b, last_i)
    v = stack(v_ref[...].astype(F32), dv)
    sc = _bmm_nt(stack(q * jnp.exp(b - b_mid), dk), stack(k * jnp.exp(b_mid - b), dk))
    o_intra = _bmm(jnp.where(incl[None], sc, 0.0), v)
    q_in = stack(q * jnp.exp(b), dk)
    kv_t = _bmm_tn(v, stack(k * jnp.exp(b_last - b), dk))
    dec = stack(jnp.exp(b_last), dk)[:, 0:1, :]
    s_t = st_ref[...]
    for c in order:
        sl = slice(c * nh, (c + 1) * nh)
        o_c = o_intra[sl] + _bmm_nt(q_in[sl], s_t)
        o_ref[c * C:(c + 1) * C, :] = jnp.concatenate([o_c[h] for h in range(nh)], axis=1)
        s_t = s_t * dec[sl] + kv_t[sl]
    st_ref[...] = s_t


def _gated_recurrence(q, k, v, la, kdir, B, S, nctx_blocks, rev, nh, dk, dv):
    T = B * S
    NB = S // ROW_TILE
    wk, wv = nh * dk, nh * dv
    assert q.shape[1] == wk
    d = 1 if rev else 0
    nc = nctx_blocks

    def blk(t):
        if not rev:
            return t
        return jnp.where(t < nc, nc - 1 - t, NB - 1 - (t - nc))

    kern = functools.partial(_rec_kernel, rev=rev, nh=nh, dk=dk, dv=dv, nchunk=ROW_TILE // CHUNK)
    return pl.pallas_call(
        kern,
        grid=(B, NB),
        in_specs=[pl.BlockSpec((ROW_TILE, wk), lambda b, t: (b * NB + blk(t), 0)),
                  pl.BlockSpec((None, ROW_TILE, wk), lambda b, t: (kdir, b * NB + blk(t), 0)),
                  pl.BlockSpec((ROW_TILE, wv), lambda b, t: (b * NB + blk(t), 0)),
                  pl.BlockSpec((None, ROW_TILE, wk), lambda b, t: (d, b * NB + blk(t), 0))],
        out_specs=pl.BlockSpec((ROW_TILE, wv), lambda b, t: (b * NB + blk(t), 0)),
        out_shape=jax.ShapeDtypeStruct((T, wv), F32),
        scratch_shapes=[pltpu.VMEM((nh, dv, dk), F32)],
        compiler_params=_cparams(("parallel", "arbitrary")),
        name="gated_rec_bwd" if rev else "gated_rec_fwd",
    )(q, k, v, la)


def _pack_rows(x):
    w = x.shape[1] // 2
    lo = lax.bitcast_convert_type(x[:, :w].astype(BF16).astype(F32), U32)
    hi = lax.bitcast_convert_type(x[:, w:].astype(BF16).astype(F32), U32)
    return (lo >> 16) | (hi & jnp.uint32(0xFFFF0000))


def _unpack_rows(p):
    lo = lax.bitcast_convert_type(p << 16, F32)
    hi = lax.bitcast_convert_type(p & jnp.uint32(0xFFFF0000), F32)
    return lo, hi


ROW_SUB = 4


def _store_row_slabs(ref, val):
    m = val.shape[0]
    for j in range(ROW_SUB):
        ref[pl.ds(j, m, stride=ROW_SUB), :] = val[:, j * LANES:(j + 1) * LANES]


def _load_row_slabs(ref, r0=0, m=None):
    m = ref.shape[0] // ROW_SUB - r0 if m is None else m
    return jnp.concatenate([ref[pl.ds(r0 * ROW_SUB + j, m, stride=ROW_SUB), :] for j in range(ROW_SUB)], axis=1)


def _post_mix(y, x, mod_ref, wo_ref, lng_ref, lnb_ref, alpha, xl_ref, u_ref, up_ref):
    g1 = mod_ref[2:3, :]
    sh2 = mod_ref[3:4, :]
    sc2 = mod_ref[4:5, :]
    yo = jnp.dot(y.astype(BF16), wo_ref[...], preferred_element_type=F32)
    xl = _layer_norm_rows(alpha * x + g1 * yo, lng_ref[...], lnb_ref[...], LN_EPS)
    xl_ref[...] = xl
    u = xl * (1.0 + sc2) + sh2
    u_ref[...] = u
    _store_row_slabs(up_ref, _pack_rows(u))


def _out0_kernel(gf_ref, gb_ref, hf_ref, hb_ref, r_ref, hg_ref, x_ref, mod_ref, wo_ref,
                 gg_ref, hgg_ref, lng_ref, lnb_ref, xl_ref, u_ref, up_ref, *, alpha):
    def heads(o, g, gate):
        parts = []
        for hh in range(o.shape[1] // LANES):
            seg = o[:, hh * LANES:(hh + 1) * LANES]
            ms = jnp.mean(seg * seg, axis=-1, keepdims=True)
            parts.append(seg * lax.rsqrt(ms + 1e-6) * g)
        return jnp.concatenate(parts, axis=1) * _silu(gate.astype(F32))

    y = jnp.concatenate([heads(gf_ref[...] + gb_ref[...], gg_ref[...], r_ref[...]),
                         heads(hf_ref[...] + hb_ref[...], hgg_ref[...], hg_ref[...])], axis=1)
    _post_mix(y, x_ref[...], mod_ref, wo_ref, lng_ref, lnb_ref, alpha, xl_ref, u_ref, up_ref)


def _readout0(o_gf, o_gb, o_hf, o_hb, r, hgate, xcat, mods, nctx_tiles, w_out, gla_g, hg_g, ln_g, ln_b, alpha):
    B, S, D = xcat.shape
    J = S // ROW_TILE
    T = B * S

    def rows(w):
        return pl.BlockSpec((ROW_TILE, w), lambda b, j: (b * J + j, 0))

    def const(shape):
        return pl.BlockSpec(shape, lambda b, j: (0,) * len(shape))

    sd = jax.ShapeDtypeStruct
    return pl.pallas_call(
        functools.partial(_out0_kernel, alpha=alpha),
        grid=(B, J),
        in_specs=[rows(GLA_V), rows(GLA_V), rows(HG_W), rows(HG_W), rows(GLA_V), rows(HG_W), rows(D),
                  pl.BlockSpec((None, 6, D), lambda b, j: (jnp.where(j < nctx_tiles, B, b), 0, 0)),
                  const((GLA_V + HG_W, D)), const((1, GLA_DV)), const((1, HG_EXPAND)), const((1, D)), const((1, D))],
        out_specs=[rows(D), rows(D), pl.BlockSpec((ROW_TILE * ROW_SUB, LANES), lambda b, j: (b * J + j, 0))],
        out_shape=[sd((T, D), F32), sd((T, D), F32), sd((T * ROW_SUB, LANES), U32)],
        compiler_params=_cparams(("parallel", "parallel")),
        name="readout0",
    )(o_gf, o_gb, o_hf, o_hb, r, hgate, xcat.reshape(T, D), mods, w_out.astype(BF16),
      gla_g.reshape(1, -1), hg_g.reshape(1, -1), ln_g.reshape(1, D), ln_b.reshape(1, D))


def _route_kernel(u_ref, rwh_ref, rwl_ref, rb_ref, tri_ref, eidx_ref, gate_ref, rank_ref, cnt_ref, carry_ref):
    i = pl.program_id(0)

    @pl.when(i == 0)
    def _():
        carry_ref[...] = jnp.zeros_like(carry_ref)

    u = u_ref[...]
    TM = u.shape[0]
    E = N_EXPERTS
    u_hi = u.astype(BF16)
    u_lo = (u - u_hi.astype(F32)).astype(BF16)
    nt = (((1,), (1,)), ((), ()))
    logits = (lax.dot_general(rwh_ref[...], u_hi, nt, preferred_element_type=F32)
              + lax.dot_general(rwh_ref[...], u_lo, nt, preferred_element_type=F32)
              + lax.dot_general(rwl_ref[...], u_hi, nt, preferred_element_type=F32))
    scores = jax.nn.sigmoid(logits)
    sel = scores + rb_ref[...]
    neg = -jnp.inf
    sel3 = sel.reshape(N_GROUPS, PER_GROUP, TM)
    io_g = lax.broadcasted_iota(I32, sel3.shape, 1)
    m1 = jnp.max(sel3, axis=1, keepdims=True)
    i1 = jnp.min(jnp.where(sel3 == m1, io_g, PER_GROUP), axis=1, keepdims=True)
    m2 = jnp.max(jnp.where(io_g == i1, neg, sel3), axis=1, keepdims=True)
    grp = m1 + m2
    io_n = lax.broadcasted_iota(I32, grp.shape, 0)
    keep = jnp.zeros(grp.shape, jnp.bool_)
    for _ in range(TOPK_GROUPS):
        m = jnp.max(grp, axis=0, keepdims=True)
        idx = jnp.min(jnp.where(grp == m, io_n, N_GROUPS), axis=0, keepdims=True)
        hit = io_n == idx
        keep = jnp.logical_or(keep, hit)
        grp = jnp.where(hit, neg, grp)
    sel = jnp.where(keep, sel3, neg).reshape(E, TM)
    io_e = lax.broadcasted_iota(I32, (E, TM), 0)
    base = carry_ref[...]
    tri = tri_ref[...]
    e_rows, g_rows, r_rows = [], [], []
    for _ in range(TOP_K):
        m = jnp.max(sel, axis=0, keepdims=True)
        idx = jnp.min(jnp.where(sel == m, io_e, E), axis=0, keepdims=True)
        hit = io_e == idx
        hit_f = jnp.where(hit, 1.0, 0.0)
        g_rows.append(jnp.sum(jnp.where(hit, scores, 0.0), axis=0, keepdims=True))
        prefix = jnp.dot(hit_f.astype(BF16), tri, preferred_element_type=F32)
        r_rows.append(jnp.sum(jnp.where(hit, prefix + base, 0.0), axis=0, keepdims=True))
        base = base + jnp.sum(hit_f, axis=1, keepdims=True)
        e_rows.append(idx)
        sel = jnp.where(hit, neg, sel)
    carry_ref[...] = base
    cnt_ref[...] = base
    g = jnp.concatenate(g_rows, axis=0)
    gate_ref[...] = g / jnp.sum(g, axis=0, keepdims=True) * ROUTED_SCALE
    eidx_ref[...] = jnp.concatenate(e_rows, axis=0)
    rank_ref[...] = jnp.concatenate(r_rows, axis=0).astype(I32)


def _route(u, router_w, router_bias):
    T, D = u.shape
    E = N_EXPERTS
    n = T // ROW_TILE
    tri = jnp.asarray(np.triu(np.ones((ROW_TILE, ROW_TILE), np.float32), 1), BF16)
    sd = jax.ShapeDtypeStruct
    cols = pl.BlockSpec((TOP_K, ROW_TILE), lambda i: (0, i))
    rwt = router_w.T
    rw_hi = rwt.astype(BF16)
    rw_lo = (rwt - rw_hi.astype(F32)).astype(BF16)
    return pl.pallas_call(
        _route_kernel,
        grid=(n,),
        in_specs=[pl.BlockSpec((ROW_TILE, D), lambda i: (i, 0)),
                  pl.BlockSpec((E, D), lambda i: (0, 0)),
                  pl.BlockSpec((E, D), lambda i: (0, 0)),
                  pl.BlockSpec((E, 1), lambda i: (0, 0)),
                  pl.BlockSpec((ROW_TILE, ROW_TILE), lambda i: (0, 0))],
        out_specs=[cols, cols, cols, pl.BlockSpec((E, 1), lambda i: (0, 0))],
        out_shape=[sd((TOP_K, T), I32), sd((TOP_K, T), F32), sd((TOP_K, T), I32), sd((E, 1), F32)],
        scratch_shapes=[pltpu.VMEM((E, 1), F32)],
        compiler_params=_cparams(("arbitrary",)),
        name="moe_route",
    )(u, rw_hi, rw_lo, router_bias.reshape(E, 1), tri)


def _dest_kernel(e_ref, r_ref, ps_ref, d_ref):
    e = e_ref[...]
    TM = e.shape[1]
    io_e = lax.broadcasted_iota(I32, (N_EXPERTS, TM), 0)
    ps = ps_ref[...]
    rows = []
    for k in range(TOP_K):
        rows.append(jnp.sum(jnp.where(io_e == e[k:k + 1, :], ps, 0.0), axis=0, keepdims=True))
    d_ref[...] = jnp.concatenate(rows, axis=0).astype(I32) + r_ref[...]


def _slot_of(eidx, rank, pad_start):
    T = eidx.shape[1]
    cols = pl.BlockSpec((TOP_K, ROW_TILE), lambda i: (0, i))
    return pl.pallas_call(
        _dest_kernel,
        grid=(T // ROW_TILE,),
        in_specs=[cols, cols, pl.BlockSpec((N_EXPERTS, 1), lambda i: (0, 0))],
        out_specs=cols,
        out_shape=jax.ShapeDtypeStruct((TOP_K, T), I32),
        compiler_params=_cparams(("parallel",)),
        name="moe_slot",
    )(eidx, rank, pad_start.astype(F32).reshape(N_EXPERTS, 1))


def _dispatch_kernel(z0_ref, zn_ref, dest_ref, u_ref, xs_ref, zero_scr, sem, zsem):
    TM = u_ref.shape[0]

    @pl.when(pl.program_id(0) == 0)
    def _():
        zero_scr[...] = jnp.zeros_like(zero_scr)

        def zcopy(e):
            row0 = pl.multiple_of(z0_ref[e], MOE_BLK)
            return pltpu.make_async_copy(zero_scr, xs_ref.at[pl.ds(row0, MOE_BLK)], zsem)

        def zstart(e, c):
            @pl.when(zn_ref[e] > 0)
            def _():
                zcopy(e).start()
            return c

        def zwait(e, c):
            @pl.when(zn_ref[e] > 0)
            def _():
                zcopy(e).wait()
            return c

        lax.fori_loop(0, N_EXPERTS, zstart, 0)
        lax.fori_loop(0, N_EXPERTS, zwait, 0)

    def copy(t, k):
        return pltpu.make_async_copy(u_ref.at[t], xs_ref.at[dest_ref[k, t]], sem)

    def issue(t, c):
        for k in range(TOP_K):
            copy(t, k).start(priority=k % 2)
        return c

    lax.fori_loop(0, TM, issue, 0)
    for k in range(TOP_K):
        pltpu.make_async_copy(u_ref, xs_ref.at[pl.ds(0, TM)], sem).wait()


def _dispatch(up, dest, n_slots, pad_row0, pad_rows):
    T = up.shape[0]
    slab = up.shape[1:]
    return pl.pallas_call(
        _dispatch_kernel,
        grid_spec=pltpu.PrefetchScalarGridSpec(
            num_scalar_prefetch=2,
            grid=(T // ROW_TILE,),
            in_specs=[pl.BlockSpec((TOP_K, ROW_TILE), lambda i, z0, zn: (0, i), memory_space=pltpu.SMEM),
                      pl.BlockSpec((ROW_TILE,) + slab, lambda i, z0, zn: (i, 0, 0))],
            out_specs=pl.BlockSpec(memory_space=pl.ANY),
            scratch_shapes=[pltpu.VMEM((MOE_BLK,) + slab, U32), pltpu.SemaphoreType.DMA, pltpu.SemaphoreType.DMA]),
        out_shape=jax.ShapeDtypeStruct((n_slots,) + slab, U32),
        compiler_params=_cparams(("arbitrary",)),
        name="moe_dispatch",
    )(pad_row0, pad_rows, dest, up)


def _expert_kernel(be_ref, nu_ref, first_ref, nxt_ref, par_ref, xs_ref, w13_hbm, w2_hbm,
                   ys_ref, w13_buf, w2_buf, w13_bf, w2_bf, sem, *, layer):
    i = pl.program_id(0)

    def fetch(e, slot):
        return (pltpu.make_async_copy(w13_hbm.at[layer, e], w13_buf.at[slot], sem.at[0, slot]),
                pltpu.make_async_copy(w2_hbm.at[layer, e], w2_buf.at[slot], sem.at[1, slot]))

    @pl.when(i == 0)
    def _():
        for c in fetch(be_ref[0], 0):
            c.start()

    @pl.when(jnp.logical_and(i < nu_ref[0], first_ref[i] == 1))
    def _():
        slot = par_ref[i]
        for c in fetch(be_ref[i], slot):
            c.wait()

        @pl.when(nxt_ref[i] >= 0)
        def _():
            for c in fetch(nxt_ref[i], 1 - slot):
                c.start()

        w13_bf[...] = w13_buf[slot].astype(BF16)
        w2_bf[...] = w2_buf[slot].astype(BF16)

    @pl.when(i < nu_ref[0])
    def _():
        m = xs_ref.shape[0] // ROW_SUB
        lo, hi = _unpack_rows(jnp.concatenate(
            [xs_ref[pl.ds(j, m, stride=ROW_SUB), :] for j in range(ROW_SUB)], axis=1))
        x = jnp.concatenate([lo, hi], axis=1).astype(BF16)
        h = jnp.dot(x, w13_bf[...], preferred_element_type=F32)
        a = _silu(h[:, :EXPERT_FF]) * h[:, EXPERT_FF:]
        y = _pack_rows(jnp.dot(a.astype(BF16), w2_bf[...], preferred_element_type=F32))
        for j in range(ROW_SUB):
            ys_ref[pl.ds(j, m, stride=ROW_SUB), :] = y[:, j * LANES:(j + 1) * LANES]


def _expert_gemm(xs, blk_exp, n_used, cnt, w13, w2, layer):
    NP = xs.shape[0]
    slab = xs.shape[1:]
    D = 2 * slab[0] * slab[1]
    nblk = NP // MOE_BLK
    F2 = w13.shape[3]
    E = cnt.shape[0]
    blk = jnp.arange(nblk, dtype=I32)
    first = jnp.concatenate([jnp.ones((1,), I32), (blk_exp[1:] != blk_exp[:-1]).astype(I32)])
    first = jnp.where(blk < n_used[0], first, 0)
    par = (jnp.cumsum(first) - 1) % 2
    ids = jnp.where(cnt > 0, jnp.arange(E, dtype=I32), E)
    suffix_min = lax.cummin(ids, axis=0, reverse=True)
    next_active = jnp.concatenate([suffix_min[1:], jnp.full((1,), E, I32)])
    next_active = jnp.where(next_active >= E, -1, next_active)
    nxt = jnp.take(next_active, blk_exp)

    def xmap(i, be, nu, fi, nx, pa):
        return (jnp.minimum(i, nu[0] - 1), 0)

    flat = (NP * slab[0], slab[1])
    ys = pl.pallas_call(
        functools.partial(_expert_kernel, layer=layer),
        grid_spec=pltpu.PrefetchScalarGridSpec(
            num_scalar_prefetch=5,
            grid=(nblk,),
            in_specs=[pl.BlockSpec((MOE_BLK * slab[0], slab[1]), xmap),
                      pl.BlockSpec(memory_space=pl.ANY), pl.BlockSpec(memory_space=pl.ANY)],
            out_specs=pl.BlockSpec((MOE_BLK * slab[0], slab[1]), xmap),
            scratch_shapes=[pltpu.VMEM((2, D, F2), F32), pltpu.VMEM((2, F2 // 2, D), F32),
                            pltpu.VMEM((D, F2), BF16), pltpu.VMEM((F2 // 2, D), BF16),
                            pltpu.SemaphoreType.DMA((2, 2))]),
        out_shape=jax.ShapeDtypeStruct(flat, U32),
        compiler_params=_cparams(("arbitrary",)),
        name="moe_experts",
    )(blk_exp, n_used, first, nxt.astype(I32), par.astype(I32), xs.reshape(flat), w13, w2)
    return ys.reshape((NP,) + slab)


def _combine_kernel(mt_ref, dcur_ref, dnxt_ref, gate_ref, up_ref, xl_ref, mod_ref, s13_ref, s2_ref, lng_ref, lnb_ref,
                    ys_ref, o_ref, rows_a, rows_b, rows_c, rows_d, sem_a, sem_b, sem_c, sem_d, *, alpha, n_steps):
    del mt_ref
    i = pl.program_id(0)
    H = CMB_TILE
    W = ROW_SUB * LANES

    def issue(dref, col0, rows, sem):
        for t in range(H):
            for k in range(TOP_K):
                pltpu.make_async_copy(ys_ref.at[dref[k, col0 + t]], rows.at[k, pl.ds(t * ROW_SUB, ROW_SUB)],
                                      sem).start(priority=k % 2)

    def wait(rows, sem):
        for k in range(TOP_K):
            pltpu.make_async_copy(rows.at[k], rows.at[k], sem).wait()

    def compute(rows, r0):
        lo, hi = _unpack_rows(_load_row_slabs(up_ref, r0, H))
        hs = _bdot(jnp.concatenate([lo, hi], axis=1), s13_ref[...])
        sh = _bdot(_silu(hs[:, :SHARED_FF]) * hs[:, SHARED_FF:], s2_ref[...])
        acc_lo, acc_hi = sh[:, :W], sh[:, W:]
        gate = gate_ref[r0:r0 + H]
        for k in range(TOP_K):
            lo, hi = _unpack_rows(_load_row_slabs(rows.at[k]))
            g = gate[:, k:k + 1]
            acc_lo = acc_lo + lo * g
            acc_hi = acc_hi + hi * g
        acc = jnp.concatenate([acc_lo, acc_hi], axis=1)
        g2 = mod_ref[5:6, :]
        o_ref[r0:r0 + H] = _layer_norm_rows(alpha * xl_ref[r0:r0 + H] + g2 * acc, lng_ref[...], lnb_ref[...], LN_EPS)

    pair0 = ((rows_a, sem_a), (rows_b, sem_b))
    pair1 = ((rows_c, sem_c), (rows_d, sem_d))

    @pl.when(i == 0)
    def _():
        issue(dcur_ref, 0, *pair0[0])
        issue(dcur_ref, H, *pair0[1])

    def step(cur, nxt):
        wait(*cur[0])
        issue(dnxt_ref, 0, *nxt[0])
        compute(cur[0][0], 0)
        wait(*cur[1])
        issue(dnxt_ref, H, *nxt[1])
        compute(cur[1][0], H)

    @pl.when(i % 2 == 0)
    def _():
        step(pair0, pair1)

    @pl.when(i % 2 == 1)
    def _():
        step(pair1, pair0)

    @pl.when(i == n_steps - 1)
    def _():
        for rows, sem in (pair1 if (n_steps - 1) % 2 == 0 else pair0):
            wait(rows, sem)


def _combine(dest, gate_t, up, xl, ys, mods, mod_of_tile, sh_w13, sh_w2, ln_g, ln_b, alpha):
    T, D = xl.shape
    TM = 2 * CMB_TILE
    n = T // TM

    def rows(w):
        return pl.BlockSpec((TM, w), lambda i, mt: (i, 0))

    def const(shape):
        return pl.BlockSpec(shape, lambda i, mt: (0,) * len(shape))

    return pl.pallas_call(
        functools.partial(_combine_kernel, alpha=alpha, n_steps=n),
        grid_spec=pltpu.PrefetchScalarGridSpec(
            num_scalar_prefetch=1,
            grid=(n,),
            in_specs=[pl.BlockSpec((TOP_K, TM), lambda i, mt: (0, i), memory_space=pltpu.SMEM),
                      pl.BlockSpec((TOP_K, TM), lambda i, mt: (0, jnp.minimum(i + 1, n - 1)), memory_space=pltpu.SMEM),
                      rows(TOP_K), pl.BlockSpec((TM * ROW_SUB, LANES), lambda i, mt: (i, 0)), rows(D),
                      pl.BlockSpec((None, 6, D), lambda i, mt: (mt[i], 0, 0)),
                      const((D, 2 * SHARED_FF)), const((SHARED_FF, D)), const((1, D)), const((1, D)),
                      pl.BlockSpec(memory_space=pl.ANY)],
            out_specs=rows(D),
            scratch_shapes=[pltpu.VMEM((TOP_K, CMB_TILE * ROW_SUB, LANES), U32)] * 4
                           + [pltpu.SemaphoreType.DMA] * 4),
        out_shape=jax.ShapeDtypeStruct((T, D), F32),
        compiler_params=_cparams(("arbitrary",)),
        name="moe_combine",
    )(mod_of_tile, dest, dest, gate_t, up, xl, mods, sh_w13.astype(BF16), sh_w2.astype(BF16),
      ln_g.reshape(1, D), ln_b.reshape(1, D), ys)


def _moe_block(u, up, xl, mods, mod_of_tile, router_w, router_bias, w13, w2, layer, sh_w13, sh_w2, ln_g, ln_b, alpha):
    T, D = u.shape
    up = up.reshape(T, ROW_SUB, LANES)
    eidx, gate, rank, counts = _route(u, router_w, router_bias)
    cnt = counts.reshape(N_EXPERTS).astype(I32)
    padded = (cnt + MOE_BLK - 1) // MOE_BLK * MOE_BLK
    pad_end = jnp.cumsum(padded)
    pad_start = pad_end - padded
    nblk = T * TOP_K // MOE_BLK + N_EXPERTS
    n_used = (pad_end[-1] // MOE_BLK).astype(I32)
    blk_row0 = jnp.arange(nblk, dtype=I32) * MOE_BLK
    blk_exp = jnp.sum((pad_end[None, :] <= blk_row0[:, None]).astype(I32), axis=1)
    last_e = jnp.max(jnp.where(cnt > 0, jnp.arange(N_EXPERTS, dtype=I32), 0))
    blk_exp = jnp.minimum(blk_exp, last_e)
    dest = _slot_of(eidx, rank, pad_start)
    xs = _dispatch(up, dest, nblk * MOE_BLK, jnp.maximum(pad_end - MOE_BLK, 0), padded)
    ys = _expert_gemm(xs, blk_exp, n_used.reshape(1), cnt, w13, w2, layer)
    return _combine(dest, gate.T, up.reshape(T * ROW_SUB, LANES), xl, ys, mods, mod_of_tile, sh_w13, sh_w2,
                    ln_g, ln_b, alpha)


def _in1_kernel(x_ref, mod_ref, wh_ref, wr_ref, ph_ref, pr_ref, *, nctx_tiles):
    sh1 = mod_ref[0:1, :]
    sc1 = mod_ref[1:2, :]
    h = (x_ref[...] * (1.0 + sc1) + sh1).astype(BF16)
    pr_ref[...] = jnp.dot(h, wr_ref[...], preferred_element_type=F32)

    @pl.when(pl.program_id(1) >= nctx_tiles)
    def _():
        ph_ref[...] = jnp.dot(h, wh_ref[...], preferred_element_type=F32)


RW_PW = 1920


def _rw_reorder(t):
    a = 3 * RW_W
    lo = 2 * RW_DECAY_LORA + RW_AAA_LORA
    out = jnp.concatenate([t[..., :a], t[..., a + lo:], t[..., a:a + lo]], axis=-1)
    pad = [(0, 0)] * (t.ndim - 1) + [(0, RW_PW - out.shape[-1])]
    return jnp.pad(out, pad)


def _in_proj1(xcat, mods, nctx_tiles, w_in):
    B, S, D = xcat.shape
    J = S // ROW_TILE
    T = B * S
    wh = w_in[:, :HY_WIDTH].astype(BF16)
    wr = _rw_reorder(w_in[:, HY_WIDTH:]).astype(BF16)

    def rows(w):
        return pl.BlockSpec((ROW_TILE, w), lambda b, j: (b * J + j, 0))

    def const(shape):
        return pl.BlockSpec(shape, lambda b, j: (0,) * len(shape))

    sd = jax.ShapeDtypeStruct
    JL = J - nctx_tiles
    return pl.pallas_call(
        functools.partial(_in1_kernel, nctx_tiles=nctx_tiles),
        grid=(B, J),
        in_specs=[rows(D), pl.BlockSpec((None, 6, D), lambda b, j: (jnp.where(j < nctx_tiles, B, b), 0, 0)),
                  const((D, HY_WIDTH)), const((D, RW_PW))],
        out_specs=[pl.BlockSpec((ROW_TILE, HY_WIDTH), lambda b, j: (b * JL + jnp.maximum(j - nctx_tiles, 0), 0)),
                   rows(RW_PW)],
        out_shape=[sd((B * JL * ROW_TILE, HY_WIDTH), F32), sd((T, RW_PW), F32)],
        compiler_params=_cparams(("parallel", "arbitrary")),
        name="in_proj1",
    )(xcat.reshape(T, D), mods, wh, wr)


def _rw_streams_kernel(prev_ref, main_ref, next_ref, mu_ref, w0_ref, w2_ref, a0_ref, a2_ref, g2_ref,
                       kk_ref, ka_ref, bd_ref,
                       r_ref, k_ref, v_ref, kkn_ref, bb_ref, ld_ref, g_ref, sh_scr, *, nctx_tiles, n_lat_tiles):
    j = pl.program_id(1)
    TM = main_ref.shape[0]
    W = main_ref.shape[1]
    H = GRID_W
    p = main_ref[...]
    ext = jnp.concatenate([prev_ref[...], p, next_ref[...]], axis=0)
    left = ext[H - 1:H - 1 + TM]
    right = ext[H + 1:H + 1 + TM]
    up = ext[0:TM]
    down = ext[2 * H:2 * H + TM]
    i = lax.broadcasted_iota(I32, (TM, W), 0)
    lane = lax.broadcasted_iota(I32, (TM, W), 1)
    even = (lane & 1) == 0
    c4 = lane & 3
    jl = j - nctx_tiles

    @pl.when(j < nctx_tiles)
    def _():
        lo = jnp.where(j == 0, 1, 0)
        hi = jnp.where(j == nctx_tiles - 1, TM - 1, TM)
        sh_scr[...] = jnp.where(even, jnp.where(i >= lo, left, 0.0), jnp.where(i < hi, right, 0.0))

    @pl.when(j >= nctx_tiles)
    def _():
        col = i & (H - 1)
        up_lo = jnp.where(jl == 0, H, 0)
        down_hi = jnp.where(jl == n_lat_tiles - 1, TM - H, TM)
        l_v = jnp.where(col != 0, left, 0.0)
        r_v = jnp.where(col != H - 1, right, 0.0)
        u_v = jnp.where(i >= up_lo, up, 0.0)
        d_v = jnp.where(i < down_hi, down, 0.0)
        sh_scr[...] = jnp.where(c4 == 0, l_v, jnp.where(c4 == 1, r_v, jnp.where(c4 == 2, u_v, d_v)))

    pm = p + mu_ref[...] * (sh_scr[...] - p)
    r = pm[:, 0:RW_W]
    k = pm[:, RW_W:2 * RW_W]
    v = pm[:, 2 * RW_W:3 * RW_W]
    o = 3 * RW_W
    gl = pm[:, o:o + RW_GATE_LORA]
    o += RW_GATE_LORA
    wl_f = pm[:, o:o + RW_DECAY_LORA]
    wl_b = pm[:, o + RW_DECAY_LORA:o + 2 * RW_DECAY_LORA]
    al = pm[:, o + 2 * RW_DECAY_LORA:o + 2 * RW_DECAY_LORA + RW_AAA_LORA]
    for d, wl in enumerate((wl_f, wl_b)):
        z = w0_ref[d:d + 1, :] + _bdot(jnp.tanh(wl), w2_ref[d])
        w = -(jnp.maximum(-z, 0.0) + jnp.log(1.0 + jnp.exp(-jnp.abs(z)))) - 0.5
        ld_ref[d] = -jnp.exp(w)
    a = jax.nn.sigmoid(a0_ref[...] + _bdot(al, a2_ref[...]))
    g_ref[...] = _bdot(jax.nn.sigmoid(gl), g2_ref[...])
    kk = k * kk_ref[...]
    n2 = _head_sums(kk * kk, bd_ref[...])
    kkn = kk / jnp.maximum(jnp.sqrt(n2), 1e-12)
    r_ref[...] = r.astype(BF16)
    k_ref[...] = (k * (1.0 + (a - 1.0) * ka_ref[...])).astype(BF16)
    v_ref[...] = v.astype(BF16)
    kkn_ref[...] = kkn.astype(BF16)
    bb_ref[...] = (kkn * a).astype(BF16)


def _head_sum_matrix(width, hd):
    i = np.arange(width)
    return jnp.asarray((i[:, None] // hd == i[None, :] // hd).astype(np.float32), BF16)


def _rw_streams(p_rw, B, S, nctx_tiles, mu, w0, w2, a0, a2, g2, k_k, k_a):
    T = B * S
    J = S // ROW_TILE
    HB = ROW_TILE // GRID_W
    NH = S // GRID_W

    def rows(w):
        return pl.BlockSpec((ROW_TILE, w), lambda b, j: (b * J + j, 0))

    def rows2(w):
        return pl.BlockSpec((2, ROW_TILE, w), lambda b, j: (0, b * J + j, 0))

    def const(shape):
        return pl.BlockSpec(shape, lambda b, j: (0,) * len(shape))

    sd = jax.ShapeDtypeStruct
    kern = functools.partial(_rw_streams_kernel, nctx_tiles=nctx_tiles, n_lat_tiles=J - nctx_tiles)
    return pl.pallas_call(
        kern,
        grid=(B, J),
        in_specs=[pl.BlockSpec((GRID_W, RW_PW), lambda b, j: (b * NH + jnp.maximum(j * HB - 1, 0), 0)),
                  rows(RW_PW),
                  pl.BlockSpec((GRID_W, RW_PW), lambda b, j: (b * NH + jnp.minimum((j + 1) * HB, NH - 1), 0)),
                  const((1, RW_PW)), const((2, RW_W)), const((2, RW_DECAY_LORA, RW_W)), const((1, RW_W)),
                  const((RW_AAA_LORA, RW_W)), const((RW_GATE_LORA, RW_W)), const((1, RW_W)), const((1, RW_W)),
                  const((RW_W, RW_W))],
        out_specs=[rows(RW_W), rows(RW_W), rows(RW_W), rows(RW_W), rows(RW_W), rows2(RW_W), rows(RW_W)],
        out_shape=[sd((T, RW_W), BF16)] * 5 + [sd((2, T, RW_W), F32), sd((T, RW_W), F32)],
        scratch_shapes=[pltpu.VMEM((ROW_TILE, RW_PW), F32)],
        compiler_params=_cparams(("parallel", "parallel")),
        name="rwkv_streams",
    )(p_rw, p_rw, p_rw, _rw_reorder(mu).reshape(1, RW_PW), w0, w2, a0.reshape(1, RW_W), a2, g2,
      k_k.reshape(1, RW_W), k_a.reshape(1, RW_W), _head_sum_matrix(RW_W, RW_HEAD_DIM))


def _rwkv_kernel(r_ref, k_ref, v_ref, kk_ref, bb_ref, ld_ref, o_ref, st_ref, *, rev, nchunk):
    t = pl.program_id(1)

    @pl.when(t == 0)
    def _():
        st_ref[...] = jnp.zeros_like(st_ref)

    C = RW_CHUNK
    hd = RW_HEAD_DIM
    NH = RW_HEADS
    row = lax.broadcasted_iota(I32, (C, C), 0)
    col = lax.broadcasted_iota(I32, (C, C), 1)
    incl = ((col >= row) if rev else (col <= row))[None]
    strict = ((col > row) if rev else (col < row))[None]
    tri = jnp.where(incl[0], 1.0, 0.0).astype(BF16)
    eye = (row == col)[None]
    last_i = 0 if rev else C - 1
    order = range(nchunk - 1, -1, -1) if rev else range(nchunk)
    n_double = int(math.log2(C)) - 1

    def stack(x):
        return jnp.stack([x[c * C:(c + 1) * C, h * hd:(h + 1) * hd] for c in range(nchunk) for h in range(NH)], 0)

    ld = ld_ref[...]
    g = jnp.concatenate([_split_dot(tri, ld[c * C:(c + 1) * C]) for c in range(nchunk)], axis=0)
    g_last = jnp.concatenate([jnp.broadcast_to(g[c * C + last_i:c * C + last_i + 1], (C, NH * hd))
                              for c in range(nchunk)], axis=0)
    k = k_ref[...].astype(F32)
    bb = bb_ref[...].astype(F32)
    eng = jnp.exp(-g)
    e_end = jnp.exp(g_last - g)
    kk_t = stack(kk_ref[...].astype(F32) * jnp.exp(g - ld))
    b_t = stack(bb * eng)
    k_t = stack(k * eng)
    r_t = stack(r_ref[...].astype(F32) * jnp.exp(g))
    b_bar = stack(bb * e_end)
    k_bar = stack(k * e_end)
    dec = stack(jnp.exp(g_last))
    v = stack(v_ref[...].astype(F32))
    m_b = jnp.where(strict, _bmm_nt(kk_t, b_t), 0.0)
    m_k = jnp.where(strict, _bmm_nt(kk_t, k_t), 0.0)
    a_rb = jnp.where(incl, _bmm_nt(r_t, b_t), 0.0)
    a_rk = jnp.where(incl, _bmm_nt(r_t, k_t), 0.0)
    pw = -m_b
    tinv = jnp.where(eye, 1.0, 0.0) + pw
    for _ in range(n_double):
        pw = _bmm(pw, pw)
        tinv = tinv + _bmm(tinv, pw)
    a_t = _bmm(tinv, kk_t)
    u_b = _bmm(tinv, _bmm(m_k, v))
    q_h = r_t - _bmm(a_rb, a_t)
    o_h = _bmm(a_rk, v) - _bmm(a_rb, u_b)
    g_m = jnp.where(eye, dec, 0.0) - _bmm_tn(a_t, b_bar)
    h_m = _bmm_tn(v, k_bar) - _bmm_tn(u_b, b_bar)
    s = st_ref[...]
    for c in order:
        sl = slice(c * NH, (c + 1) * NH)
        o_c = _bmm_nt(q_h[sl], s) + o_h[sl]
        o_ref[c * C:(c + 1) * C, :] = jnp.concatenate([o_c[h] for h in range(NH)], axis=1)
        s = _bmm(s, g_m[sl]) + h_m[sl]
    st_ref[...] = s


def _rwkv_scan(r, k, v, kk, bb, ld, B, S, nctx_blocks, rev):
    T = B * S
    NB = S // ROW_TILE
    d = 1 if rev else 0
    nc = nctx_blocks

    def blk(t):
        if not rev:
            return t
        return jnp.where(t < nc, nc - 1 - t, NB - 1 - (t - nc))

    spec = pl.BlockSpec((ROW_TILE, RW_W), lambda b, t: (b * NB + blk(t), 0))
    kern = functools.partial(_rwkv_kernel, rev=rev, nchunk=ROW_TILE // RW_CHUNK)
    return pl.pallas_call(
        kern,
        grid=(B, NB),
        in_specs=[spec, spec, spec, spec, spec,
                  pl.BlockSpec((None, ROW_TILE, RW_W), lambda b, t: (d, b * NB + blk(t), 0))],
        out_specs=spec,
        out_shape=jax.ShapeDtypeStruct((T, RW_W), F32),
        scratch_shapes=[pltpu.VMEM((RW_HEADS, RW_HEAD_DIM, RW_HEAD_DIM), F32)],
        compiler_params=_cparams(("parallel", "arbitrary")),
        name="rwkv_scan_bwd" if rev else "rwkv_scan_fwd",
    )(r, k, v, kk, bb, ld)


def _filter_mlp_kernel(z_ref, w1_ref, b1_ref, w2_ref, b2_ref, w3_ref, sf_ref, win_ref, f_ref):
    h = jnp.sin(sf_ref[0:1, :] * (_hdot(z_ref[...], w1_ref[...]) + b1_ref[...]))
    h = jnp.sin(sf_ref[1:2, :] * (_hdot(h, w2_ref[...]) + b2_ref[...]))
    f_ref[...] = _hdot(h, w3_ref[...]) * win_ref[...]


def _hyena_filters(L, w1, b1, w2, b2, w3, sin_freq):
    t = np.linspace(0.0, 1.0, L, dtype=np.float32)[:, None]
    bands = (HY_EMB - 1) // 2
    wpos = (2.0 * math.pi * np.arange(L, dtype=np.float32)[:, None] / L).astype(np.float32)
    fr = np.linspace(1e-4, bands - 1, bands, dtype=np.float32)[None, :]
    z = np.concatenate([t, np.cos(fr * wpos), -np.sin(fr * wpos)], -1).astype(np.float32)
    max_decay = math.log(HY_TARGET) / HY_FAST_DECAY
    min_decay = math.log(HY_TARGET) / HY_SLOW_DECAY
    deltas = np.linspace(min_decay, max_decay, HY_CH, dtype=np.float32)
    window = np.exp(-t * np.abs(deltas)).astype(np.float32)
    dist = np.concatenate([np.arange(L), [0], np.arange(L - 1, 0, -1)])
    z2 = np.pad(z[dist], ((0, 0), (0, LANES - HY_EMB)))
    win2 = np.tile(window[dist], (1, HY_ORDER))
    win2[L] = 0.0
    w1p = jnp.pad(w1, ((0, LANES - HY_EMB), (0, 0)))
    Hd = w1.shape[1]
    FW = HY_ORDER * HY_CH
    w3s = jnp.transpose(w3.reshape(Hd, HY_ORDER, 2, HY_CH), (2, 0, 1, 3)).reshape(2, Hd, FW)
    TR = min(L, 512)
    nl = L // TR
    return pl.pallas_call(
        _filter_mlp_kernel,
        grid=(2 * nl,),
        in_specs=[pl.BlockSpec((TR, LANES), lambda i: (i, 0)),
                  pl.BlockSpec((LANES, Hd), lambda i: (0, 0)), pl.BlockSpec((1, Hd), lambda i: (0, 0)),
                  pl.BlockSpec((Hd, Hd), lambda i: (0, 0)), pl.BlockSpec((1, Hd), lambda i: (0, 0)),
                  pl.BlockSpec((None, Hd, FW), lambda i: (i // nl, 0, 0)), pl.BlockSpec((2, Hd), lambda i: (0, 0)),
                  pl.BlockSpec((TR, FW), lambda i: (i, 0))],
        out_specs=pl.BlockSpec((TR, FW), lambda i: (i, 0)),
        out_shape=jax.ShapeDtypeStruct((2 * L, FW), F32),
        compiler_params=_cparams(("parallel",)),
        name="hyena_filter_mlp",
    )(jnp.asarray(z2), w1p, b1.reshape(1, Hd), w2, b2.reshape(1, Hd), w3s, sin_freq, jnp.asarray(win2))


def _sconv_kernel(p_ref, w_ref, b_ref, o_ref):
    p = p_ref[...]
    L = p.shape[0]
    i = lax.broadcasted_iota(I32, p.shape, 0)
    prev = jnp.where(i == 0, 0.0, pltpu.roll(p, 1, 0))
    nxt = jnp.where(i == L - 1, 0.0, pltpu.roll(p, L - 1, 0))
    o_ref[...] = prev * w_ref[0:1, :] + p * w_ref[1:2, :] + nxt * w_ref[2:3, :] + b_ref[...]


def _short_conv(p_hy, B, S, Lc, conv_w, conv_b):
    L = S - Lc
    p3 = p_hy.reshape(B, L, HY_WIDTH)
    return pl.pallas_call(
        _sconv_kernel,
        grid=(B, HY_WIDTH // LANES),
        in_specs=[pl.BlockSpec((None, L, LANES), lambda b, c: (b, 0, c)),
                  pl.BlockSpec((HY_SHORT, LANES), lambda b, c: (0, c)),
                  pl.BlockSpec((1, LANES), lambda b, c: (0, c))],
        out_specs=pl.BlockSpec((None, L, LANES), lambda b, c: (b, 0, c)),
        out_shape=jax.ShapeDtypeStruct((B, L, HY_WIDTH), F32),
        compiler_params=_cparams(("parallel", "parallel")),
        name="hyena_short_conv",
    )(p3, conv_w, conv_b.reshape(1, HY_WIDTH))


def _dft_constants(L):
    n = 2 * L
    n2 = FFT_N2
    n1 = n // n2
    na = n1 // 2
    g8 = SUBLANES
    w1 = np.exp(-2j * np.pi * np.outer(np.arange(n1), np.arange(n1)) / n1)
    eye8 = np.eye(g8)

    def kron_fwd(w, real_in):
        k1n, an = w.shape
        blocks = np.stack([np.stack([w.real, -w.imag], 1), np.stack([w.imag, w.real], 1)], 1)
        if real_in:
            blocks = blocks[:, :, 0:1, :]
        m = np.einsum('kria,bc->krbiac', blocks, eye8)
        return m.reshape(k1n * 2 * g8, blocks.shape[2] * an * g8)

    m1 = kron_fwd(w1[:, :na], False)
    m1f = kron_fwd(w1, True)
    cw = np.conj(w1[:, :na]).T / n
    blocks = np.stack([np.stack([cw.real, -cw.imag], 1), np.stack([cw.imag, cw.real], 1)], 0)
    m1inv = np.einsum('iark,bc->iabkrc', blocks, eye8).reshape(2 * na * g8, n1 * 2 * g8)
    w2 = np.exp(-2j * np.pi * np.outer(np.arange(n2), np.arange(n2)) / n2)
    w2big = np.block([[w2.real, -w2.imag], [w2.imag, w2.real]])
    w2c = np.conj(w2)
    iw2big = np.block([[w2c.real, -w2c.imag], [w2c.imag, w2c.real]])
    tw = np.exp(-2j * np.pi * np.arange(n2) / n)

    def lanes(cols):
        z = np.concatenate([np.broadcast_to(col[:, None], (n2, LANES)) for col in cols], axis=1)
        return np.stack([z.real, z.imag], 0)

    tw0 = lanes([tw ** q for q in range(FFT_G)])
    twg = lanes([tw ** FFT_G] * FFT_G)
    c = lambda x, dt: jnp.asarray(np.ascontiguousarray(x), dt)
    return dict(m1=c(m1, BF16), m1f=c(m1f, BF16), m1inv=c(m1inv, BF16), w2=c(w2big, BF16), iw2=c(iw2big, BF16),
                tw0=c(tw0, F32), twg=c(twg, F32), n1=n1, na=na)


def _cplx_rows(z, twr, twi, conj):
    n2 = z.shape[0] // 2
    zr, zi = z[:n2], z[n2:]
    if conj:
        return jnp.concatenate([zr * twr + zi * twi, zi * twr - zr * twi], axis=0)
    return jnp.concatenate([zr * twr - zi * twi, zi * twr + zr * twi], axis=0)


def _split_dot(m, x):
    hi = x.astype(BF16)
    lo = (x - hi.astype(F32)).astype(BF16)
    return jnp.dot(m, hi, preferred_element_type=F32) + jnp.dot(m, lo, preferred_element_type=F32)


def _next_twiddle(tw, twg_ref):
    twr, twi = tw
    b_r, b_i = twg_ref[0], twg_ref[1]
    return twr * b_r - twi * b_i, twr * b_i + twi * b_r


def _load_group(scr, j, C):
    return jnp.concatenate([scr[j * FFT_G + q].reshape(2 * FFT_N2, C) for q in range(FFT_G)], axis=1)


def _filt_fft_kernel(f_ref, m1_ref, w2_ref, tw0_ref, twg_ref, o_ref, y_scr, *, n1):
    nbg = FFT_N2 // SUBLANES
    C = f_ref.shape[-1]
    for bg in range(nbg):
        yg = _split_dot(m1_ref[...], f_ref[:, bg].reshape(n1 * SUBLANES, C))
        y_scr[:, :, bg] = yg.reshape(n1, 2, SUBLANES, C)

    def body(j, tw):
        y = _cplx_rows(_load_group(y_scr, j, C), tw[0], tw[1], False)
        z = _split_dot(w2_ref[...], y)
        for q in range(FFT_G):
            o_ref[j * FFT_G + q] = z[:, q * C:(q + 1) * C]
        return _next_twiddle(tw, twg_ref)

    lax.fori_loop(0, n1 // FFT_G, body, (tw0_ref[0], tw0_ref[1]))


def _filter_spectrum(filt, L, dc):
    n1 = dc['n1']
    nbg = FFT_N2 // SUBLANES
    ncb = HY_CH // LANES
    full = filt.reshape(n1, nbg, SUBLANES, HY_ORDER * HY_CH)
    return pl.pallas_call(
        functools.partial(_filt_fft_kernel, n1=n1),
        grid=(HY_ORDER, ncb),
        in_specs=[pl.BlockSpec((n1, nbg, SUBLANES, LANES), lambda o, c: (0, 0, 0, o * ncb + c)),
                  pl.BlockSpec(dc['m1f'].shape, lambda o, c: (0, 0)),
                  pl.BlockSpec(dc['w2'].shape, lambda o, c: (0, 0)),
                  pl.BlockSpec(dc['tw0'].shape, lambda o, c: (0, 0, 0)),
                  pl.BlockSpec(dc['twg'].shape, lambda o, c: (0, 0, 0))],
        out_specs=pl.BlockSpec((None, n1, 2 * FFT_N2, LANES), lambda o, c: (o, 0, 0, c)),
        out_shape=jax.ShapeDtypeStruct((HY_ORDER, n1, 2 * FFT_N2, HY_CH), F32),
        scratch_shapes=[pltpu.VMEM((n1, 2, nbg, SUBLANES, LANES), F32)],
        compiler_params=_cparams(("parallel", "parallel")),
        name="hyena_filter_fft",
    )(full, dc['m1f'], dc['w2'], dc['tw0'], dc['twg'])


def _hyconv_kernel(za_ref, zb_ref, ga_ref, gb_ref, ff_ref, bias_ref, m1_ref, m1i_ref, w2_ref, iw2_ref,
                   tw0_ref, twg_ref, oa_ref, ob_ref, y_scr, *, n1, na):
    nbg = FFT_N2 // SUBLANES
    C = za_ref.shape[-1]
    half = na * SUBLANES
    for bg in range(nbg):
        xg = jnp.concatenate([za_ref[:, bg].reshape(half, C), zb_ref[:, bg].reshape(half, C)], axis=0)
        yg = jnp.dot(m1_ref[...], xg.astype(BF16), preferred_element_type=F32)
        y_scr[:, :, bg] = yg.reshape(n1, 2, SUBLANES, C)

    def body(j, tw):
        y = _cplx_rows(_load_group(y_scr, j, C), tw[0], tw[1], False)
        z = jnp.dot(w2_ref[...], y.astype(BF16), preferred_element_type=F32)
        f = _load_group(ff_ref, j, C)
        zr, zi = z[:FFT_N2], z[FFT_N2:]
        fr, fi = f[:FFT_N2], f[FFT_N2:]
        p = jnp.concatenate([zr * fr - zi * fi, zr * fi + zi * fr], axis=0)
        v = jnp.dot(iw2_ref[...], p.astype(BF16), preferred_element_type=F32)
        v = _cplx_rows(v, tw[0], tw[1], True)
        for q in range(FFT_G):
            y_scr[j * FFT_G + q] = v[:, q * C:(q + 1) * C].reshape(2, nbg, SUBLANES, C)
        return _next_twiddle(tw, twg_ref)

    lax.fori_loop(0, n1 // FFT_G, body, (tw0_ref[0], tw0_ref[1]))
    bias = bias_ref[...]
    for bg in range(nbg):
        vg = y_scr[:, :, bg].reshape(n1 * 2 * SUBLANES, C)
        out = jnp.dot(m1i_ref[...], vg.astype(BF16), preferred_element_type=F32)
        ya = out[:half].reshape(na, SUBLANES, C)
        yb = out[half:].reshape(na, SUBLANES, C)
        za = za_ref[:, bg]
        zb = zb_ref[:, bg]
        oa_ref[:, bg] = ga_ref[:, bg] * (ya + za * bias)
        ob_ref[:, bg] = gb_ref[:, bg] * (yb + zb * bias)


def _hyena_conv(z, z_col0, gates, g_col0, spec, bias, dc, B, L):
    n1, na = dc['n1'], dc['na']
    nbg = FFT_N2 // SUBLANES
    NCB = HY_CH // LANES

    def view(a):
        return a.reshape(B, na, nbg, SUBLANES, a.shape[-1])

    def seq(col0, which):
        return pl.BlockSpec((None, na, nbg, SUBLANES, LANES),
                            lambda c, p: (2 * p + which, 0, 0, 0, col0 + c))

    def const(a):
        nd = a.ndim
        return pl.BlockSpec(a.shape, lambda c, p: (0,) * nd)

    out_a, out_b = pl.pallas_call(
        functools.partial(_hyconv_kernel, n1=n1, na=na),
        grid=(NCB, B // 2),
        in_specs=[seq(z_col0, 0), seq(z_col0, 1), seq(g_col0, 0), seq(g_col0, 1),
                  pl.BlockSpec((n1, 2 * FFT_N2, LANES), lambda c, p: (0, 0, c), pipeline_mode=pl.Buffered(1)),
                  pl.BlockSpec((1, LANES), lambda c, p: (0, c)),
                  const(dc['m1']), const(dc['m1inv']), const(dc['w2']), const(dc['iw2']),
                  const(dc['tw0']), const(dc['twg'])],
        out_specs=[pl.BlockSpec((None, na, nbg, SUBLANES, LANES), lambda c, p: (p, 0, 0, 0, c)),
                   pl.BlockSpec((None, na, nbg, SUBLANES, LANES), lambda c, p: (p, 0, 0, 0, c))],
        out_shape=[jax.ShapeDtypeStruct((B // 2, na, nbg, SUBLANES, HY_CH), F32)] * 2,
        scratch_shapes=[pltpu.VMEM((n1, 2, nbg, SUBLANES, LANES), F32)],
        compiler_params=_cparams(("parallel", "arbitrary")),
        name="hyena_long_conv",
    )(view(z), view(z), view(gates), view(gates), spec, bias.reshape(1, HY_CH),
      dc['m1'], dc['m1inv'], dc['w2'], dc['iw2'], dc['tw0'], dc['twg'])
    out = jnp.stack([out_a, out_b], axis=1)
    return out.reshape(B, L, HY_CH)


def _out1_kernel(hy_ref, of_ref, ob_ref, r_ref, k_ref, v_ref, g_ref, x_ref, mod_ref, wo_ref,
                 rk_ref, lg_ref, lb_ref, bd_ref, lng_ref, lnb_ref, xl_ref, u_ref, up_ref, *, alpha):
    o = of_ref[...] + ob_ref[...]
    bd = bd_ref[...]
    inv = 1.0 / RW_HEAD_DIM
    mu = _head_sums(o, bd) * inv
    oc = o - mu
    var = _head_sums(oc * oc, bd) * inv
    on = oc * lax.rsqrt(var + RW_GN_EPS) * lg_ref[...] + lb_ref[...]
    bonus = (_head_sums(r_ref[...].astype(F32) * k_ref[...].astype(F32) * rk_ref[...], bd)
             * v_ref[...].astype(F32))
    y = jnp.concatenate([hy_ref[...], (on + bonus) * g_ref[...]], axis=1)
    _post_mix(y, x_ref[...], mod_ref, wo_ref, lng_ref, lnb_ref, alpha, xl_ref, u_ref, up_ref)


def _readout1(hy, o_f, o_b, r, k, v, g, xcat, mods, B, S, Lc, w_out, r_k, ln_g, ln_b, dn_g, dn_b, alpha):
    D = xcat.shape[-1]
    L = S - Lc
    J = S // ROW_TILE
    JL = L // ROW_TILE
    JC = Lc // ROW_TILE

    def cat_rows(w):
        return pl.BlockSpec((ROW_TILE, w), lambda b, j: (b * J + JC + j, 0))

    def lat_rows(w):
        return pl.BlockSpec((ROW_TILE, w), lambda b, j: (b * JL + j, 0))

    def const(shape):
        return pl.BlockSpec(shape, lambda b, j: (0,) * len(shape))

    sd = jax.ShapeDtypeStruct
    return pl.pallas_call(
        functools.partial(_out1_kernel, alpha=alpha),
        grid=(B, JL),
        in_specs=[lat_rows(HY_CH), cat_rows(RW_W), cat_rows(RW_W), cat_rows(RW_W), cat_rows(RW_W), cat_rows(RW_W),
                  cat_rows(RW_W), cat_rows(D),
                  pl.BlockSpec((None, 6, D), lambda b, j: (b, 0, 0)),
                  const((HY_CH + RW_W, D)), const((1, RW_W)), const((1, RW_W)), const((1, RW_W)),
                  const((RW_W, RW_W)), const((1, D)), const((1, D))],
        out_specs=[lat_rows(D), lat_rows(D),
                   pl.BlockSpec((ROW_TILE * ROW_SUB, LANES), lambda b, j: (b * JL + j, 0))],
        out_shape=[sd((B * L, D), F32), sd((B * L, D), F32), sd((B * L * ROW_SUB, LANES), U32)],
        compiler_params=_cparams(("parallel", "parallel")),
        name="readout1",
    )(hy, o_f, o_b, r, k, v, g, xcat.reshape(B * S, D), mods, w_out.astype(BF16),
      r_k.reshape(1, RW_W), ln_g.reshape(1, RW_W), ln_b.reshape(1, RW_W),
      _head_sum_matrix(RW_W, RW_HEAD_DIM), dn_g.reshape(1, D), dn_b.reshape(1, D))


def kernel(x, c, ctx, c_ctx, mod_w, mod_b, ln1_g, ln1_b, ln2_g, ln2_b, ev_w_in, ev_w_out, gla_gate_w2, gla_gate_b, gla_norm_g, hg_lb_logits, hg_norm_g, od_w_in, od_w_out, hy_conv_w, hy_conv_b, hy_ffn_w1, hy_ffn_b1, hy_ffn_w2, hy_ffn_b2, hy_ffn_w3, hy_sin_freq, hy_bias, rw_mu, rw_w0, rw_w2, rw_a0, rw_a2, rw_g2, rw_k_k, rw_k_a, rw_r_k, rw_ln_g, rw_ln_b, router_w, router_bias, exp_w13, exp_w2, sh_w13, sh_w2):
    B, L, D = x.shape
    Lc = ctx.shape[1]
    S = Lc + L
    depth = mod_w.shape[0]
    assert depth == 2 and L % ROW_TILE == 0 and Lc % ROW_TILE == 0 and B % 2 == 0
    assert L % (FFT_N2 * 2) == 0 and L % GRID_W == 0
    alpha = (2 * depth) ** 0.25
    nctx = Lc // ROW_TILE
    J = S // ROW_TILE
    T = B * S

    cc = jnp.concatenate([c, c_ctx[None, :]], axis=0)
    cc = jnp.pad(cc, ((0, (-cc.shape[0]) % SUBLANES), (0, 0)))
    hg_lb = jnp.cumsum(jax.nn.softmax(hg_lb_logits.astype(F32), axis=0), axis=0)
    xcat = jnp.concatenate([ctx, x], axis=1)

    mods = _modulation(cc, mod_w[0], mod_b[0])
    gq, gk, gv, gla, r, hq, hk, hla, hv, hgate = _in_proj0(xcat, mods, nctx, ev_w_in[0], gla_gate_w2[0],
                                                            gla_gate_b[0], hg_lb[0])
    gk3 = gk.reshape(1, T, GLA_QK)
    o_gf = _gated_recurrence(gq, gk3, gv, gla, 0, B, S, nctx, False, GLA_HEADS, GLA_DK, GLA_DV)
    o_gb = _gated_recurrence(gq, gk3, gv, gla, 0, B, S, nctx, True, GLA_HEADS, GLA_DK, GLA_DV)
    o_hf = _gated_recurrence(hq, hk, hv, hla, 0, B, S, nctx, False, HG_HEADS, HG_EXPAND, HG_EXPAND)
    o_hb = _gated_recurrence(hq, hk, hv, hla, 1, B, S, nctx, True, HG_HEADS, HG_EXPAND, HG_EXPAND)
    xl, u, up = _readout0(o_gf, o_gb, o_hf, o_hb, r, hgate, xcat, mods, nctx, ev_w_out[0], gla_norm_g[0],
                          hg_norm_g[0], ln1_g[0], ln1_b[0], alpha)
    cmb_rows = 2 * CMB_TILE
    tiles_per_b = S // cmb_rows
    tile_in_b = jnp.arange(B * tiles_per_b, dtype=I32) % tiles_per_b
    mod_of_tile = jnp.where(tile_in_b < Lc // cmb_rows, B, jnp.arange(B * tiles_per_b, dtype=I32) // tiles_per_b)
    x1 = _moe_block(u, up, xl, mods, mod_of_tile.astype(I32), router_w[0], router_bias[0], exp_w13, exp_w2, 0,
                    sh_w13[0], sh_w2[0], ln2_g[0], ln2_b[0], alpha)
    xcat = x1.reshape(B, S, D)

    mods = _modulation(cc, mod_w[1], mod_b[1])
    p_hy, p_rw = _in_proj1(xcat, mods, nctx, od_w_in[0])
    rr, rk, rv, rkk, rbb, rld, rg = _rw_streams(p_rw, B, S, nctx, rw_mu[0], rw_w0[0], rw_w2[0], rw_a0[0],
                                                 rw_a2[0], rw_g2[0], rw_k_k[0], rw_k_a[0])
    o_f = _rwkv_scan(rr, rk, rv, rkk, rbb, rld, B, S, nctx, False)
    o_b = _rwkv_scan(rr, rk, rv, rkk, rbb, rld, B, S, nctx, True)
    dc = _dft_constants(L)
    filt = _hyena_filters(L, hy_ffn_w1[0], hy_ffn_b1[0], hy_ffn_w2[0], hy_ffn_b2[0], hy_ffn_w3[0], hy_sin_freq[0])
    spec = _filter_spectrum(filt, L, dc)
    uu = _short_conv(p_hy, B, S, Lc, hy_conv_w[0], hy_conv_b[0])
    NCB = HY_CH // LANES
    z1 = _hyena_conv(uu, 0, uu, NCB, spec[0], hy_bias[0, 0], dc, B, L)
    z2 = _hyena_conv(z1, 0, uu, 2 * NCB, spec[1], hy_bias[0, 1], dc, B, L)
    xl, u, up = _readout1(z2.reshape(B * L, HY_CH), o_f, o_b, rr, rk, rv, rg, xcat, mods, B, S, Lc, od_w_out[0],
                          rw_r_k[0], rw_ln_g[0], rw_ln_b[0], ln1_g[1], ln1_b[1], alpha)
    mod_of_tile = (jnp.arange(B * (L // cmb_rows), dtype=I32) // (L // cmb_rows)).astype(I32)
    out = _moe_block(u, up, xl, mods, mod_of_tile, router_w[1], router_bias[1], exp_w13, exp_w2, 1,
                     sh_w13[1], sh_w2[1], ln2_g[1], ln2_b[1], alpha)
    return out.reshape(B, L, D)
```
